```python
import jax, jax.numpy as jnp
from jax import lax
import numpy as np

D_MODEL = 1024
BATCH = 16
SEQ = 256
DEPTH = 2
DEC_BATCH = 4
DEC_SEQ = 1024
PAST_LEN = 256

GRID_W = 64
GROUP_W = D_MODEL // 4
CONV_W = 31
CHUNK = 128
GMLP_HEADS = 4
GMLP_HD = GROUP_W // GMLP_HEADS
POOL_WINDOWS = (2, 4, 8, 16)
POOL_GROUPS = len(POOL_WINDOWS)
POOL_GW = GROUP_W // POOL_GROUPS
RET_HEADS = 4
RET_HD = GROUP_W // RET_HEADS
RET_CHUNK = 128
ROPE_BASE = 10000.0
D_FF = -(-8 * D_MODEL // (3 * 256)) * 256
IN_COLS = 11 * GROUP_W
EPS = 1e-6

kernel_name = "hybrid_parallel_groups_diffusion_step"

F32 = jnp.float32


def _rms_norm(x, g):
    xf = x.astype(F32)
    y = xf * lax.rsqrt(jnp.mean(xf * xf, axis=-1, keepdims=True) + EPS)
    return (y * g.astype(F32)).astype(x.dtype)


def _conformer_conv(a, dw, b, ln_g, ln_b, pw):
    a1, a2 = jnp.split(a, 2, axis=-1)
    h = a1 * jax.nn.sigmoid(a2)
    h = lax.conv_general_dilated(h, dw[:, None, :].astype(h.dtype), window_strides=(1,),
                                 padding=[(CONV_W // 2, CONV_W // 2)],
                                 dimension_numbers=('NWC', 'WIO', 'NWC'),
                                 feature_group_count=GROUP_W) + b
    hf = h.astype(F32)
    mu = jnp.mean(hf, axis=-1, keepdims=True)
    var = jnp.mean(jnp.square(hf - mu), axis=-1, keepdims=True)
    hn = (hf - mu) * lax.rsqrt(var + EPS) * ln_g.astype(F32) + ln_b.astype(F32)
    return jax.nn.silu(hn).astype(a.dtype) @ pw


def _chunk_gmlp(uv, ws, b):
    u, v = jnp.split(uv, 2, axis=-1)
    B, L, _ = v.shape
    vh = v.reshape(B, L // CHUNK, CHUNK, GMLP_HEADS, GMLP_HD)
    s = jnp.einsum('hpq,bnqhd->bnphd', ws, vh) + jnp.swapaxes(b, 0, 1)[:, :, None]
    return u * s.reshape(B, L, GROUP_W)


def _multiscale_pool(p, pool_w, pool_scale):
    B, L, _ = p.shape
    pf = p.astype(F32)
    csum = jnp.concatenate([jnp.zeros((B, 1, GROUP_W), F32), jnp.cumsum(pf, axis=1)], axis=1)
    t = jnp.arange(L)
    outs = []
    for gi, w in enumerate(POOL_WINDOWS):
        lo = jnp.clip(t - w // 2, 0, L)
        hi = jnp.clip(t + w // 2, 0, L)
        sl = slice(gi * POOL_GW, (gi + 1) * POOL_GW)
        cs = csum[:, :, sl]
        mean = (cs[:, hi] - cs[:, lo]) / (hi - lo).astype(F32)[None, :, None]
        outs.append(mean - pf[:, :, sl])
    d = jnp.stack(outs, axis=2).astype(p.dtype)
    y = jnp.einsum('blgc,gcd->blgd', d, pool_w).reshape(B, L, GROUP_W)
    return y * pool_scale


def _axial_rope(x):
    L = x.shape[2]
    rows = L // GRID_W
    r = jnp.repeat(jnp.arange(rows, dtype=F32), GRID_W)
    c = (jnp.arange(rows * GRID_W) % GRID_W).astype(F32)
    nf = RET_HD // 4
    inv = ROPE_BASE ** (-jnp.arange(nf, dtype=F32) / nf)
    xf = x.astype(F32)

    def rot(xh, pos):
        ang = pos[:, None] * inv
        cos, sin = jnp.cos(ang), jnp.sin(ang)
        x1, x2 = xh[..., :nf], xh[..., nf:]
        return jnp.concatenate([x1 * cos - x2 * sin, x1 * sin + x2 * cos], axis=-1)

    half = RET_HD // 2
    return jnp.concatenate([rot(xf[..., :half], r), rot(xf[..., half:], c)], axis=-1).astype(x.dtype)


def _retention_scan(q, k, v, log_gamma, s0):
    B, H, L, d = q.shape
    n = L // RET_CHUNK

    def chunks(z):
        return jnp.moveaxis(z.reshape(B, H, n, RET_CHUNK, d), 2, 0)

    i = jnp.arange(RET_CHUNK, dtype=F32)
    diff = i[:, None] - i[None, :]
    lower = diff >= 0
    dmask = jnp.where(lower, jnp.exp(jnp.where(lower, diff, 0.0)[None] * log_gamma[:, None, None]), 0.0)
    q_dec = jnp.exp((i + 1.0)[None, :] * log_gamma[:, None])
    k_dec = jnp.exp((RET_CHUNK - 1.0 - i)[None, :] * log_gamma[:, None])
    s_dec = jnp.exp(RET_CHUNK * log_gamma)

    def step(s, qkv):
        qc, kc, vc = qkv
        att = jnp.einsum('bhid,bhjd->bhij', qc, kc) * dmask
        o = (jnp.einsum('bhij,bhje->bhie', att, vc)
             + jnp.einsum('bhid,bhde->bhie', qc, s) * q_dec[:, :, None])
        s = s * s_dec[:, None, None] + jnp.einsum('bhjd,bhje->bhde', kc * k_dec[:, :, None], vc)
        return s, o

    s_fin, o = lax.scan(step, s0, (chunks(q), chunks(k), chunks(v)))
    return jnp.moveaxis(o, 0, 2).reshape(B, H, L, d), s_fin


def _retention(r, log_gamma2, s0, rotate):
    B, L, _ = r.shape

    def heads(z):
        return jnp.swapaxes(z.reshape(B, L, RET_HEADS, RET_HD), 1, 2)

    qf, kf, qb, kb, v, g = jnp.split(r, 6, axis=-1)
    qf, kf, qb, kb, v = heads(qf), heads(kf), heads(qb), heads(kb), heads(v)
    if rotate:
        qf, kf, qb, kb = _axial_rope(qf), _axial_rope(kf), _axial_rope(qb), _axial_rope(kb)
    ks = RET_HD ** -0.5
    vf = v.astype(F32)
    o_f, s_f = _retention_scan(qf.astype(F32), kf.astype(F32) * ks, vf, log_gamma2[0], s0[:, 0])
    flip = lambda z: jnp.flip(z, axis=2)
    o_b, s_b = _retention_scan(flip(qb.astype(F32)), flip(kb.astype(F32)) * ks, flip(vf), log_gamma2[1], s0[:, 1])
    o = o_f + flip(o_b)
    mu = jnp.mean(o, axis=-1, keepdims=True)
    var = jnp.mean(jnp.square(o - mu), axis=-1, keepdims=True)
    on = ((o - mu) * lax.rsqrt(var + EPS))
    on = jnp.swapaxes(on, 1, 2).reshape(B, L, GROUP_W).astype(r.dtype)
    return jax.nn.silu(g) * on, jnp.stack([s_f, s_b], axis=1)


def _layer(x, mod, s0, rotate, g_norm1, g_norm2, w_in, w_out, conv_dw, conv_b, conv_ln_g, conv_ln_b,
           conv_pw, gmlp_ws, gmlp_b, pool_w, pool_scale, ret_decay, w_ffn_in, w_ffn_out):
    sh1, sc1, ga1, sh2, sc2, ga2 = jnp.split(mod, 6, axis=-1)
    h = _rms_norm(x, g_norm1) * (1.0 + sc1) + sh1
    proj = h @ w_in
    a, uv, p, r = jnp.split(proj, [2 * GROUP_W, 4 * GROUP_W, 5 * GROUP_W], axis=-1)
    ya = _conformer_conv(a, conv_dw, conv_b, conv_ln_g, conv_ln_b, conv_pw)
    yb = _chunk_gmlp(uv, gmlp_ws, gmlp_b)
    yc = _multiscale_pool(p, pool_w, pool_scale)
    yd, s_new = _retention(r, jax.nn.log_sigmoid(ret_decay.astype(F32)), s0, rotate)
    x = x + ga1 * (jnp.concatenate([ya, yb, yc, yd], axis=-1) @ w_out)
    h = _rms_norm(x, g_norm2) * (1.0 + sc2) + sh2
    gt, up = jnp.split(h @ w_ffn_in, 2, axis=-1)
    x = x + ga2 * ((jax.nn.silu(gt) * up) @ w_ffn_out)
    return x, s_new


def setup_inputs(seed: int = 0) -> dict:
    key = jax.random.key(seed)
    ks = jax.random.split(key, 32)
    nrm = lambda k, shape, s: jax.random.normal(k, shape, F32) * s
    gam = 1.0 - 2.0 ** jnp.linspace(-5.0, -12.0, RET_HEADS, dtype=F32)
    logit = jnp.log(gam) - jnp.log1p(-gam)
    return {
        'x_prompt': nrm(ks[0], (BATCH, SEQ, D_MODEL), 1.0),
        'x_sample': nrm(ks[1], (DEC_BATCH, DEC_SEQ, D_MODEL), 1.0),
        'state_ret': nrm(ks[2], (DEC_BATCH, DEPTH, 2, RET_HEADS, RET_HD, RET_HD), 0.25),
        'c': nrm(ks[3], (DEC_BATCH, D_MODEL), 1.0),
        'c_ctx': nrm(ks[4], (D_MODEL,), 1.0),
        'w_ada': nrm(ks[5], (DEPTH, D_MODEL, 6 * D_MODEL), 0.5 * D_MODEL ** -0.5),
        'b_ada': nrm(ks[6], (DEPTH, 6 * D_MODEL), 0.02),
        'g_norm1': 1.0 + nrm(ks[7], (DEPTH, D_MODEL), 0.02),
        'g_norm2': 1.0 + nrm(ks[8], (DEPTH, D_MODEL), 0.02),
        'w_in': nrm(ks[9], (DEPTH, D_MODEL, IN_COLS), D_MODEL ** -0.5),
        'w_out': nrm(ks[10], (DEPTH, D_MODEL, D_MODEL), D_MODEL ** -0.5),
        'conv_dw': nrm(ks[11], (DEPTH, CONV_W, GROUP_W), CONV_W ** -0.5),
        'conv_b': nrm(ks[12], (DEPTH, GROUP_W), 0.02),
        'conv_ln_g': 1.0 + nrm(ks[13], (DEPTH, GROUP_W), 0.02),
        'conv_ln_b': nrm(ks[14], (DEPTH, GROUP_W), 0.02),
        'conv_pw': nrm(ks[15], (DEPTH, GROUP_W, GROUP_W), GROUP_W ** -0.5),
        'gmlp_ws': nrm(ks[16], (DEPTH, GMLP_HEADS, CHUNK, CHUNK), CHUNK ** -0.5),
        'gmlp_b': 1.0 + nrm(ks[17], (DEPTH, GMLP_HEADS, CHUNK), 0.02),
        'pool_w': nrm(ks[18], (DEPTH, POOL_GROUPS, POOL_GW, POOL_GW), POOL_GW ** -0.5),
        'pool_scale': 1.0 + nrm(ks[19], (DEPTH, GROUP_W), 0.02),
        'ret_decay': logit[None, None, :] + nrm(ks[20], (DEPTH, 2, RET_HEADS), 0.05),
        'w_ffn_in': nrm(ks[21], (DEPTH, D_MODEL, 2 * D_FF), D_MODEL ** -0.5),
        'w_ffn_out': nrm(ks[22], (DEPTH, D_FF, D_MODEL), D_FF ** -0.5),
        'g_final': 1.0 + nrm(ks[23], (D_MODEL,), 0.02),
    }


def reference(x_prompt, x_sample, state_ret, c, c_ctx, w_ada, b_ada, g_norm1, g_norm2, w_in, w_out,
              conv_dw, conv_b, conv_ln_g, conv_ln_b, conv_pw, gmlp_ws, gmlp_b, pool_w, pool_scale,
              ret_decay, w_ffn_in, w_ffn_out, g_final):
    xc = x_prompt
    s_zero = jnp.zeros((x_prompt.shape[0], 2, RET_HEADS, RET_HD, RET_HD), F32)
    ctx_states = []
    xl = x_sample
    for l in range(DEPTH):
        mod_ctx = (jax.nn.silu(c_ctx) @ w_ada[l] + b_ada[l])[None, None, :]
        xc, s_new = _layer(xc, mod_ctx, s_zero, False, g_norm1[l], g_norm2[l], w_in[l], w_out[l],
                           conv_dw[l], conv_b[l], conv_ln_g[l], conv_ln_b[l], conv_pw[l], gmlp_ws[l],
                           gmlp_b[l], pool_w[l], pool_scale[l], ret_decay[l], w_ffn_in[l], w_ffn_out[l])
        ctx_states.append(s_new)
        mod_lat = (jax.nn.silu(c) @ w_ada[l] + b_ada[l])[:, None, :]
        xl, _ = _layer(xl, mod_lat, state_ret[:, l].astype(F32), True, g_norm1[l], g_norm2[l], w_in[l],
                       w_out[l], conv_dw[l], conv_b[l], conv_ln_g[l], conv_ln_b[l], conv_pw[l], gmlp_ws[l],
                       gmlp_b[l], pool_w[l], pool_scale[l], ret_decay[l], w_ffn_in[l], w_ffn_out[l])
    y_prompt = _rms_norm(xc, g_final)
    y_sample = _rms_norm(xl, g_final)
    new_state_ret = jnp.stack(ctx_states, axis=1).astype(x_prompt.dtype)
    return (y_prompt, y_sample, new_state_ret)
```

```python
import functools

import jax
import jax.numpy as jnp
from jax import lax
from jax.experimental import pallas as pl
from jax.experimental.pallas import tpu as pltpu

F32 = jnp.float32
BF16 = jnp.bfloat16

D_MODEL = 1024
DEPTH = 2
GRID_W = 64
GROUP_W = D_MODEL // 4
CONV_W = 31
CHUNK = 128
HEADS = 4
HEAD_D = GROUP_W // HEADS
POOL_WINDOWS = (2, 4, 8, 16)
ROPE_BASE = 10000.0
D_FF = 2816
IN_COLS = 11 * GROUP_W
EPS = 1e-6

TB = 1024
ROW_TILE = 256
CONV_TILE = 64
PAD = 16
FF_TILE = 256
N_FF_TILES = D_FF // FF_TILE
ADA_TILE = 1536
MOD_ROWS = 8
VMEM_LIMIT_BYTES = 60 * 1024 * 1024


def _sigmoid(x):
    return 1.0 / (1.0 + jnp.exp(-x))


def _silu(x):
    return x * _sigmoid(x)


def _norm_mod(x, g, scale, shift):
    ms = jnp.mean(x * x, axis=-1, keepdims=True)
    return (x * lax.rsqrt(ms + EPS) * g) * (1.0 + scale) + shift


def _head_stack(x, lane_head):
    return jnp.concatenate([jnp.where(lane_head == h, x, 0.0) for h in range(HEADS)], axis=0)


def _const_spec(shape):
    zeros = (0,) * len(shape)
    return pl.BlockSpec(shape, lambda i: zeros, pipeline_mode=pl.Buffered(1))


def _ada_kernel(c_ref, w_ref, b_ref, o_ref):
    a = _silu(c_ref[...]).astype(BF16)
    o_ref[...] = jnp.dot(a, w_ref[...].astype(BF16), preferred_element_type=F32) + b_ref[...]


def _ada_rows(cs, w_ada, b_ada):
    n_tiles = 6 * D_MODEL // ADA_TILE
    return pl.pallas_call(
        _ada_kernel,
        grid=(DEPTH, n_tiles),
        in_specs=[
            pl.BlockSpec((MOD_ROWS, D_MODEL), lambda l, j: (0, 0)),
            pl.BlockSpec((None, D_MODEL, ADA_TILE), lambda l, j: (l, 0, j)),
            pl.BlockSpec((None, 1, ADA_TILE), lambda l, j: (l, 0, j)),
        ],
        out_specs=pl.BlockSpec((None, MOD_ROWS, ADA_TILE), lambda l, j: (l, 0, j)),
        out_shape=jax.ShapeDtypeStruct((DEPTH, MOD_ROWS, 6 * D_MODEL), F32),
        compiler_params=pltpu.CompilerParams(
            dimension_semantics=("arbitrary", "arbitrary"), vmem_limit_bytes=VMEM_LIMIT_BYTES),
        name="ada_rows",
    )(cs, w_ada, b_ada.reshape(DEPTH, 1, 6 * D_MODEL))


def _mixer_kernel(*refs, seq_len, rotate):
    n_seq = TB // seq_len
    n_chunk = seq_len // CHUNK
    it = iter(refs)
    x_ref, mod_ref, g1_ref, w_in_ref, w_out_ref = (next(it) for _ in range(5))
    dw_ref, cb_ref, lng_ref, lnb_ref, pw_ref = (next(it) for _ in range(5))
    wcat_ref, gbias_ref, pbd_ref, pscale_ref, lg_ref = (next(it) for _ in range(5))
    if rotate:
        cos_ref, sin_ref, s0_ref = (next(it) for _ in range(3))
    y_ref = next(it)
    if not rotate:
        st_ref = next(it)
    (hbf_ref, proj_ref, cat_ref, pad_ref, tmp_ref, of_ref,
     dcat_ref, qdec_ref, kdec_ref, sdec_ref, s_ref, gmat_ref) = it

    sh1 = mod_ref[:, 0:D_MODEL]
    sc1 = mod_ref[:, D_MODEL:2 * D_MODEL]
    ga1 = mod_ref[:, 2 * D_MODEL:3 * D_MODEL]

    lane = lax.broadcasted_iota(jnp.int32, (1, GROUP_W), 1)
    lane_head = lane // HEAD_D

    def norm_body(i, _):
        r0 = pl.multiple_of(i * ROW_TILE, ROW_TILE)
        h = _norm_mod(x_ref[pl.ds(r0, ROW_TILE), :], g1_ref[...], sc1, sh1)
        hbf_ref[pl.ds(r0, ROW_TILE), :] = h.astype(BF16)
        return 0
    lax.fori_loop(0, TB // ROW_TILE, norm_body, 0)

    def project(col0, ncols):
        def body(i, _):
            r0 = pl.multiple_of(i * ROW_TILE, ROW_TILE)
            proj_ref[pl.ds(r0, ROW_TILE), 0:ncols] = jnp.dot(
                hbf_ref[pl.ds(r0, ROW_TILE), :], w_in_ref[:, col0:col0 + ncols],
                preferred_element_type=F32)
            return 0
        lax.fori_loop(0, TB // ROW_TILE, body, 0)

    for s in range(n_seq):
        pad_ref[s, 0:PAD, :] = jnp.zeros((PAD, GROUP_W), F32)
        pad_ref[s, PAD + seq_len:PAD + seq_len + PAD, :] = jnp.zeros((PAD, GROUP_W), F32)

    tiles_per_seq = seq_len // CONV_TILE
    chunks_per_seq = seq_len // CHUNK

    def seq_and_offset(i, per_seq, size):
        if n_seq == 1:
            return 0, pl.multiple_of(i * size, size)
        return i // per_seq, pl.multiple_of((i % per_seq) * size, size)

    project(0, 2 * GROUP_W)

    def glu_body(i, _):
        s, t0 = seq_and_offset(i, chunks_per_seq, CHUNK)
        r0 = pl.multiple_of(i * CHUNK, CHUNK)
        a1 = proj_ref[pl.ds(r0, CHUNK), 0:GROUP_W]
        a2 = proj_ref[pl.ds(r0, CHUNK), GROUP_W:2 * GROUP_W]
        pad_ref[s, pl.ds(pl.multiple_of(t0 + PAD, 8), CHUNK), :] = a1 * _sigmoid(a2)
        return 0
    lax.fori_loop(0, TB // CHUNK, glu_body, 0)

    win = CONV_TILE + 2 * PAD

    def conv_body(i, _):
        s, t0 = seq_and_offset(i, tiles_per_seq, CONV_TILE)
        w = pad_ref[s, pl.ds(t0, win), :]
        acc = jnp.zeros((CONV_TILE, GROUP_W), F32) + cb_ref[...]
        for b in range(8):
            wb = w if b == 0 else pltpu.roll(w, win - b, axis=0)
            for a in range(4):
                k = 8 * a + b - 1
                if 0 <= k < CONV_W:
                    acc = acc + dw_ref[k:k + 1, :] * wb[8 * a:8 * a + CONV_TILE, :]
        tmp_ref[pl.ds(pl.multiple_of(i * CONV_TILE, CONV_TILE), CONV_TILE), :] = acc
        return 0
    lax.fori_loop(0, TB // CONV_TILE, conv_body, 0)

    def ln_pw_body(i, _):
        r0 = pl.multiple_of(i * ROW_TILE, ROW_TILE)
        c = tmp_ref[pl.ds(r0, ROW_TILE), :]
        mu = jnp.mean(c, axis=-1, keepdims=True)
        cen = c - mu
        var = jnp.mean(cen * cen, axis=-1, keepdims=True)
        hn = cen * lax.rsqrt(var + EPS) * lng_ref[...] + lnb_ref[...]
        ya = jnp.dot(_silu(hn).astype(BF16), pw_ref[...], preferred_element_type=F32)
        cat_ref[pl.ds(r0, ROW_TILE), 0:GROUP_W] = ya.astype(BF16)
        return 0
    lax.fori_loop(0, TB // ROW_TILE, ln_pw_body, 0)

    project(2 * GROUP_W, 2 * GROUP_W)

    def gmlp_body(i, _):
        r0 = pl.multiple_of(i * CHUNK, CHUNK)
        u = proj_ref[pl.ds(r0, CHUNK), 0:GROUP_W]
        v = proj_ref[pl.ds(r0, CHUNK), GROUP_W:2 * GROUP_W]
        vstack = _head_stack(v, lane_head).astype(BF16)
        sg = jnp.dot(wcat_ref[...], vstack, preferred_element_type=F32) + gbias_ref[...]
        cat_ref[pl.ds(r0, CHUNK), GROUP_W:2 * GROUP_W] = (u * sg).astype(BF16)
        return 0
    lax.fori_loop(0, TB // CHUNK, gmlp_body, 0)

    project(4 * GROUP_W, GROUP_W)

    def pool_fill_body(i, _):
        s, t0 = seq_and_offset(i, chunks_per_seq, CHUNK)
        r0 = pl.multiple_of(i * CHUNK, CHUNK)
        pad_ref[s, pl.ds(pl.multiple_of(t0 + PAD, 8), CHUNK), :] = proj_ref[pl.ds(r0, CHUNK), 0:GROUP_W]
        return 0
    lax.fori_loop(0, TB // CHUNK, pool_fill_body, 0)

    pool_group = lane // (GROUP_W // len(POOL_WINDOWS))
    half_w = jnp.where(pool_group == 0, POOL_WINDOWS[0] // 2,
                       jnp.where(pool_group == 1, POOL_WINDOWS[1] // 2,
                                 jnp.where(pool_group == 2, POOL_WINDOWS[2] // 2, POOL_WINDOWS[3] // 2)))
    pwin = CONV_TILE + 16

    def pool_body(i, _):
        s, t0 = seq_and_offset(i, tiles_per_seq, CONV_TILE)
        w = pad_ref[s, pl.ds(pl.multiple_of(t0 + PAD - 8, 8), pwin), :]
        lo, hi = [], []
        for b in range(8):
            wb = w if b == 0 else pltpu.roll(w, pwin - b, axis=0)
            lo.append(wb[0:CONV_TILE, :])
            hi.append(wb[8:8 + CONV_TILE, :])
        s2 = lo[7] + hi[0]
        s4 = s2 + lo[6] + hi[1]
        s8 = s4 + lo[5] + lo[4] + hi[2] + hi[3]
        s16 = s8 + lo[3] + lo[2] + lo[1] + lo[0] + hi[4] + hi[5] + hi[6] + hi[7]
        ssum = jnp.where(pool_group == 0, s2, jnp.where(pool_group == 1, s4,
                                                       jnp.where(pool_group == 2, s8, s16)))
        tpos = t0 + lax.broadcasted_iota(jnp.int32, (CONV_TILE, GROUP_W), 0)
        cnt = jnp.minimum(tpos + half_w, seq_len) - jnp.maximum(tpos - half_w, 0)
        dlt = ssum / cnt.astype(F32) - hi[0]
        tmp_ref[pl.ds(pl.multiple_of(i * CONV_TILE, CONV_TILE), CONV_TILE), :] = dlt
        return 0
    lax.fori_loop(0, TB // CONV_TILE, pool_body, 0)

    def pool_mix_body(i, _):
        r0 = pl.multiple_of(i * ROW_TILE, ROW_TILE)
        yc = jnp.dot(tmp_ref[pl.ds(r0, ROW_TILE), :].astype(BF16), pbd_ref[...],
                     preferred_element_type=F32) * pscale_ref[...]
        cat_ref[pl.ds(r0, ROW_TILE), 2 * GROUP_W:3 * GROUP_W] = yc.astype(BF16)
        return 0
    lax.fori_loop(0, TB // ROW_TILE, pool_mix_body, 0)

    project(5 * GROUP_W, 6 * GROUP_W)

    ri = lax.broadcasted_iota(jnp.int32, (CHUNK, HEADS * CHUNK), 0)
    ci = lax.broadcasted_iota(jnp.int32, (CHUNK, HEADS * CHUNK), 1)
    cj = ci % CHUNK
    chead = ci // CHUNK
    rq = lax.broadcasted_iota(jnp.int32, (CHUNK, GROUP_W), 0).astype(F32)
    for d in range(2):
        lgs = [lg_ref[d * HEADS + h] for h in range(HEADS)]
        lg_wide = jnp.where(chead == 0, lgs[0], jnp.where(chead == 1, lgs[1],
                                                          jnp.where(chead == 2, lgs[2], lgs[3])))
        lg_lane = jnp.where(lane_head == 0, lgs[0], jnp.where(lane_head == 1, lgs[1],
                                                              jnp.where(lane_head == 2, lgs[2], lgs[3])))
        dist = (ri - cj) if d == 0 else (cj - ri)
        keep = dist >= 0
        dcat_ref[d] = jnp.where(keep, jnp.exp(jnp.where(keep, dist, 0).astype(F32) * lg_wide), 0.0)
        if d == 0:
            qdec_ref[d] = jnp.exp((rq + 1.0) * lg_lane)
            kdec_ref[d] = jnp.exp((CHUNK - 1.0 - rq) * lg_lane)
        else:
            qdec_ref[d] = jnp.exp((CHUNK - rq) * lg_lane)
            kdec_ref[d] = jnp.exp(rq * lg_lane)
        sdec_ref[d] = jnp.exp(float(CHUNK) * lg_lane)

    rr = lax.broadcasted_iota(jnp.int32, (GROUP_W, GROUP_W), 0) // HEAD_D
    cc = lax.broadcasted_iota(jnp.int32, (GROUP_W, GROUP_W), 1) // HEAD_D
    same_head = rr == cc
    gmat_ref[...] = jnp.where(same_head, 1.0 / HEAD_D, 0.0).astype(BF16)

    lane_bit = (lane & 16) == 0
    k_scale = HEAD_D ** -0.5

    def rope(z, r0):
        if not rotate:
            return z
        cos = cos_ref[pl.ds(r0, CHUNK), :]
        sin = sin_ref[pl.ds(r0, CHUNK), :]
        halves = []
        for c0 in (0, 128):
            zz = z[:, c0:c0 + 128]
            partner = jnp.where(lane_bit[:, c0:c0 + 128],
                                pltpu.roll(zz, 128 - 16, axis=1), pltpu.roll(zz, 16, axis=1))
            halves.append(partner)
        return z * cos + jnp.concatenate(halves, axis=1) * sin

    def ret_chunk(d, r0):
        qc0 = 0 if d == 0 else 2 * GROUP_W
        q = rope(proj_ref[pl.ds(r0, CHUNK), qc0:qc0 + GROUP_W], r0)
        k = rope(proj_ref[pl.ds(r0, CHUNK), qc0 + GROUP_W:qc0 + 2 * GROUP_W], r0) * k_scale
        v = proj_ref[pl.ds(r0, CHUNK), 4 * GROUP_W:5 * GROUP_W]
        qb = q.astype(BF16)
        kstack = _head_stack(k, lane_head).astype(BF16)
        att = lax.dot_general(qb, kstack, (((1,), (1,)), ((), ())), preferred_element_type=F32)
        att = (att * dcat_ref[d]).astype(BF16)
        vstack = _head_stack(v, lane_head).astype(BF16)
        o = jnp.dot(att, vstack, preferred_element_type=F32)
        st = s_ref[...]
        o = o + jnp.dot(qb, st.astype(BF16), preferred_element_type=F32) * qdec_ref[d]
        kd = (k * kdec_ref[d]).astype(BF16)
        upd = lax.dot_general(kd, v.astype(BF16), (((0,), (0,)), ((), ())), preferred_element_type=F32)
        s_ref[...] = st * sdec_ref[d] + jnp.where(same_head, upd, 0.0)
        return o

    def finish_chunk(o, r0):
        gmat = gmat_ref[...]
        o_hi = o.astype(BF16)
        o_lo = (o - o_hi.astype(F32)).astype(BF16)
        mu = (jnp.dot(o_hi, gmat, preferred_element_type=F32)
              + jnp.dot(o_lo, gmat, preferred_element_type=F32))
        cen = o - mu
        var = jnp.dot((cen * cen).astype(BF16), gmat, preferred_element_type=F32)
        on = cen * lax.rsqrt(var + EPS)
        g = proj_ref[pl.ds(r0, CHUNK), 5 * GROUP_W:6 * GROUP_W]
        cat_ref[pl.ds(r0, CHUNK), 3 * GROUP_W:4 * GROUP_W] = (_silu(g) * on).astype(BF16)

    for s in range(n_seq):
        for d in range(2):
            s_ref[...] = jnp.zeros((GROUP_W, GROUP_W), F32)
            if rotate:
                for h in range(HEADS):
                    s_ref[h * HEAD_D:(h + 1) * HEAD_D, h * HEAD_D:(h + 1) * HEAD_D] = s0_ref[d, h]

            def scan_body(c, _, s=s, d=d):
                cidx = c if d == 0 else (n_chunk - 1 - c)
                r0 = pl.multiple_of(s * seq_len + cidx * CHUNK, CHUNK)
                o = ret_chunk(d, r0)
                if d == 0:
                    of_ref[pl.ds(r0, CHUNK), :] = o
                else:
                    finish_chunk(o + of_ref[pl.ds(r0, CHUNK), :], r0)
                return 0
            lax.fori_loop(0, n_chunk, scan_body, 0)

            if not rotate:
                for h in range(HEADS):
                    st_ref[s, d, h] = s_ref[h * HEAD_D:(h + 1) * HEAD_D, h * HEAD_D:(h + 1) * HEAD_D]

    def out_body(i, _):
        r0 = pl.multiple_of(i * ROW_TILE, ROW_TILE)
        y = jnp.dot(cat_ref[pl.ds(r0, ROW_TILE), :], w_out_ref[...], preferred_element_type=F32)
        y_ref[pl.ds(r0, ROW_TILE), :] = x_ref[pl.ds(r0, ROW_TILE), :] + ga1 * y
        return 0
    lax.fori_loop(0, TB // ROW_TILE, out_body, 0)


def _mixer(x, mod, lw, *, seq_len, rotate, mod_base, mod_stride, rope_tabs=None, s0=None):
    nb = x.shape[0]
    n_seq = TB // seq_len
    in_specs = [
        pl.BlockSpec((None, TB, D_MODEL), lambda i: (i, 0, 0)),
        pl.BlockSpec((None, 1, 6 * D_MODEL), lambda i: (mod_base + mod_stride * i, 0, 0)),
        _const_spec((1, D_MODEL)),
        _const_spec((D_MODEL, IN_COLS)),
        _const_spec((D_MODEL, D_MODEL)),
        _const_spec((CONV_W + 1, GROUP_W)),
        _const_spec((1, GROUP_W)), _const_spec((1, GROUP_W)), _const_spec((1, GROUP_W)),
        _const_spec((GROUP_W, GROUP_W)),
        _const_spec((CHUNK, HEADS * CHUNK)),
        _const_spec((CHUNK, GROUP_W)),
        _const_spec((GROUP_W, GROUP_W)),
        _const_spec((1, GROUP_W)),
        pl.BlockSpec(memory_space=pltpu.SMEM),
    ]
    args = [x, mod, lw["g1"], lw["w_in"], lw["w_out"], lw["dw"], lw["cb"], lw["lng"], lw["lnb"],
            lw["pw"], lw["wcat"], lw["gbias"], lw["pbd"], lw["pscale"], lw["lg"]]
    out_shape = [jax.ShapeDtypeStruct((nb, TB, D_MODEL), F32)]
    out_specs = [pl.BlockSpec((None, TB, D_MODEL), lambda i: (i, 0, 0))]
    if rotate:
        in_specs += [_const_spec((TB, GROUP_W)), _const_spec((TB, GROUP_W)),
                     pl.BlockSpec((None, 2, HEADS, HEAD_D, HEAD_D), lambda i: (i, 0, 0, 0, 0))]
        args += [rope_tabs[0], rope_tabs[1], s0]
    else:
        out_shape.append(jax.ShapeDtypeStruct((nb * n_seq, 2, HEADS, HEAD_D, HEAD_D), F32))
        out_specs.append(pl.BlockSpec((n_seq, 2, HEADS, HEAD_D, HEAD_D), lambda i: (i, 0, 0, 0, 0)))
    scratch = [
        pltpu.VMEM((TB, D_MODEL), BF16),
        pltpu.VMEM((TB, 6 * GROUP_W), F32),
        pltpu.VMEM((TB, D_MODEL), BF16),
        pltpu.VMEM((n_seq, seq_len + 2 * PAD, GROUP_W), F32),
        pltpu.VMEM((TB, GROUP_W), F32),
        pltpu.VMEM((TB, GROUP_W), F32),
        pltpu.VMEM((2, CHUNK, HEADS * CHUNK), F32),
        pltpu.VMEM((2, CHUNK, GROUP_W), F32),
        pltpu.VMEM((2, CHUNK, GROUP_W), F32),
        pltpu.VMEM((2, 1, GROUP_W), F32),
        pltpu.VMEM((GROUP_W, GROUP_W), F32),
        pltpu.VMEM((GROUP_W, GROUP_W), BF16),
    ]
    outs = pl.pallas_call(
        functools.partial(_mixer_kernel, seq_len=seq_len, rotate=rotate),
        grid=(nb,),
        in_specs=in_specs,
        out_specs=out_specs,
        out_shape=out_shape,
        scratch_shapes=scratch,
        compiler_params=pltpu.CompilerParams(
            dimension_semantics=("arbitrary",), vmem_limit_bytes=VMEM_LIMIT_BYTES),
        name="mixer_lat" if rotate else "mixer_ctx",
    )(*args)
    return outs


def _ffn_kernel(x_ref, mod_ref, g2_ref, wgu_ref, wo_ref, gf_ref, y_ref, hbf_ref, acc_ref, *, final_norm):
    sh2 = mod_ref[:, 3 * D_MODEL:4 * D_MODEL]
    sc2 = mod_ref[:, 4 * D_MODEL:5 * D_MODEL]
    ga2 = mod_ref[:, 5 * D_MODEL:6 * D_MODEL]

    def norm_body(i, _):
        r0 = pl.multiple_of(i * ROW_TILE, ROW_TILE)
        h = _norm_mod(x_ref[pl.ds(r0, ROW_TILE), :], g2_ref[...], sc2, sh2)
        hbf_ref[pl.ds(r0, ROW_TILE), :] = h.astype(BF16)
        return 0
    lax.fori_loop(0, TB // ROW_TILE, norm_body, 0)

    def tile_body(j, _):
        gu = jnp.dot(hbf_ref[...], wgu_ref[j], preferred_element_type=F32)
        act = (_silu(gu[:, 0:FF_TILE]) * gu[:, FF_TILE:2 * FF_TILE]).astype(BF16)
        part = jnp.dot(act, wo_ref[j], preferred_element_type=F32)

        @pl.when(j == 0)
        def _():
            acc_ref[...] = part

        @pl.when(j > 0)
        def _():
            acc_ref[...] += part
        return 0
    lax.fori_loop(0, N_FF_TILES, tile_body, 0)

    def out_body(i, _):
        r0 = pl.multiple_of(i * ROW_TILE, ROW_TILE)
        y = x_ref[pl.ds(r0, ROW_TILE), :] + ga2 * acc_ref[pl.ds(r0, ROW_TILE), :]
        if final_norm:
            ms = jnp.mean(y * y, axis=-1, keepdims=True)
            y = y * lax.rsqrt(ms + EPS) * gf_ref[...]
        y_ref[pl.ds(r0, ROW_TILE), :] = y
        return 0
    lax.fori_loop(0, TB // ROW_TILE, out_body, 0)


def _ffn(x, mod, lw, g_final, *, mod_base, mod_stride, final_norm):
    nb = x.shape[0]
    return pl.pallas_call(
        functools.partial(_ffn_kernel, final_norm=final_norm),
        grid=(nb,),
        in_specs=[
            pl.BlockSpec((None, TB, D_MODEL), lambda i: (i, 0, 0)),
            pl.BlockSpec((None, 1, 6 * D_MODEL), lambda i: (mod_base + mod_stride * i, 0, 0)),
            _const_spec((1, D_MODEL)),
            _const_spec((N_FF_TILES, D_MODEL, 2 * FF_TILE)),
            _const_spec((N_FF_TILES, FF_TILE, D_MODEL)),
            _const_spec((1, D_MODEL)),
        ],
        out_specs=pl.BlockSpec((None, TB, D_MODEL), lambda i: (i, 0, 0)),
        out_shape=jax.ShapeDtypeStruct((nb, TB, D_MODEL), F32),
        scratch_shapes=[pltpu.VMEM((TB, D_MODEL), BF16), pltpu.VMEM((TB, D_MODEL), F32)],
        compiler_params=pltpu.CompilerParams(
            dimension_semantics=("arbitrary",), vmem_limit_bytes=VMEM_LIMIT_BYTES),
        name="ffn",
    )(x, mod, lw["g2"], lw["wgu"], lw["wo"], g_final)


def _rope_tables(seq_len):
    t = jnp.arange(seq_len)
    r = (t // GRID_W).astype(F32)
    c = (t % GRID_W).astype(F32)
    nf = HEAD_D // 4
    inv = ROPE_BASE ** (-jnp.arange(nf, dtype=F32) / nf)
    ang_r = r[:, None] * inv
    ang_c = c[:, None] * inv
    cos = jnp.concatenate([jnp.cos(ang_r), jnp.cos(ang_r), jnp.cos(ang_c), jnp.cos(ang_c)], axis=-1)
    sin = jnp.concatenate([-jnp.sin(ang_r), jnp.sin(ang_r), -jnp.sin(ang_c), jnp.sin(ang_c)], axis=-1)
    return jnp.tile(cos, (1, HEADS)), jnp.tile(sin, (1, HEADS))


def _layer_weights(l, g_norm1, g_norm2, w_in, w_out, conv_dw, conv_b, conv_ln_g, conv_ln_b, conv_pw,
                   gmlp_ws, gmlp_b, pool_w, pool_scale, ret_decay, w_ffn_in, w_ffn_out):
    wg = w_ffn_in[l][:, :D_FF].reshape(D_MODEL, N_FF_TILES, FF_TILE)
    wu = w_ffn_in[l][:, D_FF:].reshape(D_MODEL, N_FF_TILES, FF_TILE)
    wgu = jnp.transpose(jnp.concatenate([wg, wu], axis=-1), (1, 0, 2)).astype(BF16)
    return {
        "g1": g_norm1[l].reshape(1, D_MODEL),
        "g2": g_norm2[l].reshape(1, D_MODEL),
        "w_in": w_in[l].astype(BF16),
        "w_out": w_out[l].astype(BF16),
        "dw": jnp.concatenate([conv_dw[l], jnp.zeros((1, GROUP_W), F32)], axis=0),
        "cb": conv_b[l].reshape(1, GROUP_W),
        "lng": conv_ln_g[l].reshape(1, GROUP_W),
        "lnb": conv_ln_b[l].reshape(1, GROUP_W),
        "pw": conv_pw[l].astype(BF16),
        "wcat": jnp.transpose(gmlp_ws[l], (1, 0, 2)).reshape(CHUNK, HEADS * CHUNK).astype(BF16),
        "gbias": jnp.repeat(gmlp_b[l].T, HEAD_D, axis=1),
        "pbd": jax.scipy.linalg.block_diag(*[pool_w[l, g] for g in range(len(POOL_WINDOWS))]).astype(BF16),
        "pscale": pool_scale[l].reshape(1, GROUP_W),
        "lg": jax.nn.log_sigmoid(ret_decay[l].astype(F32)).reshape(2 * HEADS),
        "wgu": wgu,
        "wo": w_ffn_out[l].reshape(N_FF_TILES, FF_TILE, D_MODEL).astype(BF16),
    }


def kernel(x_prompt, x_sample, state_ret, c, c_ctx, w_ada, b_ada, g_norm1, g_norm2, w_in, w_out, conv_dw,
           conv_b, conv_ln_g, conv_ln_b, conv_pw, gmlp_ws, gmlp_b, pool_w, pool_scale, ret_decay, w_ffn_in,
           w_ffn_out, g_final):
    batch, seq, _ = x_prompt.shape
    dec_batch, dec_seq, _ = x_sample.shape
    assert dec_seq == TB and TB % seq == 0 and (batch * seq) % TB == 0
    assert 1 + dec_batch <= MOD_ROWS

    cs = jnp.concatenate([c_ctx[None, :], c, jnp.zeros((MOD_ROWS - 1 - dec_batch, D_MODEL), F32)], axis=0)
    mods = _ada_rows(cs, w_ada, b_ada).reshape(DEPTH, MOD_ROWS, 1, 6 * D_MODEL)
    rope_tabs = _rope_tables(dec_seq)
    g_final2 = g_final.reshape(1, D_MODEL)

    xc = x_prompt.reshape(batch * seq // TB, TB, D_MODEL)
    xl = x_sample
    states = []
    for l in range(DEPTH):
        lw = _layer_weights(l, g_norm1, g_norm2, w_in, w_out, conv_dw, conv_b, conv_ln_g, conv_ln_b,
                            conv_pw, gmlp_ws, gmlp_b, pool_w, pool_scale, ret_decay, w_ffn_in, w_ffn_out)
        last = l == DEPTH - 1
        xc, st = _mixer(xc, mods[l], lw, seq_len=seq, rotate=False, mod_base=0, mod_stride=0)
        states.append(st)
        xc = _ffn(xc, mods[l], lw, g_final2, mod_base=0, mod_stride=0, final_norm=last)
        (xl,) = _mixer(xl, mods[l], lw, seq_len=dec_seq, rotate=True, mod_base=1, mod_stride=1,
                       rope_tabs=rope_tabs, s0=state_ret[:, l].astype(F32))
        xl = _ffn(xl, mods[l], lw, g_final2, mod_base=1, mod_stride=1, final_norm=last)

    y_prompt = xc.reshape(batch, seq, D_MODEL)
    new_state = jnp.stack(states, axis=1).astype(x_prompt.dtype)
    return (y_prompt, xl, new_state)
```

```python
import functools

import jax
import jax.numpy as jnp
from jax import lax
from jax.experimental import pallas as pl
from jax.experimental.pallas import tpu as pltpu

F32 = jnp.float32
BF16 = jnp.bfloat16

D_MODEL = 1024
DEPTH = 2
GRID_W = 64
GROUP_W = D_MODEL // 4
CONV_W = 31
CHUNK = 128
HEADS = 4
HEAD_D = GROUP_W // HEADS
POOL_WINDOWS = (2, 4, 8, 16)
ROPE_BASE = 10000.0
D_FF = 2816
IN_COLS = 11 * GROUP_W
EPS = 1e-6

TB = 1024
ROW_TILE = 256
CONV_TILE = 64
PAD = 16
FF_TILE = 256
N_FF_TILES = D_FF // FF_TILE
ADA_TILE = 1536
MOD_ROWS = 8
VMEM_LIMIT_BYTES = 60 * 1024 * 1024


def _sigmoid(x):
    return 1.0 / (1.0 + jnp.exp(-x))


def _silu(x):
    return x * _sigmoid(x)


def _norm_mod(x, g, scale, shift):
    ms = jnp.mean(x * x, axis=-1, keepdims=True)
    return (x * lax.rsqrt(ms + EPS) * g) * (1.0 + scale) + shift


def _head_stack(x, lane_head):
    return jnp.concatenate([jnp.where(lane_head == h, x, 0.0) for h in range(HEADS)], axis=0)


def _const_spec(shape):
    zeros = (0,) * len(shape)
    return pl.BlockSpec(shape, lambda i: zeros, pipeline_mode=pl.Buffered(1))


def _ada_kernel(c_ref, w_ref, b_ref, o_ref):
    a = _silu(c_ref[...]).astype(BF16)
    o_ref[...] = jnp.dot(a, w_ref[...].astype(BF16), preferred_element_type=F32) + b_ref[...]


def _ada_rows(cs, w_ada, b_ada):
    n_tiles = 6 * D_MODEL // ADA_TILE
    return pl.pallas_call(
        _ada_kernel,
        grid=(DEPTH, n_tiles),
        in_specs=[
            pl.BlockSpec((MOD_ROWS, D_MODEL), lambda l, j: (0, 0)),
            pl.BlockSpec((None, D_MODEL, ADA_TILE), lambda l, j: (l, 0, j)),
            pl.BlockSpec((None, 1, ADA_TILE), lambda l, j: (l, 0, j)),
        ],
        out_specs=pl.BlockSpec((None, MOD_ROWS, ADA_TILE), lambda l, j: (l, 0, j)),
        out_shape=jax.ShapeDtypeStruct((DEPTH, MOD_ROWS, 6 * D_MODEL), F32),
        compiler_params=pltpu.CompilerParams(
            dimension_semantics=("arbitrary", "arbitrary"), vmem_limit_bytes=VMEM_LIMIT_BYTES),
        name="ada_rows",
    )(cs, w_ada, b_ada.reshape(DEPTH, 1, 6 * D_MODEL))


def _mixer_kernel(*refs, seq_len, rotate):
    n_seq = TB // seq_len
    n_chunk = seq_len // CHUNK
    it = iter(refs)
    x_ref, mod_ref, g1_ref, w_in_ref, w_out_ref = (next(it) for _ in range(5))
    dw_ref, cb_ref, lng_ref, lnb_ref, pw_ref = (next(it) for _ in range(5))
    wcat_ref, gbias_ref, pbd_ref, pscale_ref, lg_ref = (next(it) for _ in range(5))
    if rotate:
        cos_ref, sin_ref, s0_ref = (next(it) for _ in range(3))
    y_ref = next(it)
    if not rotate:
        st_ref = next(it)
    (hbf_ref, proj_ref, cat_ref, pad_ref, tmp_ref, of_ref,
     dcat_ref, qdec_ref, kdec_ref, sdec_ref, s_ref, gmat_ref, qb_ref, upd_ref, sall_ref) = it

    sh1 = mod_ref[:, 0:D_MODEL]
    sc1 = mod_ref[:, D_MODEL:2 * D_MODEL]
    ga1 = mod_ref[:, 2 * D_MODEL:3 * D_MODEL]

    lane = lax.broadcasted_iota(jnp.int32, (1, GROUP_W), 1)
    lane_head = lane // HEAD_D

    def norm_body(i, _):
        r0 = pl.multiple_of(i * ROW_TILE, ROW_TILE)
        h = _norm_mod(x_ref[pl.ds(r0, ROW_TILE), :], g1_ref[...], sc1, sh1)
        hbf_ref[pl.ds(r0, ROW_TILE), :] = h.astype(BF16)
        return 0
    lax.fori_loop(0, TB // ROW_TILE, norm_body, 0)

    def project(col0, ncols):
        def body(i, _):
            r0 = pl.multiple_of(i * ROW_TILE, ROW_TILE)
            proj_ref[pl.ds(r0, ROW_TILE), 0:ncols] = jnp.dot(
                hbf_ref[pl.ds(r0, ROW_TILE), :], w_in_ref[:, col0:col0 + ncols],
                preferred_element_type=F32)
            return 0
        lax.fori_loop(0, TB // ROW_TILE, body, 0)

    for s in range(n_seq):
        pad_ref[s, 0:PAD, :] = jnp.zeros((PAD, GROUP_W), F32)
        pad_ref[s, PAD + seq_len:PAD + seq_len + PAD, :] = jnp.zeros((PAD, GROUP_W), F32)

    tiles_per_seq = seq_len // CONV_TILE
    chunks_per_seq = seq_len // CHUNK

    def seq_and_offset(i, per_seq, size):
        if n_seq == 1:
            return 0, pl.multiple_of(i * size, size)
        return i // per_seq, pl.multiple_of((i % per_seq) * size, size)

    project(0, 2 * GROUP_W)

    def glu_body(i, _):
        s, t0 = seq_and_offset(i, chunks_per_seq, CHUNK)
        r0 = pl.multiple_of(i * CHUNK, CHUNK)
        a1 = proj_ref[pl.ds(r0, CHUNK), 0:GROUP_W]
        a2 = proj_ref[pl.ds(r0, CHUNK), GROUP_W:2 * GROUP_W]
        pad_ref[s, pl.ds(pl.multiple_of(t0 + PAD, 8), CHUNK), :] = a1 * _sigmoid(a2)
        return 0
    lax.fori_loop(0, TB // CHUNK, glu_body, 0)

    win = CONV_TILE + 2 * PAD

    def conv_body(i, _):
        s, t0 = seq_and_offset(i, tiles_per_seq, CONV_TILE)
        w = pad_ref[s, pl.ds(t0, win), :]
        acc = jnp.zeros((CONV_TILE, GROUP_W), F32) + cb_ref[...]
        for b in range(8):
            wb = w if b == 0 else pltpu.roll(w, win - b, axis=0)
            for a in range(4):
                k = 8 * a + b - 1
                if 0 <= k < CONV_W:
                    acc = acc + dw_ref[k:k + 1, :] * wb[8 * a:8 * a + CONV_TILE, :]
        tmp_ref[pl.ds(pl.multiple_of(i * CONV_TILE, CONV_TILE), CONV_TILE), :] = acc
        return 0
    lax.fori_loop(0, TB // CONV_TILE, conv_body, 0)

    def ln_pw_body(i, _):
        r0 = pl.multiple_of(i * ROW_TILE, ROW_TILE)
        c = tmp_ref[pl.ds(r0, ROW_TILE), :]
        mu = jnp.mean(c, axis=-1, keepdims=True)
        cen = c - mu
        var = jnp.mean(cen * cen, axis=-1, keepdims=True)
        hn = cen * lax.rsqrt(var + EPS) * lng_ref[...] + lnb_ref[...]
        ya = jnp.dot(_silu(hn).astype(BF16), pw_ref[...], preferred_element_type=F32)
        cat_ref[pl.ds(r0, ROW_TILE), 0:GROUP_W] = ya.astype(BF16)
        return 0
    lax.fori_loop(0, TB // ROW_TILE, ln_pw_body, 0)

    project(2 * GROUP_W, 2 * GROUP_W)

    def gmlp_body(i, _):
        r0 = pl.multiple_of(i * CHUNK, CHUNK)
        u = proj_ref[pl.ds(r0, CHUNK), 0:GROUP_W]
        v = proj_ref[pl.ds(r0, CHUNK), GROUP_W:2 * GROUP_W]
        vstack = _head_stack(v, lane_head).astype(BF16)
        sg = jnp.dot(wcat_ref[...], vstack, preferred_element_type=F32) + gbias_ref[...]
        cat_ref[pl.ds(r0, CHUNK), GROUP_W:2 * GROUP_W] = (u * sg).astype(BF16)
        return 0
    lax.fori_loop(0, TB // CHUNK, gmlp_body, 0)

    project(4 * GROUP_W, GROUP_W)

    def pool_fill_body(i, _):
        s, t0 = seq_and_offset(i, chunks_per_seq, CHUNK)
        r0 = pl.multiple_of(i * CHUNK, CHUNK)
        pad_ref[s, pl.ds(pl.multiple_of(t0 + PAD, 8), CHUNK), :] = proj_ref[pl.ds(r0, CHUNK), 0:GROUP_W]
        return 0
    lax.fori_loop(0, TB // CHUNK, pool_fill_body, 0)

    pool_group = lane // (GROUP_W // len(POOL_WINDOWS))
    half_w = jnp.where(pool_group == 0, POOL_WINDOWS[0] // 2,
                       jnp.where(pool_group == 1, POOL_WINDOWS[1] // 2,
                                 jnp.where(pool_group == 2, POOL_WINDOWS[2] // 2, POOL_WINDOWS[3] // 2)))
    pwin = CONV_TILE + 16

    def pool_body(i, _):
        s, t0 = seq_and_offset(i, tiles_per_seq, CONV_TILE)
        w = pad_ref[s, pl.ds(pl.multiple_of(t0 + PAD - 8, 8), pwin), :]
        lo, hi = [], []
        for b in range(8):
            wb = w if b == 0 else pltpu.roll(w, pwin - b, axis=0)
            lo.append(wb[0:CONV_TILE, :])
            hi.append(wb[8:8 + CONV_TILE, :])
        s2 = lo[7] + hi[0]
        s4 = s2 + lo[6] + hi[1]
        s8 = s4 + lo[5] + lo[4] + hi[2] + hi[3]
        s16 = s8 + lo[3] + lo[2] + lo[1] + lo[0] + hi[4] + hi[5] + hi[6] + hi[7]
        ssum = jnp.where(pool_group == 0, s2, jnp.where(pool_group == 1, s4,
                                                       jnp.where(pool_group == 2, s8, s16)))
        tpos = t0 + lax.broadcasted_iota(jnp.int32, (CONV_TILE, GROUP_W), 0)
        cnt = jnp.minimum(tpos + half_w, seq_len) - jnp.maximum(tpos - half_w, 0)
        dlt = ssum / cnt.astype(F32) - hi[0]
        tmp_ref[pl.ds(pl.multiple_of(i * CONV_TILE, CONV_TILE), CONV_TILE), :] = dlt
        return 0
    lax.fori_loop(0, TB // CONV_TILE, pool_body, 0)

    def pool_mix_body(i, _):
        r0 = pl.multiple_of(i * ROW_TILE, ROW_TILE)
        yc = jnp.dot(tmp_ref[pl.ds(r0, ROW_TILE), :].astype(BF16), pbd_ref[...],
                     preferred_element_type=F32) * pscale_ref[...]
        cat_ref[pl.ds(r0, ROW_TILE), 2 * GROUP_W:3 * GROUP_W] = yc.astype(BF16)
        return 0
    lax.fori_loop(0, TB // ROW_TILE, pool_mix_body, 0)

    project(5 * GROUP_W, 6 * GROUP_W)

    ri = lax.broadcasted_iota(jnp.int32, (CHUNK, HEADS * CHUNK), 0)
    ci = lax.broadcasted_iota(jnp.int32, (CHUNK, HEADS * CHUNK), 1)
    cj = ci % CHUNK
    chead = ci // CHUNK
    rq = lax.broadcasted_iota(jnp.int32, (CHUNK, GROUP_W), 0).astype(F32)
    for d in range(2):
        lgs = [lg_ref[d * HEADS + h] for h in range(HEADS)]
        lg_wide = jnp.where(chead == 0, lgs[0], jnp.where(chead == 1, lgs[1],
                                                          jnp.where(chead == 2, lgs[2], lgs[3])))
        lg_lane = jnp.where(lane_head == 0, lgs[0], jnp.where(lane_head == 1, lgs[1],
                                                              jnp.where(lane_head == 2, lgs[2], lgs[3])))
        dist = (ri - cj) if d == 0 else (cj - ri)
        keep = dist >= 0
        dcat_ref[d] = jnp.where(keep, jnp.exp(jnp.where(keep, dist, 0).astype(F32) * lg_wide), 0.0)
        if d == 0:
            qdec_ref[d] = jnp.exp((rq + 1.0) * lg_lane)
            kdec_ref[d] = jnp.exp((CHUNK - 1.0 - rq) * lg_lane)
        else:
            qdec_ref[d] = jnp.exp((CHUNK - rq) * lg_lane)
            kdec_ref[d] = jnp.exp(rq * lg_lane)
        sdec_ref[d] = jnp.exp(float(CHUNK) * lg_lane)

    rr = lax.broadcasted_iota(jnp.int32, (GROUP_W, GROUP_W), 0) // HEAD_D
    cc = lax.broadcasted_iota(jnp.int32, (GROUP_W, GROUP_W), 1) // HEAD_D
    same_head = rr == cc
    gmat_ref[...] = jnp.where(same_head, 1.0 / HEAD_D, 0.0).astype(BF16)

    lane_bit = (lane & 16) == 0
    k_scale = HEAD_D ** -0.5

    def rope(z, r0):
        if not rotate:
            return z
        cos = cos_ref[pl.ds(r0, CHUNK), :]
        sin = sin_ref[pl.ds(r0, CHUNK), :]
        halves = []
        for c0 in (0, 128):
            zz = z[:, c0:c0 + 128]
            partner = jnp.where(lane_bit[:, c0:c0 + 128],
                                pltpu.roll(zz, 128 - 16, axis=1), pltpu.roll(zz, 16, axis=1))
            halves.append(partner)
        return z * cos + jnp.concatenate(halves, axis=1) * sin

    def intra_body(c, _):
        r0 = pl.multiple_of(c * CHUNK, CHUNK)
        v = proj_ref[pl.ds(r0, CHUNK), 4 * GROUP_W:5 * GROUP_W]
        vb = v.astype(BF16)
        vstack = _head_stack(v, lane_head).astype(BF16)
        o = None
        for d in range(2):
            qc0 = 2 * d * GROUP_W
            q = rope(proj_ref[pl.ds(r0, CHUNK), qc0:qc0 + GROUP_W], r0)
            k = rope(proj_ref[pl.ds(r0, CHUNK), qc0 + GROUP_W:qc0 + 2 * GROUP_W], r0) * k_scale
            qb = q.astype(BF16)
            qb_ref[d, pl.ds(r0, CHUNK), :] = qb
            kstack = _head_stack(k, lane_head).astype(BF16)
            att = lax.dot_general(qb, kstack, (((1,), (1,)), ((), ())), preferred_element_type=F32)
            att = (att * dcat_ref[d]).astype(BF16)
            od = jnp.dot(att, vstack, preferred_element_type=F32)
            o = od if o is None else o + od
            kd = (k * kdec_ref[d]).astype(BF16)
            upd_ref[d, c] = lax.dot_general(kd, vb, (((0,), (0,)), ((), ())), preferred_element_type=F32)
        of_ref[pl.ds(r0, CHUNK), :] = o
        return 0
    lax.fori_loop(0, TB // CHUNK, intra_body, 0, unroll=2)

    for s in range(n_seq):
        for d in range(2):
            if rotate:
                s_ref[...] = jnp.zeros((GROUP_W, GROUP_W), F32)
                for h in range(HEADS):
                    s_ref[h * HEAD_D:(h + 1) * HEAD_D, h * HEAD_D:(h + 1) * HEAD_D] = s0_ref[d, h]
            order = range(n_chunk) if d == 0 else range(n_chunk - 1, -1, -1)
            for h in range(HEADS):
                rows = slice(h * HEAD_D, (h + 1) * HEAD_D)
                st = s_ref[rows, :] if rotate else jnp.zeros((HEAD_D, GROUP_W), F32)
                for c in order:
                    cg = s * n_chunk + c
                    sall_ref[d, cg, rows, :] = jnp.where(lane_head == h, st, 0.0).astype(BF16)
                    st = st * sdec_ref[d] + upd_ref[d, cg, rows, :]
                if not rotate:
                    st_ref[s, d, h] = st[:, h * HEAD_D:(h + 1) * HEAD_D]

    def finish_body(c, _):
        r0 = pl.multiple_of(c * CHUNK, CHUNK)
        o = of_ref[pl.ds(r0, CHUNK), :]
        for d in range(2):
            o = o + jnp.dot(qb_ref[d, pl.ds(r0, CHUNK), :], sall_ref[d, c],
                            preferred_element_type=F32) * qdec_ref[d]
        gmat = gmat_ref[...]
        o_hi = o.astype(BF16)
        o_lo = (o - o_hi.astype(F32)).astype(BF16)
        mu = (jnp.dot(o_hi, gmat, preferred_element_type=F32)
              + jnp.dot(o_lo, gmat, preferred_element_type=F32))
        cen = o - mu
        var = jnp.dot((cen * cen).astype(BF16), gmat, preferred_element_type=F32)
        on = cen * lax.rsqrt(var + EPS)
        g = proj_ref[pl.ds(r0, CHUNK), 5 * GROUP_W:6 * GROUP_W]
        cat_ref[pl.ds(r0, CHUNK), 3 * GROUP_W:4 * GROUP_W] = (_silu(g) * on).astype(BF16)
        return 0
    lax.fori_loop(0, TB // CHUNK, finish_body, 0, unroll=2)

    def out_body(i, _):
        r0 = pl.multiple_of(i * ROW_TILE, ROW_TILE)
        y = jnp.dot(cat_ref[pl.ds(r0, ROW_TILE), :], w_out_ref[...], preferred_element_type=F32)
        y_ref[pl.ds(r0, ROW_TILE), :] = x_ref[pl.ds(r0, ROW_TILE), :] + ga1 * y
        return 0
    lax.fori_loop(0, TB // ROW_TILE, out_body, 0)


def _mixer(x, mod, lw, *, seq_len, rotate, mod_base, mod_stride, rope_tabs=None, s0=None):
    nb = x.shape[0]
    n_seq = TB // seq_len
    in_specs = [
        pl.BlockSpec((None, TB, D_MODEL), lambda i: (i, 0, 0)),
        pl.BlockSpec((None, 1, 6 * D_MODEL), lambda i: (mod_base + mod_stride * i, 0, 0)),
        _const_spec((1, D_MODEL)),
        _const_spec((D_MODEL, IN_COLS)),
        _const_spec((D_MODEL, D_MODEL)),
        _const_spec((CONV_W + 1, GROUP_W)),
        _const_spec((1, GROUP_W)), _const_spec((1, GROUP_W)), _const_spec((1, GROUP_W)),
        _const_spec((GROUP_W, GROUP_W)),
        _const_spec((CHUNK, HEADS * CHUNK)),
        _const_spec((CHUNK, GROUP_W)),
        _const_spec((GROUP_W, GROUP_W)),
        _const_spec((1, GROUP_W)),
        pl.BlockSpec(memory_space=pltpu.SMEM),
    ]
    args = [x, mod, lw["g1"], lw["w_in"], lw["w_out"], lw["dw"], lw["cb"], lw["lng"], lw["lnb"],
            lw["pw"], lw["wcat"], lw["gbias"], lw["pbd"], lw["pscale"], lw["lg"]]
    out_shape = [jax.ShapeDtypeStruct((nb, TB, D_MODEL), F32)]
    out_specs = [pl.BlockSpec((None, TB, D_MODEL), lambda i: (i, 0, 0))]
    if rotate:
        in_specs += [_const_spec((TB, GROUP_W)), _const_spec((TB, GROUP_W)),
                     pl.BlockSpec((None, 2, HEADS, HEAD_D, HEAD_D), lambda i: (i, 0, 0, 0, 0))]
        args += [rope_tabs[0], rope_tabs[1], s0]
    else:
        out_shape.append(jax.ShapeDtypeStruct((nb * n_seq, 2, HEADS, HEAD_D, HEAD_D), F32))
        out_specs.append(pl.BlockSpec((n_seq, 2, HEADS, HEAD_D, HEAD_D), lambda i: (i, 0, 0, 0, 0)))
    scratch = [
        pltpu.VMEM((TB, D_MODEL), BF16),
        pltpu.VMEM((TB, 6 * GROUP_W), F32),
        pltpu.VMEM((TB, D_MODEL), BF16),
        pltpu.VMEM((n_seq, seq_len + 2 * PAD, GROUP_W), F32),
        pltpu.VMEM((TB, GROUP_W), F32),
        pltpu.VMEM((TB, GROUP_W), F32),
        pltpu.VMEM((2, CHUNK, HEADS * CHUNK), F32),
        pltpu.VMEM((2, CHUNK, GROUP_W), F32),
        pltpu.VMEM((2, CHUNK, GROUP_W), F32),
        pltpu.VMEM((2, 1, GROUP_W), F32),
        pltpu.VMEM((GROUP_W, GROUP_W), F32),
        pltpu.VMEM((GROUP_W, GROUP_W), BF16),
        pltpu.VMEM((2, TB, GROUP_W), BF16),
        pltpu.VMEM((2, TB // CHUNK, GROUP_W, GROUP_W), F32),
        pltpu.VMEM((2, TB // CHUNK, GROUP_W, GROUP_W), BF16),
    ]
    outs = pl.pallas_call(
        functools.partial(_mixer_kernel, seq_len=seq_len, rotate=rotate),
        grid=(nb,),
        in_specs=in_specs,
        out_specs=out_specs,
        out_shape=out_shape,
        scratch_shapes=scratch,
        compiler_params=pltpu.CompilerParams(
            dimension_semantics=("arbitrary",), vmem_limit_bytes=VMEM_LIMIT_BYTES),
        name="mixer_lat" if rotate else "mixer_ctx",
    )(*args)
    return outs


def _ffn_kernel(x_ref, mod_ref, g2_ref, wgu_ref, wo_ref, gf_ref, y_ref, hbf_ref, act_ref, *, final_norm):
    sh2 = mod_ref[:, 3 * D_MODEL:4 * D_MODEL]
    sc2 = mod_ref[:, 4 * D_MODEL:5 * D_MODEL]
    ga2 = mod_ref[:, 5 * D_MODEL:6 * D_MODEL]

    def norm_body(i, _):
        r0 = pl.multiple_of(i * ROW_TILE, ROW_TILE)
        h = _norm_mod(x_ref[pl.ds(r0, ROW_TILE), :], g2_ref[...], sc2, sh2)
        hbf_ref[pl.ds(r0, ROW_TILE), :] = h.astype(BF16)
        return 0
    lax.fori_loop(0, TB // ROW_TILE, norm_body, 0)

    for j in range(N_FF_TILES):
        gate = jnp.dot(hbf_ref[...], wgu_ref[:, j * FF_TILE:(j + 1) * FF_TILE], preferred_element_type=F32)
        up = jnp.dot(hbf_ref[...], wgu_ref[:, D_FF + j * FF_TILE:D_FF + (j + 1) * FF_TILE],
                     preferred_element_type=F32)
        act = _silu(gate) * up
        act_ref[:, j * FF_TILE:(j + 1) * FF_TILE] = act.astype(BF16)

    y_ref[...] = jnp.dot(act_ref[...], wo_ref[...], preferred_element_type=F32)

    def out_body(i, _):
        r0 = pl.multiple_of(i * ROW_TILE, ROW_TILE)
        y = x_ref[pl.ds(r0, ROW_TILE), :] + ga2 * y_ref[pl.ds(r0, ROW_TILE), :]
        if final_norm:
            ms = jnp.mean(y * y, axis=-1, keepdims=True)
            y = y * lax.rsqrt(ms + EPS) * gf_ref[...]
        y_ref[pl.ds(r0, ROW_TILE), :] = y
        return 0
    lax.fori_loop(0, TB // ROW_TILE, out_body, 0)


def _ffn(x, mod, lw, g_final, *, mod_base, mod_stride, final_norm):
    nb = x.shape[0]
    return pl.pallas_call(
        functools.partial(_ffn_kernel, final_norm=final_norm),
        grid=(nb,),
        in_specs=[
            pl.BlockSpec((None, TB, D_MODEL), lambda i: (i, 0, 0)),
            pl.BlockSpec((None, 1, 6 * D_MODEL), lambda i: (mod_base + mod_stride * i, 0, 0)),
            _const_spec((1, D_MODEL)),
            _const_spec((D_MODEL, 2 * D_FF)),
            _const_spec((D_FF, D_MODEL)),
            _const_spec((1, D_MODEL)),
        ],
        out_specs=pl.BlockSpec((None, TB, D_MODEL), lambda i: (i, 0, 0)),
        out_shape=jax.ShapeDtypeStruct((nb, TB, D_MODEL), F32),
        scratch_shapes=[pltpu.VMEM((TB, D_MODEL), BF16), pltpu.VMEM((TB, D_FF), BF16)],
        compiler_params=pltpu.CompilerParams(
            dimension_semantics=("arbitrary",), vmem_limit_bytes=VMEM_LIMIT_BYTES),
        name="ffn",
    )(x, mod, lw["g2"], lw["wgu"], lw["wo"], g_final)


def _rope_tables(seq_len):
    t = jnp.arange(seq_len)
    r = (t // GRID_W).astype(F32)
    c = (t % GRID_W).astype(F32)
    nf = HEAD_D // 4
    inv = ROPE_BASE ** (-jnp.arange(nf, dtype=F32) / nf)
    ang_r = r[:, None] * inv
    ang_c = c[:, None] * inv
    cos = jnp.concatenate([jnp.cos(ang_r), jnp.cos(ang_r), jnp.cos(ang_c), jnp.cos(ang_c)], axis=-1)
    sin = jnp.concatenate([-jnp.sin(ang_r), jnp.sin(ang_r), -jnp.sin(ang_c), jnp.sin(ang_c)], axis=-1)
    return jnp.tile(cos, (1, HEADS)), jnp.tile(sin, (1, HEADS))


def _layer_weights(l, g_norm1, g_norm2, w_in, w_out, conv_dw, conv_b, conv_ln_g, conv_ln_b, conv_pw,
                   gmlp_ws, gmlp_b, pool_w, pool_scale, ret_decay, w_ffn_in, w_ffn_out):
    return {
        "g1": g_norm1[l].reshape(1, D_MODEL),
        "g2": g_norm2[l].reshape(1, D_MODEL),
        "w_in": w_in[l].astype(BF16),
        "w_out": w_out[l].astype(BF16),
        "dw": jnp.concatenate([conv_dw[l], jnp.zeros((1, GROUP_W), F32)], axis=0),
        "cb": conv_b[l].reshape(1, GROUP_W),
        "lng": conv_ln_g[l].reshape(1, GROUP_W),
        "lnb": conv_ln_b[l].reshape(1, GROUP_W),
        "pw": conv_pw[l].astype(BF16),
        "wcat": jnp.transpose(gmlp_ws[l], (1, 0, 2)).reshape(CHUNK, HEADS * CHUNK).astype(BF16),
        "gbias": jnp.repeat(gmlp_b[l].T, HEAD_D, axis=1),
        "pbd": jax.scipy.linalg.block_diag(*[pool_w[l, g] for g in range(len(POOL_WINDOWS))]).astype(BF16),
        "pscale": pool_scale[l].reshape(1, GROUP_W),
        "lg": jax.nn.log_sigmoid(ret_decay[l].astype(F32)).reshape(2 * HEADS),
        "wgu": w_ffn_in[l].astype(BF16),
        "wo": w_ffn_out[l].astype(BF16),
    }


def kernel(x_prompt, x_sample, state_ret, c, c_ctx, w_ada, b_ada, g_norm1, g_norm2, w_in, w_out, conv_dw,
           conv_b, conv_ln_g, conv_ln_b, conv_pw, gmlp_ws, gmlp_b, pool_w, pool_scale, ret_decay, w_ffn_in,
           w_ffn_out, g_final):
    batch, seq, _ = x_prompt.shape
    dec_batch, dec_seq, _ = x_sample.shape
    assert dec_seq == TB and TB % seq == 0 and (batch * seq) % TB == 0
    assert 1 + dec_batch <= MOD_ROWS

    cs = jnp.concatenate([c_ctx[None, :], c, jnp.zeros((MOD_ROWS - 1 - dec_batch, D_MODEL), F32)], axis=0)
    mods = _ada_rows(cs, w_ada, b_ada).reshape(DEPTH, MOD_ROWS, 1, 6 * D_MODEL)
    rope_tabs = _rope_tables(dec_seq)
    g_final2 = g_final.reshape(1, D_MODEL)

    xc = x_prompt.reshape(batch * seq // TB, TB, D_MODEL)
    xl = x_sample
    states = []
    for l in range(DEPTH):
        lw = _layer_weights(l, g_norm1, g_norm2, w_in, w_out, conv_dw, conv_b, conv_ln_g, conv_ln_b,
                            conv_pw, gmlp_ws, gmlp_b, pool_w, pool_scale, ret_decay, w_ffn_in, w_ffn_out)
        last = l == DEPTH - 1
        xc, st = _mixer(xc, mods[l], lw, seq_len=seq, rotate=False, mod_base=0, mod_stride=0)
        states.append(st)
        xc = _ffn(xc, mods[l], lw, g_final2, mod_base=0, mod_stride=0, final_norm=last)
        (xl,) = _mixer(xl, mods[l], lw, seq_len=dec_seq, rotate=True, mod_base=1, mod_stride=1,
                       rope_tabs=rope_tabs, s0=state_ret[:, l].astype(F32))
        xl = _ffn(xl, mods[l], lw, g_final2, mod_base=1, mod_stride=1, final_norm=last)

    y_prompt = xc.reshape(batch, seq, D_MODEL)
    new_state = jnp.stack(states, axis=1).astype(x_prompt.dtype)
    return (y_prompt, xl, new_state)
```

```python
import functools

import jax
import jax.numpy as jnp
from jax import lax
from jax.experimental import pallas as pl
from jax.experimental.pallas import tpu as pltpu

F32 = jnp.float32
BF16 = jnp.bfloat16

D_MODEL = 1024
DEPTH = 2
GRID_W = 64
GROUP_W = D_MODEL // 4
CONV_W = 31
CHUNK = 128
HEADS = 4
HEAD_D = GROUP_W // HEADS
POOL_WINDOWS = (2, 4, 8, 16)
ROPE_BASE = 10000.0
D_FF = 2816
IN_COLS = 11 * GROUP_W
EPS = 1e-6

TB = 1024
N_CHUNKS = TB // CHUNK
ROW_TILE = 256
CONV_TILE = 64
N_CONV_TILES = TB // CONV_TILE
PAD = 16
FF_TILE = 256
N_FF_TILES = D_FF // FF_TILE
ADA_TILE = 1536
MOD_ROWS = 8
VMEM_LIMIT_BYTES = 60 * 1024 * 1024

COL_A, COL_B, COL_C, COL_D = 0, 2 * GROUP_W, 4 * GROUP_W, 5 * GROUP_W
COL_V, COL_G = COL_D + 4 * GROUP_W, COL_D + 5 * GROUP_W


def _sigmoid(x):
    return 1.0 / (1.0 + jnp.exp(-x))


def _silu(x):
    return x * _sigmoid(x)


def _norm_mod(x, g, scale, shift):
    ms = jnp.mean(x * x, axis=-1, keepdims=True)
    return (x * lax.rsqrt(ms + EPS) * g) * (1.0 + scale) + shift


def _head_stack(x, lane_head):
    return jnp.concatenate([jnp.where(lane_head == h, x, 0.0) for h in range(HEADS)], axis=0)


def _const_spec(shape):
    zeros = (0,) * len(shape)
    return pl.BlockSpec(shape, lambda i: zeros, pipeline_mode=pl.Buffered(1))


def _ada_kernel(c_ref, w_ref, b_ref, o_ref):
    a = _silu(c_ref[...]).astype(BF16)
    o_ref[...] = jnp.dot(a, w_ref[...].astype(BF16), preferred_element_type=F32) + b_ref[...]


def _ada_rows(cs, w_ada, b_ada):
    n_tiles = 6 * D_MODEL // ADA_TILE
    return pl.pallas_call(
        _ada_kernel,
        grid=(DEPTH, n_tiles),
        in_specs=[
            pl.BlockSpec((MOD_ROWS, D_MODEL), lambda l, j: (0, 0)),
            pl.BlockSpec((None, D_MODEL, ADA_TILE), lambda l, j: (l, 0, j)),
            pl.BlockSpec((None, 1, ADA_TILE), lambda l, j: (l, 0, j)),
        ],
        out_specs=pl.BlockSpec((None, MOD_ROWS, ADA_TILE), lambda l, j: (l, 0, j)),
        out_shape=jax.ShapeDtypeStruct((DEPTH, MOD_ROWS, 6 * D_MODEL), F32),
        compiler_params=pltpu.CompilerParams(
            dimension_semantics=("arbitrary", "arbitrary"), vmem_limit_bytes=VMEM_LIMIT_BYTES),
        name="ada_rows",
    )(cs, w_ada, b_ada.reshape(DEPTH, 1, 6 * D_MODEL))


def _mixer_kernel(*refs, seq_len, rotate):
    n_seq = TB // seq_len
    n_chunk = seq_len // CHUNK
    it = iter(refs)
    x_ref, mod_ref, g1_ref, w_in_ref, w_out_ref = (next(it) for _ in range(5))
    dw_ref, cb_ref, lng_ref, lnb_ref, pw_ref = (next(it) for _ in range(5))
    wcat_ref, gbias_ref, pbd_ref, pscale_ref, lg_ref = (next(it) for _ in range(5))
    if rotate:
        cos_ref, sin_ref, s0_ref = (next(it) for _ in range(3))
    y_ref = next(it)
    if not rotate:
        st_ref = next(it)
    (hbf_ref, proj_ref, cat_ref, pad_ref, tmp_ref, of_ref,
     dcat_ref, qdec_ref, kdec_ref, sdec_ref, gmat_ref, qb_ref, upd_ref, sall_ref) = it

    sh1 = mod_ref[:, 0:D_MODEL]
    sc1 = mod_ref[:, D_MODEL:2 * D_MODEL]
    ga1 = mod_ref[:, 2 * D_MODEL:3 * D_MODEL]

    lane = lax.broadcasted_iota(jnp.int32, (1, GROUP_W), 1)
    lane_head = lane // HEAD_D
    tiles_per_seq = seq_len // CONV_TILE
    chunks_per_seq = seq_len // CHUNK

    def seq_and_offset(i, per_seq, size):
        if n_seq == 1:
            return 0, pl.multiple_of(i * size, size)
        return i // per_seq, pl.multiple_of((i % per_seq) * size, size)

    def norm_body(i, _):
        r0 = pl.multiple_of(i * ROW_TILE, ROW_TILE)
        h = _norm_mod(x_ref[pl.ds(r0, ROW_TILE), :], g1_ref[...], sc1, sh1).astype(BF16)
        hbf_ref[pl.ds(r0, ROW_TILE), :] = h
        proj_ref[pl.ds(r0, ROW_TILE), COL_A:COL_B] = jnp.dot(
            h, w_in_ref[:, COL_A:COL_B], preferred_element_type=F32)
        return 0
    lax.fori_loop(0, TB // ROW_TILE, norm_body, 0, unroll=2)

    for s in range(n_seq):
        pad_ref[s, 0:PAD, :] = jnp.zeros((PAD, GROUP_W), F32)
        pad_ref[s, PAD + seq_len:PAD + seq_len + PAD, :] = jnp.zeros((PAD, GROUP_W), F32)

    def glu_body(i, _):
        s, t0 = seq_and_offset(i, chunks_per_seq, CHUNK)
        r0 = pl.multiple_of(i * CHUNK, CHUNK)
        a1 = proj_ref[pl.ds(r0, CHUNK), COL_A:COL_A + GROUP_W]
        a2 = proj_ref[pl.ds(r0, CHUNK), COL_A + GROUP_W:COL_B]
        pad_ref[s, pl.ds(pl.multiple_of(t0 + PAD, 8), CHUNK), :] = a1 * _sigmoid(a2)
        return 0
    lax.fori_loop(0, N_CHUNKS, glu_body, 0, unroll=2)

    win = CONV_TILE + 2 * PAD

    def conv_tile(i):
        s, t0 = divmod(i, tiles_per_seq)
        t0 *= CONV_TILE
        w = pad_ref[s, t0:t0 + win, :]
        acc = jnp.zeros((CONV_TILE, GROUP_W), F32) + cb_ref[...]
        for b in range(8):
            wb = w if b == 0 else pltpu.roll(w, win - b, axis=0)
            for a in range(4):
                k = 8 * a + b - 1
                if 0 <= k < CONV_W:
                    acc = acc + dw_ref[k:k + 1, :] * wb[8 * a:8 * a + CONV_TILE, :]
        tmp_ref[i * CONV_TILE:(i + 1) * CONV_TILE, :] = acc

    col_tiles = list(range(COL_B, IN_COLS, GROUP_W))
    conv_per_step = -(-N_CONV_TILES // len(col_tiles))
    conv_next = 0
    for c0 in col_tiles:
        proj_ref[:, c0:c0 + GROUP_W] = jnp.dot(
            hbf_ref[...], w_in_ref[:, c0:c0 + GROUP_W], preferred_element_type=F32)
        for _ in range(conv_per_step):
            if conv_next < N_CONV_TILES:
                conv_tile(conv_next)
                conv_next += 1

    def ln_pw_body(i, _):
        r0 = pl.multiple_of(i * ROW_TILE, ROW_TILE)
        c = tmp_ref[pl.ds(r0, ROW_TILE), :]
        mu = jnp.mean(c, axis=-1, keepdims=True)
        cen = c - mu
        var = jnp.mean(cen * cen, axis=-1, keepdims=True)
        hn = cen * lax.rsqrt(var + EPS) * lng_ref[...] + lnb_ref[...]
        ya = jnp.dot(_silu(hn).astype(BF16), pw_ref[...], preferred_element_type=F32)
        cat_ref[pl.ds(r0, ROW_TILE), 0:GROUP_W] = ya.astype(BF16)
        return 0
    lax.fori_loop(0, TB // ROW_TILE, ln_pw_body, 0, unroll=2)

    def gmlp_body(i, _):
        r0 = pl.multiple_of(i * CHUNK, CHUNK)
        u = proj_ref[pl.ds(r0, CHUNK), COL_B:COL_B + GROUP_W]
        v = proj_ref[pl.ds(r0, CHUNK), COL_B + GROUP_W:COL_C]
        vstack = _head_stack(v, lane_head).astype(BF16)
        sg = jnp.dot(wcat_ref[...], vstack, preferred_element_type=F32) + gbias_ref[...]
        cat_ref[pl.ds(r0, CHUNK), GROUP_W:2 * GROUP_W] = (u * sg).astype(BF16)
        return 0
    lax.fori_loop(0, N_CHUNKS, gmlp_body, 0, unroll=4)

    def pool_fill_body(i, _):
        s, t0 = seq_and_offset(i, chunks_per_seq, CHUNK)
        r0 = pl.multiple_of(i * CHUNK, CHUNK)
        pad_ref[s, pl.ds(pl.multiple_of(t0 + PAD, 8), CHUNK), :] = proj_ref[pl.ds(r0, CHUNK), COL_C:COL_D]
        return 0
    lax.fori_loop(0, N_CHUNKS, pool_fill_body, 0, unroll=2)

    pool_group = lane // (GROUP_W // len(POOL_WINDOWS))
    half_w = jnp.where(pool_group == 0, POOL_WINDOWS[0] // 2,
                       jnp.where(pool_group == 1, POOL_WINDOWS[1] // 2,
                                 jnp.where(pool_group == 2, POOL_WINDOWS[2] // 2, POOL_WINDOWS[3] // 2)))
    pwin = CONV_TILE + 16

    def pool_tile(i):
        s, t0 = seq_and_offset(i, tiles_per_seq, CONV_TILE)
        w = pad_ref[s, pl.ds(pl.multiple_of(t0 + PAD - 8, 8), pwin), :]
        lo, hi = [], []
        for b in range(8):
            wb = w if b == 0 else pltpu.roll(w, pwin - b, axis=0)
            lo.append(wb[0:CONV_TILE, :])
            hi.append(wb[8:8 + CONV_TILE, :])
        s2 = lo[7] + hi[0]
        s4 = s2 + lo[6] + hi[1]
        s8 = s4 + lo[5] + lo[4] + hi[2] + hi[3]
        s16 = s8 + lo[3] + lo[2] + lo[1] + lo[0] + hi[4] + hi[5] + hi[6] + hi[7]
        ssum = jnp.where(pool_group == 0, s2, jnp.where(pool_group == 1, s4,
                                                       jnp.where(pool_group == 2, s8, s16)))
        tpos = t0 + lax.broadcasted_iota(jnp.int32, (CONV_TILE, GROUP_W), 0)
        cnt = jnp.minimum(tpos + half_w, seq_len) - jnp.maximum(tpos - half_w, 0)
        dlt = ssum / cnt.astype(F32) - hi[0]
        tmp_ref[pl.ds(pl.multiple_of(i * CONV_TILE, CONV_TILE), CONV_TILE), :] = dlt

    ri = lax.broadcasted_iota(jnp.int32, (CHUNK, HEADS * CHUNK), 0)
    ci = lax.broadcasted_iota(jnp.int32, (CHUNK, HEADS * CHUNK), 1)
    cj = ci % CHUNK
    chead = ci // CHUNK
    rq = lax.broadcasted_iota(jnp.int32, (CHUNK, GROUP_W), 0).astype(F32)
    for d in range(2):
        lgs = [lg_ref[d * HEADS + h] for h in range(HEADS)]
        lg_wide = jnp.where(chead == 0, lgs[0], jnp.where(chead == 1, lgs[1],
                                                          jnp.where(chead == 2, lgs[2], lgs[3])))
        lg_lane = jnp.where(lane_head == 0, lgs[0], jnp.where(lane_head == 1, lgs[1],
                                                              jnp.where(lane_head == 2, lgs[2], lgs[3])))
        dist = (ri - cj) if d == 0 else (cj - ri)
        keep = dist >= 0
        dcat_ref[d] = jnp.where(keep, jnp.exp(jnp.where(keep, dist, 0).astype(F32) * lg_wide), 0.0)
        if d == 0:
            qdec_ref[d] = jnp.exp((rq + 1.0) * lg_lane)
            kdec_ref[d] = jnp.exp((CHUNK - 1.0 - rq) * lg_lane)
        else:
            qdec_ref[d] = jnp.exp((CHUNK - rq) * lg_lane)
            kdec_ref[d] = jnp.exp(rq * lg_lane)
        sdec_ref[d] = jnp.exp(float(CHUNK) * lg_lane)

    rr = lax.broadcasted_iota(jnp.int32, (GROUP_W, GROUP_W), 0) // HEAD_D
    cc = lax.broadcasted_iota(jnp.int32, (GROUP_W, GROUP_W), 1) // HEAD_D
    gmat_ref[...] = jnp.where(rr == cc, 1.0 / HEAD_D, 0.0).astype(BF16)

    lane_bit = (lane & 16) == 0
    k_scale = HEAD_D ** -0.5

    def rope(z, r0):
        if not rotate:
            return z
        cos = cos_ref[pl.ds(r0, CHUNK), :]
        sin = sin_ref[pl.ds(r0, CHUNK), :]
        halves = []
        for c0 in (0, 128):
            zz = z[:, c0:c0 + 128]
            partner = jnp.where(lane_bit[:, c0:c0 + 128],
                                pltpu.roll(zz, 128 - 16, axis=1), pltpu.roll(zz, 16, axis=1))
            halves.append(partner)
        return z * cos + jnp.concatenate(halves, axis=1) * sin

    def intra_body(c, _):
        r0 = pl.multiple_of(c * CHUNK, CHUNK)
        v = proj_ref[pl.ds(r0, CHUNK), COL_V:COL_V + GROUP_W]
        vb = v.astype(BF16)
        vstack = _head_stack(v, lane_head).astype(BF16)
        o = None
        for d in range(2):
            qc0 = COL_D + 2 * d * GROUP_W
            q = rope(proj_ref[pl.ds(r0, CHUNK), qc0:qc0 + GROUP_W], r0)
            k = rope(proj_ref[pl.ds(r0, CHUNK), qc0 + GROUP_W:qc0 + 2 * GROUP_W], r0) * k_scale
            qb = q.astype(BF16)
            qb_ref[d, pl.ds(r0, CHUNK), :] = qb
            kstack = _head_stack(k, lane_head).astype(BF16)
            att = lax.dot_general(qb, kstack, (((1,), (1,)), ((), ())), preferred_element_type=F32)
            att = (att * dcat_ref[d]).astype(BF16)
            od = jnp.dot(att, vstack, preferred_element_type=F32)
            o = od if o is None else o + od
            kd = (k * kdec_ref[d]).astype(BF16)
            upd = lax.dot_general(kd, vb, (((0,), (0,)), ((), ())), preferred_element_type=F32)
            compact = None
            for h in range(HEADS):
                part = jnp.where(lane_head == h, upd[h * HEAD_D:(h + 1) * HEAD_D, :], 0.0)
                compact = part if compact is None else compact + part
            upd_ref[d, c] = compact
        of_ref[pl.ds(r0, CHUNK), :] = o
        for t in range(CHUNK // CONV_TILE):
            pool_tile(c * (CHUNK // CONV_TILE) + t)
        return 0
    lax.fori_loop(0, N_CHUNKS, intra_body, 0, unroll=2)

    def pool_mix_body(i, _):
        r0 = pl.multiple_of(i * ROW_TILE, ROW_TILE)
        yc = jnp.dot(tmp_ref[pl.ds(r0, ROW_TILE), :].astype(BF16), pbd_ref[...],
                     preferred_element_type=F32) * pscale_ref[...]
        cat_ref[pl.ds(r0, ROW_TILE), 2 * GROUP_W:3 * GROUP_W] = yc.astype(BF16)
        return 0
    lax.fori_loop(0, TB // ROW_TILE, pool_mix_body, 0, unroll=2)

    for s in range(n_seq):
        for d in range(2):
            st = s0_ref[d] if rotate else jnp.zeros((HEAD_D, GROUP_W), F32)
            order = range(n_chunk) if d == 0 else range(n_chunk - 1, -1, -1)
            for c in order:
                cg = s * n_chunk + c
                for h in range(HEADS):
                    sall_ref[d, cg, h * HEAD_D:(h + 1) * HEAD_D, :] = (
                        jnp.where(lane_head == h, st, 0.0).astype(BF16))
                st = st * sdec_ref[d] + upd_ref[d, cg]
            if not rotate:
                for h in range(HEADS):
                    st_ref[s, d, h] = st[:, h * HEAD_D:(h + 1) * HEAD_D]

    def finish_body(c, _):
        r0 = pl.multiple_of(c * CHUNK, CHUNK)
        o = of_ref[pl.ds(r0, CHUNK), :]
        for d in range(2):
            o = o + jnp.dot(qb_ref[d, pl.ds(r0, CHUNK), :], sall_ref[d, c],
                            preferred_element_type=F32) * qdec_ref[d]
        gmat = gmat_ref[...]
        o_hi = o.astype(BF16)
        o_lo = (o - o_hi.astype(F32)).astype(BF16)
        mu = (jnp.dot(o_hi, gmat, preferred_element_type=F32)
              + jnp.dot(o_lo, gmat, preferred_element_type=F32))
        cen = o - mu
        var = jnp.dot((cen * cen).astype(BF16), gmat, preferred_element_type=F32)
        on = cen * lax.rsqrt(var + EPS)
        g = proj_ref[pl.ds(r0, CHUNK), COL_G:COL_G + GROUP_W]
        cat_ref[pl.ds(r0, CHUNK), 3 * GROUP_W:4 * GROUP_W] = (_silu(g) * on).astype(BF16)
        return 0
    lax.fori_loop(0, N_CHUNKS, finish_body, 0, unroll=4)

    def out_body(i, _):
        r0 = pl.multiple_of(i * ROW_TILE, ROW_TILE)
        y = jnp.dot(cat_ref[pl.ds(r0, ROW_TILE), :], w_out_ref[...], preferred_element_type=F32)
        y_ref[pl.ds(r0, ROW_TILE), :] = x_ref[pl.ds(r0, ROW_TILE), :] + ga1 * y
        return 0
    lax.fori_loop(0, TB // ROW_TILE, out_body, 0, unroll=2)


def _mixer(x, mod, lw, *, seq_len, rotate, mod_base, mod_stride, rope_tabs=None, s0=None):
    nb = x.shape[0]
    n_seq = TB // seq_len
    in_specs = [
        pl.BlockSpec((None, TB, D_MODEL), lambda i: (i, 0, 0)),
        pl.BlockSpec((None, 1, 6 * D_MODEL), lambda i: (mod_base + mod_stride * i, 0, 0)),
        _const_spec((1, D_MODEL)),
        _const_spec((D_MODEL, IN_COLS)),
        _const_spec((D_MODEL, D_MODEL)),
        _const_spec((CONV_W + 1, GROUP_W)),
        _const_spec((1, GROUP_W)), _const_spec((1, GROUP_W)), _const_spec((1, GROUP_W)),
        _const_spec((GROUP_W, GROUP_W)),
        _const_spec((CHUNK, HEADS * CHUNK)),
        _const_spec((CHUNK, GROUP_W)),
        _const_spec((GROUP_W, GROUP_W)),
        _const_spec((1, GROUP_W)),
        pl.BlockSpec(memory_space=pltpu.SMEM),
    ]
    args = [x, mod, lw["g1"], lw["w_in"], lw["w_out"], lw["dw"], lw["cb"], lw["lng"], lw["lnb"],
            lw["pw"], lw["wcat"], lw["gbias"], lw["pbd"], lw["pscale"], lw["lg"]]
    out_shape = [jax.ShapeDtypeStruct((nb, TB, D_MODEL), F32)]
    out_specs = [pl.BlockSpec((None, TB, D_MODEL), lambda i: (i, 0, 0))]
    if rotate:
        in_specs += [_const_spec((TB, GROUP_W)), _const_spec((TB, GROUP_W)),
                     pl.BlockSpec((None, 2, HEAD_D, GROUP_W), lambda i: (i, 0, 0, 0))]
        args += [rope_tabs[0], rope_tabs[1], s0]
    else:
        out_shape.append(jax.ShapeDtypeStruct((nb * n_seq, 2, HEADS, HEAD_D, HEAD_D), F32))
        out_specs.append(pl.BlockSpec((n_seq, 2, HEADS, HEAD_D, HEAD_D), lambda i: (i, 0, 0, 0, 0)))
    scratch = [
        pltpu.VMEM((TB, D_MODEL), BF16),
        pltpu.VMEM((TB, IN_COLS), F32),
        pltpu.VMEM((TB, D_MODEL), BF16),
        pltpu.VMEM((n_seq, seq_len + 2 * PAD, GROUP_W), F32),
        pltpu.VMEM((TB, GROUP_W), F32),
        pltpu.VMEM((TB, GROUP_W), F32),
        pltpu.VMEM((2, CHUNK, HEADS * CHUNK), F32),
        pltpu.VMEM((2, CHUNK, GROUP_W), F32),
        pltpu.VMEM((2, CHUNK, GROUP_W), F32),
        pltpu.VMEM((2, 1, GROUP_W), F32),
        pltpu.VMEM((GROUP_W, GROUP_W), BF16),
        pltpu.VMEM((2, TB, GROUP_W), BF16),
        pltpu.VMEM((2, N_CHUNKS, HEAD_D, GROUP_W), F32),
        pltpu.VMEM((2, N_CHUNKS, GROUP_W, GROUP_W), BF16),
    ]
    outs = pl.pallas_call(
        functools.partial(_mixer_kernel, seq_len=seq_len, rotate=rotate),
        grid=(nb,),
        in_specs=in_specs,
        out_specs=out_specs,
        out_shape=out_shape,
        scratch_shapes=scratch,
        compiler_params=pltpu.CompilerParams(
            dimension_semantics=("arbitrary",), vmem_limit_bytes=VMEM_LIMIT_BYTES),
        name="mixer_lat" if rotate else "mixer_ctx",
    )(*args)
    return outs


def _ffn_kernel(x_ref, mod_ref, g2_ref, wgu_ref, wo_ref, gf_ref, y_ref, hbf_ref, act_ref, *, final_norm):
    sh2 = mod_ref[:, 3 * D_MODEL:4 * D_MODEL]
    sc2 = mod_ref[:, 4 * D_MODEL:5 * D_MODEL]
    ga2 = mod_ref[:, 5 * D_MODEL:6 * D_MODEL]

    def norm_body(i, _):
        r0 = pl.multiple_of(i * ROW_TILE, ROW_TILE)
        h = _norm_mod(x_ref[pl.ds(r0, ROW_TILE), :], g2_ref[...], sc2, sh2)
        hbf_ref[pl.ds(r0, ROW_TILE), :] = h.astype(BF16)
        return 0
    lax.fori_loop(0, TB // ROW_TILE, norm_body, 0)

    for j in range(N_FF_TILES):
        gate = jnp.dot(hbf_ref[...], wgu_ref[:, j * FF_TILE:(j + 1) * FF_TILE], preferred_element_type=F32)
        up = jnp.dot(hbf_ref[...], wgu_ref[:, D_FF + j * FF_TILE:D_FF + (j + 1) * FF_TILE],
                     preferred_element_type=F32)
        act = _silu(gate) * up
        act_ref[:, j * FF_TILE:(j + 1) * FF_TILE] = act.astype(BF16)

    y_ref[...] = jnp.dot(act_ref[...], wo_ref[...], preferred_element_type=F32)

    def out_body(i, _):
        r0 = pl.multiple_of(i * ROW_TILE, ROW_TILE)
        y = x_ref[pl.ds(r0, ROW_TILE), :] + ga2 * y_ref[pl.ds(r0, ROW_TILE), :]
        if final_norm:
            ms = jnp.mean(y * y, axis=-1, keepdims=True)
            y = y * lax.rsqrt(ms + EPS) * gf_ref[...]
        y_ref[pl.ds(r0, ROW_TILE), :] = y
        return 0
    lax.fori_loop(0, TB // ROW_TILE, out_body, 0)


def _ffn(x, mod, lw, g_final, *, mod_base, mod_stride, final_norm):
    nb = x.shape[0]
    return pl.pallas_call(
        functools.partial(_ffn_kernel, final_norm=final_norm),
        grid=(nb,),
        in_specs=[
            pl.BlockSpec((None, TB, D_MODEL), lambda i: (i, 0, 0)),
            pl.BlockSpec((None, 1, 6 * D_MODEL), lambda i: (mod_base + mod_stride * i, 0, 0)),
            _const_spec((1, D_MODEL)),
            _const_spec((D_MODEL, 2 * D_FF)),
            _const_spec((D_FF, D_MODEL)),
            _const_spec((1, D_MODEL)),
        ],
        out_specs=pl.BlockSpec((None, TB, D_MODEL), lambda i: (i, 0, 0)),
        out_shape=jax.ShapeDtypeStruct((nb, TB, D_MODEL), F32),
        scratch_shapes=[pltpu.VMEM((TB, D_MODEL), BF16), pltpu.VMEM((TB, D_FF), BF16)],
        compiler_params=pltpu.CompilerParams(
            dimension_semantics=("arbitrary",), vmem_limit_bytes=VMEM_LIMIT_BYTES),
        name="ffn",
    )(x, mod, lw["g2"], lw["wgu"], lw["wo"], g_final)


def _rope_tables(seq_len):
    t = jnp.arange(seq_len)
    r = (t // GRID_W).astype(F32)
    c = (t % GRID_W).astype(F32)
    nf = HEAD_D // 4
    inv = ROPE_BASE ** (-jnp.arange(nf, dtype=F32) / nf)
    ang_r = r[:, None] * inv
    ang_c = c[:, None] * inv
    cos = jnp.concatenate([jnp.cos(ang_r), jnp.cos(ang_r), jnp.cos(ang_c), jnp.cos(ang_c)], axis=-1)
    sin = jnp.concatenate([-jnp.sin(ang_r), jnp.sin(ang_r), -jnp.sin(ang_c), jnp.sin(ang_c)], axis=-1)
    return jnp.tile(cos, (1, HEADS)), jnp.tile(sin, (1, HEADS))


def _layer_weights(l, g_norm1, g_norm2, w_in, w_out, conv_dw, conv_b, conv_ln_g, conv_ln_b, conv_pw,
                   gmlp_ws, gmlp_b, pool_w, pool_scale, ret_decay, w_ffn_in, w_ffn_out):
    return {
        "g1": g_norm1[l].reshape(1, D_MODEL),
        "g2": g_norm2[l].reshape(1, D_MODEL),
        "w_in": w_in[l].astype(BF16),
        "w_out": w_out[l].astype(BF16),
        "dw": jnp.concatenate([conv_dw[l], jnp.zeros((1, GROUP_W), F32)], axis=0),
        "cb": conv_b[l].reshape(1, GROUP_W),
        "lng": conv_ln_g[l].reshape(1, GROUP_W),
        "lnb": conv_ln_b[l].reshape(1, GROUP_W),
        "pw": conv_pw[l].astype(BF16),
        "wcat": jnp.transpose(gmlp_ws[l], (1, 0, 2)).reshape(CHUNK, HEADS * CHUNK).astype(BF16),
        "gbias": jnp.repeat(gmlp_b[l].T, HEAD_D, axis=1),
        "pbd": jax.scipy.linalg.block_diag(*[pool_w[l, g] for g in range(len(POOL_WINDOWS))]).astype(BF16),
        "pscale": pool_scale[l].reshape(1, GROUP_W),
        "lg": jax.nn.log_sigmoid(ret_decay[l].astype(F32)).reshape(2 * HEADS),
        "wgu": w_ffn_in[l].astype(BF16),
        "wo": w_ffn_out[l].astype(BF16),
    }


def kernel(x_prompt, x_sample, state_ret, c, c_ctx, w_ada, b_ada, g_norm1, g_norm2, w_in, w_out, conv_dw,
           conv_b, conv_ln_g, conv_ln_b, conv_pw, gmlp_ws, gmlp_b, pool_w, pool_scale, ret_decay, w_ffn_in,
           w_ffn_out, g_final):
    batch, seq, _ = x_prompt.shape
    dec_batch, dec_seq, _ = x_sample.shape
    assert dec_seq == TB and TB % seq == 0 and (batch * seq) % TB == 0
    assert 1 + dec_batch <= MOD_ROWS

    cs = jnp.concatenate([c_ctx[None, :], c, jnp.zeros((MOD_ROWS - 1 - dec_batch, D_MODEL), F32)], axis=0)
    mods = _ada_rows(cs, w_ada, b_ada).reshape(DEPTH, MOD_ROWS, 1, 6 * D_MODEL)
    rope_tabs = _rope_tables(dec_seq)
    g_final2 = g_final.reshape(1, D_MODEL)
    s0_all = jnp.transpose(state_ret.astype(F32), (0, 1, 2, 4, 3, 5)).reshape(
        dec_batch, DEPTH, 2, HEAD_D, GROUP_W)

    xc = x_prompt.reshape(batch * seq // TB, TB, D_MODEL)
    xl = x_sample
    states = []
    for l in range(DEPTH):
        lw = _layer_weights(l, g_norm1, g_norm2, w_in, w_out, conv_dw, conv_b, conv_ln_g, conv_ln_b,
                            conv_pw, gmlp_ws, gmlp_b, pool_w, pool_scale, ret_decay, w_ffn_in, w_ffn_out)
        last = l == DEPTH - 1
        xc, st = _mixer(xc, mods[l], lw, seq_len=seq, rotate=False, mod_base=0, mod_stride=0)
        states.append(st)
        xc = _ffn(xc, mods[l], lw, g_final2, mod_base=0, mod_stride=0, final_norm=last)
        (xl,) = _mixer(xl, mods[l], lw, seq_len=dec_seq, rotate=True, mod_base=1, mod_stride=1,
                       rope_tabs=rope_tabs, s0=s0_all[:, l])
        xl = _ffn(xl, mods[l], lw, g_final2, mod_base=1, mod_stride=1, final_norm=last)

    y_prompt = xc.reshape(batch, seq, D_MODEL)
    new_state = jnp.stack(states, axis=1).astype(x_prompt.dtype)
    return (y_prompt, xl, new_state)
```

```python
import functools

import jax
import jax.numpy as jnp
from jax import lax
from jax.experimental import pallas as pl
from jax.experimental.pallas import tpu as pltpu

F32 = jnp.float32
BF16 = jnp.bfloat16

D_MODEL = 1024
DEPTH = 2
GRID_W = 64
GROUP_W = D_MODEL // 4
CONV_W = 31
CHUNK = 128
HEADS = 4
HEAD_D = GROUP_W // HEADS
POOL_WINDOWS = (2, 4, 8, 16)
ROPE_BASE = 10000.0
D_FF = 2816
IN_COLS = 11 * GROUP_W
EPS = 1e-6

TB = 1024
N_CHUNKS = TB // CHUNK
ROW_TILE = 256
CONV_TILE = 64
N_CONV_TILES = TB // CONV_TILE
PAD = 16
FF_TILE = 256
N_FF_TILES = D_FF // FF_TILE
ADA_TILE = 1536
MOD_ROWS = 8
VMEM_LIMIT_BYTES = 60 * 1024 * 1024

COL_A, COL_B, COL_C, COL_D = 0, 2 * GROUP_W, 4 * GROUP_W, 5 * GROUP_W
COL_V, COL_G = COL_D + 4 * GROUP_W, COL_D + 5 * GROUP_W


def _sigmoid(x):
    return 1.0 / (1.0 + jnp.exp(-x))


def _silu(x):
    return x * _sigmoid(x)


def _norm_mod(x, g, scale, shift):
    ms = jnp.mean(x * x, axis=-1, keepdims=True)
    return (x * lax.rsqrt(ms + EPS) * g) * (1.0 + scale) + shift


def _head_stack(x, lane_head):
    return jnp.concatenate([jnp.where(lane_head == h, x, 0.0) for h in range(HEADS)], axis=0)


def _const_spec(shape):
    zeros = (0,) * len(shape)
    return pl.BlockSpec(shape, lambda i: zeros, pipeline_mode=pl.Buffered(1))


def _ada_kernel(c_ref, w_ref, b_ref, o_ref):
    a = _silu(c_ref[...]).astype(BF16)
    o_ref[...] = jnp.dot(a, w_ref[...].astype(BF16), preferred_element_type=F32) + b_ref[...]


def _ada_rows(cs, w_ada, b_ada):
    n_tiles = 6 * D_MODEL // ADA_TILE
    return pl.pallas_call(
        _ada_kernel,
        grid=(DEPTH, n_tiles),
        in_specs=[
            pl.BlockSpec((MOD_ROWS, D_MODEL), lambda l, j: (0, 0)),
            pl.BlockSpec((None, D_MODEL, ADA_TILE), lambda l, j: (l, 0, j)),
            pl.BlockSpec((None, 1, ADA_TILE), lambda l, j: (l, 0, j)),
        ],
        out_specs=pl.BlockSpec((None, MOD_ROWS, ADA_TILE), lambda l, j: (l, 0, j)),
        out_shape=jax.ShapeDtypeStruct((DEPTH, MOD_ROWS, 6 * D_MODEL), F32),
        compiler_params=pltpu.CompilerParams(
            dimension_semantics=("arbitrary", "arbitrary"), vmem_limit_bytes=VMEM_LIMIT_BYTES),
        name="ada_rows",
    )(cs, w_ada, b_ada.reshape(DEPTH, 1, 6 * D_MODEL))


def _mixer_kernel(*refs, seq_len, rotate):
    n_seq = TB // seq_len
    n_chunk = seq_len // CHUNK
    it = iter(refs)
    x_ref, mod_ref, g1_ref, w_in_ref, w_out_ref = (next(it) for _ in range(5))
    dw_ref, cb_ref, lng_ref, lnb_ref, pw_ref = (next(it) for _ in range(5))
    wcat_ref, gbias_ref, pbd_ref, pscale_ref, lg_ref = (next(it) for _ in range(5))
    if rotate:
        cos_ref, sin_ref, s0_ref = (next(it) for _ in range(3))
    y_ref = next(it)
    if not rotate:
        st_ref = next(it)
    (hbf_ref, proj_ref, cat_ref, pad_ref, pad2_ref, tmp_ref, tmp2_ref, of_ref,
     dcat_ref, qdec_ref, kdec_ref, sdec_ref, gmat_ref, qb_ref, upd_ref, sall_ref) = it

    sh1 = mod_ref[:, 0:D_MODEL]
    sc1 = mod_ref[:, D_MODEL:2 * D_MODEL]
    ga1 = mod_ref[:, 2 * D_MODEL:3 * D_MODEL]

    lane = lax.broadcasted_iota(jnp.int32, (1, GROUP_W), 1)
    lane_head = lane // HEAD_D
    tiles_per_seq = seq_len // CONV_TILE
    chunks_per_seq = seq_len // CHUNK

    def seq_and_offset(i, per_seq, size):
        if n_seq == 1:
            return 0, pl.multiple_of(i * size, size)
        return i // per_seq, pl.multiple_of((i % per_seq) * size, size)

    def norm_body(i, _):
        r0 = pl.multiple_of(i * ROW_TILE, ROW_TILE)
        h = _norm_mod(x_ref[pl.ds(r0, ROW_TILE), :], g1_ref[...], sc1, sh1).astype(BF16)
        hbf_ref[pl.ds(r0, ROW_TILE), :] = h
        proj_ref[pl.ds(r0, ROW_TILE), COL_A:COL_B] = jnp.dot(
            h, w_in_ref[:, COL_A:COL_B], preferred_element_type=F32)
        return 0
    lax.fori_loop(0, TB // ROW_TILE, norm_body, 0, unroll=2)

    for s in range(n_seq):
        for ref in (pad_ref, pad2_ref):
            ref[s, 0:PAD, :] = jnp.zeros((PAD, GROUP_W), F32)
            ref[s, PAD + seq_len:PAD + seq_len + PAD, :] = jnp.zeros((PAD, GROUP_W), F32)

    def glu_body(i, _):
        s, t0 = seq_and_offset(i, chunks_per_seq, CHUNK)
        r0 = pl.multiple_of(i * CHUNK, CHUNK)
        a1 = proj_ref[pl.ds(r0, CHUNK), COL_A:COL_A + GROUP_W]
        a2 = proj_ref[pl.ds(r0, CHUNK), COL_A + GROUP_W:COL_B]
        pad_ref[s, pl.ds(pl.multiple_of(t0 + PAD, 8), CHUNK), :] = a1 * _sigmoid(a2)
        return 0
    lax.fori_loop(0, N_CHUNKS, glu_body, 0, unroll=2)

    win = CONV_TILE + 2 * PAD

    def conv_tile(i):
        s, t0 = divmod(i, tiles_per_seq)
        t0 *= CONV_TILE
        w = pad_ref[s, t0:t0 + win, :]
        acc = jnp.zeros((CONV_TILE, GROUP_W), F32) + cb_ref[...]
        for b in range(8):
            wb = w if b == 0 else pltpu.roll(w, win - b, axis=0)
            for a in range(4):
                k = 8 * a + b - 1
                if 0 <= k < CONV_W:
                    acc = acc + dw_ref[k:k + 1, :] * wb[8 * a:8 * a + CONV_TILE, :]
        tmp_ref[i * CONV_TILE:(i + 1) * CONV_TILE, :] = acc

    pwin = CONV_TILE + 16
    lane128 = lax.broadcasted_iota(jnp.int32, (1, 128), 1)
    first_half = lane128 < HEAD_D

    def pool_tile(i):
        s, t0 = divmod(i, tiles_per_seq)
        t0 *= CONV_TILE
        interior = t0 >= 8 and t0 + CONV_TILE + 8 <= seq_len
        outs = []
        for col, (w_small, w_big) in enumerate(((2, 4), (8, 16))):
            w = pad2_ref[s, t0 + PAD - 8:t0 + PAD - 8 + pwin, col * 128:(col + 1) * 128]

            rolled = {0: w}

            def shifted(b, w=w, rolled=rolled):
                if b not in rolled:
                    rolled[b] = pltpu.roll(w, pwin - b, axis=0)
                return rolled[b]

            def lo(b):
                return shifted(b)[0:CONV_TILE, :]

            def hi(b):
                return shifted(b)[8:8 + CONV_TILE, :]

            tok = hi(0)
            if col == 0:
                s_small = lo(7) + tok
                s_big = s_small + lo(6) + hi(1)
            else:
                s_small = lo(7) + tok + lo(6) + hi(1) + lo(5) + lo(4) + hi(2) + hi(3)
                s_big = s_small + lo(3) + lo(2) + lo(1) + lo(0) + hi(4) + hi(5) + hi(6) + hi(7)
            ssum = jnp.where(first_half, s_small, s_big)
            if interior:
                mean = ssum * jnp.where(first_half, 1.0 / w_small, 1.0 / w_big)
            else:
                half = jnp.where(first_half, w_small // 2, w_big // 2)
                tpos = t0 + lax.broadcasted_iota(jnp.int32, (CONV_TILE, 128), 0)
                cnt = jnp.minimum(tpos + half, seq_len) - jnp.maximum(tpos - half, 0)
                mean = ssum / cnt.astype(F32)
            outs.append(mean - tok)
        tmp2_ref[i * CONV_TILE:(i + 1) * CONV_TILE, :] = jnp.concatenate(outs, axis=1)

    col_tiles = [COL_C] + list(range(COL_B, COL_C, GROUP_W)) + list(range(COL_D, IN_COLS, GROUP_W))
    per_step = -(-N_CONV_TILES // (len(col_tiles) - 1))
    conv_next = 0
    pool_next = 0
    for step, c0 in enumerate(col_tiles):
        proj_ref[:, c0:c0 + GROUP_W] = jnp.dot(
            hbf_ref[...], w_in_ref[:, c0:c0 + GROUP_W], preferred_element_type=F32)
        if step == 1:
            for s in range(n_seq):
                pad2_ref[s, PAD:PAD + seq_len, :] = proj_ref[s * seq_len:(s + 1) * seq_len, COL_C:COL_D]
        for _ in range(per_step):
            if conv_next < N_CONV_TILES:
                conv_tile(conv_next)
                conv_next += 1
            if step >= 1 and pool_next < N_CONV_TILES:
                pool_tile(pool_next)
                pool_next += 1
    assert conv_next == N_CONV_TILES and pool_next == N_CONV_TILES

    def ln_pw_body(i, _):
        r0 = pl.multiple_of(i * ROW_TILE, ROW_TILE)
        c = tmp_ref[pl.ds(r0, ROW_TILE), :]
        mu = jnp.mean(c, axis=-1, keepdims=True)
        cen = c - mu
        var = jnp.mean(cen * cen, axis=-1, keepdims=True)
        hn = cen * lax.rsqrt(var + EPS) * lng_ref[...] + lnb_ref[...]
        ya = jnp.dot(_silu(hn).astype(BF16), pw_ref[...], preferred_element_type=F32)
        cat_ref[pl.ds(r0, ROW_TILE), 0:GROUP_W] = ya.astype(BF16)
        return 0
    lax.fori_loop(0, TB // ROW_TILE, ln_pw_body, 0, unroll=2)

    def gmlp_body(i, _):
        r0 = pl.multiple_of(i * CHUNK, CHUNK)
        u = proj_ref[pl.ds(r0, CHUNK), COL_B:COL_B + GROUP_W]
        v = proj_ref[pl.ds(r0, CHUNK), COL_B + GROUP_W:COL_C]
        vstack = _head_stack(v, lane_head).astype(BF16)
        sg = jnp.dot(wcat_ref[...], vstack, preferred_element_type=F32) + gbias_ref[...]
        cat_ref[pl.ds(r0, CHUNK), GROUP_W:2 * GROUP_W] = (u * sg).astype(BF16)
        return 0
    lax.fori_loop(0, N_CHUNKS, gmlp_body, 0, unroll=4)

    def pool_mix_body(i, _):
        r0 = pl.multiple_of(i * ROW_TILE, ROW_TILE)
        yc = jnp.dot(tmp2_ref[pl.ds(r0, ROW_TILE), :].astype(BF16), pbd_ref[...],
                     preferred_element_type=F32) * pscale_ref[...]
        cat_ref[pl.ds(r0, ROW_TILE), 2 * GROUP_W:3 * GROUP_W] = yc.astype(BF16)
        return 0
    lax.fori_loop(0, TB // ROW_TILE, pool_mix_body, 0, unroll=2)

    ri = lax.broadcasted_iota(jnp.int32, (CHUNK, HEADS * CHUNK), 0)
    ci = lax.broadcasted_iota(jnp.int32, (CHUNK, HEADS * CHUNK), 1)
    cj = ci % CHUNK
    chead = ci // CHUNK
    rq = lax.broadcasted_iota(jnp.int32, (CHUNK, GROUP_W), 0).astype(F32)
    for d in range(2):
        lgs = [lg_ref[d * HEADS + h] for h in range(HEADS)]
        lg_wide = jnp.where(chead == 0, lgs[0], jnp.where(chead == 1, lgs[1],
                                                          jnp.where(chead == 2, lgs[2], lgs[3])))
        lg_lane = jnp.where(lane_head == 0, lgs[0], jnp.where(lane_head == 1, lgs[1],
                                                              jnp.where(lane_head == 2, lgs[2], lgs[3])))
        dist = (ri - cj) if d == 0 else (cj - ri)
        keep = dist >= 0
        dcat_ref[d] = jnp.where(keep, jnp.exp(jnp.where(keep, dist, 0).astype(F32) * lg_wide), 0.0)
        if d == 0:
            qdec_ref[d] = jnp.exp((rq + 1.0) * lg_lane)
            kdec_ref[d] = jnp.exp((CHUNK - 1.0 - rq) * lg_lane)
        else:
            qdec_ref[d] = jnp.exp((CHUNK - rq) * lg_lane)
            kdec_ref[d] = jnp.exp(rq * lg_lane)
        sdec_ref[d] = jnp.exp(float(CHUNK) * lg_lane)

    rr = lax.broadcasted_iota(jnp.int32, (GROUP_W, GROUP_W), 0) // HEAD_D
    cc = lax.broadcasted_iota(jnp.int32, (GROUP_W, GROUP_W), 1) // HEAD_D
    gmat_ref[...] = jnp.where(rr == cc, 1.0 / HEAD_D, 0.0).astype(BF16)

    lane_bit = (lane & 16) == 0
    k_scale = HEAD_D ** -0.5

    def rope(z, r0):
        if not rotate:
            return z
        cos = cos_ref[pl.ds(r0, CHUNK), :]
        sin = sin_ref[pl.ds(r0, CHUNK), :]
        halves = []
        for c0 in (0, 128):
            zz = z[:, c0:c0 + 128]
            partner = jnp.where(lane_bit[:, c0:c0 + 128],
                                pltpu.roll(zz, 128 - 16, axis=1), pltpu.roll(zz, 16, axis=1))
            halves.append(partner)
        return z * cos + jnp.concatenate(halves, axis=1) * sin

    def pair_stack(zb, pair):
        zero = jnp.zeros_like(zb)
        return jnp.concatenate([jnp.where(lane_head == h, zb, zero) for h in (2 * pair, 2 * pair + 1)], axis=0)

    def intra_body(c, _):
        r0 = pl.multiple_of(c * CHUNK, CHUNK)
        v = proj_ref[pl.ds(r0, CHUNK), COL_V:COL_V + GROUP_W]
        vb = v.astype(BF16)
        vstacks = [pair_stack(vb, pair) for pair in range(2)]
        o = None
        for d in range(2):
            qc0 = COL_D + 2 * d * GROUP_W
            q = rope(proj_ref[pl.ds(r0, CHUNK), qc0:qc0 + GROUP_W], r0)
            k = rope(proj_ref[pl.ds(r0, CHUNK), qc0 + GROUP_W:qc0 + 2 * GROUP_W], r0) * k_scale
            qb = q.astype(BF16)
            kb = k.astype(BF16)
            qb_ref[d, pl.ds(r0, CHUNK), :] = qb
            for pair in range(2):
                att = lax.dot_general(qb, pair_stack(kb, pair), (((1,), (1,)), ((), ())),
                                      preferred_element_type=F32)
                att = (att * dcat_ref[d, :, pair * 2 * CHUNK:(pair + 1) * 2 * CHUNK]).astype(BF16)
                od = jnp.dot(att, vstacks[pair], preferred_element_type=F32)
                o = od if o is None else o + od
            kd = (k * kdec_ref[d]).astype(BF16)
            upd = lax.dot_general(kd, vb, (((0,), (0,)), ((), ())), preferred_element_type=F32)
            compact = None
            for h in range(HEADS):
                part = jnp.where(lane_head == h, upd[h * HEAD_D:(h + 1) * HEAD_D, :], 0.0)
                compact = part if compact is None else compact + part
            upd_ref[d, c] = compact
        of_ref[pl.ds(r0, CHUNK), :] = o
        return 0
    lax.fori_loop(0, N_CHUNKS, intra_body, 0, unroll=2)

    for s in range(n_seq):
        for d in range(2):
            st = s0_ref[d] if rotate else jnp.zeros((HEAD_D, GROUP_W), F32)
            order = range(n_chunk) if d == 0 else range(n_chunk - 1, -1, -1)
            for c in order:
                cg = s * n_chunk + c
                for h in range(HEADS):
                    sall_ref[d, cg, h * HEAD_D:(h + 1) * HEAD_D, :] = (
                        jnp.where(lane_head == h, st, 0.0).astype(BF16))
                st = st * sdec_ref[d] + upd_ref[d, cg]
            if not rotate:
                for h in range(HEADS):
                    st_ref[s, d, h] = st[:, h * HEAD_D:(h + 1) * HEAD_D]

    def cross_body(c, _):
        r0 = pl.multiple_of(c * CHUNK, CHUNK)
        o = of_ref[pl.ds(r0, CHUNK), :]
        for d in range(2):
            o = o + jnp.dot(qb_ref[d, pl.ds(r0, CHUNK), :], sall_ref[d, c],
                            preferred_element_type=F32) * qdec_ref[d]
        of_ref[pl.ds(r0, CHUNK), :] = o
        return 0
    lax.fori_loop(0, N_CHUNKS, cross_body, 0, unroll=2)

    def center_body(i, _):
        r0 = pl.multiple_of(i * ROW_TILE, ROW_TILE)
        o = of_ref[pl.ds(r0, ROW_TILE), :]
        gmat = gmat_ref[...]
        o_hi = o.astype(BF16)
        o_lo = (o - o_hi.astype(F32)).astype(BF16)
        mu = (jnp.dot(o_hi, gmat, preferred_element_type=F32)
              + jnp.dot(o_lo, gmat, preferred_element_type=F32))
        of_ref[pl.ds(r0, ROW_TILE), :] = o - mu
        return 0
    lax.fori_loop(0, TB // ROW_TILE, center_body, 0, unroll=2)

    def gate_body(i, _):
        r0 = pl.multiple_of(i * ROW_TILE, ROW_TILE)
        cen = of_ref[pl.ds(r0, ROW_TILE), :]
        var = jnp.dot((cen * cen).astype(BF16), gmat_ref[...], preferred_element_type=F32)
        on = cen * lax.rsqrt(var + EPS)
        g = proj_ref[pl.ds(r0, ROW_TILE), COL_G:COL_G + GROUP_W]
        cat_ref[pl.ds(r0, ROW_TILE), 3 * GROUP_W:4 * GROUP_W] = (_silu(g) * on).astype(BF16)
        return 0
    lax.fori_loop(0, TB // ROW_TILE, gate_body, 0, unroll=2)

    def out_body(i, _):
        r0 = pl.multiple_of(i * ROW_TILE, ROW_TILE)
        y = jnp.dot(cat_ref[pl.ds(r0, ROW_TILE), :], w_out_ref[...], preferred_element_type=F32)
        y_ref[pl.ds(r0, ROW_TILE), :] = x_ref[pl.ds(r0, ROW_TILE), :] + ga1 * y
        return 0
    lax.fori_loop(0, TB // ROW_TILE, out_body, 0, unroll=2)


def _mixer(x, mod, lw, *, seq_len, rotate, mod_base, mod_stride, rope_tabs=None, s0=None):
    nb = x.shape[0]
    n_seq = TB // seq_len
    in_specs = [
        pl.BlockSpec((None, TB, D_MODEL), lambda i: (i, 0, 0)),
        pl.BlockSpec((None, 1, 6 * D_MODEL), lambda i: (mod_base + mod_stride * i, 0, 0)),
        _const_spec((1, D_MODEL)),
        _const_spec((D_MODEL, IN_COLS)),
        _const_spec((D_MODEL, D_MODEL)),
        _const_spec((CONV_W + 1, GROUP_W)),
        _const_spec((1, GROUP_W)), _const_spec((1, GROUP_W)), _const_spec((1, GROUP_W)),
        _const_spec((GROUP_W, GROUP_W)),
        _const_spec((CHUNK, HEADS * CHUNK)),
        _const_spec((CHUNK, GROUP_W)),
        _const_spec((GROUP_W, GROUP_W)),
        _const_spec((1, GROUP_W)),
        pl.BlockSpec(memory_space=pltpu.SMEM),
    ]
    args = [x, mod, lw["g1"], lw["w_in"], lw["w_out"], lw["dw"], lw["cb"], lw["lng"], lw["lnb"],
            lw["pw"], lw["wcat"], lw["gbias"], lw["pbd"], lw["pscale"], lw["lg"]]
    out_shape = [jax.ShapeDtypeStruct((nb, TB, D_MODEL), F32)]
    out_specs = [pl.BlockSpec((None, TB, D_MODEL), lambda i: (i, 0, 0))]
    if rotate:
        in_specs += [_const_spec((TB, GROUP_W)), _const_spec((TB, GROUP_W)),
                     pl.BlockSpec((None, 2, HEAD_D, GROUP_W), lambda i: (i, 0, 0, 0))]
        args += [rope_tabs[0], rope_tabs[1], s0]
    else:
        out_shape.append(jax.ShapeDtypeStruct((nb * n_seq, 2, HEADS, HEAD_D, HEAD_D), F32))
        out_specs.append(pl.BlockSpec((n_seq, 2, HEADS, HEAD_D, HEAD_D), lambda i: (i, 0, 0, 0, 0)))
    scratch = [
        pltpu.VMEM((TB, D_MODEL), BF16),
        pltpu.VMEM((TB, IN_COLS), F32),
        pltpu.VMEM((TB, D_MODEL), BF16),
        pltpu.VMEM((n_seq, seq_len + 2 * PAD, GROUP_W), F32),
        pltpu.VMEM((n_seq, seq_len + 2 * PAD, GROUP_W), F32),
        pltpu.VMEM((TB, GROUP_W), F32),
        pltpu.VMEM((TB, GROUP_W), F32),
        pltpu.VMEM((TB, GROUP_W), F32),
        pltpu.VMEM((2, CHUNK, HEADS * CHUNK), F32),
        pltpu.VMEM((2, CHUNK, GROUP_W), F32),
        pltpu.VMEM((2, CHUNK, GROUP_W), F32),
        pltpu.VMEM((2, 1, GROUP_W), F32),
        pltpu.VMEM((GROUP_W, GROUP_W), BF16),
        pltpu.VMEM((2, TB, GROUP_W), BF16),
        pltpu.VMEM((2, N_CHUNKS, HEAD_D, GROUP_W), F32),
        pltpu.VMEM((2, N_CHUNKS, GROUP_W, GROUP_W), BF16),
    ]
    outs = pl.pallas_call(
        functools.partial(_mixer_kernel, seq_len=seq_len, rotate=rotate),
        grid=(nb,),
        in_specs=in_specs,
        out_specs=out_specs,
        out_shape=out_shape,
        scratch_shapes=scratch,
        compiler_params=pltpu.CompilerParams(
            dimension_semantics=("arbitrary",), vmem_limit_bytes=VMEM_LIMIT_BYTES),
        name="mixer_lat" if rotate else "mixer_ctx",
    )(*args)
    return outs


def _ffn_kernel(x_ref, mod_ref, g2_ref, wgu_ref, wo_ref, gf_ref, y_ref, hbf_ref, act_ref, *, final_norm):
    sh2 = mod_ref[:, 3 * D_MODEL:4 * D_MODEL]
    sc2 = mod_ref[:, 4 * D_MODEL:5 * D_MODEL]
    ga2 = mod_ref[:, 5 * D_MODEL:6 * D_MODEL]

    def norm_body(i, _):
        r0 = pl.multiple_of(i * ROW_TILE, ROW_TILE)
        h = _norm_mod(x_ref[pl.ds(r0, ROW_TILE), :], g2_ref[...], sc2, sh2)
        hbf_ref[pl.ds(r0, ROW_TILE), :] = h.astype(BF16)
        return 0
    lax.fori_loop(0, TB // ROW_TILE, norm_body, 0)

    for j in range(N_FF_TILES):
        gate = jnp.dot(hbf_ref[...], wgu_ref[:, j * FF_TILE:(j + 1) * FF_TILE], preferred_element_type=F32)
        up = jnp.dot(hbf_ref[...], wgu_ref[:, D_FF + j * FF_TILE:D_FF + (j + 1) * FF_TILE],
                     preferred_element_type=F32)
        act = _silu(gate) * up
        act_ref[:, j * FF_TILE:(j + 1) * FF_TILE] = act.astype(BF16)

    y_ref[...] = jnp.dot(act_ref[...], wo_ref[...], preferred_element_type=F32)

    def out_body(i, _):
        r0 = pl.multiple_of(i * ROW_TILE, ROW_TILE)
        y = x_ref[pl.ds(r0, ROW_TILE), :] + ga2 * y_ref[pl.ds(r0, ROW_TILE), :]
        if final_norm:
            ms = jnp.mean(y * y, axis=-1, keepdims=True)
            y = y * lax.rsqrt(ms + EPS) * gf_ref[...]
        y_ref[pl.ds(r0, ROW_TILE), :] = y
        return 0
    lax.fori_loop(0, TB // ROW_TILE, out_body, 0)


def _ffn(x, mod, lw, g_final, *, mod_base, mod_stride, final_norm):
    nb = x.shape[0]
    return pl.pallas_call(
        functools.partial(_ffn_kernel, final_norm=final_norm),
        grid=(nb,),
        in_specs=[
            pl.BlockSpec((None, TB, D_MODEL), lambda i: (i, 0, 0)),
            pl.BlockSpec((None, 1, 6 * D_MODEL), lambda i: (mod_base + mod_stride * i, 0, 0)),
            _const_spec((1, D_MODEL)),
            _const_spec((D_MODEL, 2 * D_FF)),
            _const_spec((D_FF, D_MODEL)),
            _const_spec((1, D_MODEL)),
        ],
        out_specs=pl.BlockSpec((None, TB, D_MODEL), lambda i: (i, 0, 0)),
        out_shape=jax.ShapeDtypeStruct((nb, TB, D_MODEL), F32),
        scratch_shapes=[pltpu.VMEM((TB, D_MODEL), BF16), pltpu.VMEM((TB, D_FF), BF16)],
        compiler_params=pltpu.CompilerParams(
            dimension_semantics=("arbitrary",), vmem_limit_bytes=VMEM_LIMIT_BYTES),
        name="ffn",
    )(x, mod, lw["g2"], lw["wgu"], lw["wo"], g_final)


def _rope_tables(seq_len):
    t = jnp.arange(seq_len)
    r = (t // GRID_W).astype(F32)
    c = (t % GRID_W).astype(F32)
    nf = HEAD_D // 4
    inv = ROPE_BASE ** (-jnp.arange(nf, dtype=F32) / nf)
    ang_r = r[:, None] * inv
    ang_c = c[:, None] * inv
    cos = jnp.concatenate([jnp.cos(ang_r), jnp.cos(ang_r), jnp.cos(ang_c), jnp.cos(ang_c)], axis=-1)
    sin = jnp.concatenate([-jnp.sin(ang_r), jnp.sin(ang_r), -jnp.sin(ang_c), jnp.sin(ang_c)], axis=-1)
    return jnp.tile(cos, (1, HEADS)), jnp.tile(sin, (1, HEADS))


def _layer_weights(l, g_norm1, g_norm2, w_in, w_out, conv_dw, conv_b, conv_ln_g, conv_ln_b, conv_pw,
                   gmlp_ws, gmlp_b, pool_w, pool_scale, ret_decay, w_ffn_in, w_ffn_out):
    return {
        "g1": g_norm1[l].reshape(1, D_MODEL),
        "g2": g_norm2[l].reshape(1, D_MODEL),
        "w_in": w_in[l].astype(BF16),
        "w_out": w_out[l].astype(BF16),
        "dw": jnp.concatenate([conv_dw[l], jnp.zeros((1, GROUP_W), F32)], axis=0),
        "cb": conv_b[l].reshape(1, GROUP_W),
        "lng": conv_ln_g[l].reshape(1, GROUP_W),
        "lnb": conv_ln_b[l].reshape(1, GROUP_W),
        "pw": conv_pw[l].astype(BF16),
        "wcat": jnp.transpose(gmlp_ws[l], (1, 0, 2)).reshape(CHUNK, HEADS * CHUNK).astype(BF16),
        "gbias": jnp.repeat(gmlp_b[l].T, HEAD_D, axis=1),
        "pbd": jax.scipy.linalg.block_diag(*[pool_w[l, g] for g in range(len(POOL_WINDOWS))]).astype(BF16),
        "pscale": pool_scale[l].reshape(1, GROUP_W),
        "lg": jax.nn.log_sigmoid(ret_decay[l].astype(F32)).reshape(2 * HEADS),
        "wgu": w_ffn_in[l].astype(BF16),
        "wo": w_ffn_out[l].astype(BF16),
    }


def kernel(x_prompt, x_sample, state_ret, c, c_ctx, w_ada, b_ada, g_norm1, g_norm2, w_in, w_out, conv_dw,
           conv_b, conv_ln_g, conv_ln_b, conv_pw, gmlp_ws, gmlp_b, pool_w, pool_scale, ret_decay, w_ffn_in,
           w_ffn_out, g_final):
    batch, seq, _ = x_prompt.shape
    dec_batch, dec_seq, _ = x_sample.shape
    assert dec_seq == TB and TB % seq == 0 and (batch * seq) % TB == 0
    assert 1 + dec_batch <= MOD_ROWS

    cs = jnp.concatenate([c_ctx[None, :], c, jnp.zeros((MOD_ROWS - 1 - dec_batch, D_MODEL), F32)], axis=0)
    mods = _ada_rows(cs, w_ada, b_ada).reshape(DEPTH, MOD_ROWS, 1, 6 * D_MODEL)
    rope_tabs = _rope_tables(dec_seq)
    g_final2 = g_final.reshape(1, D_MODEL)
    s0_all = jnp.transpose(state_ret.astype(F32), (0, 1, 2, 4, 3, 5)).reshape(
        dec_batch, DEPTH, 2, HEAD_D, GROUP_W)

    xc = x_prompt.reshape(batch * seq // TB, TB, D_MODEL)
    xl = x_sample
    states = []
    for l in range(DEPTH):
        lw = _layer_weights(l, g_norm1, g_norm2, w_in, w_out, conv_dw, conv_b, conv_ln_g, conv_ln_b,
                            conv_pw, gmlp_ws, gmlp_b, pool_w, pool_scale, ret_decay, w_ffn_in, w_ffn_out)
        last = l == DEPTH - 1
        xc, st = _mixer(xc, mods[l], lw, seq_len=seq, rotate=False, mod_base=0, mod_stride=0)
        states.append(st)
        xc = _ffn(xc, mods[l], lw, g_final2, mod_base=0, mod_stride=0, final_norm=last)
        (xl,) = _mixer(xl, mods[l], lw, seq_len=dec_seq, rotate=True, mod_base=1, mod_stride=1,
                       rope_tabs=rope_tabs, s0=s0_all[:, l])
        xl = _ffn(xl, mods[l], lw, g_final2, mod_base=1, mod_stride=1, final_norm=last)

    y_prompt = xc.reshape(batch, seq, D_MODEL)
    new_state = jnp.stack(states, axis=1).astype(x_prompt.dtype)
    return (y_prompt, xl, new_state)
```

```python
import functools

import jax
import jax.numpy as jnp
from jax import lax
from jax.experimental import pallas as pl
from jax.experimental.pallas import tpu as pltpu

F32 = jnp.float32
BF16 = jnp.bfloat16

D_MODEL = 1024
DEPTH = 2
GRID_W = 64
GROUP_W = D_MODEL // 4
CONV_W = 31
CHUNK = 128
HEADS = 4
HEAD_D = GROUP_W // HEADS
POOL_WINDOWS = (2, 4, 8, 16)
ROPE_BASE = 10000.0
D_FF = 2816
IN_COLS = 11 * GROUP_W
EPS = 1e-6

TB = 1024
N_CHUNKS = TB // CHUNK
ROW_TILE = 256
CONV_TILE = 64
N_CONV_TILES = TB // CONV_TILE
PAD = 16
FF_TILE = 256
N_FF_TILES = D_FF // FF_TILE
FF_LOOKAHEAD = 2
FF_RING = FF_LOOKAHEAD + 1
ADA_TILE = 1536
MOD_ROWS = 8
VMEM_LIMIT_BYTES = 60 * 1024 * 1024

COL_A, COL_B, COL_C, COL_D = 0, 2 * GROUP_W, 4 * GROUP_W, 5 * GROUP_W
COL_V, COL_G = COL_D + 4 * GROUP_W, COL_D + 5 * GROUP_W


def _sigmoid(x):
    return 1.0 / (1.0 + jnp.exp(-x))


def _silu(x):
    return x * _sigmoid(x)


def _norm_mod(x, g, scale, shift):
    ms = jnp.mean(x * x, axis=-1, keepdims=True)
    return (x * lax.rsqrt(ms + EPS) * g) * (1.0 + scale) + shift


def _head_stack(x, lane_head):
    return jnp.concatenate([jnp.where(lane_head == h, x, 0.0) for h in range(HEADS)], axis=0)


def _const_spec(shape):
    zeros = (0,) * len(shape)
    return pl.BlockSpec(shape, lambda i: zeros, pipeline_mode=pl.Buffered(1))


def _ada_kernel(c_ref, w_ref, b_ref, o_ref):
    a = _silu(c_ref[...]).astype(BF16)
    o_ref[...] = jnp.dot(a, w_ref[...].astype(BF16), preferred_element_type=F32) + b_ref[...]


def _ada_rows(cs, w_ada, b_ada):
    n_tiles = 6 * D_MODEL // ADA_TILE
    return pl.pallas_call(
        _ada_kernel,
        grid=(DEPTH, n_tiles),
        in_specs=[
            pl.BlockSpec((MOD_ROWS, D_MODEL), lambda l, j: (0, 0)),
            pl.BlockSpec((None, D_MODEL, ADA_TILE), lambda l, j: (l, 0, j)),
            pl.BlockSpec((None, 1, ADA_TILE), lambda l, j: (l, 0, j)),
        ],
        out_specs=pl.BlockSpec((None, MOD_ROWS, ADA_TILE), lambda l, j: (l, 0, j)),
        out_shape=jax.ShapeDtypeStruct((DEPTH, MOD_ROWS, 6 * D_MODEL), F32),
        compiler_params=pltpu.CompilerParams(
            dimension_semantics=("arbitrary", "arbitrary"), vmem_limit_bytes=VMEM_LIMIT_BYTES),
        name="ada_rows",
    )(cs, w_ada, b_ada.reshape(DEPTH, 1, 6 * D_MODEL))


def _mixer_kernel(*refs, seq_len, rotate):
    n_seq = TB // seq_len
    n_chunk = seq_len // CHUNK
    it = iter(refs)
    x_ref, mod_ref, g1_ref, w_in_ref, w_out_ref = (next(it) for _ in range(5))
    dw_ref, cb_ref, lng_ref, lnb_ref, pw_ref = (next(it) for _ in range(5))
    wcat_ref, gbias_ref, pbd_ref, pscale_ref, lg_ref = (next(it) for _ in range(5))
    if rotate:
        cos_ref, sin_ref, s0_ref = (next(it) for _ in range(3))
    y_ref = next(it)
    if not rotate:
        st_ref = next(it)
    (hbf_ref, proj_ref, cat_ref, pad_ref, pad2_ref, tmp_ref, tmp2_ref, of_ref,
     dcat_ref, qdec_ref, kdec_ref, sdec_ref, gmat_ref, qb_ref, upd_ref, sall_ref) = it

    sh1 = mod_ref[:, 0:D_MODEL]
    sc1 = mod_ref[:, D_MODEL:2 * D_MODEL]
    ga1 = mod_ref[:, 2 * D_MODEL:3 * D_MODEL]

    lane = lax.broadcasted_iota(jnp.int32, (1, GROUP_W), 1)
    lane_head = lane // HEAD_D
    tiles_per_seq = seq_len // CONV_TILE
    chunks_per_seq = seq_len // CHUNK

    def seq_and_offset(i, per_seq, size):
        if n_seq == 1:
            return 0, pl.multiple_of(i * size, size)
        return i // per_seq, pl.multiple_of((i % per_seq) * size, size)

    def norm_body(i, _):
        r0 = pl.multiple_of(i * ROW_TILE, ROW_TILE)
        h = _norm_mod(x_ref[pl.ds(r0, ROW_TILE), :], g1_ref[...], sc1, sh1).astype(BF16)
        hbf_ref[pl.ds(r0, ROW_TILE), :] = h
        proj_ref[pl.ds(r0, ROW_TILE), COL_A:COL_B] = jnp.dot(
            h, w_in_ref[:, COL_A:COL_B], preferred_element_type=F32)
        return 0
    lax.fori_loop(0, TB // ROW_TILE, norm_body, 0, unroll=2)

    for s in range(n_seq):
        for ref in (pad_ref, pad2_ref):
            ref[s, 0:PAD, :] = jnp.zeros((PAD, GROUP_W), F32)
            ref[s, PAD + seq_len:PAD + seq_len + PAD, :] = jnp.zeros((PAD, GROUP_W), F32)

    def glu_body(i, _):
        s, t0 = seq_and_offset(i, chunks_per_seq, CHUNK)
        r0 = pl.multiple_of(i * CHUNK, CHUNK)
        a1 = proj_ref[pl.ds(r0, CHUNK), COL_A:COL_A + GROUP_W]
        a2 = proj_ref[pl.ds(r0, CHUNK), COL_A + GROUP_W:COL_B]
        pad_ref[s, pl.ds(pl.multiple_of(t0 + PAD, 8), CHUNK), :] = a1 * _sigmoid(a2)
        return 0
    lax.fori_loop(0, N_CHUNKS, glu_body, 0, unroll=2)

    win = CONV_TILE + 2 * PAD

    def conv_tile(i):
        s, t0 = divmod(i, tiles_per_seq)
        t0 *= CONV_TILE
        w = pad_ref[s, t0:t0 + win, :]
        acc = jnp.zeros((CONV_TILE, GROUP_W), F32) + cb_ref[...]
        for b in range(8):
            wb = w if b == 0 else pltpu.roll(w, win - b, axis=0)
            for a in range(4):
                k = 8 * a + b - 1
                if 0 <= k < CONV_W:
                    acc = acc + dw_ref[k:k + 1, :] * wb[8 * a:8 * a + CONV_TILE, :]
        tmp_ref[i * CONV_TILE:(i + 1) * CONV_TILE, :] = acc

    pwin = CONV_TILE + 16
    lane128 = lax.broadcasted_iota(jnp.int32, (1, 128), 1)
    first_half = lane128 < HEAD_D

    def pool_tile(i):
        s, t0 = divmod(i, tiles_per_seq)
        t0 *= CONV_TILE
        interior = t0 >= 8 and t0 + CONV_TILE + 8 <= seq_len
        outs = []
        for col, (w_small, w_big) in enumerate(((2, 4), (8, 16))):
            w = pad2_ref[s, t0 + PAD - 8:t0 + PAD - 8 + pwin, col * 128:(col + 1) * 128]

            rolled = {0: w}

            def shifted(b, w=w, rolled=rolled):
                if b not in rolled:
                    rolled[b] = pltpu.roll(w, pwin - b, axis=0)
                return rolled[b]

            def lo(b):
                return shifted(b)[0:CONV_TILE, :]

            def hi(b):
                return shifted(b)[8:8 + CONV_TILE, :]

            tok = hi(0)
            if col == 0:
                s_small = lo(7) + tok
                s_big = s_small + lo(6) + hi(1)
            else:
                s_small = lo(7) + tok + lo(6) + hi(1) + lo(5) + lo(4) + hi(2) + hi(3)
                s_big = s_small + lo(3) + lo(2) + lo(1) + lo(0) + hi(4) + hi(5) + hi(6) + hi(7)
            ssum = jnp.where(first_half, s_small, s_big)
            if interior:
                mean = ssum * jnp.where(first_half, 1.0 / w_small, 1.0 / w_big)
            else:
                half = jnp.where(first_half, w_small // 2, w_big // 2)
                tpos = t0 + lax.broadcasted_iota(jnp.int32, (CONV_TILE, 128), 0)
                cnt = jnp.minimum(tpos + half, seq_len) - jnp.maximum(tpos - half, 0)
                mean = ssum / cnt.astype(F32)
            outs.append(mean - tok)
        tmp2_ref[i * CONV_TILE:(i + 1) * CONV_TILE, :] = jnp.concatenate(outs, axis=1)

    col_tiles = [COL_C] + list(range(COL_B, COL_C, GROUP_W)) + list(range(COL_D, IN_COLS, GROUP_W))
    per_step = -(-N_CONV_TILES // (len(col_tiles) - 1))
    conv_next = 0
    pool_next = 0
    for step, c0 in enumerate(col_tiles):
        proj_ref[:, c0:c0 + GROUP_W] = jnp.dot(
            hbf_ref[...], w_in_ref[:, c0:c0 + GROUP_W], preferred_element_type=F32)
        if step == 1:
            for s in range(n_seq):
                pad2_ref[s, PAD:PAD + seq_len, :] = proj_ref[s * seq_len:(s + 1) * seq_len, COL_C:COL_D]
        for _ in range(per_step):
            if conv_next < N_CONV_TILES:
                conv_tile(conv_next)
                conv_next += 1
            if step >= 1 and pool_next < N_CONV_TILES:
                pool_tile(pool_next)
                pool_next += 1
    assert conv_next == N_CONV_TILES and pool_next == N_CONV_TILES

    def ln_pw_body(i, _):
        r0 = pl.multiple_of(i * ROW_TILE, ROW_TILE)
        c = tmp_ref[pl.ds(r0, ROW_TILE), :]
        mu = jnp.mean(c, axis=-1, keepdims=True)
        cen = c - mu
        var = jnp.mean(cen * cen, axis=-1, keepdims=True)
        hn = cen * lax.rsqrt(var + EPS) * lng_ref[...] + lnb_ref[...]
        ya = jnp.dot(_silu(hn).astype(BF16), pw_ref[...], preferred_element_type=F32)
        cat_ref[pl.ds(r0, ROW_TILE), 0:GROUP_W] = ya.astype(BF16)
        return 0
    lax.fori_loop(0, TB // ROW_TILE, ln_pw_body, 0, unroll=2)

    def gmlp_body(i, _):
        r0 = pl.multiple_of(i * CHUNK, CHUNK)
        u = proj_ref[pl.ds(r0, CHUNK), COL_B:COL_B + GROUP_W]
        v = proj_ref[pl.ds(r0, CHUNK), COL_B + GROUP_W:COL_C]
        vstack = _head_stack(v, lane_head).astype(BF16)
        sg = jnp.dot(wcat_ref[...], vstack, preferred_element_type=F32) + gbias_ref[...]
        cat_ref[pl.ds(r0, CHUNK), GROUP_W:2 * GROUP_W] = (u * sg).astype(BF16)
        return 0
    lax.fori_loop(0, N_CHUNKS, gmlp_body, 0, unroll=4)

    def pool_mix_body(i, _):
        r0 = pl.multiple_of(i * ROW_TILE, ROW_TILE)
        yc = jnp.dot(tmp2_ref[pl.ds(r0, ROW_TILE), :].astype(BF16), pbd_ref[...],
                     preferred_element_type=F32) * pscale_ref[...]
        cat_ref[pl.ds(r0, ROW_TILE), 2 * GROUP_W:3 * GROUP_W] = yc.astype(BF16)
        return 0
    lax.fori_loop(0, TB // ROW_TILE, pool_mix_body, 0, unroll=2)

    ri = lax.broadcasted_iota(jnp.int32, (CHUNK, HEADS * CHUNK), 0)
    ci = lax.broadcasted_iota(jnp.int32, (CHUNK, HEADS * CHUNK), 1)
    cj = ci % CHUNK
    chead = ci // CHUNK
    rq = lax.broadcasted_iota(jnp.int32, (CHUNK, GROUP_W), 0).astype(F32)
    for d in range(2):
        lgs = [lg_ref[d * HEADS + h] for h in range(HEADS)]
        lg_wide = jnp.where(chead == 0, lgs[0], jnp.where(chead == 1, lgs[1],
                                                          jnp.where(chead == 2, lgs[2], lgs[3])))
        lg_lane = jnp.where(lane_head == 0, lgs[0], jnp.where(lane_head == 1, lgs[1],
                                                              jnp.where(lane_head == 2, lgs[2], lgs[3])))
        dist = (ri - cj) if d == 0 else (cj - ri)
        keep = dist >= 0
        dcat_ref[d] = jnp.where(keep, jnp.exp(jnp.where(keep, dist, 0).astype(F32) * lg_wide), 0.0)
        if d == 0:
            qdec_ref[d] = jnp.exp((rq + 1.0) * lg_lane)
            kdec_ref[d] = jnp.exp((CHUNK - 1.0 - rq) * lg_lane)
        else:
            qdec_ref[d] = jnp.exp((CHUNK - rq) * lg_lane)
            kdec_ref[d] = jnp.exp(rq * lg_lane)
        sdec_ref[d] = jnp.exp(float(CHUNK) * lg_lane)

    rr = lax.broadcasted_iota(jnp.int32, (GROUP_W, GROUP_W), 0) // HEAD_D
    cc = lax.broadcasted_iota(jnp.int32, (GROUP_W, GROUP_W), 1) // HEAD_D
    gmat_ref[...] = jnp.where(rr == cc, 1.0 / HEAD_D, 0.0).astype(BF16)

    lane_bit = (lane & 16) == 0
    k_scale = HEAD_D ** -0.5

    def rope(z, r0):
        if not rotate:
            return z
        cos = cos_ref[pl.ds(r0, CHUNK), :]
        sin = sin_ref[pl.ds(r0, CHUNK), :]
        halves = []
        for c0 in (0, 128):
            zz = z[:, c0:c0 + 128]
            partner = jnp.where(lane_bit[:, c0:c0 + 128],
                                pltpu.roll(zz, 128 - 16, axis=1), pltpu.roll(zz, 16, axis=1))
            halves.append(partner)
        return z * cos + jnp.concatenate(halves, axis=1) * sin

    def pair_stack(zb, pair):
        zero = jnp.zeros_like(zb)
        return jnp.concatenate([jnp.where(lane_head == h, zb, zero) for h in (2 * pair, 2 * pair + 1)], axis=0)

    def intra_body(c, _):
        r0 = pl.multiple_of(c * CHUNK, CHUNK)
        v = proj_ref[pl.ds(r0, CHUNK), COL_V:COL_V + GROUP_W]
        vb = v.astype(BF16)
        vstacks = [pair_stack(vb, pair) for pair in range(2)]
        o = None
        for d in range(2):
            qc0 = COL_D + 2 * d * GROUP_W
            q = rope(proj_ref[pl.ds(r0, CHUNK), qc0:qc0 + GROUP_W], r0)
            k = rope(proj_ref[pl.ds(r0, CHUNK), qc0 + GROUP_W:qc0 + 2 * GROUP_W], r0) * k_scale
            qb = q.astype(BF16)
            kb = k.astype(BF16)
            qb_ref[d, pl.ds(r0, CHUNK), :] = qb
            for pair in range(2):
                att = lax.dot_general(qb, pair_stack(kb, pair), (((1,), (1,)), ((), ())),
                                      preferred_element_type=F32)
                att = (att * dcat_ref[d, :, pair * 2 * CHUNK:(pair + 1) * 2 * CHUNK]).astype(BF16)
                od = jnp.dot(att, vstacks[pair], preferred_element_type=F32)
                o = od if o is None else o + od
            kd = (k * kdec_ref[d]).astype(BF16)
            upd = lax.dot_general(kd, vb, (((0,), (0,)), ((), ())), preferred_element_type=F32)
            compact = None
            for h in range(HEADS):
                part = jnp.where(lane_head == h, upd[h * HEAD_D:(h + 1) * HEAD_D, :], 0.0)
                compact = part if compact is None else compact + part
            upd_ref[d, c] = compact
        of_ref[pl.ds(r0, CHUNK), :] = o
        return 0
    lax.fori_loop(0, N_CHUNKS, intra_body, 0, unroll=2)

    for s in range(n_seq):
        for d in range(2):
            st = s0_ref[d] if rotate else jnp.zeros((HEAD_D, GROUP_W), F32)
            order = range(n_chunk) if d == 0 else range(n_chunk - 1, -1, -1)
            for c in order:
                cg = s * n_chunk + c
                for h in range(HEADS):
                    sall_ref[d, cg, h * HEAD_D:(h + 1) * HEAD_D, :] = (
                        jnp.where(lane_head == h, st, 0.0).astype(BF16))
                st = st * sdec_ref[d] + upd_ref[d, cg]
            if not rotate:
                for h in range(HEADS):
                    st_ref[s, d, h] = st[:, h * HEAD_D:(h + 1) * HEAD_D]

    def cross_body(c, _):
        r0 = pl.multiple_of(c * CHUNK, CHUNK)
        o = of_ref[pl.ds(r0, CHUNK), :]
        for d in range(2):
            o = o + jnp.dot(qb_ref[d, pl.ds(r0, CHUNK), :], sall_ref[d, c],
                            preferred_element_type=F32) * qdec_ref[d]
        of_ref[pl.ds(r0, CHUNK), :] = o
        return 0
    lax.fori_loop(0, N_CHUNKS, cross_body, 0, unroll=2)

    def center_body(i, _):
        r0 = pl.multiple_of(i * ROW_TILE, ROW_TILE)
        o = of_ref[pl.ds(r0, ROW_TILE), :]
        gmat = gmat_ref[...]
        o_hi = o.astype(BF16)
        o_lo = (o - o_hi.astype(F32)).astype(BF16)
        mu = (jnp.dot(o_hi, gmat, preferred_element_type=F32)
              + jnp.dot(o_lo, gmat, preferred_element_type=F32))
        of_ref[pl.ds(r0, ROW_TILE), :] = o - mu
        return 0
    lax.fori_loop(0, TB // ROW_TILE, center_body, 0, unroll=2)

    def gate_body(i, _):
        r0 = pl.multiple_of(i * ROW_TILE, ROW_TILE)
        cen = of_ref[pl.ds(r0, ROW_TILE), :]
        var = jnp.dot((cen * cen).astype(BF16), gmat_ref[...], preferred_element_type=F32)
        on = cen * lax.rsqrt(var + EPS)
        g = proj_ref[pl.ds(r0, ROW_TILE), COL_G:COL_G + GROUP_W]
        cat_ref[pl.ds(r0, ROW_TILE), 3 * GROUP_W:4 * GROUP_W] = (_silu(g) * on).astype(BF16)
        return 0
    lax.fori_loop(0, TB // ROW_TILE, gate_body, 0, unroll=2)

    def out_body(i, _):
        r0 = pl.multiple_of(i * ROW_TILE, ROW_TILE)
        y = jnp.dot(cat_ref[pl.ds(r0, ROW_TILE), :], w_out_ref[...], preferred_element_type=F32)
        y_ref[pl.ds(r0, ROW_TILE), :] = x_ref[pl.ds(r0, ROW_TILE), :] + ga1 * y
        return 0
    lax.fori_loop(0, TB // ROW_TILE, out_body, 0, unroll=2)


def _mixer(x, mod, lw, *, seq_len, rotate, mod_base, mod_stride, rope_tabs=None, s0=None):
    nb = x.shape[0]
    n_seq = TB // seq_len
    in_specs = [
        pl.BlockSpec((None, TB, D_MODEL), lambda i: (i, 0, 0)),
        pl.BlockSpec((None, 1, 6 * D_MODEL), lambda i: (jnp.maximum(mod_base + mod_stride * i, 0), 0, 0)),
        _const_spec((1, D_MODEL)),
        _const_spec((D_MODEL, IN_COLS)),
        _const_spec((D_MODEL, D_MODEL)),
        _const_spec((CONV_W + 1, GROUP_W)),
        _const_spec((1, GROUP_W)), _const_spec((1, GROUP_W)), _const_spec((1, GROUP_W)),
        _const_spec((GROUP_W, GROUP_W)),
        _const_spec((CHUNK, HEADS * CHUNK)),
        _const_spec((CHUNK, GROUP_W)),
        _const_spec((GROUP_W, GROUP_W)),
        _const_spec((1, GROUP_W)),
        pl.BlockSpec(memory_space=pltpu.SMEM),
    ]
    args = [x, mod, lw["g1"], lw["w_in"], lw["w_out"], lw["dw"], lw["cb"], lw["lng"], lw["lnb"],
            lw["pw"], lw["wcat"], lw["gbias"], lw["pbd"], lw["pscale"], lw["lg"]]
    out_shape = [jax.ShapeDtypeStruct((nb, TB, D_MODEL), F32)]
    out_specs = [pl.BlockSpec((None, TB, D_MODEL), lambda i: (i, 0, 0))]
    if rotate:
        in_specs += [_const_spec((TB, GROUP_W)), _const_spec((TB, GROUP_W)),
                     pl.BlockSpec((None, 2, HEAD_D, GROUP_W), lambda i: (i, 0, 0, 0))]
        args += [rope_tabs[0], rope_tabs[1], s0]
    else:
        out_shape.append(jax.ShapeDtypeStruct((nb * n_seq, 2, HEADS, HEAD_D, HEAD_D), F32))
        out_specs.append(pl.BlockSpec((n_seq, 2, HEADS, HEAD_D, HEAD_D), lambda i: (i, 0, 0, 0, 0)))
    scratch = [
        pltpu.VMEM((TB, D_MODEL), BF16),
        pltpu.VMEM((TB, IN_COLS), F32),
        pltpu.VMEM((TB, D_MODEL), BF16),
        pltpu.VMEM((n_seq, seq_len + 2 * PAD, GROUP_W), F32),
        pltpu.VMEM((n_seq, seq_len + 2 * PAD, GROUP_W), F32),
        pltpu.VMEM((TB, GROUP_W), F32),
        pltpu.VMEM((TB, GROUP_W), F32),
        pltpu.VMEM((TB, GROUP_W), F32),
        pltpu.VMEM((2, CHUNK, HEADS * CHUNK), F32),
        pltpu.VMEM((2, CHUNK, GROUP_W), F32),
        pltpu.VMEM((2, CHUNK, GROUP_W), F32),
        pltpu.VMEM((2, 1, GROUP_W), F32),
        pltpu.VMEM((GROUP_W, GROUP_W), BF16),
        pltpu.VMEM((2, TB, GROUP_W), BF16),
        pltpu.VMEM((2, N_CHUNKS, HEAD_D, GROUP_W), F32),
        pltpu.VMEM((2, N_CHUNKS, GROUP_W, GROUP_W), BF16),
    ]
    outs = pl.pallas_call(
        functools.partial(_mixer_kernel, seq_len=seq_len, rotate=rotate),
        grid=(nb,),
        in_specs=in_specs,
        out_specs=out_specs,
        out_shape=out_shape,
        scratch_shapes=scratch,
        compiler_params=pltpu.CompilerParams(
            dimension_semantics=("arbitrary",), vmem_limit_bytes=VMEM_LIMIT_BYTES),
        name="mixer_lat" if rotate else "mixer_ctx",
    )(*args)
    return outs


def _ffn_kernel(x_ref, mod_ref, g2_ref, w_in_hbm, w_out_hbm, gf_ref, y_ref,
                hbf_ref, act_ref, wo_ref, stg_g_ref, stg_u_ref, stg_o_ref, sem, *, layer, n_blocks, final_norm):
    pid = pl.program_id(0)
    sh2 = mod_ref[:, 3 * D_MODEL:4 * D_MODEL]
    sc2 = mod_ref[:, 4 * D_MODEL:5 * D_MODEL]
    ga2 = mod_ref[:, 5 * D_MODEL:6 * D_MODEL]

    def tile_copies(j, slot):
        return (
            pltpu.make_async_copy(w_in_hbm.at[layer, :, pl.ds(j * FF_TILE, FF_TILE)],
                                  stg_g_ref.at[slot], sem.at[0, slot]),
            pltpu.make_async_copy(w_in_hbm.at[layer, :, pl.ds(D_FF + j * FF_TILE, FF_TILE)],
                                  stg_u_ref.at[slot], sem.at[1, slot]),
            pltpu.make_async_copy(w_out_hbm.at[layer, pl.ds(j * FF_TILE, FF_TILE), :],
                                  stg_o_ref.at[slot], sem.at[2, slot]),
        )

    def ring_slot(block, j):
        return lax.rem(block * (N_FF_TILES % FF_RING) + j, FF_RING)

    def start_tile(block, j):
        for cp in tile_copies(j, ring_slot(block, j)):
            cp.start()

    @pl.when(pid == 0)
    def _():
        for j in range(FF_LOOKAHEAD):
            start_tile(pid, j)

    def norm_body(i, _):
        r0 = pl.multiple_of(i * ROW_TILE, ROW_TILE)
        h = _norm_mod(x_ref[pl.ds(r0, ROW_TILE), :], g2_ref[...], sc2, sh2)
        hbf_ref[pl.ds(r0, ROW_TILE), :] = h.astype(BF16)
        return 0
    lax.fori_loop(0, TB // ROW_TILE, norm_body, 0)

    for j in range(N_FF_TILES):
        ahead = j + FF_LOOKAHEAD
        if ahead < N_FF_TILES:
            start_tile(pid, ahead)
        else:
            @pl.when(pid + 1 < n_blocks)
            def _(ahead=ahead):
                start_tile(pid + 1, ahead - N_FF_TILES)
        slot = ring_slot(pid, j)
        for cp in tile_copies(j, slot):
            cp.wait()
        gate = jnp.dot(hbf_ref[...], stg_g_ref[slot].astype(BF16), preferred_element_type=F32)
        up = jnp.dot(hbf_ref[...], stg_u_ref[slot].astype(BF16), preferred_element_type=F32)
        act = _silu(gate) * up
        act_ref[:, j * FF_TILE:(j + 1) * FF_TILE] = act.astype(BF16)
        wo_ref[j * FF_TILE:(j + 1) * FF_TILE, :] = stg_o_ref[slot].astype(BF16)

    y_ref[...] = jnp.dot(act_ref[...], wo_ref[...], preferred_element_type=F32)

    def out_body(i, _):
        r0 = pl.multiple_of(i * ROW_TILE, ROW_TILE)
        y = x_ref[pl.ds(r0, ROW_TILE), :] + ga2 * y_ref[pl.ds(r0, ROW_TILE), :]
        if final_norm:
            ms = jnp.mean(y * y, axis=-1, keepdims=True)
            y = y * lax.rsqrt(ms + EPS) * gf_ref[...]
        y_ref[pl.ds(r0, ROW_TILE), :] = y
        return 0
    lax.fori_loop(0, TB // ROW_TILE, out_body, 0)


def _ffn(x, mod, g2, w_ffn_in, w_ffn_out, g_final, *, layer, mod_base, mod_stride, final_norm):
    nb = x.shape[0]
    return pl.pallas_call(
        functools.partial(_ffn_kernel, layer=layer, n_blocks=nb, final_norm=final_norm),
        grid=(nb,),
        in_specs=[
            pl.BlockSpec((None, TB, D_MODEL), lambda i: (i, 0, 0)),
            pl.BlockSpec((None, 1, 6 * D_MODEL), lambda i: (jnp.maximum(mod_base + mod_stride * i, 0), 0, 0)),
            _const_spec((1, D_MODEL)),
            pl.BlockSpec(memory_space=pl.ANY),
            pl.BlockSpec(memory_space=pl.ANY),
            _const_spec((1, D_MODEL)),
        ],
        out_specs=pl.BlockSpec((None, TB, D_MODEL), lambda i: (i, 0, 0)),
        out_shape=jax.ShapeDtypeStruct((nb, TB, D_MODEL), F32),
        scratch_shapes=[
            pltpu.VMEM((TB, D_MODEL), BF16),
            pltpu.VMEM((TB, D_FF), BF16),
            pltpu.VMEM((D_FF, D_MODEL), BF16),
            pltpu.VMEM((FF_RING, D_MODEL, FF_TILE), F32),
            pltpu.VMEM((FF_RING, D_MODEL, FF_TILE), F32),
            pltpu.VMEM((FF_RING, FF_TILE, D_MODEL), F32),
            pltpu.SemaphoreType.DMA((3, FF_RING)),
        ],
        compiler_params=pltpu.CompilerParams(
            dimension_semantics=("arbitrary",), vmem_limit_bytes=VMEM_LIMIT_BYTES),
        name="ffn",
    )(x, mod, g2, w_ffn_in, w_ffn_out, g_final)


def _rope_tables(seq_len):
    t = jnp.arange(seq_len)
    r = (t // GRID_W).astype(F32)
    c = (t % GRID_W).astype(F32)
    nf = HEAD_D // 4
    inv = ROPE_BASE ** (-jnp.arange(nf, dtype=F32) / nf)
    ang_r = r[:, None] * inv
    ang_c = c[:, None] * inv
    cos = jnp.concatenate([jnp.cos(ang_r), jnp.cos(ang_r), jnp.cos(ang_c), jnp.cos(ang_c)], axis=-1)
    sin = jnp.concatenate([-jnp.sin(ang_r), jnp.sin(ang_r), -jnp.sin(ang_c), jnp.sin(ang_c)], axis=-1)
    return jnp.tile(cos, (1, HEADS)), jnp.tile(sin, (1, HEADS))


def _layer_weights(l, g_norm1, g_norm2, w_in, w_out, conv_dw, conv_b, conv_ln_g, conv_ln_b, conv_pw,
                   gmlp_ws, gmlp_b, pool_w, pool_scale, ret_decay, w_ffn_in, w_ffn_out):
    return {
        "g1": g_norm1[l].reshape(1, D_MODEL),
        "g2": g_norm2[l].reshape(1, D_MODEL),
        "w_in": w_in[l].astype(BF16),
        "w_out": w_out[l].astype(BF16),
        "dw": jnp.concatenate([conv_dw[l], jnp.zeros((1, GROUP_W), F32)], axis=0),
        "cb": conv_b[l].reshape(1, GROUP_W),
        "lng": conv_ln_g[l].reshape(1, GROUP_W),
        "lnb": conv_ln_b[l].reshape(1, GROUP_W),
        "pw": conv_pw[l].astype(BF16),
        "wcat": jnp.transpose(gmlp_ws[l], (1, 0, 2)).reshape(CHUNK, HEADS * CHUNK).astype(BF16),
        "gbias": jnp.repeat(gmlp_b[l].T, HEAD_D, axis=1),
        "pbd": jax.scipy.linalg.block_diag(*[pool_w[l, g] for g in range(len(POOL_WINDOWS))]).astype(BF16),
        "pscale": pool_scale[l].reshape(1, GROUP_W),
        "lg": jax.nn.log_sigmoid(ret_decay[l].astype(F32)).reshape(2 * HEADS),
    }


def kernel(x_prompt, x_sample, state_ret, c, c_ctx, w_ada, b_ada, g_norm1, g_norm2, w_in, w_out, conv_dw,
           conv_b, conv_ln_g, conv_ln_b, conv_pw, gmlp_ws, gmlp_b, pool_w, pool_scale, ret_decay, w_ffn_in,
           w_ffn_out, g_final):
    batch, seq, _ = x_prompt.shape
    dec_batch, dec_seq, _ = x_sample.shape
    assert dec_seq == TB and TB % seq == 0 and (batch * seq) % TB == 0
    assert 1 + dec_batch <= MOD_ROWS

    cs = jnp.concatenate([c_ctx[None, :], c, jnp.zeros((MOD_ROWS - 1 - dec_batch, D_MODEL), F32)], axis=0)
    mods = _ada_rows(cs, w_ada, b_ada).reshape(DEPTH, MOD_ROWS, 1, 6 * D_MODEL)
    rope_tabs = _rope_tables(dec_seq)
    g_final2 = g_final.reshape(1, D_MODEL)
    s0_all = jnp.transpose(state_ret.astype(F32), (0, 1, 2, 4, 3, 5)).reshape(
        dec_batch, DEPTH, 2, HEAD_D, GROUP_W)

    xc = x_prompt.reshape(batch * seq // TB, TB, D_MODEL)
    xl = x_sample
    states = []
    for l in range(DEPTH):
        lw = _layer_weights(l, g_norm1, g_norm2, w_in, w_out, conv_dw, conv_b, conv_ln_g, conv_ln_b,
                            conv_pw, gmlp_ws, gmlp_b, pool_w, pool_scale, ret_decay, w_ffn_in, w_ffn_out)
        last = l == DEPTH - 1
        xc, st = _mixer(xc, mods[l], lw, seq_len=seq, rotate=False, mod_base=0, mod_stride=0)
        states.append(st)
        xc = _ffn(xc, mods[l], lw["g2"], w_ffn_in, w_ffn_out, g_final2, layer=l, mod_base=0, mod_stride=0,
                  final_norm=last)
        (xl,) = _mixer(xl, mods[l], lw, seq_len=dec_seq, rotate=True, mod_base=1, mod_stride=1,
                       rope_tabs=rope_tabs, s0=s0_all[:, l])
        xl = _ffn(xl, mods[l], lw["g2"], w_ffn_in, w_ffn_out, g_final2, layer=l, mod_base=1, mod_stride=1,
                  final_norm=last)

    y_prompt = xc.reshape(batch, seq, D_MODEL)
    new_state = jnp.stack(states, axis=1).astype(x_prompt.dtype)
    return (y_prompt, xl, new_state)
```

```python
import functools

import jax
import jax.numpy as jnp
from jax import lax
from jax.experimental import pallas as pl
from jax.experimental.pallas import tpu as pltpu

F32 = jnp.float32
BF16 = jnp.bfloat16

D_MODEL = 1024
DEPTH = 2
GRID_W = 64
GROUP_W = D_MODEL // 4
CONV_W = 31
CHUNK = 128
HEADS = 4
HEAD_D = GROUP_W // HEADS
POOL_WINDOWS = (2, 4, 8, 16)
ROPE_BASE = 10000.0
D_FF = 2816
IN_COLS = 11 * GROUP_W
EPS = 1e-6

TB = 1024
N_CHUNKS = TB // CHUNK
ROW_TILE = 256
CONV_TILE = 64
N_CONV_TILES = TB // CONV_TILE
PAD = 16
FF_TILE = 256
N_FF_TILES = D_FF // FF_TILE
FF_LOOKAHEAD = 2
FF_RING = FF_LOOKAHEAD + 1
ADA_TILE = 1536
MOD_ROWS = 8
VMEM_LIMIT_BYTES = 60 * 1024 * 1024

COL_A, COL_B, COL_C, COL_D = 0, 2 * GROUP_W, 4 * GROUP_W, 5 * GROUP_W
COL_V, COL_G = COL_D + 4 * GROUP_W, COL_D + 5 * GROUP_W


def _sigmoid(x):
    return 1.0 / (1.0 + jnp.exp(-x))


def _silu(x):
    return x * _sigmoid(x)


def _norm_mod(x, g, scale, shift):
    ms = jnp.mean(x * x, axis=-1, keepdims=True)
    return (x * lax.rsqrt(ms + EPS) * g) * (1.0 + scale) + shift


def _head_stack(x, lane_head):
    return jnp.concatenate([jnp.where(lane_head == h, x, 0.0) for h in range(HEADS)], axis=0)


def _const_spec(shape):
    zeros = (0,) * len(shape)
    return pl.BlockSpec(shape, lambda i: zeros, pipeline_mode=pl.Buffered(1))


def _ada_kernel(c_ref, w_ref, b_ref, o_ref):
    a = _silu(c_ref[...]).astype(BF16)
    o_ref[...] = jnp.dot(a, w_ref[...].astype(BF16), preferred_element_type=F32) + b_ref[...]


def _ada_rows(cs, w_ada, b_ada):
    n_tiles = 6 * D_MODEL // ADA_TILE
    return pl.pallas_call(
        _ada_kernel,
        grid=(DEPTH, n_tiles),
        in_specs=[
            pl.BlockSpec((MOD_ROWS, D_MODEL), lambda l, j: (0, 0)),
            pl.BlockSpec((None, D_MODEL, ADA_TILE), lambda l, j: (l, 0, j)),
            pl.BlockSpec((None, 1, ADA_TILE), lambda l, j: (l, 0, j)),
        ],
        out_specs=pl.BlockSpec((None, MOD_ROWS, ADA_TILE), lambda l, j: (l, 0, j)),
        out_shape=jax.ShapeDtypeStruct((DEPTH, MOD_ROWS, 6 * D_MODEL), F32),
        compiler_params=pltpu.CompilerParams(
            dimension_semantics=("arbitrary", "arbitrary"), vmem_limit_bytes=VMEM_LIMIT_BYTES),
        name="ada_rows",
    )(cs, w_ada, b_ada.reshape(DEPTH, 1, 6 * D_MODEL))


def _mixer_kernel(*refs, seq_len, rotate):
    n_seq = TB // seq_len
    n_chunk = seq_len // CHUNK
    it = iter(refs)
    x_ref, mod_ref, g1_ref, w_in_ref, w_out_ref = (next(it) for _ in range(5))
    dw_ref, cb_ref, lng_ref, lnb_ref, pw_ref = (next(it) for _ in range(5))
    wcat_ref, gbias_ref, pbd_ref, pscale_ref, lg_ref = (next(it) for _ in range(5))
    if rotate:
        cos_ref, sin_ref, s0_ref = (next(it) for _ in range(3))
    y_ref = next(it)
    if not rotate:
        st_ref = next(it)
    (hbf_ref, proj_ref, cat_ref, pad_ref, pad2_ref, tmp_ref, tmp2_ref, of_ref,
     dcat_ref, qdec_ref, kdec_ref, sdec_ref, gmat_ref, qb_ref, upd_ref, sall_ref) = it

    sh1 = mod_ref[:, 0:D_MODEL]
    sc1 = mod_ref[:, D_MODEL:2 * D_MODEL]
    ga1 = mod_ref[:, 2 * D_MODEL:3 * D_MODEL]

    lane = lax.broadcasted_iota(jnp.int32, (1, GROUP_W), 1)
    lane_head = lane // HEAD_D
    tiles_per_seq = seq_len // CONV_TILE
    chunks_per_seq = seq_len // CHUNK

    def seq_and_offset(i, per_seq, size):
        if n_seq == 1:
            return 0, pl.multiple_of(i * size, size)
        return i // per_seq, pl.multiple_of((i % per_seq) * size, size)

    def norm_body(i, _):
        r0 = pl.multiple_of(i * ROW_TILE, ROW_TILE)
        h = _norm_mod(x_ref[pl.ds(r0, ROW_TILE), :], g1_ref[...], sc1, sh1).astype(BF16)
        hbf_ref[pl.ds(r0, ROW_TILE), :] = h
        proj_ref[pl.ds(r0, ROW_TILE), COL_A:COL_B] = jnp.dot(
            h, w_in_ref[:, COL_A:COL_B], preferred_element_type=F32)
        return 0
    lax.fori_loop(0, TB // ROW_TILE, norm_body, 0, unroll=2)

    for s in range(n_seq):
        for ref in (pad_ref, pad2_ref):
            ref[s, 0:PAD, :] = jnp.zeros((PAD, GROUP_W), F32)
            ref[s, PAD + seq_len:PAD + seq_len + PAD, :] = jnp.zeros((PAD, GROUP_W), F32)

    def glu_body(i, _):
        s, t0 = seq_and_offset(i, chunks_per_seq, CHUNK)
        r0 = pl.multiple_of(i * CHUNK, CHUNK)
        a1 = proj_ref[pl.ds(r0, CHUNK), COL_A:COL_A + GROUP_W]
        a2 = proj_ref[pl.ds(r0, CHUNK), COL_A + GROUP_W:COL_B]
        pad_ref[s, pl.ds(pl.multiple_of(t0 + PAD, 8), CHUNK), :] = a1 * _sigmoid(a2)
        return 0
    lax.fori_loop(0, N_CHUNKS, glu_body, 0, unroll=2)

    win = CONV_TILE + 2 * PAD

    def conv_tile(i):
        s, t0 = divmod(i, tiles_per_seq)
        t0 *= CONV_TILE
        w = pad_ref[s, t0:t0 + win, :]
        acc = jnp.zeros((CONV_TILE, GROUP_W), F32) + cb_ref[...]
        for b in range(8):
            wb = w if b == 0 else pltpu.roll(w, win - b, axis=0)
            for a in range(4):
                k = 8 * a + b - 1
                if 0 <= k < CONV_W:
                    acc = acc + dw_ref[k:k + 1, :] * wb[8 * a:8 * a + CONV_TILE, :]
        tmp_ref[i * CONV_TILE:(i + 1) * CONV_TILE, :] = acc

    pwin = CONV_TILE + 16
    lane128 = lax.broadcasted_iota(jnp.int32, (1, 128), 1)
    first_half = lane128 < HEAD_D

    def pool_tile(i):
        s, t0 = divmod(i, tiles_per_seq)
        t0 *= CONV_TILE
        interior = t0 >= 8 and t0 + CONV_TILE + 8 <= seq_len
        outs = []
        for col, (w_small, w_big) in enumerate(((2, 4), (8, 16))):
            w = pad2_ref[s, t0 + PAD - 8:t0 + PAD - 8 + pwin, col * 128:(col + 1) * 128]

            rolled = {0: w}

            def shifted(b, w=w, rolled=rolled):
                if b not in rolled:
                    rolled[b] = pltpu.roll(w, pwin - b, axis=0)
                return rolled[b]

            def lo(b):
                return shifted(b)[0:CONV_TILE, :]

            def hi(b):
                return shifted(b)[8:8 + CONV_TILE, :]

            tok = hi(0)
            if col == 0:
                s_small = lo(7) + tok
                s_big = s_small + lo(6) + hi(1)
            else:
                s_small = lo(7) + tok + lo(6) + hi(1) + lo(5) + lo(4) + hi(2) + hi(3)
                s_big = s_small + lo(3) + lo(2) + lo(1) + lo(0) + hi(4) + hi(5) + hi(6) + hi(7)
            ssum = jnp.where(first_half, s_small, s_big)
            if interior:
                mean = ssum * jnp.where(first_half, 1.0 / w_small, 1.0 / w_big)
            else:
                half = jnp.where(first_half, w_small // 2, w_big // 2)
                tpos = t0 + lax.broadcasted_iota(jnp.int32, (CONV_TILE, 128), 0)
                cnt = jnp.minimum(tpos + half, seq_len) - jnp.maximum(tpos - half, 0)
                mean = ssum / cnt.astype(F32)
            outs.append(mean - tok)
        tmp2_ref[i * CONV_TILE:(i + 1) * CONV_TILE, :] = jnp.concatenate(outs, axis=1)

    col_tiles = [COL_C] + list(range(COL_B, COL_C, GROUP_W)) + list(range(COL_D, IN_COLS, GROUP_W))
    per_step = -(-N_CONV_TILES // (len(col_tiles) - 1))
    conv_next = 0
    pool_next = 0
    for step, c0 in enumerate(col_tiles):
        proj_ref[:, c0:c0 + GROUP_W] = jnp.dot(
            hbf_ref[...], w_in_ref[:, c0:c0 + GROUP_W], preferred_element_type=F32)
        if step == 1:
            for s in range(n_seq):
                pad2_ref[s, PAD:PAD + seq_len, :] = proj_ref[s * seq_len:(s + 1) * seq_len, COL_C:COL_D]
        for _ in range(per_step):
            if conv_next < N_CONV_TILES:
                conv_tile(conv_next)
                conv_next += 1
            if step >= 1 and pool_next < N_CONV_TILES:
                pool_tile(pool_next)
                pool_next += 1
    assert conv_next == N_CONV_TILES and pool_next == N_CONV_TILES

    def ln_pw_body(i, _):
        r0 = pl.multiple_of(i * ROW_TILE, ROW_TILE)
        c = tmp_ref[pl.ds(r0, ROW_TILE), :]
        mu = jnp.mean(c, axis=-1, keepdims=True)
        cen = c - mu
        var = jnp.mean(cen * cen, axis=-1, keepdims=True)
        hn = cen * lax.rsqrt(var + EPS) * lng_ref[...] + lnb_ref[...]
        ya = jnp.dot(_silu(hn).astype(BF16), pw_ref[...], preferred_element_type=F32)
        cat_ref[pl.ds(r0, ROW_TILE), 0:GROUP_W] = ya.astype(BF16)
        return 0
    lax.fori_loop(0, TB // ROW_TILE, ln_pw_body, 0, unroll=2)

    def gmlp_body(i, _):
        r0 = pl.multiple_of(i * CHUNK, CHUNK)
        u = proj_ref[pl.ds(r0, CHUNK), COL_B:COL_B + GROUP_W]
        v = proj_ref[pl.ds(r0, CHUNK), COL_B + GROUP_W:COL_C]
        vstack = _head_stack(v, lane_head).astype(BF16)
        sg = jnp.dot(wcat_ref[...], vstack, preferred_element_type=F32) + gbias_ref[...]
        cat_ref[pl.ds(r0, CHUNK), GROUP_W:2 * GROUP_W] = (u * sg).astype(BF16)
        return 0
    lax.fori_loop(0, N_CHUNKS, gmlp_body, 0, unroll=4)

    def pool_mix_body(i, _):
        r0 = pl.multiple_of(i * ROW_TILE, ROW_TILE)
        yc = jnp.dot(tmp2_ref[pl.ds(r0, ROW_TILE), :].astype(BF16), pbd_ref[...],
                     preferred_element_type=F32) * pscale_ref[...]
        cat_ref[pl.ds(r0, ROW_TILE), 2 * GROUP_W:3 * GROUP_W] = yc.astype(BF16)
        return 0
    lax.fori_loop(0, TB // ROW_TILE, pool_mix_body, 0, unroll=2)

    ri = lax.broadcasted_iota(jnp.int32, (CHUNK, HEADS * CHUNK), 0)
    ci = lax.broadcasted_iota(jnp.int32, (CHUNK, HEADS * CHUNK), 1)
    cj = ci % CHUNK
    chead = ci // CHUNK
    rq = lax.broadcasted_iota(jnp.int32, (CHUNK, GROUP_W), 0).astype(F32)
    for d in range(2):
        lgs = [lg_ref[d * HEADS + h] for h in range(HEADS)]
        lg_wide = jnp.where(chead == 0, lgs[0], jnp.where(chead == 1, lgs[1],
                                                          jnp.where(chead == 2, lgs[2], lgs[3])))
        lg_lane = jnp.where(lane_head == 0, lgs[0], jnp.where(lane_head == 1, lgs[1],
                                                              jnp.where(lane_head == 2, lgs[2], lgs[3])))
        dist = (ri - cj) if d == 0 else (cj - ri)
        keep = dist >= 0
        dcat_ref[d] = jnp.where(keep, jnp.exp(jnp.where(keep, dist, 0).astype(F32) * lg_wide), 0.0)
        if d == 0:
            qdec_ref[d] = jnp.exp((rq + 1.0) * lg_lane)
            kdec_ref[d] = jnp.exp((CHUNK - 1.0 - rq) * lg_lane)
        else:
            qdec_ref[d] = jnp.exp((CHUNK - rq) * lg_lane)
            kdec_ref[d] = jnp.exp(rq * lg_lane)
        sdec_ref[d] = jnp.exp(float(CHUNK) * lg_lane)

    rr = lax.broadcasted_iota(jnp.int32, (GROUP_W, GROUP_W), 0) // HEAD_D
    cc = lax.broadcasted_iota(jnp.int32, (GROUP_W, GROUP_W), 1) // HEAD_D
    gmat_ref[...] = jnp.where(rr == cc, 1.0 / HEAD_D, 0.0).astype(BF16)

    lane_bit = (lane & 16) == 0
    k_scale = HEAD_D ** -0.5

    def rope(z, r0):
        if not rotate:
            return z
        cos = cos_ref[pl.ds(r0, CHUNK), :]
        sin = sin_ref[pl.ds(r0, CHUNK), :]
        halves = []
        for c0 in (0, 128):
            zz = z[:, c0:c0 + 128]
            partner = jnp.where(lane_bit[:, c0:c0 + 128],
                                pltpu.roll(zz, 128 - 16, axis=1), pltpu.roll(zz, 16, axis=1))
            halves.append(partner)
        return z * cos + jnp.concatenate(halves, axis=1) * sin

    def pair_stack(zb, pair):
        zero = jnp.zeros_like(zb)
        return jnp.concatenate([jnp.where(lane_head == h, zb, zero) for h in (2 * pair, 2 * pair + 1)], axis=0)

    def intra_body(c, _):
        r0 = pl.multiple_of(c * CHUNK, CHUNK)
        v = proj_ref[pl.ds(r0, CHUNK), COL_V:COL_V + GROUP_W]
        vb = v.astype(BF16)
        vstacks = [pair_stack(vb, pair) for pair in range(2)]
        o = None
        for d in range(2):
            qc0 = COL_D + 2 * d * GROUP_W
            q = rope(proj_ref[pl.ds(r0, CHUNK), qc0:qc0 + GROUP_W], r0)
            k = rope(proj_ref[pl.ds(r0, CHUNK), qc0 + GROUP_W:qc0 + 2 * GROUP_W], r0) * k_scale
            qb = q.astype(BF16)
            kb = k.astype(BF16)
            qb_ref[d, pl.ds(r0, CHUNK), :] = qb
            for pair in range(2):
                att = lax.dot_general(qb, pair_stack(kb, pair), (((1,), (1,)), ((), ())),
                                      preferred_element_type=F32)
                att = (att * dcat_ref[d, :, pair * 2 * CHUNK:(pair + 1) * 2 * CHUNK]).astype(BF16)
                od = jnp.dot(att, vstacks[pair], preferred_element_type=F32)
                o = od if o is None else o + od
            kd = (k * kdec_ref[d]).astype(BF16)
            upd = lax.dot_general(kd, vb, (((0,), (0,)), ((), ())), preferred_element_type=F32)
            compact = None
            for h in range(HEADS):
                part = jnp.where(lane_head == h, upd[h * HEAD_D:(h + 1) * HEAD_D, :], 0.0)
                compact = part if compact is None else compact + part
            upd_ref[d, c] = compact
        of_ref[pl.ds(r0, CHUNK), :] = o
        return 0
    lax.fori_loop(0, N_CHUNKS, intra_body, 0, unroll=2)

    for s in range(n_seq):
        for d in range(2):
            st = s0_ref[d] if rotate else jnp.zeros((HEAD_D, GROUP_W), F32)
            order = range(n_chunk) if d == 0 else range(n_chunk - 1, -1, -1)
            for c in order:
                cg = s * n_chunk + c
                for h in range(HEADS):
                    sall_ref[d, cg, h * HEAD_D:(h + 1) * HEAD_D, :] = (
                        jnp.where(lane_head == h, st, 0.0).astype(BF16))
                st = st * sdec_ref[d] + upd_ref[d, cg]
            if not rotate:
                for h in range(HEADS):
                    st_ref[s, d, h] = st[:, h * HEAD_D:(h + 1) * HEAD_D]

    def cross_body(c, _):
        r0 = pl.multiple_of(c * CHUNK, CHUNK)
        o = of_ref[pl.ds(r0, CHUNK), :]
        for d in range(2):
            o = o + jnp.dot(qb_ref[d, pl.ds(r0, CHUNK), :], sall_ref[d, c],
                            preferred_element_type=F32) * qdec_ref[d]
        of_ref[pl.ds(r0, CHUNK), :] = o
        return 0
    lax.fori_loop(0, N_CHUNKS, cross_body, 0, unroll=2)

    def center_body(i, _):
        r0 = pl.multiple_of(i * ROW_TILE, ROW_TILE)
        o = of_ref[pl.ds(r0, ROW_TILE), :]
        gmat = gmat_ref[...]
        o_hi = o.astype(BF16)
        o_lo = (o - o_hi.astype(F32)).astype(BF16)
        mu = (jnp.dot(o_hi, gmat, preferred_element_type=F32)
              + jnp.dot(o_lo, gmat, preferred_element_type=F32))
        of_ref[pl.ds(r0, ROW_TILE), :] = o - mu
        return 0
    lax.fori_loop(0, TB // ROW_TILE, center_body, 0, unroll=2)

    def gate_body(i, _):
        r0 = pl.multiple_of(i * ROW_TILE, ROW_TILE)
        cen = of_ref[pl.ds(r0, ROW_TILE), :]
        var = jnp.dot((cen * cen).astype(BF16), gmat_ref[...], preferred_element_type=F32)
        on = cen * lax.rsqrt(var + EPS)
        g = proj_ref[pl.ds(r0, ROW_TILE), COL_G:COL_G + GROUP_W]
        cat_ref[pl.ds(r0, ROW_TILE), 3 * GROUP_W:4 * GROUP_W] = (_silu(g) * on).astype(BF16)
        return 0
    lax.fori_loop(0, TB // ROW_TILE, gate_body, 0, unroll=2)

    def out_body(i, _):
        r0 = pl.multiple_of(i * ROW_TILE, ROW_TILE)
        y = jnp.dot(cat_ref[pl.ds(r0, ROW_TILE), :], w_out_ref[...], preferred_element_type=F32)
        y_ref[pl.ds(r0, ROW_TILE), :] = x_ref[pl.ds(r0, ROW_TILE), :] + ga1 * y
        return 0
    lax.fori_loop(0, TB // ROW_TILE, out_body, 0, unroll=2)


def _mixer(x, mod, lw, *, seq_len, rotate, mod_base, mod_stride, rope_tabs=None, s0=None):
    nb = x.shape[0]
    n_seq = TB // seq_len
    in_specs = [
        pl.BlockSpec((None, TB, D_MODEL), lambda i: (i, 0, 0)),
        pl.BlockSpec((None, 1, 6 * D_MODEL), lambda i: (jnp.maximum(mod_base + mod_stride * i, 0), 0, 0)),
        _const_spec((1, D_MODEL)),
        _const_spec((D_MODEL, IN_COLS)),
        _const_spec((D_MODEL, D_MODEL)),
        _const_spec((CONV_W + 1, GROUP_W)),
        _const_spec((1, GROUP_W)), _const_spec((1, GROUP_W)), _const_spec((1, GROUP_W)),
        _const_spec((GROUP_W, GROUP_W)),
        _const_spec((CHUNK, HEADS * CHUNK)),
        _const_spec((CHUNK, GROUP_W)),
        _const_spec((GROUP_W, GROUP_W)),
        _const_spec((1, GROUP_W)),
        pl.BlockSpec(memory_space=pltpu.SMEM),
    ]
    args = [x, mod, lw["g1"], lw["w_in"], lw["w_out"], lw["dw"], lw["cb"], lw["lng"], lw["lnb"],
            lw["pw"], lw["wcat"], lw["gbias"], lw["pbd"], lw["pscale"], lw["lg"]]
    out_shape = [jax.ShapeDtypeStruct((nb, TB, D_MODEL), F32)]
    out_specs = [pl.BlockSpec((None, TB, D_MODEL), lambda i: (i, 0, 0))]
    if rotate:
        in_specs += [_const_spec((TB, GROUP_W)), _const_spec((TB, GROUP_W)),
                     pl.BlockSpec((None, 2, HEAD_D, GROUP_W), lambda i: (i, 0, 0, 0))]
        args += [rope_tabs[0], rope_tabs[1], s0]
    else:
        out_shape.append(jax.ShapeDtypeStruct((nb * n_seq, 2, HEADS, HEAD_D, HEAD_D), F32))
        out_specs.append(pl.BlockSpec((n_seq, 2, HEADS, HEAD_D, HEAD_D), lambda i: (i, 0, 0, 0, 0)))
    scratch = [
        pltpu.VMEM((TB, D_MODEL), BF16),
        pltpu.VMEM((TB, IN_COLS), F32),
        pltpu.VMEM((TB, D_MODEL), BF16),
        pltpu.VMEM((n_seq, seq_len + 2 * PAD, GROUP_W), F32),
        pltpu.VMEM((n_seq, seq_len + 2 * PAD, GROUP_W), F32),
        pltpu.VMEM((TB, GROUP_W), F32),
        pltpu.VMEM((TB, GROUP_W), F32),
        pltpu.VMEM((TB, GROUP_W), F32),
        pltpu.VMEM((2, CHUNK, HEADS * CHUNK), F32),
        pltpu.VMEM((2, CHUNK, GROUP_W), F32),
        pltpu.VMEM((2, CHUNK, GROUP_W), F32),
        pltpu.VMEM((2, 1, GROUP_W), F32),
        pltpu.VMEM((GROUP_W, GROUP_W), BF16),
        pltpu.VMEM((2, TB, GROUP_W), BF16),
        pltpu.VMEM((2, N_CHUNKS, HEAD_D, GROUP_W), F32),
        pltpu.VMEM((2, N_CHUNKS, GROUP_W, GROUP_W), BF16),
    ]
    outs = pl.pallas_call(
        functools.partial(_mixer_kernel, seq_len=seq_len, rotate=rotate),
        grid=(nb,),
        in_specs=in_specs,
        out_specs=out_specs,
        out_shape=out_shape,
        scratch_shapes=scratch,
        compiler_params=pltpu.CompilerParams(
            dimension_semantics=("arbitrary",), vmem_limit_bytes=VMEM_LIMIT_BYTES),
        name="mixer_lat" if rotate else "mixer_ctx",
    )(*args)
    return outs


def _ffn_kernel(x_ref, mod_ref, g2_ref, w_in_hbm, w_out_hbm, gf_ref, y_ref,
                hbf_ref, act_ref, wgu_ref, wo_ref, stg_g_ref, stg_u_ref, stg_o_ref, sem,
                *, layer, final_norm):
    pid = pl.program_id(0)
    sh2 = mod_ref[:, 3 * D_MODEL:4 * D_MODEL]
    sc2 = mod_ref[:, 4 * D_MODEL:5 * D_MODEL]
    ga2 = mod_ref[:, 5 * D_MODEL:6 * D_MODEL]

    def tile_copies(j, slot):
        return (
            pltpu.make_async_copy(w_in_hbm.at[layer, :, pl.ds(j * FF_TILE, FF_TILE)],
                                  stg_g_ref.at[slot], sem.at[0, slot]),
            pltpu.make_async_copy(w_in_hbm.at[layer, :, pl.ds(D_FF + j * FF_TILE, FF_TILE)],
                                  stg_u_ref.at[slot], sem.at[1, slot]),
            pltpu.make_async_copy(w_out_hbm.at[layer, pl.ds(j * FF_TILE, FF_TILE), :],
                                  stg_o_ref.at[slot], sem.at[2, slot]),
        )

    def start_tile(j):
        for cp in tile_copies(j, j % FF_RING):
            cp.start()

    @pl.when(pid == 0)
    def _():
        for j in range(FF_LOOKAHEAD):
            start_tile(j)

    def norm_body(i, _):
        r0 = pl.multiple_of(i * ROW_TILE, ROW_TILE)
        h = _norm_mod(x_ref[pl.ds(r0, ROW_TILE), :], g2_ref[...], sc2, sh2)
        hbf_ref[pl.ds(r0, ROW_TILE), :] = h.astype(BF16)
        return 0
    lax.fori_loop(0, TB // ROW_TILE, norm_body, 0)

    def hidden_tiles(fetch):
        for j in range(N_FF_TILES):
            gcols = slice(j * FF_TILE, (j + 1) * FF_TILE)
            ucols = slice(D_FF + j * FF_TILE, D_FF + (j + 1) * FF_TILE)
            if fetch:
                if j + FF_LOOKAHEAD < N_FF_TILES:
                    start_tile(j + FF_LOOKAHEAD)
                slot = j % FF_RING
                for cp in tile_copies(j, slot):
                    cp.wait()
                wgu_ref[:, gcols] = stg_g_ref[slot].astype(BF16)
                wgu_ref[:, ucols] = stg_u_ref[slot].astype(BF16)
                wo_ref[gcols, :] = stg_o_ref[slot].astype(BF16)
            gate = jnp.dot(hbf_ref[...], wgu_ref[:, gcols], preferred_element_type=F32)
            up = jnp.dot(hbf_ref[...], wgu_ref[:, ucols], preferred_element_type=F32)
            act_ref[:, gcols] = (_silu(gate) * up).astype(BF16)

    @pl.when(pid == 0)
    def _():
        hidden_tiles(True)

    @pl.when(pid != 0)
    def _():
        hidden_tiles(False)

    y_ref[...] = jnp.dot(act_ref[...], wo_ref[...], preferred_element_type=F32)

    def out_body(i, _):
        r0 = pl.multiple_of(i * ROW_TILE, ROW_TILE)
        y = x_ref[pl.ds(r0, ROW_TILE), :] + ga2 * y_ref[pl.ds(r0, ROW_TILE), :]
        if final_norm:
            ms = jnp.mean(y * y, axis=-1, keepdims=True)
            y = y * lax.rsqrt(ms + EPS) * gf_ref[...]
        y_ref[pl.ds(r0, ROW_TILE), :] = y
        return 0
    lax.fori_loop(0, TB // ROW_TILE, out_body, 0)


def _ffn(x, mod, g2, w_ffn_in, w_ffn_out, g_final, *, layer, mod_base, mod_stride, final_norm):
    nb = x.shape[0]
    return pl.pallas_call(
        functools.partial(_ffn_kernel, layer=layer, final_norm=final_norm),
        grid=(nb,),
        in_specs=[
            pl.BlockSpec((None, TB, D_MODEL), lambda i: (i, 0, 0)),
            pl.BlockSpec((None, 1, 6 * D_MODEL), lambda i: (jnp.maximum(mod_base + mod_stride * i, 0), 0, 0)),
            _const_spec((1, D_MODEL)),
            pl.BlockSpec(memory_space=pl.ANY),
            pl.BlockSpec(memory_space=pl.ANY),
            _const_spec((1, D_MODEL)),
        ],
        out_specs=pl.BlockSpec((None, TB, D_MODEL), lambda i: (i, 0, 0)),
        out_shape=jax.ShapeDtypeStruct((nb, TB, D_MODEL), F32),
        scratch_shapes=[
            pltpu.VMEM((TB, D_MODEL), BF16),
            pltpu.VMEM((TB, D_FF), BF16),
            pltpu.VMEM((D_MODEL, 2 * D_FF), BF16),
            pltpu.VMEM((D_FF, D_MODEL), BF16),
            pltpu.VMEM((FF_RING, D_MODEL, FF_TILE), F32),
            pltpu.VMEM((FF_RING, D_MODEL, FF_TILE), F32),
            pltpu.VMEM((FF_RING, FF_TILE, D_MODEL), F32),
            pltpu.SemaphoreType.DMA((3, FF_RING)),
        ],
        compiler_params=pltpu.CompilerParams(
            dimension_semantics=("arbitrary",), vmem_limit_bytes=VMEM_LIMIT_BYTES),
        name="ffn",
    )(x, mod, g2, w_ffn_in, w_ffn_out, g_final)


def _rope_tables(seq_len):
    t = jnp.arange(seq_len)
    r = (t // GRID_W).astype(F32)
    c = (t % GRID_W).astype(F32)
    nf = HEAD_D // 4
    inv = ROPE_BASE ** (-jnp.arange(nf, dtype=F32) / nf)
    ang_r = r[:, None] * inv
    ang_c = c[:, None] * inv
    cos = jnp.concatenate([jnp.cos(ang_r), jnp.cos(ang_r), jnp.cos(ang_c), jnp.cos(ang_c)], axis=-1)
    sin = jnp.concatenate([-jnp.sin(ang_r), jnp.sin(ang_r), -jnp.sin(ang_c), jnp.sin(ang_c)], axis=-1)
    return jnp.tile(cos, (1, HEADS)), jnp.tile(sin, (1, HEADS))


def _layer_weights(l, g_norm1, g_norm2, w_in, w_out, conv_dw, conv_b, conv_ln_g, conv_ln_b, conv_pw,
                   gmlp_ws, gmlp_b, pool_w, pool_scale, ret_decay, w_ffn_in, w_ffn_out):
    return {
        "g1": g_norm1[l].reshape(1, D_MODEL),
        "g2": g_norm2[l].reshape(1, D_MODEL),
        "w_in": w_in[l].astype(BF16),
        "w_out": w_out[l].astype(BF16),
        "dw": jnp.concatenate([conv_dw[l], jnp.zeros((1, GROUP_W), F32)], axis=0),
        "cb": conv_b[l].reshape(1, GROUP_W),
        "lng": conv_ln_g[l].reshape(1, GROUP_W),
        "lnb": conv_ln_b[l].reshape(1, GROUP_W),
        "pw": conv_pw[l].astype(BF16),
        "wcat": jnp.transpose(gmlp_ws[l], (1, 0, 2)).reshape(CHUNK, HEADS * CHUNK).astype(BF16),
        "gbias": jnp.repeat(gmlp_b[l].T, HEAD_D, axis=1),
        "pbd": jax.scipy.linalg.block_diag(*[pool_w[l, g] for g in range(len(POOL_WINDOWS))]).astype(BF16),
        "pscale": pool_scale[l].reshape(1, GROUP_W),
        "lg": jax.nn.log_sigmoid(ret_decay[l].astype(F32)).reshape(2 * HEADS),
    }


def kernel(x_prompt, x_sample, state_ret, c, c_ctx, w_ada, b_ada, g_norm1, g_norm2, w_in, w_out, conv_dw,
           conv_b, conv_ln_g, conv_ln_b, conv_pw, gmlp_ws, gmlp_b, pool_w, pool_scale, ret_decay, w_ffn_in,
           w_ffn_out, g_final):
    batch, seq, _ = x_prompt.shape
    dec_batch, dec_seq, _ = x_sample.shape
    assert dec_seq == TB and TB % seq == 0 and (batch * seq) % TB == 0
    assert 1 + dec_batch <= MOD_ROWS

    cs = jnp.concatenate([c_ctx[None, :], c, jnp.zeros((MOD_ROWS - 1 - dec_batch, D_MODEL), F32)], axis=0)
    mods = _ada_rows(cs, w_ada, b_ada).reshape(DEPTH, MOD_ROWS, 1, 6 * D_MODEL)
    rope_tabs = _rope_tables(dec_seq)
    g_final2 = g_final.reshape(1, D_MODEL)
    s0_all = jnp.transpose(state_ret.astype(F32), (0, 1, 2, 4, 3, 5)).reshape(
        dec_batch, DEPTH, 2, HEAD_D, GROUP_W)

    xc = x_prompt.reshape(batch * seq // TB, TB, D_MODEL)
    xl = x_sample
    states = []
    for l in range(DEPTH):
        lw = _layer_weights(l, g_norm1, g_norm2, w_in, w_out, conv_dw, conv_b, conv_ln_g, conv_ln_b,
                            conv_pw, gmlp_ws, gmlp_b, pool_w, pool_scale, ret_decay, w_ffn_in, w_ffn_out)
        last = l == DEPTH - 1
        xc, st = _mixer(xc, mods[l], lw, seq_len=seq, rotate=False, mod_base=0, mod_stride=0)
        states.append(st)
        xc = _ffn(xc, mods[l], lw["g2"], w_ffn_in, w_ffn_out, g_final2, layer=l, mod_base=0, mod_stride=0,
                  final_norm=last)
        (xl,) = _mixer(xl, mods[l], lw, seq_len=dec_seq, rotate=True, mod_base=1, mod_stride=1,
                       rope_tabs=rope_tabs, s0=s0_all[:, l])
        xl = _ffn(xl, mods[l], lw["g2"], w_ffn_in, w_ffn_out, g_final2, layer=l, mod_base=1, mod_stride=1,
                  final_norm=last)

    y_prompt = xc.reshape(batch, seq, D_MODEL)
    new_state = jnp.stack(states, axis=1).astype(x_prompt.dtype)
    return (y_prompt, xl, new_state)
```

```python
import functools

import jax
import jax.numpy as jnp
from jax import lax
from jax.experimental import pallas as pl
from jax.experimental.pallas import tpu as pltpu

F32 = jnp.float32
BF16 = jnp.bfloat16

D_MODEL = 1024
DEPTH = 2
GRID_W = 64
GROUP_W = D_MODEL // 4
CONV_W = 31
CHUNK = 128
HEADS = 4
HEAD_D = GROUP_W // HEADS
POOL_WINDOWS = (2, 4, 8, 16)
ROPE_BASE = 10000.0
D_FF = 2816
IN_COLS = 11 * GROUP_W
EPS = 1e-6

TB = 1024
N_CHUNKS = TB // CHUNK
ROW_TILE = 256
CONV_TILE = 64
N_CONV_TILES = TB // CONV_TILE
PAD = 16
FF_TILE = 256
N_FF_TILES = D_FF // FF_TILE
FF_LOOKAHEAD = 2
FF_RING = FF_LOOKAHEAD + 1
ADA_TILE = 1536
MOD_ROWS = 8
VMEM_LIMIT_BYTES = 60 * 1024 * 1024

COL_A, COL_B, COL_C, COL_D = 0, 2 * GROUP_W, 4 * GROUP_W, 5 * GROUP_W
COL_V, COL_G = COL_D + 4 * GROUP_W, COL_D + 5 * GROUP_W


def _sigmoid(x):
    return 1.0 / (1.0 + jnp.exp(-x))


def _silu(x):
    return x * _sigmoid(x)


def _norm_mod(x, g, scale, shift):
    ms = jnp.mean(x * x, axis=-1, keepdims=True)
    return (x * lax.rsqrt(ms + EPS) * g) * (1.0 + scale) + shift


def _head_stack(x, lane_head):
    return jnp.concatenate([jnp.where(lane_head == h, x, 0.0) for h in range(HEADS)], axis=0)


def _const_spec(shape):
    zeros = (0,) * len(shape)
    return pl.BlockSpec(shape, lambda i: zeros, pipeline_mode=pl.Buffered(1))


def _ada_kernel(c_ref, w_ref, b_ref, o_ref):
    a = _silu(c_ref[...]).astype(BF16)
    o_ref[...] = jnp.dot(a, w_ref[...].astype(BF16), preferred_element_type=F32) + b_ref[...]


def _ada_rows(cs, w_ada, b_ada):
    n_tiles = 6 * D_MODEL // ADA_TILE
    return pl.pallas_call(
        _ada_kernel,
        grid=(DEPTH, n_tiles),
        in_specs=[
            pl.BlockSpec((MOD_ROWS, D_MODEL), lambda l, j: (0, 0)),
            pl.BlockSpec((None, D_MODEL, ADA_TILE), lambda l, j: (l, 0, j)),
            pl.BlockSpec((None, 1, ADA_TILE), lambda l, j: (l, 0, j)),
        ],
        out_specs=pl.BlockSpec((None, MOD_ROWS, ADA_TILE), lambda l, j: (l, 0, j)),
        out_shape=jax.ShapeDtypeStruct((DEPTH, MOD_ROWS, 6 * D_MODEL), F32),
        compiler_params=pltpu.CompilerParams(
            dimension_semantics=("arbitrary", "arbitrary"), vmem_limit_bytes=VMEM_LIMIT_BYTES),
        name="ada_rows",
    )(cs, w_ada, b_ada.reshape(DEPTH, 1, 6 * D_MODEL))


def _mixer_kernel(*refs, seq_len, rotate):
    n_seq = TB // seq_len
    n_chunk = seq_len // CHUNK
    it = iter(refs)
    x_ref, mod_ref, g1_ref, w_in_ref, w_out_ref = (next(it) for _ in range(5))
    dw_ref, cb_ref, lng_ref, lnb_ref, pw_ref = (next(it) for _ in range(5))
    wcat_ref, gbias_ref, pbd_ref, pscale_ref, lg_ref = (next(it) for _ in range(5))
    if rotate:
        cos_ref, sin_ref, s0_ref = (next(it) for _ in range(3))
    y_ref = next(it)
    if not rotate:
        st_ref = next(it)
    (hbf_ref, proj_ref, cat_ref, pad_ref, pad2_ref, tmp_ref, tmp2_ref, of_ref,
     dcat_ref, qdec_ref, kdec_ref, sdec_ref, gmat_ref, qb_ref, upd_ref, sall_ref) = it

    sh1 = mod_ref[:, 0:D_MODEL]
    sc1 = mod_ref[:, D_MODEL:2 * D_MODEL]
    ga1 = mod_ref[:, 2 * D_MODEL:3 * D_MODEL]

    lane = lax.broadcasted_iota(jnp.int32, (1, GROUP_W), 1)
    lane_head = lane // HEAD_D
    tiles_per_seq = seq_len // CONV_TILE
    chunks_per_seq = seq_len // CHUNK

    def seq_and_offset(i, per_seq, size):
        if n_seq == 1:
            return 0, pl.multiple_of(i * size, size)
        return i // per_seq, pl.multiple_of((i % per_seq) * size, size)

    def norm_body(i, _):
        r0 = pl.multiple_of(i * ROW_TILE, ROW_TILE)
        h = _norm_mod(x_ref[pl.ds(r0, ROW_TILE), :], g1_ref[...], sc1, sh1).astype(BF16)
        hbf_ref[pl.ds(r0, ROW_TILE), :] = h
        proj_ref[pl.ds(r0, ROW_TILE), COL_A:COL_B] = jnp.dot(
            h, w_in_ref[:, COL_A:COL_B], preferred_element_type=F32)
        return 0
    lax.fori_loop(0, TB // ROW_TILE, norm_body, 0, unroll=2)

    for s in range(n_seq):
        for ref in (pad_ref, pad2_ref):
            ref[s, 0:PAD, :] = jnp.zeros((PAD, GROUP_W), F32)
            ref[s, PAD + seq_len:PAD + seq_len + PAD, :] = jnp.zeros((PAD, GROUP_W), F32)

    def glu_body(i, _):
        s, t0 = seq_and_offset(i, chunks_per_seq, CHUNK)
        r0 = pl.multiple_of(i * CHUNK, CHUNK)
        a1 = proj_ref[pl.ds(r0, CHUNK), COL_A:COL_A + GROUP_W]
        a2 = proj_ref[pl.ds(r0, CHUNK), COL_A + GROUP_W:COL_B]
        pad_ref[s, pl.ds(pl.multiple_of(t0 + PAD, 8), CHUNK), :] = a1 * _sigmoid(a2)
        return 0
    lax.fori_loop(0, N_CHUNKS, glu_body, 0, unroll=2)

    win = CONV_TILE + 2 * PAD

    def conv_tile(i):
        s, t0 = divmod(i, tiles_per_seq)
        t0 *= CONV_TILE
        for c0 in range(0, GROUP_W, 128):
            w = pad_ref[s, t0:t0 + win, c0:c0 + 128]
            acc = jnp.zeros((CONV_TILE, 128), F32) + cb_ref[:, c0:c0 + 128]
            for b in range(8):
                wb = w if b == 0 else pltpu.roll(w, win - b, axis=0)
                for a in range(4):
                    k = 8 * a + b - 1
                    if 0 <= k < CONV_W:
                        acc = acc + dw_ref[k:k + 1, c0:c0 + 128] * wb[8 * a:8 * a + CONV_TILE, :]
            tmp_ref[i * CONV_TILE:(i + 1) * CONV_TILE, c0:c0 + 128] = acc

    pwin = CONV_TILE + 16
    lane128 = lax.broadcasted_iota(jnp.int32, (1, 128), 1)
    first_half = lane128 < HEAD_D

    def pool_tile(i):
        s, t0 = divmod(i, tiles_per_seq)
        t0 *= CONV_TILE
        interior = t0 >= 8 and t0 + CONV_TILE + 8 <= seq_len
        outs = []
        for col, (w_small, w_big) in enumerate(((2, 4), (8, 16))):
            w = pad2_ref[s, t0 + PAD - 8:t0 + PAD - 8 + pwin, col * 128:(col + 1) * 128]

            rolled = {0: w}

            def shifted(b, w=w, rolled=rolled):
                if b not in rolled:
                    rolled[b] = pltpu.roll(w, pwin - b, axis=0)
                return rolled[b]

            def lo(b):
                return shifted(b)[0:CONV_TILE, :]

            def hi(b):
                return shifted(b)[8:8 + CONV_TILE, :]

            tok = hi(0)
            if col == 0:
                s_small = lo(7) + tok
                s_big = s_small + lo(6) + hi(1)
            else:
                s_small = lo(7) + tok + lo(6) + hi(1) + lo(5) + lo(4) + hi(2) + hi(3)
                s_big = s_small + lo(3) + lo(2) + lo(1) + lo(0) + hi(4) + hi(5) + hi(6) + hi(7)
            ssum = jnp.where(first_half, s_small, s_big)
            if interior:
                mean = ssum * jnp.where(first_half, 1.0 / w_small, 1.0 / w_big)
            else:
                half = jnp.where(first_half, w_small // 2, w_big // 2)
                tpos = t0 + lax.broadcasted_iota(jnp.int32, (CONV_TILE, 128), 0)
                cnt = jnp.minimum(tpos + half, seq_len) - jnp.maximum(tpos - half, 0)
                mean = ssum / cnt.astype(F32)
            outs.append(mean - tok)
        tmp2_ref[i * CONV_TILE:(i + 1) * CONV_TILE, :] = jnp.concatenate(outs, axis=1)

    col_tiles = [COL_C] + list(range(COL_B, COL_C, GROUP_W)) + list(range(COL_D, IN_COLS, GROUP_W))
    per_step = -(-N_CONV_TILES // (len(col_tiles) - 1))
    conv_next = 0
    pool_next = 0
    for step, c0 in enumerate(col_tiles):
        proj_ref[:, c0:c0 + GROUP_W] = jnp.dot(
            hbf_ref[...], w_in_ref[:, c0:c0 + GROUP_W], preferred_element_type=F32)
        if step == 1:
            for s in range(n_seq):
                pad2_ref[s, PAD:PAD + seq_len, :] = proj_ref[s * seq_len:(s + 1) * seq_len, COL_C:COL_D]
        for _ in range(per_step):
            if conv_next < N_CONV_TILES:
                conv_tile(conv_next)
                conv_next += 1
            if step >= 1 and pool_next < N_CONV_TILES:
                pool_tile(pool_next)
                pool_next += 1
    assert conv_next == N_CONV_TILES and pool_next == N_CONV_TILES

    def ln_pw_body(i, _):
        r0 = pl.multiple_of(i * ROW_TILE, ROW_TILE)
        c = tmp_ref[pl.ds(r0, ROW_TILE), :]
        mu = jnp.mean(c, axis=-1, keepdims=True)
        cen = c - mu
        var = jnp.mean(cen * cen, axis=-1, keepdims=True)
        hn = cen * lax.rsqrt(var + EPS) * lng_ref[...] + lnb_ref[...]
        ya = jnp.dot(_silu(hn).astype(BF16), pw_ref[...], preferred_element_type=F32)
        cat_ref[pl.ds(r0, ROW_TILE), 0:GROUP_W] = ya.astype(BF16)
        return 0
    lax.fori_loop(0, TB // ROW_TILE, ln_pw_body, 0, unroll=2)

    def gmlp_body(i, _):
        r0 = pl.multiple_of(i * CHUNK, CHUNK)
        u = proj_ref[pl.ds(r0, CHUNK), COL_B:COL_B + GROUP_W]
        v = proj_ref[pl.ds(r0, CHUNK), COL_B + GROUP_W:COL_C]
        vstack = _head_stack(v, lane_head).astype(BF16)
        sg = jnp.dot(wcat_ref[...], vstack, preferred_element_type=F32) + gbias_ref[...]
        cat_ref[pl.ds(r0, CHUNK), GROUP_W:2 * GROUP_W] = (u * sg).astype(BF16)
        return 0
    lax.fori_loop(0, N_CHUNKS, gmlp_body, 0, unroll=4)

    def pool_mix_body(i, _):
        r0 = pl.multiple_of(i * ROW_TILE, ROW_TILE)
        yc = jnp.dot(tmp2_ref[pl.ds(r0, ROW_TILE), :].astype(BF16), pbd_ref[...],
                     preferred_element_type=F32) * pscale_ref[...]
        cat_ref[pl.ds(r0, ROW_TILE), 2 * GROUP_W:3 * GROUP_W] = yc.astype(BF16)
        return 0
    lax.fori_loop(0, TB // ROW_TILE, pool_mix_body, 0, unroll=2)

    ri = lax.broadcasted_iota(jnp.int32, (CHUNK, HEADS * CHUNK), 0)
    ci = lax.broadcasted_iota(jnp.int32, (CHUNK, HEADS * CHUNK), 1)
    cj = ci % CHUNK
    chead = ci // CHUNK
    rq = lax.broadcasted_iota(jnp.int32, (CHUNK, GROUP_W), 0).astype(F32)
    for d in range(2):
        lgs = [lg_ref[d * HEADS + h] for h in range(HEADS)]
        lg_wide = jnp.where(chead == 0, lgs[0], jnp.where(chead == 1, lgs[1],
                                                          jnp.where(chead == 2, lgs[2], lgs[3])))
        lg_lane = jnp.where(lane_head == 0, lgs[0], jnp.where(lane_head == 1, lgs[1],
                                                              jnp.where(lane_head == 2, lgs[2], lgs[3])))
        dist = (ri - cj) if d == 0 else (cj - ri)
        keep = dist >= 0
        dcat_ref[d] = jnp.where(keep, jnp.exp(jnp.where(keep, dist, 0).astype(F32) * lg_wide), 0.0)
        if d == 0:
            qdec_ref[d] = jnp.exp((rq + 1.0) * lg_lane)
            kdec_ref[d] = jnp.exp((CHUNK - 1.0 - rq) * lg_lane)
        else:
            qdec_ref[d] = jnp.exp((CHUNK - rq) * lg_lane)
            kdec_ref[d] = jnp.exp(rq * lg_lane)
        sdec_ref[d] = jnp.exp(float(CHUNK) * lg_lane)

    rr = lax.broadcasted_iota(jnp.int32, (GROUP_W, GROUP_W), 0) // HEAD_D
    cc = lax.broadcasted_iota(jnp.int32, (GROUP_W, GROUP_W), 1) // HEAD_D
    gmat_ref[...] = jnp.where(rr == cc, 1.0 / HEAD_D, 0.0).astype(BF16)

    lane_bit = (lane & 16) == 0
    k_scale = HEAD_D ** -0.5

    def rope(z, r0):
        if not rotate:
            return z
        cos = cos_ref[pl.ds(r0, CHUNK), :]
        sin = sin_ref[pl.ds(r0, CHUNK), :]
        halves = []
        for c0 in (0, 128):
            zz = z[:, c0:c0 + 128]
            partner = jnp.where(lane_bit[:, c0:c0 + 128],
                                pltpu.roll(zz, 128 - 16, axis=1), pltpu.roll(zz, 16, axis=1))
            halves.append(partner)
        return z * cos + jnp.concatenate(halves, axis=1) * sin

    def pair_stack(zb, pair):
        zero = jnp.zeros_like(zb)
        return jnp.concatenate([jnp.where(lane_head == h, zb, zero) for h in (2 * pair, 2 * pair + 1)], axis=0)

    def intra_body(c, _):
        r0 = pl.multiple_of(c * CHUNK, CHUNK)
        v = proj_ref[pl.ds(r0, CHUNK), COL_V:COL_V + GROUP_W]
        vb = v.astype(BF16)
        vstacks = [pair_stack(vb, pair) for pair in range(2)]
        o = None
        for d in range(2):
            qc0 = COL_D + 2 * d * GROUP_W
            q = rope(proj_ref[pl.ds(r0, CHUNK), qc0:qc0 + GROUP_W], r0)
            k = rope(proj_ref[pl.ds(r0, CHUNK), qc0 + GROUP_W:qc0 + 2 * GROUP_W], r0) * k_scale
            qb = q.astype(BF16)
            kb = k.astype(BF16)
            qb_ref[d, pl.ds(r0, CHUNK), :] = qb
            for pair in range(2):
                att = lax.dot_general(qb, pair_stack(kb, pair), (((1,), (1,)), ((), ())),
                                      preferred_element_type=F32)
                att = (att * dcat_ref[d, :, pair * 2 * CHUNK:(pair + 1) * 2 * CHUNK]).astype(BF16)
                od = jnp.dot(att, vstacks[pair], preferred_element_type=F32)
                o = od if o is None else o + od
            kd = (k * kdec_ref[d]).astype(BF16)
            upd = lax.dot_general(kd, vb, (((0,), (0,)), ((), ())), preferred_element_type=F32)
            compact = None
            for h in range(HEADS):
                part = jnp.where(lane_head == h, upd[h * HEAD_D:(h + 1) * HEAD_D, :], 0.0)
                compact = part if compact is None else compact + part
            upd_ref[d, c] = compact
        of_ref[pl.ds(r0, CHUNK), :] = o
        return 0
    lax.fori_loop(0, N_CHUNKS, intra_body, 0, unroll=2)

    for s in range(n_seq):
        for d in range(2):
            st = s0_ref[d] if rotate else jnp.zeros((HEAD_D, GROUP_W), F32)
            order = range(n_chunk) if d == 0 else range(n_chunk - 1, -1, -1)
            for c in order:
                cg = s * n_chunk + c
                for h in range(HEADS):
                    sall_ref[d, cg, h * HEAD_D:(h + 1) * HEAD_D, :] = (
                        jnp.where(lane_head == h, st, 0.0).astype(BF16))
                st = st * sdec_ref[d] + upd_ref[d, cg]
            if not rotate:
                for h in range(HEADS):
                    st_ref[s, d, h] = st[:, h * HEAD_D:(h + 1) * HEAD_D]

    def cross_body(c, _):
        r0 = pl.multiple_of(c * CHUNK, CHUNK)
        o = of_ref[pl.ds(r0, CHUNK), :]
        for d in range(2):
            o = o + jnp.dot(qb_ref[d, pl.ds(r0, CHUNK), :], sall_ref[d, c],
                            preferred_element_type=F32) * qdec_ref[d]
        of_ref[pl.ds(r0, CHUNK), :] = o
        return 0
    lax.fori_loop(0, N_CHUNKS, cross_body, 0, unroll=2)

    def center_body(i, _):
        r0 = pl.multiple_of(i * ROW_TILE, ROW_TILE)
        o = of_ref[pl.ds(r0, ROW_TILE), :]
        gmat = gmat_ref[...]
        o_hi = o.astype(BF16)
        o_lo = (o - o_hi.astype(F32)).astype(BF16)
        mu = (jnp.dot(o_hi, gmat, preferred_element_type=F32)
              + jnp.dot(o_lo, gmat, preferred_element_type=F32))
        of_ref[pl.ds(r0, ROW_TILE), :] = o - mu
        return 0
    lax.fori_loop(0, TB // ROW_TILE, center_body, 0, unroll=2)

    def gate_body(i, _):
        r0 = pl.multiple_of(i * ROW_TILE, ROW_TILE)
        cen = of_ref[pl.ds(r0, ROW_TILE), :]
        var = jnp.dot((cen * cen).astype(BF16), gmat_ref[...], preferred_element_type=F32)
        on = cen * lax.rsqrt(var + EPS)
        g = proj_ref[pl.ds(r0, ROW_TILE), COL_G:COL_G + GROUP_W]
        cat_ref[pl.ds(r0, ROW_TILE), 3 * GROUP_W:4 * GROUP_W] = (_silu(g) * on).astype(BF16)
        return 0
    lax.fori_loop(0, TB // ROW_TILE, gate_body, 0, unroll=2)

    def out_body(i, _):
        r0 = pl.multiple_of(i * ROW_TILE, ROW_TILE)
        y = jnp.dot(cat_ref[pl.ds(r0, ROW_TILE), :], w_out_ref[...], preferred_element_type=F32)
        y_ref[pl.ds(r0, ROW_TILE), :] = x_ref[pl.ds(r0, ROW_TILE), :] + ga1 * y
        return 0
    lax.fori_loop(0, TB // ROW_TILE, out_body, 0, unroll=2)


def _mixer(x, mod, lw, *, seq_len, rotate, mod_base, mod_stride, rope_tabs=None, s0=None):
    nb = x.shape[0]
    n_seq = TB // seq_len
    in_specs = [
        pl.BlockSpec((None, TB, D_MODEL), lambda i: (i, 0, 0)),
        pl.BlockSpec((None, 1, 6 * D_MODEL), lambda i: (jnp.maximum(mod_base + mod_stride * i, 0), 0, 0)),
        _const_spec((1, D_MODEL)),
        _const_spec((D_MODEL, IN_COLS)),
        _const_spec((D_MODEL, D_MODEL)),
        _const_spec((CONV_W + 1, GROUP_W)),
        _const_spec((1, GROUP_W)), _const_spec((1, GROUP_W)), _const_spec((1, GROUP_W)),
        _const_spec((GROUP_W, GROUP_W)),
        _const_spec((CHUNK, HEADS * CHUNK)),
        _const_spec((CHUNK, GROUP_W)),
        _const_spec((GROUP_W, GROUP_W)),
        _const_spec((1, GROUP_W)),
        pl.BlockSpec(memory_space=pltpu.SMEM),
    ]
    args = [x, mod, lw["g1"], lw["w_in"], lw["w_out"], lw["dw"], lw["cb"], lw["lng"], lw["lnb"],
            lw["pw"], lw["wcat"], lw["gbias"], lw["pbd"], lw["pscale"], lw["lg"]]
    out_shape = [jax.ShapeDtypeStruct((nb, TB, D_MODEL), F32)]
    out_specs = [pl.BlockSpec((None, TB, D_MODEL), lambda i: (i, 0, 0))]
    if rotate:
        in_specs += [_const_spec((TB, GROUP_W)), _const_spec((TB, GROUP_W)),
                     pl.BlockSpec((None, 2, HEAD_D, GROUP_W), lambda i: (i, 0, 0, 0))]
        args += [rope_tabs[0], rope_tabs[1], s0]
    else:
        out_shape.append(jax.ShapeDtypeStruct((nb * n_seq, 2, HEADS, HEAD_D, HEAD_D), F32))
        out_specs.append(pl.BlockSpec((n_seq, 2, HEADS, HEAD_D, HEAD_D), lambda i: (i, 0, 0, 0, 0)))
    scratch = [
        pltpu.VMEM((TB, D_MODEL), BF16),
        pltpu.VMEM((TB, IN_COLS), F32),
        pltpu.VMEM((TB, D_MODEL), BF16),
        pltpu.VMEM((n_seq, seq_len + 2 * PAD, GROUP_W), F32),
        pltpu.VMEM((n_seq, seq_len + 2 * PAD, GROUP_W), F32),
        pltpu.VMEM((TB, GROUP_W), F32),
        pltpu.VMEM((TB, GROUP_W), F32),
        pltpu.VMEM((TB, GROUP_W), F32),
        pltpu.VMEM((2, CHUNK, HEADS * CHUNK), F32),
        pltpu.VMEM((2, CHUNK, GROUP_W), F32),
        pltpu.VMEM((2, CHUNK, GROUP_W), F32),
        pltpu.VMEM((2, 1, GROUP_W), F32),
        pltpu.VMEM((GROUP_W, GROUP_W), BF16),
        pltpu.VMEM((2, TB, GROUP_W), BF16),
        pltpu.VMEM((2, N_CHUNKS, HEAD_D, GROUP_W), F32),
        pltpu.VMEM((2, N_CHUNKS, GROUP_W, GROUP_W), BF16),
    ]
    outs = pl.pallas_call(
        functools.partial(_mixer_kernel, seq_len=seq_len, rotate=rotate),
        grid=(nb,),
        in_specs=in_specs,
        out_specs=out_specs,
        out_shape=out_shape,
        scratch_shapes=scratch,
        compiler_params=pltpu.CompilerParams(
            dimension_semantics=("arbitrary",), vmem_limit_bytes=VMEM_LIMIT_BYTES),
        name="mixer_lat" if rotate else "mixer_ctx",
    )(*args)
    return outs


def _ffn_kernel(x_ref, mod_ref, g2_ref, w_in_hbm, w_out_hbm, gf_ref, y_ref,
                hbf_ref, act_ref, wgu_ref, wo_ref, stg_g_ref, stg_u_ref, stg_o_ref, sem,
                *, layer, final_norm):
    pid = pl.program_id(0)
    sh2 = mod_ref[:, 3 * D_MODEL:4 * D_MODEL]
    sc2 = mod_ref[:, 4 * D_MODEL:5 * D_MODEL]
    ga2 = mod_ref[:, 5 * D_MODEL:6 * D_MODEL]

    def tile_copies(j, slot):
        return (
            pltpu.make_async_copy(w_in_hbm.at[layer, :, pl.ds(j * FF_TILE, FF_TILE)],
                                  stg_g_ref.at[slot], sem.at[0, slot]),
            pltpu.make_async_copy(w_in_hbm.at[layer, :, pl.ds(D_FF + j * FF_TILE, FF_TILE)],
                                  stg_u_ref.at[slot], sem.at[1, slot]),
            pltpu.make_async_copy(w_out_hbm.at[layer, pl.ds(j * FF_TILE, FF_TILE), :],
                                  stg_o_ref.at[slot], sem.at[2, slot]),
        )

    def start_tile(j):
        for cp in tile_copies(j, j % FF_RING):
            cp.start()

    @pl.when(pid == 0)
    def _():
        for j in range(FF_LOOKAHEAD):
            start_tile(j)

    def norm_body(i, _):
        r0 = pl.multiple_of(i * ROW_TILE, ROW_TILE)
        h = _norm_mod(x_ref[pl.ds(r0, ROW_TILE), :], g2_ref[...], sc2, sh2)
        hbf_ref[pl.ds(r0, ROW_TILE), :] = h.astype(BF16)
        return 0
    lax.fori_loop(0, TB // ROW_TILE, norm_body, 0)

    def hidden_tiles(fetch):
        for j in range(N_FF_TILES):
            gcols = slice(j * FF_TILE, (j + 1) * FF_TILE)
            ucols = slice(D_FF + j * FF_TILE, D_FF + (j + 1) * FF_TILE)
            if fetch:
                if j + FF_LOOKAHEAD < N_FF_TILES:
                    start_tile(j + FF_LOOKAHEAD)
                slot = j % FF_RING
                for cp in tile_copies(j, slot):
                    cp.wait()
                wgu_ref[:, gcols] = stg_g_ref[slot].astype(BF16)
                wgu_ref[:, ucols] = stg_u_ref[slot].astype(BF16)
                wo_ref[gcols, :] = stg_o_ref[slot].astype(BF16)
            gate = jnp.dot(hbf_ref[...], wgu_ref[:, gcols], preferred_element_type=F32)
            up = jnp.dot(hbf_ref[...], wgu_ref[:, ucols], preferred_element_type=F32)
            act_ref[:, gcols] = (_silu(gate) * up).astype(BF16)

    @pl.when(pid == 0)
    def _():
        hidden_tiles(True)

    @pl.when(pid != 0)
    def _():
        hidden_tiles(False)

    y_ref[...] = jnp.dot(act_ref[...], wo_ref[...], preferred_element_type=F32)

    def out_body(i, _):
        r0 = pl.multiple_of(i * ROW_TILE, ROW_TILE)
        y = x_ref[pl.ds(r0, ROW_TILE), :] + ga2 * y_ref[pl.ds(r0, ROW_TILE), :]
        if final_norm:
            ms = jnp.mean(y * y, axis=-1, keepdims=True)
            y = y * lax.rsqrt(ms + EPS) * gf_ref[...]
        y_ref[pl.ds(r0, ROW_TILE), :] = y
        return 0
    lax.fori_loop(0, TB // ROW_TILE, out_body, 0)


def _ffn(x, mod, g2, w_ffn_in, w_ffn_out, g_final, *, layer, mod_base, mod_stride, final_norm):
    nb = x.shape[0]
    return pl.pallas_call(
        functools.partial(_ffn_kernel, layer=layer, final_norm=final_norm),
        grid=(nb,),
        in_specs=[
            pl.BlockSpec((None, TB, D_MODEL), lambda i: (i, 0, 0)),
            pl.BlockSpec((None, 1, 6 * D_MODEL), lambda i: (jnp.maximum(mod_base + mod_stride * i, 0), 0, 0)),
            _const_spec((1, D_MODEL)),
            pl.BlockSpec(memory_space=pl.ANY),
            pl.BlockSpec(memory_space=pl.ANY),
            _const_spec((1, D_MODEL)),
        ],
        out_specs=pl.BlockSpec((None, TB, D_MODEL), lambda i: (i, 0, 0)),
        out_shape=jax.ShapeDtypeStruct((nb, TB, D_MODEL), F32),
        scratch_shapes=[
            pltpu.VMEM((TB, D_MODEL), BF16),
            pltpu.VMEM((TB, D_FF), BF16),
            pltpu.VMEM((D_MODEL, 2 * D_FF), BF16),
            pltpu.VMEM((D_FF, D_MODEL), BF16),
            pltpu.VMEM((FF_RING, D_MODEL, FF_TILE), F32),
            pltpu.VMEM((FF_RING, D_MODEL, FF_TILE), F32),
            pltpu.VMEM((FF_RING, FF_TILE, D_MODEL), F32),
            pltpu.SemaphoreType.DMA((3, FF_RING)),
        ],
        compiler_params=pltpu.CompilerParams(
            dimension_semantics=("arbitrary",), vmem_limit_bytes=VMEM_LIMIT_BYTES),
        name="ffn",
    )(x, mod, g2, w_ffn_in, w_ffn_out, g_final)


def _rope_tables(seq_len):
    t = jnp.arange(seq_len)
    r = (t // GRID_W).astype(F32)
    c = (t % GRID_W).astype(F32)
    nf = HEAD_D // 4
    inv = ROPE_BASE ** (-jnp.arange(nf, dtype=F32) / nf)
    ang_r = r[:, None] * inv
    ang_c = c[:, None] * inv
    cos = jnp.concatenate([jnp.cos(ang_r), jnp.cos(ang_r), jnp.cos(ang_c), jnp.cos(ang_c)], axis=-1)
    sin = jnp.concatenate([-jnp.sin(ang_r), jnp.sin(ang_r), -jnp.sin(ang_c), jnp.sin(ang_c)], axis=-1)
    return jnp.tile(cos, (1, HEADS)), jnp.tile(sin, (1, HEADS))


def _layer_weights(l, g_norm1, g_norm2, w_in, w_out, conv_dw, conv_b, conv_ln_g, conv_ln_b, conv_pw,
                   gmlp_ws, gmlp_b, pool_w, pool_scale, ret_decay, w_ffn_in, w_ffn_out):
    return {
        "g1": g_norm1[l].reshape(1, D_MODEL),
        "g2": g_norm2[l].reshape(1, D_MODEL),
        "w_in": w_in[l].astype(BF16),
        "w_out": w_out[l].astype(BF16),
        "dw": jnp.concatenate([conv_dw[l], jnp.zeros((1, GROUP_W), F32)], axis=0),
        "cb": conv_b[l].reshape(1, GROUP_W),
        "lng": conv_ln_g[l].reshape(1, GROUP_W),
        "lnb": conv_ln_b[l].reshape(1, GROUP_W),
        "pw": conv_pw[l].astype(BF16),
        "wcat": jnp.transpose(gmlp_ws[l], (1, 0, 2)).reshape(CHUNK, HEADS * CHUNK).astype(BF16),
        "gbias": jnp.repeat(gmlp_b[l].T, HEAD_D, axis=1),
        "pbd": jax.scipy.linalg.block_diag(*[pool_w[l, g] for g in range(len(POOL_WINDOWS))]).astype(BF16),
        "pscale": pool_scale[l].reshape(1, GROUP_W),
        "lg": jax.nn.log_sigmoid(ret_decay[l].astype(F32)).reshape(2 * HEADS),
    }


def kernel(x_prompt, x_sample, state_ret, c, c_ctx, w_ada, b_ada, g_norm1, g_norm2, w_in, w_out, conv_dw,
           conv_b, conv_ln_g, conv_ln_b, conv_pw, gmlp_ws, gmlp_b, pool_w, pool_scale, ret_decay, w_ffn_in,
           w_ffn_out, g_final):
    batch, seq, _ = x_prompt.shape
    dec_batch, dec_seq, _ = x_sample.shape
    assert dec_seq == TB and TB % seq == 0 and (batch * seq) % TB == 0
    assert 1 + dec_batch <= MOD_ROWS

    cs = jnp.concatenate([c_ctx[None, :], c, jnp.zeros((MOD_ROWS - 1 - dec_batch, D_MODEL), F32)], axis=0)
    mods = _ada_rows(cs, w_ada, b_ada).reshape(DEPTH, MOD_ROWS, 1, 6 * D_MODEL)
    rope_tabs = _rope_tables(dec_seq)
    g_final2 = g_final.reshape(1, D_MODEL)
    s0_all = jnp.transpose(state_ret.astype(F32), (0, 1, 2, 4, 3, 5)).reshape(
        dec_batch, DEPTH, 2, HEAD_D, GROUP_W)

    xc = x_prompt.reshape(batch * seq // TB, TB, D_MODEL)
    xl = x_sample
    states = []
    for l in range(DEPTH):
        lw = _layer_weights(l, g_norm1, g_norm2, w_in, w_out, conv_dw, conv_b, conv_ln_g, conv_ln_b,
                            conv_pw, gmlp_ws, gmlp_b, pool_w, pool_scale, ret_decay, w_ffn_in, w_ffn_out)
        last = l == DEPTH - 1
        xc, st = _mixer(xc, mods[l], lw, seq_len=seq, rotate=False, mod_base=0, mod_stride=0)
        states.append(st)
        xc = _ffn(xc, mods[l], lw["g2"], w_ffn_in, w_ffn_out, g_final2, layer=l, mod_base=0, mod_stride=0,
                  final_norm=last)
        (xl,) = _mixer(xl, mods[l], lw, seq_len=dec_seq, rotate=True, mod_base=1, mod_stride=1,
                       rope_tabs=rope_tabs, s0=s0_all[:, l])
        xl = _ffn(xl, mods[l], lw["g2"], w_ffn_in, w_ffn_out, g_final2, layer=l, mod_base=1, mod_stride=1,
                  final_norm=last)

    y_prompt = xc.reshape(batch, seq, D_MODEL)
    new_state = jnp.stack(states, axis=1).astype(x_prompt.dtype)
    return (y_prompt, xl, new_state)
```

```python
import functools

import jax
import jax.numpy as jnp
from jax import lax
from jax.experimental import pallas as pl
from jax.experimental.pallas import tpu as pltpu

F32 = jnp.float32
BF16 = jnp.bfloat16

D_MODEL = 1024
DEPTH = 2
GRID_W = 64
GROUP_W = D_MODEL // 4
CONV_W = 31
CHUNK = 128
HEADS = 4
HEAD_D = GROUP_W // HEADS
POOL_WINDOWS = (2, 4, 8, 16)
ROPE_BASE = 10000.0
D_FF = 2816
IN_COLS = 11 * GROUP_W
EPS = 1e-6

TB = 1024
N_CHUNKS = TB // CHUNK
ROW_TILE = 256
CONV_TILE = 64
N_CONV_TILES = TB // CONV_TILE
PAD = 16
FF_TILE = 256
N_FF_TILES = D_FF // FF_TILE
FF_OUT_ROWS = 256
FF_LOOKAHEAD = 2
FF_RING = FF_LOOKAHEAD + 1
ADA_TILE = 1536
MOD_ROWS = 8
VMEM_LIMIT_BYTES = 60 * 1024 * 1024

COL_A, COL_B, COL_C, COL_D = 0, 2 * GROUP_W, 4 * GROUP_W, 5 * GROUP_W
COL_V, COL_G = COL_D + 4 * GROUP_W, COL_D + 5 * GROUP_W


def _sigmoid(x):
    return 1.0 / (1.0 + jnp.exp(-x))


def _silu(x):
    return x * _sigmoid(x)


def _norm_mod(x, g, scale, shift):
    ms = jnp.mean(x * x, axis=-1, keepdims=True)
    return (x * lax.rsqrt(ms + EPS) * g) * (1.0 + scale) + shift


def _head_stack(x, lane_head):
    return jnp.concatenate([jnp.where(lane_head == h, x, 0.0) for h in range(HEADS)], axis=0)


def _const_spec(shape):
    zeros = (0,) * len(shape)
    return pl.BlockSpec(shape, lambda i: zeros, pipeline_mode=pl.Buffered(1))


def _ada_kernel(c_ref, w_ref, b_ref, o_ref):
    a = _silu(c_ref[...]).astype(BF16)
    o_ref[...] = jnp.dot(a, w_ref[...].astype(BF16), preferred_element_type=F32) + b_ref[...]


def _ada_rows(cs, w_ada, b_ada):
    n_tiles = 6 * D_MODEL // ADA_TILE
    return pl.pallas_call(
        _ada_kernel,
        grid=(DEPTH, n_tiles),
        in_specs=[
            pl.BlockSpec((MOD_ROWS, D_MODEL), lambda l, j: (0, 0)),
            pl.BlockSpec((None, D_MODEL, ADA_TILE), lambda l, j: (l, 0, j)),
            pl.BlockSpec((None, 1, ADA_TILE), lambda l, j: (l, 0, j)),
        ],
        out_specs=pl.BlockSpec((None, MOD_ROWS, ADA_TILE), lambda l, j: (l, 0, j)),
        out_shape=jax.ShapeDtypeStruct((DEPTH, MOD_ROWS, 6 * D_MODEL), F32),
        compiler_params=pltpu.CompilerParams(
            dimension_semantics=("arbitrary", "arbitrary"), vmem_limit_bytes=VMEM_LIMIT_BYTES),
        name="ada_rows",
    )(cs, w_ada, b_ada.reshape(DEPTH, 1, 6 * D_MODEL))


def _mixer_kernel(*refs, seq_len, rotate):
    n_seq = TB // seq_len
    n_chunk = seq_len // CHUNK
    it = iter(refs)
    x_ref, mod_ref, g1_ref, w_in_ref, w_out_ref = (next(it) for _ in range(5))
    dw_ref, cb_ref, lng_ref, lnb_ref, pw_ref = (next(it) for _ in range(5))
    wcat_ref, gbias_ref, pbd_ref, pscale_ref, lg_ref = (next(it) for _ in range(5))
    if rotate:
        cos_ref, sin_ref, s0_ref = (next(it) for _ in range(3))
    y_ref = next(it)
    if not rotate:
        st_ref = next(it)
    (hbf_ref, proj_ref, cat_ref, pad_ref, pad2_ref, tmp_ref, tmp2_ref, of_ref,
     dcat_ref, qdec_ref, kdec_ref, sdec_ref, gmat_ref, qb_ref, upd_ref, sall_ref) = it

    sh1 = mod_ref[:, 0:D_MODEL]
    sc1 = mod_ref[:, D_MODEL:2 * D_MODEL]
    ga1 = mod_ref[:, 2 * D_MODEL:3 * D_MODEL]

    lane = lax.broadcasted_iota(jnp.int32, (1, GROUP_W), 1)
    lane_head = lane // HEAD_D
    tiles_per_seq = seq_len // CONV_TILE
    chunks_per_seq = seq_len // CHUNK

    def seq_and_offset(i, per_seq, size):
        if n_seq == 1:
            return 0, pl.multiple_of(i * size, size)
        return i // per_seq, pl.multiple_of((i % per_seq) * size, size)

    def norm_body(i, _):
        r0 = pl.multiple_of(i * ROW_TILE, ROW_TILE)
        h = _norm_mod(x_ref[pl.ds(r0, ROW_TILE), :], g1_ref[...], sc1, sh1).astype(BF16)
        hbf_ref[pl.ds(r0, ROW_TILE), :] = h
        proj_ref[pl.ds(r0, ROW_TILE), COL_A:COL_B] = jnp.dot(
            h, w_in_ref[:, COL_A:COL_B], preferred_element_type=F32)
        return 0
    lax.fori_loop(0, TB // ROW_TILE, norm_body, 0, unroll=2)

    for s in range(n_seq):
        for ref in (pad_ref, pad2_ref):
            ref[s, 0:PAD, :] = jnp.zeros((PAD, GROUP_W), F32)
            ref[s, PAD + seq_len:PAD + seq_len + PAD, :] = jnp.zeros((PAD, GROUP_W), F32)

    def glu_body(i, _):
        s, t0 = seq_and_offset(i, chunks_per_seq, CHUNK)
        r0 = pl.multiple_of(i * CHUNK, CHUNK)
        a1 = proj_ref[pl.ds(r0, CHUNK), COL_A:COL_A + GROUP_W]
        a2 = proj_ref[pl.ds(r0, CHUNK), COL_A + GROUP_W:COL_B]
        pad_ref[s, pl.ds(pl.multiple_of(t0 + PAD, 8), CHUNK), :] = a1 * _sigmoid(a2)
        return 0
    lax.fori_loop(0, N_CHUNKS, glu_body, 0, unroll=2)

    win = CONV_TILE + 2 * PAD

    def conv_tile(i):
        s, t0 = divmod(i, tiles_per_seq)
        t0 *= CONV_TILE
        for c0 in range(0, GROUP_W, 128):
            w = pad_ref[s, t0:t0 + win, c0:c0 + 128]
            acc = jnp.zeros((CONV_TILE, 128), F32) + cb_ref[:, c0:c0 + 128]
            for b in range(8):
                wb = w if b == 0 else pltpu.roll(w, win - b, axis=0)
                for a in range(4):
                    k = 8 * a + b - 1
                    if 0 <= k < CONV_W:
                        acc = acc + dw_ref[k:k + 1, c0:c0 + 128] * wb[8 * a:8 * a + CONV_TILE, :]
            tmp_ref[i * CONV_TILE:(i + 1) * CONV_TILE, c0:c0 + 128] = acc

    pwin = CONV_TILE + 16
    lane128 = lax.broadcasted_iota(jnp.int32, (1, 128), 1)
    first_half = lane128 < HEAD_D

    def pool_tile(i):
        s, t0 = divmod(i, tiles_per_seq)
        t0 *= CONV_TILE
        interior = t0 >= 8 and t0 + CONV_TILE + 8 <= seq_len
        outs = []
        for col, (w_small, w_big) in enumerate(((2, 4), (8, 16))):
            w = pad2_ref[s, t0 + PAD - 8:t0 + PAD - 8 + pwin, col * 128:(col + 1) * 128]

            rolled = {0: w}

            def shifted(b, w=w, rolled=rolled):
                if b not in rolled:
                    rolled[b] = pltpu.roll(w, pwin - b, axis=0)
                return rolled[b]

            def lo(b):
                return shifted(b)[0:CONV_TILE, :]

            def hi(b):
                return shifted(b)[8:8 + CONV_TILE, :]

            tok = hi(0)
            if col == 0:
                s_small = lo(7) + tok
                s_big = s_small + lo(6) + hi(1)
            else:
                s_small = lo(7) + tok + lo(6) + hi(1) + lo(5) + lo(4) + hi(2) + hi(3)
                s_big = s_small + lo(3) + lo(2) + lo(1) + lo(0) + hi(4) + hi(5) + hi(6) + hi(7)
            ssum = jnp.where(first_half, s_small, s_big)
            if interior:
                mean = ssum * jnp.where(first_half, 1.0 / w_small, 1.0 / w_big)
            else:
                half = jnp.where(first_half, w_small // 2, w_big // 2)
                tpos = t0 + lax.broadcasted_iota(jnp.int32, (CONV_TILE, 128), 0)
                cnt = jnp.minimum(tpos + half, seq_len) - jnp.maximum(tpos - half, 0)
                mean = ssum / cnt.astype(F32)
            outs.append(mean - tok)
        tmp2_ref[i * CONV_TILE:(i + 1) * CONV_TILE, :] = jnp.concatenate(outs, axis=1)

    col_tiles = [COL_C] + list(range(COL_B, COL_C, GROUP_W)) + list(range(COL_D, IN_COLS, GROUP_W))
    per_step = -(-N_CONV_TILES // (len(col_tiles) - 1))
    conv_next = 0
    pool_next = 0
    for step, c0 in enumerate(col_tiles):
        proj_ref[:, c0:c0 + GROUP_W] = jnp.dot(
            hbf_ref[...], w_in_ref[:, c0:c0 + GROUP_W], preferred_element_type=F32)
        if step == 1:
            for s in range(n_seq):
                pad2_ref[s, PAD:PAD + seq_len, :] = proj_ref[s * seq_len:(s + 1) * seq_len, COL_C:COL_D]
        for _ in range(per_step):
            if conv_next < N_CONV_TILES:
                conv_tile(conv_next)
                conv_next += 1
            if step >= 1 and pool_next < N_CONV_TILES:
                pool_tile(pool_next)
                pool_next += 1
    assert conv_next == N_CONV_TILES and pool_next == N_CONV_TILES

    def ln_pw_body(i, _):
        r0 = pl.multiple_of(i * ROW_TILE, ROW_TILE)
        c = tmp_ref[pl.ds(r0, ROW_TILE), :]
        mu = jnp.mean(c, axis=-1, keepdims=True)
        cen = c - mu
        var = jnp.mean(cen * cen, axis=-1, keepdims=True)
        hn = cen * lax.rsqrt(var + EPS) * lng_ref[...] + lnb_ref[...]
        ya = jnp.dot(_silu(hn).astype(BF16), pw_ref[...], preferred_element_type=F32)
        cat_ref[pl.ds(r0, ROW_TILE), 0:GROUP_W] = ya.astype(BF16)
        return 0
    lax.fori_loop(0, TB // ROW_TILE, ln_pw_body, 0, unroll=2)

    def gmlp_body(i, _):
        r0 = pl.multiple_of(i * CHUNK, CHUNK)
        u = proj_ref[pl.ds(r0, CHUNK), COL_B:COL_B + GROUP_W]
        v = proj_ref[pl.ds(r0, CHUNK), COL_B + GROUP_W:COL_C]
        vstack = _head_stack(v, lane_head).astype(BF16)
        sg = jnp.dot(wcat_ref[...], vstack, preferred_element_type=F32) + gbias_ref[...]
        cat_ref[pl.ds(r0, CHUNK), GROUP_W:2 * GROUP_W] = (u * sg).astype(BF16)
        return 0
    lax.fori_loop(0, N_CHUNKS, gmlp_body, 0, unroll=4)

    def pool_mix_body(i, _):
        r0 = pl.multiple_of(i * ROW_TILE, ROW_TILE)
        yc = jnp.dot(tmp2_ref[pl.ds(r0, ROW_TILE), :].astype(BF16), pbd_ref[...],
                     preferred_element_type=F32) * pscale_ref[...]
        cat_ref[pl.ds(r0, ROW_TILE), 2 * GROUP_W:3 * GROUP_W] = yc.astype(BF16)
        return 0
    lax.fori_loop(0, TB // ROW_TILE, pool_mix_body, 0, unroll=2)

    ri = lax.broadcasted_iota(jnp.int32, (CHUNK, HEADS * CHUNK), 0)
    ci = lax.broadcasted_iota(jnp.int32, (CHUNK, HEADS * CHUNK), 1)
    cj = ci % CHUNK
    chead = ci // CHUNK
    rq = lax.broadcasted_iota(jnp.int32, (CHUNK, GROUP_W), 0).astype(F32)
    for d in range(2):
        lgs = [lg_ref[d * HEADS + h] for h in range(HEADS)]
        lg_wide = jnp.where(chead == 0, lgs[0], jnp.where(chead == 1, lgs[1],
                                                          jnp.where(chead == 2, lgs[2], lgs[3])))
        lg_lane = jnp.where(lane_head == 0, lgs[0], jnp.where(lane_head == 1, lgs[1],
                                                              jnp.where(lane_head == 2, lgs[2], lgs[3])))
        dist = (ri - cj) if d == 0 else (cj - ri)
        keep = dist >= 0
        dcat_ref[d] = jnp.where(keep, jnp.exp(jnp.where(keep, dist, 0).astype(F32) * lg_wide), 0.0)
        if d == 0:
            qdec_ref[d] = jnp.exp((rq + 1.0) * lg_lane)
            kdec_ref[d] = jnp.exp((CHUNK - 1.0 - rq) * lg_lane)
        else:
            qdec_ref[d] = jnp.exp((CHUNK - rq) * lg_lane)
            kdec_ref[d] = jnp.exp(rq * lg_lane)
        sdec_ref[d] = jnp.exp(float(CHUNK) * lg_lane)

    rr = lax.broadcasted_iota(jnp.int32, (GROUP_W, GROUP_W), 0) // HEAD_D
    cc = lax.broadcasted_iota(jnp.int32, (GROUP_W, GROUP_W), 1) // HEAD_D
    gmat_ref[...] = jnp.where(rr == cc, 1.0 / HEAD_D, 0.0).astype(BF16)

    lane_bit = (lane & 16) == 0
    k_scale = HEAD_D ** -0.5

    def rope(z, r0):
        if not rotate:
            return z
        cos = cos_ref[pl.ds(r0, CHUNK), :]
        sin = sin_ref[pl.ds(r0, CHUNK), :]
        halves = []
        for c0 in (0, 128):
            zz = z[:, c0:c0 + 128]
            partner = jnp.where(lane_bit[:, c0:c0 + 128],
                                pltpu.roll(zz, 128 - 16, axis=1), pltpu.roll(zz, 16, axis=1))
            halves.append(partner)
        return z * cos + jnp.concatenate(halves, axis=1) * sin

    def pair_stack(zb, pair):
        zero = jnp.zeros_like(zb)
        return jnp.concatenate([jnp.where(lane_head == h, zb, zero) for h in (2 * pair, 2 * pair + 1)], axis=0)

    def intra_body(c, _):
        r0 = pl.multiple_of(c * CHUNK, CHUNK)
        v = proj_ref[pl.ds(r0, CHUNK), COL_V:COL_V + GROUP_W]
        vb = v.astype(BF16)
        vstacks = [pair_stack(vb, pair) for pair in range(2)]
        o = None
        for d in range(2):
            qc0 = COL_D + 2 * d * GROUP_W
            q = rope(proj_ref[pl.ds(r0, CHUNK), qc0:qc0 + GROUP_W], r0)
            k = rope(proj_ref[pl.ds(r0, CHUNK), qc0 + GROUP_W:qc0 + 2 * GROUP_W], r0) * k_scale
            qb = q.astype(BF16)
            kb = k.astype(BF16)
            qb_ref[d, pl.ds(r0, CHUNK), :] = qb
            for pair in range(2):
                att = lax.dot_general(qb, pair_stack(kb, pair), (((1,), (1,)), ((), ())),
                                      preferred_element_type=F32)
                att = (att * dcat_ref[d, :, pair * 2 * CHUNK:(pair + 1) * 2 * CHUNK]).astype(BF16)
                od = jnp.dot(att, vstacks[pair], preferred_element_type=F32)
                o = od if o is None else o + od
            kd = (k * kdec_ref[d]).astype(BF16)
            upd = lax.dot_general(kd, vb, (((0,), (0,)), ((), ())), preferred_element_type=F32)
            compact = None
            for h in range(HEADS):
                part = jnp.where(lane_head == h, upd[h * HEAD_D:(h + 1) * HEAD_D, :], 0.0)
                compact = part if compact is None else compact + part
            upd_ref[d, c] = compact
        of_ref[pl.ds(r0, CHUNK), :] = o
        return 0
    lax.fori_loop(0, N_CHUNKS, intra_body, 0, unroll=2)

    for s in range(n_seq):
        for d in range(2):
            st = s0_ref[d] if rotate else jnp.zeros((HEAD_D, GROUP_W), F32)
            order = range(n_chunk) if d == 0 else range(n_chunk - 1, -1, -1)
            for c in order:
                cg = s * n_chunk + c
                for h in range(HEADS):
                    sall_ref[d, cg, h * HEAD_D:(h + 1) * HEAD_D, :] = (
                        jnp.where(lane_head == h, st, 0.0).astype(BF16))
                st = st * sdec_ref[d] + upd_ref[d, cg]
            if not rotate:
                for h in range(HEADS):
                    st_ref[s, d, h] = st[:, h * HEAD_D:(h + 1) * HEAD_D]

    def cross_body(c, _):
        r0 = pl.multiple_of(c * CHUNK, CHUNK)
        o = of_ref[pl.ds(r0, CHUNK), :]
        for d in range(2):
            o = o + jnp.dot(qb_ref[d, pl.ds(r0, CHUNK), :], sall_ref[d, c],
                            preferred_element_type=F32) * qdec_ref[d]
        of_ref[pl.ds(r0, CHUNK), :] = o
        return 0
    lax.fori_loop(0, N_CHUNKS, cross_body, 0, unroll=2)

    def center_body(i, _):
        r0 = pl.multiple_of(i * ROW_TILE, ROW_TILE)
        o = of_ref[pl.ds(r0, ROW_TILE), :]
        gmat = gmat_ref[...]
        o_hi = o.astype(BF16)
        o_lo = (o - o_hi.astype(F32)).astype(BF16)
        mu = (jnp.dot(o_hi, gmat, preferred_element_type=F32)
              + jnp.dot(o_lo, gmat, preferred_element_type=F32))
        of_ref[pl.ds(r0, ROW_TILE), :] = o - mu
        return 0
    lax.fori_loop(0, TB // ROW_TILE, center_body, 0, unroll=2)

    def gate_body(i, _):
        r0 = pl.multiple_of(i * ROW_TILE, ROW_TILE)
        cen = of_ref[pl.ds(r0, ROW_TILE), :]
        var = jnp.dot((cen * cen).astype(BF16), gmat_ref[...], preferred_element_type=F32)
        on = cen * lax.rsqrt(var + EPS)
        g = proj_ref[pl.ds(r0, ROW_TILE), COL_G:COL_G + GROUP_W]
        cat_ref[pl.ds(r0, ROW_TILE), 3 * GROUP_W:4 * GROUP_W] = (_silu(g) * on).astype(BF16)
        return 0
    lax.fori_loop(0, TB // ROW_TILE, gate_body, 0, unroll=2)

    def out_body(i, _):
        r0 = pl.multiple_of(i * ROW_TILE, ROW_TILE)
        y = jnp.dot(cat_ref[pl.ds(r0, ROW_TILE), :], w_out_ref[...], preferred_element_type=F32)
        y_ref[pl.ds(r0, ROW_TILE), :] = x_ref[pl.ds(r0, ROW_TILE), :] + ga1 * y
        return 0
    lax.fori_loop(0, TB // ROW_TILE, out_body, 0, unroll=2)


def _mixer(x, mod, lw, *, seq_len, rotate, mod_base, mod_stride, rope_tabs=None, s0=None):
    nb = x.shape[0]
    n_seq = TB // seq_len
    in_specs = [
        pl.BlockSpec((None, TB, D_MODEL), lambda i: (i, 0, 0)),
        pl.BlockSpec((None, 1, 6 * D_MODEL), lambda i: (jnp.maximum(mod_base + mod_stride * i, 0), 0, 0)),
        _const_spec((1, D_MODEL)),
        _const_spec((D_MODEL, IN_COLS)),
        _const_spec((D_MODEL, D_MODEL)),
        _const_spec((CONV_W + 1, GROUP_W)),
        _const_spec((1, GROUP_W)), _const_spec((1, GROUP_W)), _const_spec((1, GROUP_W)),
        _const_spec((GROUP_W, GROUP_W)),
        _const_spec((CHUNK, HEADS * CHUNK)),
        _const_spec((CHUNK, GROUP_W)),
        _const_spec((GROUP_W, GROUP_W)),
        _const_spec((1, GROUP_W)),
        pl.BlockSpec(memory_space=pltpu.SMEM),
    ]
    args = [x, mod, lw["g1"], lw["w_in"], lw["w_out"], lw["dw"], lw["cb"], lw["lng"], lw["lnb"],
            lw["pw"], lw["wcat"], lw["gbias"], lw["pbd"], lw["pscale"], lw["lg"]]
    out_shape = [jax.ShapeDtypeStruct((nb, TB, D_MODEL), F32)]
    out_specs = [pl.BlockSpec((None, TB, D_MODEL), lambda i: (i, 0, 0))]
    if rotate:
        in_specs += [_const_spec((TB, GROUP_W)), _const_spec((TB, GROUP_W)),
                     pl.BlockSpec((None, 2, HEAD_D, GROUP_W), lambda i: (i, 0, 0, 0))]
        args += [rope_tabs[0], rope_tabs[1], s0]
    else:
        out_shape.append(jax.ShapeDtypeStruct((nb * n_seq, 2, HEADS, HEAD_D, HEAD_D), F32))
        out_specs.append(pl.BlockSpec((n_seq, 2, HEADS, HEAD_D, HEAD_D), lambda i: (i, 0, 0, 0, 0)))
    scratch = [
        pltpu.VMEM((TB, D_MODEL), BF16),
        pltpu.VMEM((TB, IN_COLS), F32),
        pltpu.VMEM((TB, D_MODEL), BF16),
        pltpu.VMEM((n_seq, seq_len + 2 * PAD, GROUP_W), F32),
        pltpu.VMEM((n_seq, seq_len + 2 * PAD, GROUP_W), F32),
        pltpu.VMEM((TB, GROUP_W), F32),
        pltpu.VMEM((TB, GROUP_W), F32),
        pltpu.VMEM((TB, GROUP_W), F32),
        pltpu.VMEM((2, CHUNK, HEADS * CHUNK), F32),
        pltpu.VMEM((2, CHUNK, GROUP_W), F32),
        pltpu.VMEM((2, CHUNK, GROUP_W), F32),
        pltpu.VMEM((2, 1, GROUP_W), F32),
        pltpu.VMEM((GROUP_W, GROUP_W), BF16),
        pltpu.VMEM((2, TB, GROUP_W), BF16),
        pltpu.VMEM((2, N_CHUNKS, HEAD_D, GROUP_W), F32),
        pltpu.VMEM((2, N_CHUNKS, GROUP_W, GROUP_W), BF16),
    ]
    outs = pl.pallas_call(
        functools.partial(_mixer_kernel, seq_len=seq_len, rotate=rotate),
        grid=(nb,),
        in_specs=in_specs,
        out_specs=out_specs,
        out_shape=out_shape,
        scratch_shapes=scratch,
        compiler_params=pltpu.CompilerParams(
            dimension_semantics=("arbitrary",), vmem_limit_bytes=VMEM_LIMIT_BYTES),
        name="mixer_lat" if rotate else "mixer_ctx",
    )(*args)
    return outs


def _ffn_kernel(x_ref, mod_ref, g2_ref, w_in_hbm, w_out_hbm, gf_ref, y_ref,
                hbf_ref, act_ref, wgu_ref, wo_ref, stg_g_ref, stg_u_ref, stg_o_ref, sem,
                *, layer, final_norm):
    pid = pl.program_id(0)
    sh2 = mod_ref[:, 3 * D_MODEL:4 * D_MODEL]
    sc2 = mod_ref[:, 4 * D_MODEL:5 * D_MODEL]
    ga2 = mod_ref[:, 5 * D_MODEL:6 * D_MODEL]

    def tile_copies(j, slot):
        return (
            pltpu.make_async_copy(w_in_hbm.at[layer, :, pl.ds(j * FF_TILE, FF_TILE)],
                                  stg_g_ref.at[slot], sem.at[0, slot]),
            pltpu.make_async_copy(w_in_hbm.at[layer, :, pl.ds(D_FF + j * FF_TILE, FF_TILE)],
                                  stg_u_ref.at[slot], sem.at[1, slot]),
            pltpu.make_async_copy(w_out_hbm.at[layer, pl.ds(j * FF_TILE, FF_TILE), :],
                                  stg_o_ref.at[slot], sem.at[2, slot]),
        )

    def start_tile(j):
        for cp in tile_copies(j, j % FF_RING):
            cp.start()

    @pl.when(pid == 0)
    def _():
        for j in range(FF_LOOKAHEAD):
            start_tile(j)

    def norm_body(i, _):
        r0 = pl.multiple_of(i * ROW_TILE, ROW_TILE)
        h = _norm_mod(x_ref[pl.ds(r0, ROW_TILE), :], g2_ref[...], sc2, sh2)
        hbf_ref[pl.ds(r0, ROW_TILE), :] = h.astype(BF16)
        return 0
    lax.fori_loop(0, TB // ROW_TILE, norm_body, 0)

    def hidden_tiles(fetch):
        for j in range(N_FF_TILES):
            gcols = slice(j * FF_TILE, (j + 1) * FF_TILE)
            ucols = slice(D_FF + j * FF_TILE, D_FF + (j + 1) * FF_TILE)
            if fetch:
                if j + FF_LOOKAHEAD < N_FF_TILES:
                    start_tile(j + FF_LOOKAHEAD)
                slot = j % FF_RING
                for cp in tile_copies(j, slot):
                    cp.wait()
                wgu_ref[:, gcols] = stg_g_ref[slot].astype(BF16)
                wgu_ref[:, ucols] = stg_u_ref[slot].astype(BF16)
                wo_ref[gcols, :] = stg_o_ref[slot].astype(BF16)
            gate = jnp.dot(hbf_ref[...], wgu_ref[:, gcols], preferred_element_type=F32)
            up = jnp.dot(hbf_ref[...], wgu_ref[:, ucols], preferred_element_type=F32)
            act_ref[:, gcols] = (_silu(gate) * up).astype(BF16)

    @pl.when(pid == 0)
    def _():
        hidden_tiles(True)

    @pl.when(pid != 0)
    def _():
        hidden_tiles(False)

    for r0 in range(0, TB, FF_OUT_ROWS):
        rows = slice(r0, r0 + FF_OUT_ROWS)
        y = x_ref[rows, :] + ga2 * jnp.dot(act_ref[rows, :], wo_ref[...], preferred_element_type=F32)
        if final_norm:
            ms = jnp.mean(y * y, axis=-1, keepdims=True)
            y = y * lax.rsqrt(ms + EPS) * gf_ref[...]
        y_ref[rows, :] = y


def _ffn(x, mod, g2, w_ffn_in, w_ffn_out, g_final, *, layer, mod_base, mod_stride, final_norm):
    nb = x.shape[0]
    return pl.pallas_call(
        functools.partial(_ffn_kernel, layer=layer, final_norm=final_norm),
        grid=(nb,),
        in_specs=[
            pl.BlockSpec((None, TB, D_MODEL), lambda i: (i, 0, 0)),
            pl.BlockSpec((None, 1, 6 * D_MODEL), lambda i: (jnp.maximum(mod_base + mod_stride * i, 0), 0, 0)),
            _const_spec((1, D_MODEL)),
            pl.BlockSpec(memory_space=pl.ANY),
            pl.BlockSpec(memory_space=pl.ANY),
            _const_spec((1, D_MODEL)),
        ],
        out_specs=pl.BlockSpec((None, TB, D_MODEL), lambda i: (i, 0, 0)),
        out_shape=jax.ShapeDtypeStruct((nb, TB, D_MODEL), F32),
        scratch_shapes=[
            pltpu.VMEM((TB, D_MODEL), BF16),
            pltpu.VMEM((TB, D_FF), BF16),
            pltpu.VMEM((D_MODEL, 2 * D_FF), BF16),
            pltpu.VMEM((D_FF, D_MODEL), BF16),
            pltpu.VMEM((FF_RING, D_MODEL, FF_TILE), F32),
            pltpu.VMEM((FF_RING, D_MODEL, FF_TILE), F32),
            pltpu.VMEM((FF_RING, FF_TILE, D_MODEL), F32),
            pltpu.SemaphoreType.DMA((3, FF_RING)),
        ],
        compiler_params=pltpu.CompilerParams(
            dimension_semantics=("arbitrary",), vmem_limit_bytes=VMEM_LIMIT_BYTES),
        name="ffn",
    )(x, mod, g2, w_ffn_in, w_ffn_out, g_final)


def _rope_tables(seq_len):
    t = jnp.arange(seq_len)
    r = (t // GRID_W).astype(F32)
    c = (t % GRID_W).astype(F32)
    nf = HEAD_D // 4
    inv = ROPE_BASE ** (-jnp.arange(nf, dtype=F32) / nf)
    ang_r = r[:, None] * inv
    ang_c = c[:, None] * inv
    cos = jnp.concatenate([jnp.cos(ang_r), jnp.cos(ang_r), jnp.cos(ang_c), jnp.cos(ang_c)], axis=-1)
    sin = jnp.concatenate([-jnp.sin(ang_r), jnp.sin(ang_r), -jnp.sin(ang_c), jnp.sin(ang_c)], axis=-1)
    return jnp.tile(cos, (1, HEADS)), jnp.tile(sin, (1, HEADS))


def _layer_weights(l, g_norm1, g_norm2, w_in, w_out, conv_dw, conv_b, conv_ln_g, conv_ln_b, conv_pw,
                   gmlp_ws, gmlp_b, pool_w, pool_scale, ret_decay, w_ffn_in, w_ffn_out):
    return {
        "g1": g_norm1[l].reshape(1, D_MODEL),
        "g2": g_norm2[l].reshape(1, D_MODEL),
        "w_in": w_in[l].astype(BF16),
        "w_out": w_out[l].astype(BF16),
        "dw": jnp.concatenate([conv_dw[l], jnp.zeros((1, GROUP_W), F32)], axis=0),
        "cb": conv_b[l].reshape(1, GROUP_W),
        "lng": conv_ln_g[l].reshape(1, GROUP_W),
        "lnb": conv_ln_b[l].reshape(1, GROUP_W),
        "pw": conv_pw[l].astype(BF16),
        "wcat": jnp.transpose(gmlp_ws[l], (1, 0, 2)).reshape(CHUNK, HEADS * CHUNK).astype(BF16),
        "gbias": jnp.repeat(gmlp_b[l].T, HEAD_D, axis=1),
        "pbd": jax.scipy.linalg.block_diag(*[pool_w[l, g] for g in range(len(POOL_WINDOWS))]).astype(BF16),
        "pscale": pool_scale[l].reshape(1, GROUP_W),
        "lg": jax.nn.log_sigmoid(ret_decay[l].astype(F32)).reshape(2 * HEADS),
    }


def kernel(x_prompt, x_sample, state_ret, c, c_ctx, w_ada, b_ada, g_norm1, g_norm2, w_in, w_out, conv_dw,
           conv_b, conv_ln_g, conv_ln_b, conv_pw, gmlp_ws, gmlp_b, pool_w, pool_scale, ret_decay, w_ffn_in,
           w_ffn_out, g_final):
    batch, seq, _ = x_prompt.shape
    dec_batch, dec_seq, _ = x_sample.shape
    assert dec_seq == TB and TB % seq == 0 and (batch * seq) % TB == 0
    assert 1 + dec_batch <= MOD_ROWS

    cs = jnp.concatenate([c_ctx[None, :], c, jnp.zeros((MOD_ROWS - 1 - dec_batch, D_MODEL), F32)], axis=0)
    mods = _ada_rows(cs, w_ada, b_ada).reshape(DEPTH, MOD_ROWS, 1, 6 * D_MODEL)
    rope_tabs = _rope_tables(dec_seq)
    g_final2 = g_final.reshape(1, D_MODEL)
    s0_all = jnp.transpose(state_ret.astype(F32), (0, 1, 2, 4, 3, 5)).reshape(
        dec_batch, DEPTH, 2, HEAD_D, GROUP_W)

    xc = x_prompt.reshape(batch * seq // TB, TB, D_MODEL)
    xl = x_sample
    states = []
    for l in range(DEPTH):
        lw = _layer_weights(l, g_norm1, g_norm2, w_in, w_out, conv_dw, conv_b, conv_ln_g, conv_ln_b,
                            conv_pw, gmlp_ws, gmlp_b, pool_w, pool_scale, ret_decay, w_ffn_in, w_ffn_out)
        last = l == DEPTH - 1
        xc, st = _mixer(xc, mods[l], lw, seq_len=seq, rotate=False, mod_base=0, mod_stride=0)
        states.append(st)
        xc = _ffn(xc, mods[l], lw["g2"], w_ffn_in, w_ffn_out, g_final2, layer=l, mod_base=0, mod_stride=0,
                  final_norm=last)
        (xl,) = _mixer(xl, mods[l], lw, seq_len=dec_seq, rotate=True, mod_base=1, mod_stride=1,
                       rope_tabs=rope_tabs, s0=s0_all[:, l])
        xl = _ffn(xl, mods[l], lw["g2"], w_ffn_in, w_ffn_out, g_final2, layer=l, mod_base=1, mod_stride=1,
                  final_norm=last)

    y_prompt = xc.reshape(batch, seq, D_MODEL)
    new_state = jnp.stack(states, axis=1).astype(x_prompt.dtype)
    return (y_prompt, xl, new_state)
```

```python
import functools

import jax
import jax.numpy as jnp
import numpy as np
from jax import lax
from jax.experimental import pallas as pl
from jax.experimental.pallas import tpu as pltpu

F32 = jnp.float32
BF16 = jnp.bfloat16

D_MODEL = 1024
DEPTH = 2
GRID_W = 64
GROUP_W = D_MODEL // 4
CONV_W = 31
CHUNK = 128
HEADS = 4
HEAD_D = GROUP_W // HEADS
POOL_WINDOWS = (2, 4, 8, 16)
ROPE_BASE = 10000.0
D_FF = 2816
IN_COLS = 11 * GROUP_W
EPS = 1e-6

TB = 1024
N_CHUNKS = TB // CHUNK
ROW_TILE = 256
CONV_TILE = 64
N_CONV_TILES = TB // CONV_TILE
PAD = 16
FF_TILE = 256
N_FF_TILES = D_FF // FF_TILE
FF_OUT_ROWS = 256
FF_LOOKAHEAD = 2
FF_RING = FF_LOOKAHEAD + 1
ADA_TILE = 1536
MOD_ROWS = 8
VMEM_LIMIT_BYTES = 60 * 1024 * 1024

COL_A, COL_B, COL_C, COL_D = 0, 2 * GROUP_W, 4 * GROUP_W, 5 * GROUP_W
COL_V, COL_G = COL_D + 4 * GROUP_W, COL_D + 5 * GROUP_W


def _sigmoid(x):
    return 1.0 / (1.0 + jnp.exp(-x))


def _silu(x):
    return x * _sigmoid(x)


def _norm_mod(x, g, scale, shift):
    ms = jnp.mean(x * x, axis=-1, keepdims=True)
    return (x * lax.rsqrt(ms + EPS) * g) * (1.0 + scale) + shift


def _head_stack(x, lane_head):
    return jnp.concatenate([jnp.where(lane_head == h, x, 0.0) for h in range(HEADS)], axis=0)


def _const_spec(shape):
    zeros = (0,) * len(shape)
    return pl.BlockSpec(shape, lambda i: zeros, pipeline_mode=pl.Buffered(1))


def _ada_kernel(c_ref, w_ref, b_ref, o_ref):
    a = _silu(c_ref[...]).astype(BF16)
    o_ref[...] = jnp.dot(a, w_ref[...].astype(BF16), preferred_element_type=F32) + b_ref[...]


def _ada_rows(cs, w_ada, b_ada):
    n_tiles = 6 * D_MODEL // ADA_TILE
    return pl.pallas_call(
        _ada_kernel,
        grid=(DEPTH, n_tiles),
        in_specs=[
            pl.BlockSpec((MOD_ROWS, D_MODEL), lambda l, j: (0, 0)),
            pl.BlockSpec((None, D_MODEL, ADA_TILE), lambda l, j: (l, 0, j)),
            pl.BlockSpec((None, 1, ADA_TILE), lambda l, j: (l, 0, j)),
        ],
        out_specs=pl.BlockSpec((None, MOD_ROWS, ADA_TILE), lambda l, j: (l, 0, j)),
        out_shape=jax.ShapeDtypeStruct((DEPTH, MOD_ROWS, 6 * D_MODEL), F32),
        compiler_params=pltpu.CompilerParams(
            dimension_semantics=("arbitrary", "arbitrary"), vmem_limit_bytes=VMEM_LIMIT_BYTES),
        name="ada_rows",
    )(cs, w_ada, b_ada.reshape(DEPTH, 1, 6 * D_MODEL))


def _mixer_kernel(*refs, layer, seq_len, rotate):
    n_seq = TB // seq_len
    n_chunk = seq_len // CHUNK
    it = iter(refs)
    x_ref, mod_ref, g1_ref, w_in_ref, w_out_ref = (next(it) for _ in range(5))
    dw_ref, cb_ref, lng_ref, lnb_ref, pw_ref = (next(it) for _ in range(5))
    ws_ref, gbias_ref, poolw_ref, pscale_ref, lg_ref = (next(it) for _ in range(5))
    if rotate:
        cos_ref, sin_ref, s0_ref = (next(it) for _ in range(3))
    elif layer > 0:
        next(it)
    y_ref = next(it)
    if not rotate:
        st_ref = next(it)
    (hbf_ref, proj_ref, pad_ref, pad2_ref, tmp_ref, tmp2_ref,
     dcat_ref, qdec_ref, kdec_ref, sdec_ref, gmat_ref, qb_ref, upd_ref, sall_ref,
     pwb_ref, wcat_ref, pbd_ref, s0c_ref) = it
    cat_ref = hbf_ref
    of_ref = tmp_ref

    pwb_ref[...] = pw_ref[...].astype(BF16)
    pbd_ref[...] = jnp.zeros((GROUP_W, GROUP_W), BF16)
    for h in range(HEADS):
        wcat_ref[:, h * CHUNK:(h + 1) * CHUNK] = ws_ref[h].astype(BF16)
        pbd_ref[h * HEAD_D:(h + 1) * HEAD_D, h * HEAD_D:(h + 1) * HEAD_D] = poolw_ref[h].astype(BF16)
        if rotate:
            for d in range(2):
                s0c_ref[d, :, h * HEAD_D:(h + 1) * HEAD_D] = s0_ref[d, h]

    sh1 = mod_ref[:, 0:D_MODEL]
    sc1 = mod_ref[:, D_MODEL:2 * D_MODEL]
    ga1 = mod_ref[:, 2 * D_MODEL:3 * D_MODEL]

    lane = lax.broadcasted_iota(jnp.int32, (1, GROUP_W), 1)
    lane_head = lane // HEAD_D
    tiles_per_seq = seq_len // CONV_TILE
    chunks_per_seq = seq_len // CHUNK

    def seq_and_offset(i, per_seq, size):
        if n_seq == 1:
            return 0, pl.multiple_of(i * size, size)
        return i // per_seq, pl.multiple_of((i % per_seq) * size, size)

    def norm_body(i, _):
        r0 = pl.multiple_of(i * ROW_TILE, ROW_TILE)
        h = _norm_mod(x_ref[pl.ds(r0, ROW_TILE), :], g1_ref[...], sc1, sh1).astype(BF16)
        hbf_ref[pl.ds(r0, ROW_TILE), :] = h
        proj_ref[pl.ds(r0, ROW_TILE), COL_A:COL_B] = jnp.dot(
            h, w_in_ref[:, COL_A:COL_B], preferred_element_type=F32)
        return 0
    lax.fori_loop(0, TB // ROW_TILE, norm_body, 0, unroll=2)

    for s in range(n_seq):
        for ref in (pad_ref, pad2_ref):
            ref[s, 0:PAD, :] = jnp.zeros((PAD, GROUP_W), F32)
            ref[s, PAD + seq_len:PAD + seq_len + PAD, :] = jnp.zeros((PAD, GROUP_W), F32)

    def glu_body(i, _):
        s, t0 = seq_and_offset(i, chunks_per_seq, CHUNK)
        r0 = pl.multiple_of(i * CHUNK, CHUNK)
        a1 = proj_ref[pl.ds(r0, CHUNK), COL_A:COL_A + GROUP_W]
        a2 = proj_ref[pl.ds(r0, CHUNK), COL_A + GROUP_W:COL_B]
        pad_ref[s, pl.ds(pl.multiple_of(t0 + PAD, 8), CHUNK), :] = a1 * _sigmoid(a2)
        return 0
    lax.fori_loop(0, N_CHUNKS, glu_body, 0, unroll=2)

    win = CONV_TILE + 2 * PAD

    def conv_tile(i):
        s, t0 = divmod(i, tiles_per_seq)
        t0 *= CONV_TILE
        for c0 in range(0, GROUP_W, 128):
            w = pad_ref[s, t0:t0 + win, c0:c0 + 128]
            acc = jnp.zeros((CONV_TILE, 128), F32) + cb_ref[:, c0:c0 + 128]
            for b in range(8):
                wb = w if b == 0 else pltpu.roll(w, win - b, axis=0)
                for a in range(4):
                    k = 8 * a + b - 1
                    if 0 <= k < CONV_W:
                        acc = acc + dw_ref[k:k + 1, c0:c0 + 128] * wb[8 * a:8 * a + CONV_TILE, :]
            tmp_ref[i * CONV_TILE:(i + 1) * CONV_TILE, c0:c0 + 128] = acc

    pwin = CONV_TILE + 16
    lane128 = lax.broadcasted_iota(jnp.int32, (1, 128), 1)
    first_half = lane128 < HEAD_D

    def pool_tile(i):
        s, t0 = divmod(i, tiles_per_seq)
        t0 *= CONV_TILE
        interior = t0 >= 8 and t0 + CONV_TILE + 8 <= seq_len
        outs = []
        for col, (w_small, w_big) in enumerate(((2, 4), (8, 16))):
            w = pad2_ref[s, t0 + PAD - 8:t0 + PAD - 8 + pwin, col * 128:(col + 1) * 128]

            rolled = {0: w}

            def shifted(b, w=w, rolled=rolled):
                if b not in rolled:
                    rolled[b] = pltpu.roll(w, pwin - b, axis=0)
                return rolled[b]

            def lo(b):
                return shifted(b)[0:CONV_TILE, :]

            def hi(b):
                return shifted(b)[8:8 + CONV_TILE, :]

            tok = hi(0)
            if col == 0:
                s_small = lo(7) + tok
                s_big = s_small + lo(6) + hi(1)
            else:
                s_small = lo(7) + tok + lo(6) + hi(1) + lo(5) + lo(4) + hi(2) + hi(3)
                s_big = s_small + lo(3) + lo(2) + lo(1) + lo(0) + hi(4) + hi(5) + hi(6) + hi(7)
            ssum = jnp.where(first_half, s_small, s_big)
            if interior:
                mean = ssum * jnp.where(first_half, 1.0 / w_small, 1.0 / w_big)
            else:
                half = jnp.where(first_half, w_small // 2, w_big // 2)
                tpos = t0 + lax.broadcasted_iota(jnp.int32, (CONV_TILE, 128), 0)
                cnt = jnp.minimum(tpos + half, seq_len) - jnp.maximum(tpos - half, 0)
                mean = ssum / cnt.astype(F32)
            outs.append(mean - tok)
        tmp2_ref[i * CONV_TILE:(i + 1) * CONV_TILE, :] = jnp.concatenate(outs, axis=1)

    col_tiles = [COL_C] + list(range(COL_B, COL_C, GROUP_W)) + list(range(COL_D, IN_COLS, GROUP_W))
    per_step = -(-N_CONV_TILES // (len(col_tiles) - 1))
    conv_next = 0
    pool_next = 0
    for step, c0 in enumerate(col_tiles):
        proj_ref[:, c0:c0 + GROUP_W] = jnp.dot(
            hbf_ref[...], w_in_ref[:, c0:c0 + GROUP_W], preferred_element_type=F32)
        if step == 1:
            for s in range(n_seq):
                pad2_ref[s, PAD:PAD + seq_len, :] = proj_ref[s * seq_len:(s + 1) * seq_len, COL_C:COL_D]
        for _ in range(per_step):
            if conv_next < N_CONV_TILES:
                conv_tile(conv_next)
                conv_next += 1
            if step >= 1 and pool_next < N_CONV_TILES:
                pool_tile(pool_next)
                pool_next += 1
    assert conv_next == N_CONV_TILES and pool_next == N_CONV_TILES

    def ln_pw_body(i, _):
        r0 = pl.multiple_of(i * ROW_TILE, ROW_TILE)
        c = tmp_ref[pl.ds(r0, ROW_TILE), :]
        mu = jnp.mean(c, axis=-1, keepdims=True)
        cen = c - mu
        var = jnp.mean(cen * cen, axis=-1, keepdims=True)
        hn = cen * lax.rsqrt(var + EPS) * lng_ref[...] + lnb_ref[...]
        ya = jnp.dot(_silu(hn).astype(BF16), pwb_ref[...], preferred_element_type=F32)
        cat_ref[pl.ds(r0, ROW_TILE), 0:GROUP_W] = ya.astype(BF16)
        return 0
    lax.fori_loop(0, TB // ROW_TILE, ln_pw_body, 0, unroll=2)

    def gmlp_body(i, _):
        r0 = pl.multiple_of(i * CHUNK, CHUNK)
        u = proj_ref[pl.ds(r0, CHUNK), COL_B:COL_B + GROUP_W]
        v = proj_ref[pl.ds(r0, CHUNK), COL_B + GROUP_W:COL_C]
        vstack = _head_stack(v, lane_head).astype(BF16)
        sg = jnp.dot(wcat_ref[...], vstack, preferred_element_type=F32) + gbias_ref[...]
        cat_ref[pl.ds(r0, CHUNK), GROUP_W:2 * GROUP_W] = (u * sg).astype(BF16)
        return 0
    lax.fori_loop(0, N_CHUNKS, gmlp_body, 0, unroll=4)

    def pool_mix_body(i, _):
        r0 = pl.multiple_of(i * ROW_TILE, ROW_TILE)
        yc = jnp.dot(tmp2_ref[pl.ds(r0, ROW_TILE), :].astype(BF16), pbd_ref[...],
                     preferred_element_type=F32) * pscale_ref[...]
        cat_ref[pl.ds(r0, ROW_TILE), 2 * GROUP_W:3 * GROUP_W] = yc.astype(BF16)
        return 0
    lax.fori_loop(0, TB // ROW_TILE, pool_mix_body, 0, unroll=2)

    ri = lax.broadcasted_iota(jnp.int32, (CHUNK, HEADS * CHUNK), 0)
    ci = lax.broadcasted_iota(jnp.int32, (CHUNK, HEADS * CHUNK), 1)
    cj = ci % CHUNK
    chead = ci // CHUNK
    rq = lax.broadcasted_iota(jnp.int32, (CHUNK, GROUP_W), 0).astype(F32)
    for d in range(2):
        lgs = [lg_ref[(2 * layer + d) * HEADS + h] for h in range(HEADS)]
        lg_wide = jnp.where(chead == 0, lgs[0], jnp.where(chead == 1, lgs[1],
                                                          jnp.where(chead == 2, lgs[2], lgs[3])))
        lg_lane = jnp.where(lane_head == 0, lgs[0], jnp.where(lane_head == 1, lgs[1],
                                                              jnp.where(lane_head == 2, lgs[2], lgs[3])))
        dist = (ri - cj) if d == 0 else (cj - ri)
        keep = dist >= 0
        dcat_ref[d] = jnp.where(keep, jnp.exp(jnp.where(keep, dist, 0).astype(F32) * lg_wide), 0.0)
        if d == 0:
            qdec_ref[d] = jnp.exp((rq + 1.0) * lg_lane)
            kdec_ref[d] = jnp.exp((CHUNK - 1.0 - rq) * lg_lane)
        else:
            qdec_ref[d] = jnp.exp((CHUNK - rq) * lg_lane)
            kdec_ref[d] = jnp.exp(rq * lg_lane)
        sdec_ref[d] = jnp.exp(float(CHUNK) * lg_lane)

    rr = lax.broadcasted_iota(jnp.int32, (GROUP_W, GROUP_W), 0) // HEAD_D
    cc = lax.broadcasted_iota(jnp.int32, (GROUP_W, GROUP_W), 1) // HEAD_D
    gmat_ref[...] = jnp.where(rr == cc, 1.0 / HEAD_D, 0.0).astype(BF16)

    lane_bit = (lane & 16) == 0
    k_scale = HEAD_D ** -0.5

    def rope(z, r0):
        if not rotate:
            return z
        cos = cos_ref[pl.ds(r0, CHUNK), :]
        sin = sin_ref[pl.ds(r0, CHUNK), :]
        halves = []
        for c0 in (0, 128):
            zz = z[:, c0:c0 + 128]
            partner = jnp.where(lane_bit[:, c0:c0 + 128],
                                pltpu.roll(zz, 128 - 16, axis=1), pltpu.roll(zz, 16, axis=1))
            halves.append(partner)
        return z * cos + jnp.concatenate(halves, axis=1) * sin

    def pair_stack(zb, pair):
        zero = jnp.zeros_like(zb)
        return jnp.concatenate([jnp.where(lane_head == h, zb, zero) for h in (2 * pair, 2 * pair + 1)], axis=0)

    def intra_body(c, _):
        r0 = pl.multiple_of(c * CHUNK, CHUNK)
        v = proj_ref[pl.ds(r0, CHUNK), COL_V:COL_V + GROUP_W]
        vb = v.astype(BF16)
        vstacks = [pair_stack(vb, pair) for pair in range(2)]
        o = None
        for d in range(2):
            qc0 = COL_D + 2 * d * GROUP_W
            q = rope(proj_ref[pl.ds(r0, CHUNK), qc0:qc0 + GROUP_W], r0)
            k = rope(proj_ref[pl.ds(r0, CHUNK), qc0 + GROUP_W:qc0 + 2 * GROUP_W], r0) * k_scale
            qb = q.astype(BF16)
            kb = k.astype(BF16)
            qb_ref[d, pl.ds(r0, CHUNK), :] = qb
            for pair in range(2):
                att = lax.dot_general(qb, pair_stack(kb, pair), (((1,), (1,)), ((), ())),
                                      preferred_element_type=F32)
                att = (att * dcat_ref[d, :, pair * 2 * CHUNK:(pair + 1) * 2 * CHUNK]).astype(BF16)
                od = jnp.dot(att, vstacks[pair], preferred_element_type=F32)
                o = od if o is None else o + od
            kd = (k * kdec_ref[d]).astype(BF16)
            upd = lax.dot_general(kd, vb, (((0,), (0,)), ((), ())), preferred_element_type=F32)
            compact = None
            for h in range(HEADS):
                part = jnp.where(lane_head == h, upd[h * HEAD_D:(h + 1) * HEAD_D, :], 0.0)
                compact = part if compact is None else compact + part
            upd_ref[d, c] = compact
        of_ref[pl.ds(r0, CHUNK), :] = o
        return 0
    lax.fori_loop(0, N_CHUNKS, intra_body, 0, unroll=2)

    for s in range(n_seq):
        for d in range(2):
            st = s0c_ref[d] if rotate else jnp.zeros((HEAD_D, GROUP_W), F32)
            order = range(n_chunk) if d == 0 else range(n_chunk - 1, -1, -1)
            for c in order:
                cg = s * n_chunk + c
                for h in range(HEADS):
                    sall_ref[d, cg, h * HEAD_D:(h + 1) * HEAD_D, :] = (
                        jnp.where(lane_head == h, st, 0.0).astype(BF16))
                st = st * sdec_ref[d] + upd_ref[d, cg]
            if not rotate:
                for h in range(HEADS):
                    if layer == 0:
                        st_ref[s, 0, d, h] = st[:, h * HEAD_D:(h + 1) * HEAD_D]
                    else:
                        st_ref[s, d, h] = st[:, h * HEAD_D:(h + 1) * HEAD_D]
    if not rotate and layer == 0:
        for s in range(n_seq):
            for later in range(1, DEPTH):
                for d in range(2):
                    for h in range(HEADS):
                        st_ref[s, later, d, h] = jnp.zeros((HEAD_D, HEAD_D), F32)

    def cross_body(c, _):
        r0 = pl.multiple_of(c * CHUNK, CHUNK)
        o = of_ref[pl.ds(r0, CHUNK), :]
        for d in range(2):
            o = o + jnp.dot(qb_ref[d, pl.ds(r0, CHUNK), :], sall_ref[d, c],
                            preferred_element_type=F32) * qdec_ref[d]
        of_ref[pl.ds(r0, CHUNK), :] = o
        return 0
    lax.fori_loop(0, N_CHUNKS, cross_body, 0, unroll=2)

    def center_body(i, _):
        r0 = pl.multiple_of(i * ROW_TILE, ROW_TILE)
        o = of_ref[pl.ds(r0, ROW_TILE), :]
        gmat = gmat_ref[...]
        o_hi = o.astype(BF16)
        o_lo = (o - o_hi.astype(F32)).astype(BF16)
        mu = (jnp.dot(o_hi, gmat, preferred_element_type=F32)
              + jnp.dot(o_lo, gmat, preferred_element_type=F32))
        of_ref[pl.ds(r0, ROW_TILE), :] = o - mu
        return 0
    lax.fori_loop(0, TB // ROW_TILE, center_body, 0, unroll=2)

    def gate_body(i, _):
        r0 = pl.multiple_of(i * ROW_TILE, ROW_TILE)
        cen = of_ref[pl.ds(r0, ROW_TILE), :]
        var = jnp.dot((cen * cen).astype(BF16), gmat_ref[...], preferred_element_type=F32)
        on = cen * lax.rsqrt(var + EPS)
        g = proj_ref[pl.ds(r0, ROW_TILE), COL_G:COL_G + GROUP_W]
        cat_ref[pl.ds(r0, ROW_TILE), 3 * GROUP_W:4 * GROUP_W] = (_silu(g) * on).astype(BF16)
        return 0
    lax.fori_loop(0, TB // ROW_TILE, gate_body, 0, unroll=2)

    def out_body(i, _):
        r0 = pl.multiple_of(i * ROW_TILE, ROW_TILE)
        y = jnp.dot(cat_ref[pl.ds(r0, ROW_TILE), :], w_out_ref[...], preferred_element_type=F32)
        y_ref[pl.ds(r0, ROW_TILE), :] = x_ref[pl.ds(r0, ROW_TILE), :] + ga1 * y
        return 0
    lax.fori_loop(0, TB // ROW_TILE, out_body, 0, unroll=2)


def _layer_spec(shape, layer):
    zeros = (0,) * len(shape)
    return pl.BlockSpec((None,) + tuple(shape), lambda i: (layer,) + zeros, pipeline_mode=pl.Buffered(1))


def _mod_spec(layer, mod_base, mod_stride):
    return pl.BlockSpec((None, 1, 6 * D_MODEL),
                        lambda i: (layer * MOD_ROWS + jnp.maximum(mod_base + mod_stride * i, 0), 0, 0))


def _mixer(x, mods, pp, *, layer, seq_len, rotate, mod_base, mod_stride, rope_tabs=None, s0=None, states=None):
    nb = x.shape[0]
    n_seq = TB // seq_len
    in_specs = [
        pl.BlockSpec((None, TB, D_MODEL), lambda i: (i, 0, 0)),
        _mod_spec(layer, mod_base, mod_stride),
        _layer_spec((1, D_MODEL), layer),
        _layer_spec((D_MODEL, IN_COLS), layer),
        _layer_spec((D_MODEL, D_MODEL), layer),
        _layer_spec((CONV_W, GROUP_W), layer),
        _layer_spec((1, GROUP_W), layer), _layer_spec((1, GROUP_W), layer), _layer_spec((1, GROUP_W), layer),
        _layer_spec((GROUP_W, GROUP_W), layer),
        _layer_spec((HEADS, CHUNK, CHUNK), layer),
        _layer_spec((CHUNK, GROUP_W), layer),
        _layer_spec((HEADS, HEAD_D, HEAD_D), layer),
        _layer_spec((1, GROUP_W), layer),
        pl.BlockSpec(memory_space=pltpu.SMEM),
    ]
    args = [x, mods, pp["g1"], pp["w_in"], pp["w_out"], pp["dw"], pp["cb"], pp["lng"], pp["lnb"],
            pp["pw"], pp["ws"], pp["gbias"], pp["pool_w"], pp["pscale"], pp["lg"]]
    out_shape = [jax.ShapeDtypeStruct((nb, TB, D_MODEL), F32)]
    out_specs = [pl.BlockSpec((None, TB, D_MODEL), lambda i: (i, 0, 0))]
    aliases = {}
    if rotate:
        in_specs += [_const_spec((TB, GROUP_W)), _const_spec((TB, GROUP_W)),
                     pl.BlockSpec((None, None, 2, HEADS, HEAD_D, HEAD_D), lambda i: (i, layer, 0, 0, 0, 0))]
        args += [rope_tabs[0], rope_tabs[1], s0]
    else:
        out_shape.append(jax.ShapeDtypeStruct((nb * n_seq, DEPTH, 2, HEADS, HEAD_D, HEAD_D), F32))
        if layer == 0:
            out_specs.append(pl.BlockSpec((n_seq, DEPTH, 2, HEADS, HEAD_D, HEAD_D),
                                          lambda i: (i, 0, 0, 0, 0, 0)))
        else:
            in_specs.append(pl.BlockSpec(memory_space=pl.ANY))
            args.append(states)
            aliases = {len(args) - 1: 1}
            out_specs.append(pl.BlockSpec((n_seq, None, 2, HEADS, HEAD_D, HEAD_D),
                                          lambda i: (i, layer, 0, 0, 0, 0)))
    scratch = [
        pltpu.VMEM((TB, D_MODEL), BF16),
        pltpu.VMEM((TB, IN_COLS), F32),
        pltpu.VMEM((n_seq, seq_len + 2 * PAD, GROUP_W), F32),
        pltpu.VMEM((n_seq, seq_len + 2 * PAD, GROUP_W), F32),
        pltpu.VMEM((TB, GROUP_W), F32),
        pltpu.VMEM((TB, GROUP_W), F32),
        pltpu.VMEM((2, CHUNK, HEADS * CHUNK), F32),
        pltpu.VMEM((2, CHUNK, GROUP_W), F32),
        pltpu.VMEM((2, CHUNK, GROUP_W), F32),
        pltpu.VMEM((2, 1, GROUP_W), F32),
        pltpu.VMEM((GROUP_W, GROUP_W), BF16),
        pltpu.VMEM((2, TB, GROUP_W), BF16),
        pltpu.VMEM((2, N_CHUNKS, HEAD_D, GROUP_W), F32),
        pltpu.VMEM((2, N_CHUNKS, GROUP_W, GROUP_W), BF16),
        pltpu.VMEM((GROUP_W, GROUP_W), BF16),
        pltpu.VMEM((CHUNK, HEADS * CHUNK), BF16),
        pltpu.VMEM((GROUP_W, GROUP_W), BF16),
        pltpu.VMEM((2, HEAD_D, GROUP_W), F32),
    ]
    outs = pl.pallas_call(
        functools.partial(_mixer_kernel, layer=layer, seq_len=seq_len, rotate=rotate),
        grid=(nb,),
        in_specs=in_specs,
        out_specs=out_specs,
        out_shape=out_shape,
        scratch_shapes=scratch,
        input_output_aliases=aliases,
        compiler_params=pltpu.CompilerParams(
            dimension_semantics=("arbitrary",), vmem_limit_bytes=VMEM_LIMIT_BYTES),
        name="mixer_lat" if rotate else "mixer_ctx",
    )(*args)
    return outs


def _ffn_kernel(x_ref, mod_ref, g2_ref, w_in_hbm, w_out_hbm, gf_ref, y_ref,
                hbf_ref, act_ref, wgu_ref, wo_ref, stg_g_ref, stg_u_ref, stg_o_ref, sem,
                *, layer, final_norm):
    pid = pl.program_id(0)
    sh2 = mod_ref[:, 3 * D_MODEL:4 * D_MODEL]
    sc2 = mod_ref[:, 4 * D_MODEL:5 * D_MODEL]
    ga2 = mod_ref[:, 5 * D_MODEL:6 * D_MODEL]

    def tile_copies(j, slot):
        return (
            pltpu.make_async_copy(w_in_hbm.at[layer, :, pl.ds(j * FF_TILE, FF_TILE)],
                                  stg_g_ref.at[slot], sem.at[0, slot]),
            pltpu.make_async_copy(w_in_hbm.at[layer, :, pl.ds(D_FF + j * FF_TILE, FF_TILE)],
                                  stg_u_ref.at[slot], sem.at[1, slot]),
            pltpu.make_async_copy(w_out_hbm.at[layer, pl.ds(j * FF_TILE, FF_TILE), :],
                                  stg_o_ref.at[slot], sem.at[2, slot]),
        )

    def start_tile(j):
        for cp in tile_copies(j, j % FF_RING):
            cp.start()

    @pl.when(pid == 0)
    def _():
        for j in range(FF_LOOKAHEAD):
            start_tile(j)

    def norm_body(i, _):
        r0 = pl.multiple_of(i * ROW_TILE, ROW_TILE)
        h = _norm_mod(x_ref[pl.ds(r0, ROW_TILE), :], g2_ref[...], sc2, sh2)
        hbf_ref[pl.ds(r0, ROW_TILE), :] = h.astype(BF16)
        return 0
    lax.fori_loop(0, TB // ROW_TILE, norm_body, 0)

    def hidden_tiles(fetch):
        for j in range(N_FF_TILES):
            gcols = slice(j * FF_TILE, (j + 1) * FF_TILE)
            ucols = slice(D_FF + j * FF_TILE, D_FF + (j + 1) * FF_TILE)
            if fetch:
                if j + FF_LOOKAHEAD < N_FF_TILES:
                    start_tile(j + FF_LOOKAHEAD)
                slot = j % FF_RING
                for cp in tile_copies(j, slot):
                    cp.wait()
                wgu_ref[:, gcols] = stg_g_ref[slot].astype(BF16)
                wgu_ref[:, ucols] = stg_u_ref[slot].astype(BF16)
                wo_ref[gcols, :] = stg_o_ref[slot].astype(BF16)
            gate = jnp.dot(hbf_ref[...], wgu_ref[:, gcols], preferred_element_type=F32)
            up = jnp.dot(hbf_ref[...], wgu_ref[:, ucols], preferred_element_type=F32)
            act_ref[:, gcols] = (_silu(gate) * up).astype(BF16)

    @pl.when(pid == 0)
    def _():
        hidden_tiles(True)

    @pl.when(pid != 0)
    def _():
        hidden_tiles(False)

    for r0 in range(0, TB, FF_OUT_ROWS):
        rows = slice(r0, r0 + FF_OUT_ROWS)
        y = x_ref[rows, :] + ga2 * jnp.dot(act_ref[rows, :], wo_ref[...], preferred_element_type=F32)
        if final_norm:
            ms = jnp.mean(y * y, axis=-1, keepdims=True)
            y = y * lax.rsqrt(ms + EPS) * gf_ref[...]
        y_ref[rows, :] = y


def _ffn(x, mods, g2, w_ffn_in, w_ffn_out, g_final, *, layer, mod_base, mod_stride, final_norm):
    nb = x.shape[0]
    return pl.pallas_call(
        functools.partial(_ffn_kernel, layer=layer, final_norm=final_norm),
        grid=(nb,),
        in_specs=[
            pl.BlockSpec((None, TB, D_MODEL), lambda i: (i, 0, 0)),
            _mod_spec(layer, mod_base, mod_stride),
            _layer_spec((1, D_MODEL), layer),
            pl.BlockSpec(memory_space=pl.ANY),
            pl.BlockSpec(memory_space=pl.ANY),
            _const_spec((1, D_MODEL)),
        ],
        out_specs=pl.BlockSpec((None, TB, D_MODEL), lambda i: (i, 0, 0)),
        out_shape=jax.ShapeDtypeStruct((nb, TB, D_MODEL), F32),
        scratch_shapes=[
            pltpu.VMEM((TB, D_MODEL), BF16),
            pltpu.VMEM((TB, D_FF), BF16),
            pltpu.VMEM((D_MODEL, 2 * D_FF), BF16),
            pltpu.VMEM((D_FF, D_MODEL), BF16),
            pltpu.VMEM((FF_RING, D_MODEL, FF_TILE), F32),
            pltpu.VMEM((FF_RING, D_MODEL, FF_TILE), F32),
            pltpu.VMEM((FF_RING, FF_TILE, D_MODEL), F32),
            pltpu.SemaphoreType.DMA((3, FF_RING)),
        ],
        compiler_params=pltpu.CompilerParams(
            dimension_semantics=("arbitrary",), vmem_limit_bytes=VMEM_LIMIT_BYTES),
        name="ffn",
    )(x, mods, g2, w_ffn_in, w_ffn_out, g_final)


def _rope_tables(seq_len):
    t = np.arange(seq_len)
    r = (t // GRID_W).astype(np.float32)
    c = (t % GRID_W).astype(np.float32)
    nf = HEAD_D // 4
    inv = np.float32(ROPE_BASE) ** (-np.arange(nf, dtype=np.float32) / np.float32(nf))
    ang_r = r[:, None] * inv
    ang_c = c[:, None] * inv
    cos = np.concatenate([np.cos(ang_r), np.cos(ang_r), np.cos(ang_c), np.cos(ang_c)], axis=-1)
    sin = np.concatenate([-np.sin(ang_r), np.sin(ang_r), -np.sin(ang_c), np.sin(ang_c)], axis=-1)
    return (jnp.asarray(np.tile(cos, (1, HEADS)), dtype=F32), jnp.asarray(np.tile(sin, (1, HEADS)), dtype=F32))


def kernel(x_prompt, x_sample, state_ret, c, c_ctx, w_ada, b_ada, g_norm1, g_norm2, w_in, w_out, conv_dw,
           conv_b, conv_ln_g, conv_ln_b, conv_pw, gmlp_ws, gmlp_b, pool_w, pool_scale, ret_decay, w_ffn_in,
           w_ffn_out, g_final):
    batch, seq, _ = x_prompt.shape
    dec_batch, dec_seq, _ = x_sample.shape
    assert dec_seq == TB and TB % seq == 0 and (batch * seq) % TB == 0
    assert 1 + dec_batch <= MOD_ROWS

    cs = jnp.concatenate([c_ctx[None, :], c, jnp.zeros((MOD_ROWS - 1 - dec_batch, D_MODEL), F32)], axis=0)
    mods = _ada_rows(cs, w_ada, b_ada).reshape(DEPTH * MOD_ROWS, 1, 6 * D_MODEL)
    rope_tabs = _rope_tables(dec_seq)
    g_final2 = g_final.reshape(1, D_MODEL)
    pp = {
        "g1": g_norm1.reshape(DEPTH, 1, D_MODEL),
        "g2": g_norm2.reshape(DEPTH, 1, D_MODEL),
        "w_in": w_in.astype(BF16),
        "w_out": w_out.astype(BF16),
        "dw": conv_dw,
        "cb": conv_b.reshape(DEPTH, 1, GROUP_W),
        "lng": conv_ln_g.reshape(DEPTH, 1, GROUP_W),
        "lnb": conv_ln_b.reshape(DEPTH, 1, GROUP_W),
        "pw": conv_pw,
        "ws": gmlp_ws,
        "gbias": jnp.repeat(jnp.swapaxes(gmlp_b, 1, 2), HEAD_D, axis=2),
        "pool_w": pool_w,
        "pscale": pool_scale.reshape(DEPTH, 1, GROUP_W),
        "lg": jax.nn.log_sigmoid(ret_decay.astype(F32)).reshape(DEPTH * 2 * HEADS),
    }

    xc = x_prompt.reshape(batch * seq // TB, TB, D_MODEL)
    xl = x_sample
    states = None
    for l in range(DEPTH):
        last = l == DEPTH - 1
        xc, states = _mixer(xc, mods, pp, layer=l, seq_len=seq, rotate=False, mod_base=0, mod_stride=0,
                            states=states)
        xc = _ffn(xc, mods, pp["g2"], w_ffn_in, w_ffn_out, g_final2, layer=l, mod_base=0, mod_stride=0,
                  final_norm=last)
        (xl,) = _mixer(xl, mods, pp, layer=l, seq_len=dec_seq, rotate=True, mod_base=1, mod_stride=1,
                       rope_tabs=rope_tabs, s0=state_ret.astype(F32))
        xl = _ffn(xl, mods, pp["g2"], w_ffn_in, w_ffn_out, g_final2, layer=l, mod_base=1, mod_stride=1,
                  final_norm=last)

    y_prompt = xc.reshape(batch, seq, D_MODEL)
    return (y_prompt, xl, states.astype(x_prompt.dtype))
```

```python
import functools

import jax
import jax.numpy as jnp
import numpy as np
from jax import lax
from jax.experimental import pallas as pl
from jax.experimental.pallas import tpu as pltpu

F32 = jnp.float32
BF16 = jnp.bfloat16

D_MODEL = 1024
DEPTH = 2
GRID_W = 64
GROUP_W = D_MODEL // 4
CONV_W = 31
CHUNK = 128
HEADS = 4
HEAD_D = GROUP_W // HEADS
POOL_WINDOWS = (2, 4, 8, 16)
ROPE_BASE = 10000.0
D_FF = 2816
IN_COLS = 11 * GROUP_W
EPS = 1e-6

TB = 1024
N_CHUNKS = TB // CHUNK
ROW_TILE = 256
CONV_TILE = 64
N_CONV_TILES = TB // CONV_TILE
PAD = 16
FF_TILE = 256
N_FF_TILES = D_FF // FF_TILE
FF_OUT_ROWS = 256
FF_LOOKAHEAD = 2
FF_RING = FF_LOOKAHEAD + 1
ADA_TILE = 1536
MOD_ROWS = 8
VMEM_LIMIT_BYTES = 60 * 1024 * 1024

COL_A, COL_B, COL_C, COL_D = 0, 2 * GROUP_W, 4 * GROUP_W, 5 * GROUP_W
COL_V, COL_G = COL_D + 4 * GROUP_W, COL_D + 5 * GROUP_W


def _sigmoid(x):
    return 1.0 / (1.0 + jnp.exp(-x))


def _silu(x):
    return x * _sigmoid(x)


def _norm_mod(x, g, scale, shift):
    ms = jnp.mean(x * x, axis=-1, keepdims=True)
    return (x * lax.rsqrt(ms + EPS) * g) * (1.0 + scale) + shift


def _head_stack(x, lane_head):
    return jnp.concatenate([jnp.where(lane_head == h, x, 0.0) for h in range(HEADS)], axis=0)


def _const_spec(shape):
    zeros = (0,) * len(shape)
    return pl.BlockSpec(shape, lambda i: zeros, pipeline_mode=pl.Buffered(1))


def _ada_kernel(c_ref, w_ref, b_ref, o_ref):
    a = _silu(c_ref[...]).astype(BF16)
    o_ref[...] = jnp.dot(a, w_ref[...].astype(BF16), preferred_element_type=F32) + b_ref[...]


def _ada_rows(cs, w_ada, b_ada):
    n_tiles = 6 * D_MODEL // ADA_TILE
    return pl.pallas_call(
        _ada_kernel,
        grid=(DEPTH, n_tiles),
        in_specs=[
            pl.BlockSpec((MOD_ROWS, D_MODEL), lambda l, j: (0, 0)),
            pl.BlockSpec((None, D_MODEL, ADA_TILE), lambda l, j: (l, 0, j)),
            pl.BlockSpec((None, 1, ADA_TILE), lambda l, j: (l, 0, j)),
        ],
        out_specs=pl.BlockSpec((None, MOD_ROWS, ADA_TILE), lambda l, j: (l, 0, j)),
        out_shape=jax.ShapeDtypeStruct((DEPTH, MOD_ROWS, 6 * D_MODEL), F32),
        compiler_params=pltpu.CompilerParams(
            dimension_semantics=("arbitrary", "arbitrary"), vmem_limit_bytes=VMEM_LIMIT_BYTES),
        name="ada_rows",
    )(cs, w_ada, b_ada.reshape(DEPTH, 1, 6 * D_MODEL))


def _mixer_kernel(*refs, layer, seq_len, rotate):
    n_seq = TB // seq_len
    n_chunk = seq_len // CHUNK
    it = iter(refs)
    x_ref, mod_ref, g1_ref, w_in_ref, w_out_ref = (next(it) for _ in range(5))
    dw_ref, cb_ref, lng_ref, lnb_ref, pw_ref = (next(it) for _ in range(5))
    ws_ref, gbias_ref, poolw_ref, pscale_ref, lg_ref = (next(it) for _ in range(5))
    if rotate:
        cos_ref, sin_ref, s0_ref = (next(it) for _ in range(3))
    elif layer > 0:
        next(it)
    y_ref = next(it)
    if not rotate:
        st_ref = next(it)
    (hbf_ref, proj_ref, pad_ref, pad2_ref, tmp_ref, tmp2_ref,
     dcat_ref, qdec_ref, kdec_ref, sdec_ref, gmat_ref, qb_ref, upd_ref, sall_ref,
     pwb_ref, wcat_ref, pbd_ref, s0c_ref) = it
    cat_ref = hbf_ref
    of_ref = tmp_ref

    pwb_ref[...] = pw_ref[...].astype(BF16)
    pbd_ref[...] = jnp.zeros((GROUP_W, GROUP_W), BF16)
    for h in range(HEADS):
        wcat_ref[:, h * CHUNK:(h + 1) * CHUNK] = ws_ref[h].astype(BF16)
        pbd_ref[h * HEAD_D:(h + 1) * HEAD_D, h * HEAD_D:(h + 1) * HEAD_D] = poolw_ref[h].astype(BF16)
        if rotate:
            for d in range(2):
                s0c_ref[d, :, h * HEAD_D:(h + 1) * HEAD_D] = s0_ref[d, h]

    sh1 = mod_ref[:, 0:D_MODEL]
    sc1 = mod_ref[:, D_MODEL:2 * D_MODEL]
    ga1 = mod_ref[:, 2 * D_MODEL:3 * D_MODEL]

    lane = lax.broadcasted_iota(jnp.int32, (1, GROUP_W), 1)
    lane_head = lane // HEAD_D
    tiles_per_seq = seq_len // CONV_TILE
    chunks_per_seq = seq_len // CHUNK

    def seq_and_offset(i, per_seq, size):
        if n_seq == 1:
            return 0, pl.multiple_of(i * size, size)
        return i // per_seq, pl.multiple_of((i % per_seq) * size, size)

    def norm_body(i, _):
        r0 = pl.multiple_of(i * ROW_TILE, ROW_TILE)
        h = _norm_mod(x_ref[pl.ds(r0, ROW_TILE), :], g1_ref[...], sc1, sh1).astype(BF16)
        hbf_ref[pl.ds(r0, ROW_TILE), :] = h
        proj_ref[pl.ds(r0, ROW_TILE), COL_A:COL_B] = jnp.dot(
            h, w_in_ref[:, COL_A:COL_B], preferred_element_type=F32)
        return 0
    lax.fori_loop(0, TB // ROW_TILE, norm_body, 0, unroll=True)

    for s in range(n_seq):
        for ref in (pad_ref, pad2_ref):
            ref[s, 0:PAD, :] = jnp.zeros((PAD, GROUP_W), F32)
            ref[s, PAD + seq_len:PAD + seq_len + PAD, :] = jnp.zeros((PAD, GROUP_W), F32)

    def glu_body(i, _):
        s, t0 = seq_and_offset(i, chunks_per_seq, CHUNK)
        r0 = pl.multiple_of(i * CHUNK, CHUNK)
        a1 = proj_ref[pl.ds(r0, CHUNK), COL_A:COL_A + GROUP_W]
        a2 = proj_ref[pl.ds(r0, CHUNK), COL_A + GROUP_W:COL_B]
        pad_ref[s, pl.ds(pl.multiple_of(t0 + PAD, 8), CHUNK), :] = a1 * _sigmoid(a2)
        return 0
    lax.fori_loop(0, N_CHUNKS, glu_body, 0, unroll=True)

    win = CONV_TILE + 2 * PAD

    def conv_tile(i):
        s, t0 = divmod(i, tiles_per_seq)
        t0 *= CONV_TILE
        for c0 in range(0, GROUP_W, 128):
            w = pad_ref[s, t0:t0 + win, c0:c0 + 128]
            acc = jnp.zeros((CONV_TILE, 128), F32) + cb_ref[:, c0:c0 + 128]
            for b in range(8):
                wb = w if b == 0 else pltpu.roll(w, win - b, axis=0)
                for a in range(4):
                    k = 8 * a + b - 1
                    if 0 <= k < CONV_W:
                        acc = acc + dw_ref[k:k + 1, c0:c0 + 128] * wb[8 * a:8 * a + CONV_TILE, :]
            tmp_ref[i * CONV_TILE:(i + 1) * CONV_TILE, c0:c0 + 128] = acc

    pwin = CONV_TILE + 16
    lane128 = lax.broadcasted_iota(jnp.int32, (1, 128), 1)
    first_half = lane128 < HEAD_D

    def pool_tile(i):
        s, t0 = divmod(i, tiles_per_seq)
        t0 *= CONV_TILE
        interior = t0 >= 8 and t0 + CONV_TILE + 8 <= seq_len
        outs = []
        for col, (w_small, w_big) in enumerate(((2, 4), (8, 16))):
            w = pad2_ref[s, t0 + PAD - 8:t0 + PAD - 8 + pwin, col * 128:(col + 1) * 128]

            rolled = {0: w}

            def shifted(b, w=w, rolled=rolled):
                if b not in rolled:
                    rolled[b] = pltpu.roll(w, pwin - b, axis=0)
                return rolled[b]

            def lo(b):
                return shifted(b)[0:CONV_TILE, :]

            def hi(b):
                return shifted(b)[8:8 + CONV_TILE, :]

            tok = hi(0)
            if col == 0:
                s_small = lo(7) + tok
                s_big = s_small + lo(6) + hi(1)
            else:
                s_small = lo(7) + tok + lo(6) + hi(1) + lo(5) + lo(4) + hi(2) + hi(3)
                s_big = s_small + lo(3) + lo(2) + lo(1) + lo(0) + hi(4) + hi(5) + hi(6) + hi(7)
            ssum = jnp.where(first_half, s_small, s_big)
            if interior:
                mean = ssum * jnp.where(first_half, 1.0 / w_small, 1.0 / w_big)
            else:
                half = jnp.where(first_half, w_small // 2, w_big // 2)
                tpos = t0 + lax.broadcasted_iota(jnp.int32, (CONV_TILE, 128), 0)
                cnt = jnp.minimum(tpos + half, seq_len) - jnp.maximum(tpos - half, 0)
                mean = ssum / cnt.astype(F32)
            outs.append(mean - tok)
        tmp2_ref[i * CONV_TILE:(i + 1) * CONV_TILE, :] = jnp.concatenate(outs, axis=1)

    col_tiles = [COL_C] + list(range(COL_B, COL_C, GROUP_W)) + list(range(COL_D, IN_COLS, GROUP_W))
    per_step = -(-N_CONV_TILES // (len(col_tiles) - 1))
    conv_next = 0
    pool_next = 0
    for step, c0 in enumerate(col_tiles):
        proj_ref[:, c0:c0 + GROUP_W] = jnp.dot(
            hbf_ref[...], w_in_ref[:, c0:c0 + GROUP_W], preferred_element_type=F32)
        if step == 1:
            for s in range(n_seq):
                pad2_ref[s, PAD:PAD + seq_len, :] = proj_ref[s * seq_len:(s + 1) * seq_len, COL_C:COL_D]
        for _ in range(per_step):
            if conv_next < N_CONV_TILES:
                conv_tile(conv_next)
                conv_next += 1
            if step >= 1 and pool_next < N_CONV_TILES:
                pool_tile(pool_next)
                pool_next += 1
    assert conv_next == N_CONV_TILES and pool_next == N_CONV_TILES

    def ln_pw_body(i, _):
        r0 = pl.multiple_of(i * ROW_TILE, ROW_TILE)
        c = tmp_ref[pl.ds(r0, ROW_TILE), :]
        mu = jnp.mean(c, axis=-1, keepdims=True)
        cen = c - mu
        var = jnp.mean(cen * cen, axis=-1, keepdims=True)
        hn = cen * lax.rsqrt(var + EPS) * lng_ref[...] + lnb_ref[...]
        ya = jnp.dot(_silu(hn).astype(BF16), pwb_ref[...], preferred_element_type=F32)
        cat_ref[pl.ds(r0, ROW_TILE), 0:GROUP_W] = ya.astype(BF16)
        return 0
    lax.fori_loop(0, TB // ROW_TILE, ln_pw_body, 0, unroll=True)

    def gmlp_body(i, _):
        r0 = pl.multiple_of(i * CHUNK, CHUNK)
        u = proj_ref[pl.ds(r0, CHUNK), COL_B:COL_B + GROUP_W]
        v = proj_ref[pl.ds(r0, CHUNK), COL_B + GROUP_W:COL_C]
        vstack = _head_stack(v, lane_head).astype(BF16)
        sg = jnp.dot(wcat_ref[...], vstack, preferred_element_type=F32) + gbias_ref[...]
        cat_ref[pl.ds(r0, CHUNK), GROUP_W:2 * GROUP_W] = (u * sg).astype(BF16)
        return 0
    lax.fori_loop(0, N_CHUNKS, gmlp_body, 0, unroll=True)

    def pool_mix_body(i, _):
        r0 = pl.multiple_of(i * ROW_TILE, ROW_TILE)
        yc = jnp.dot(tmp2_ref[pl.ds(r0, ROW_TILE), :].astype(BF16), pbd_ref[...],
                     preferred_element_type=F32) * pscale_ref[...]
        cat_ref[pl.ds(r0, ROW_TILE), 2 * GROUP_W:3 * GROUP_W] = yc.astype(BF16)
        return 0
    lax.fori_loop(0, TB // ROW_TILE, pool_mix_body, 0, unroll=True)

    ri = lax.broadcasted_iota(jnp.int32, (CHUNK, HEADS * CHUNK), 0)
    ci = lax.broadcasted_iota(jnp.int32, (CHUNK, HEADS * CHUNK), 1)
    cj = ci % CHUNK
    chead = ci // CHUNK
    rq = lax.broadcasted_iota(jnp.int32, (CHUNK, GROUP_W), 0).astype(F32)
    for d in range(2):
        lgs = [lg_ref[(2 * layer + d) * HEADS + h] for h in range(HEADS)]
        lg_wide = jnp.where(chead == 0, lgs[0], jnp.where(chead == 1, lgs[1],
                                                          jnp.where(chead == 2, lgs[2], lgs[3])))
        lg_lane = jnp.where(lane_head == 0, lgs[0], jnp.where(lane_head == 1, lgs[1],
                                                              jnp.where(lane_head == 2, lgs[2], lgs[3])))
        dist = (ri - cj) if d == 0 else (cj - ri)
        keep = dist >= 0
        dcat_ref[d] = jnp.where(keep, jnp.exp(jnp.where(keep, dist, 0).astype(F32) * lg_wide), 0.0)
        if d == 0:
            qdec_ref[d] = jnp.exp((rq + 1.0) * lg_lane)
            kdec_ref[d] = jnp.exp((CHUNK - 1.0 - rq) * lg_lane)
        else:
            qdec_ref[d] = jnp.exp((CHUNK - rq) * lg_lane)
            kdec_ref[d] = jnp.exp(rq * lg_lane)
        sdec_ref[d] = jnp.exp(float(CHUNK) * lg_lane)

    rr = lax.broadcasted_iota(jnp.int32, (GROUP_W, GROUP_W), 0) // HEAD_D
    cc = lax.broadcasted_iota(jnp.int32, (GROUP_W, GROUP_W), 1) // HEAD_D
    gmat_ref[...] = jnp.where(rr == cc, 1.0 / HEAD_D, 0.0).astype(BF16)

    lane_bit = (lane & 16) == 0
    k_scale = HEAD_D ** -0.5

    def rope(z, r0):
        if not rotate:
            return z
        cos = cos_ref[pl.ds(r0, CHUNK), :]
        sin = sin_ref[pl.ds(r0, CHUNK), :]
        halves = []
        for c0 in (0, 128):
            zz = z[:, c0:c0 + 128]
            partner = jnp.where(lane_bit[:, c0:c0 + 128],
                                pltpu.roll(zz, 128 - 16, axis=1), pltpu.roll(zz, 16, axis=1))
            halves.append(partner)
        return z * cos + jnp.concatenate(halves, axis=1) * sin

    def pair_stack(zb, pair):
        zero = jnp.zeros_like(zb)
        return jnp.concatenate([jnp.where(lane_head == h, zb, zero) for h in (2 * pair, 2 * pair + 1)], axis=0)

    def intra_body(c, _):
        r0 = pl.multiple_of(c * CHUNK, CHUNK)
        v = proj_ref[pl.ds(r0, CHUNK), COL_V:COL_V + GROUP_W]
        vb = v.astype(BF16)
        vstacks = [pair_stack(vb, pair) for pair in range(2)]
        o = None
        for d in range(2):
            qc0 = COL_D + 2 * d * GROUP_W
            q = rope(proj_ref[pl.ds(r0, CHUNK), qc0:qc0 + GROUP_W], r0)
            k = rope(proj_ref[pl.ds(r0, CHUNK), qc0 + GROUP_W:qc0 + 2 * GROUP_W], r0) * k_scale
            qb = q.astype(BF16)
            kb = k.astype(BF16)
            qb_ref[d, pl.ds(r0, CHUNK), :] = qb
            for pair in range(2):
                att = lax.dot_general(qb, pair_stack(kb, pair), (((1,), (1,)), ((), ())),
                                      preferred_element_type=F32)
                att = (att * dcat_ref[d, :, pair * 2 * CHUNK:(pair + 1) * 2 * CHUNK]).astype(BF16)
                od = jnp.dot(att, vstacks[pair], preferred_element_type=F32)
                o = od if o is None else o + od
            kd = (k * kdec_ref[d]).astype(BF16)
            upd = lax.dot_general(kd, vb, (((0,), (0,)), ((), ())), preferred_element_type=F32)
            compact = None
            for h in range(HEADS):
                part = jnp.where(lane_head == h, upd[h * HEAD_D:(h + 1) * HEAD_D, :], 0.0)
                compact = part if compact is None else compact + part
            upd_ref[d, c] = compact
        of_ref[pl.ds(r0, CHUNK), :] = o
        return 0
    lax.fori_loop(0, N_CHUNKS, intra_body, 0, unroll=2)

    for s in range(n_seq):
        for d in range(2):
            st = s0c_ref[d] if rotate else jnp.zeros((HEAD_D, GROUP_W), F32)
            order = range(n_chunk) if d == 0 else range(n_chunk - 1, -1, -1)
            for c in order:
                cg = s * n_chunk + c
                for h in range(HEADS):
                    sall_ref[d, cg, h * HEAD_D:(h + 1) * HEAD_D, :] = (
                        jnp.where(lane_head == h, st, 0.0).astype(BF16))
                st = st * sdec_ref[d] + upd_ref[d, cg]
            if not rotate:
                for h in range(HEADS):
                    if layer == 0:
                        st_ref[s, 0, d, h] = st[:, h * HEAD_D:(h + 1) * HEAD_D]
                    else:
                        st_ref[s, d, h] = st[:, h * HEAD_D:(h + 1) * HEAD_D]
    if not rotate and layer == 0:
        for s in range(n_seq):
            for later in range(1, DEPTH):
                for d in range(2):
                    for h in range(HEADS):
                        st_ref[s, later, d, h] = jnp.zeros((HEAD_D, HEAD_D), F32)

    def cross_body(c, _):
        r0 = pl.multiple_of(c * CHUNK, CHUNK)
        o = of_ref[pl.ds(r0, CHUNK), :]
        for d in range(2):
            o = o + jnp.dot(qb_ref[d, pl.ds(r0, CHUNK), :], sall_ref[d, c],
                            preferred_element_type=F32) * qdec_ref[d]
        of_ref[pl.ds(r0, CHUNK), :] = o
        return 0
    lax.fori_loop(0, N_CHUNKS, cross_body, 0, unroll=True)

    def center_body(i, _):
        r0 = pl.multiple_of(i * ROW_TILE, ROW_TILE)
        o = of_ref[pl.ds(r0, ROW_TILE), :]
        gmat = gmat_ref[...]
        o_hi = o.astype(BF16)
        o_lo = (o - o_hi.astype(F32)).astype(BF16)
        mu = (jnp.dot(o_hi, gmat, preferred_element_type=F32)
              + jnp.dot(o_lo, gmat, preferred_element_type=F32))
        of_ref[pl.ds(r0, ROW_TILE), :] = o - mu
        return 0
    lax.fori_loop(0, TB // ROW_TILE, center_body, 0, unroll=True)

    def gate_body(i, _):
        r0 = pl.multiple_of(i * ROW_TILE, ROW_TILE)
        cen = of_ref[pl.ds(r0, ROW_TILE), :]
        var = jnp.dot((cen * cen).astype(BF16), gmat_ref[...], preferred_element_type=F32)
        on = cen * lax.rsqrt(var + EPS)
        g = proj_ref[pl.ds(r0, ROW_TILE), COL_G:COL_G + GROUP_W]
        cat_ref[pl.ds(r0, ROW_TILE), 3 * GROUP_W:4 * GROUP_W] = (_silu(g) * on).astype(BF16)
        return 0
    lax.fori_loop(0, TB // ROW_TILE, gate_body, 0, unroll=True)

    def out_body(i, _):
        r0 = pl.multiple_of(i * ROW_TILE, ROW_TILE)
        y = jnp.dot(cat_ref[pl.ds(r0, ROW_TILE), :], w_out_ref[...], preferred_element_type=F32)
        y_ref[pl.ds(r0, ROW_TILE), :] = x_ref[pl.ds(r0, ROW_TILE), :] + ga1 * y
        return 0
    lax.fori_loop(0, TB // ROW_TILE, out_body, 0, unroll=True)


def _layer_spec(shape, layer):
    zeros = (0,) * len(shape)
    return pl.BlockSpec((None,) + tuple(shape), lambda i: (layer,) + zeros, pipeline_mode=pl.Buffered(1))


def _mod_spec(layer, mod_base, mod_stride):
    return pl.BlockSpec((None, 1, 6 * D_MODEL),
                        lambda i: (layer * MOD_ROWS + jnp.maximum(mod_base + mod_stride * i, 0), 0, 0))


def _mixer(x, mods, pp, *, layer, seq_len, rotate, mod_base, mod_stride, rope_tabs=None, s0=None, states=None):
    nb = x.shape[0]
    n_seq = TB // seq_len
    in_specs = [
        pl.BlockSpec((None, TB, D_MODEL), lambda i: (i, 0, 0)),
        _mod_spec(layer, mod_base, mod_stride),
        _layer_spec((1, D_MODEL), layer),
        _layer_spec((D_MODEL, IN_COLS), layer),
        _layer_spec((D_MODEL, D_MODEL), layer),
        _layer_spec((CONV_W, GROUP_W), layer),
        _layer_spec((1, GROUP_W), layer), _layer_spec((1, GROUP_W), layer), _layer_spec((1, GROUP_W), layer),
        _layer_spec((GROUP_W, GROUP_W), layer),
        _layer_spec((HEADS, CHUNK, CHUNK), layer),
        _layer_spec((CHUNK, GROUP_W), layer),
        _layer_spec((HEADS, HEAD_D, HEAD_D), layer),
        _layer_spec((1, GROUP_W), layer),
        pl.BlockSpec(memory_space=pltpu.SMEM),
    ]
    args = [x, mods, pp["g1"], pp["w_in"], pp["w_out"], pp["dw"], pp["cb"], pp["lng"], pp["lnb"],
            pp["pw"], pp["ws"], pp["gbias"], pp["pool_w"], pp["pscale"], pp["lg"]]
    out_shape = [jax.ShapeDtypeStruct((nb, TB, D_MODEL), F32)]
    out_specs = [pl.BlockSpec((None, TB, D_MODEL), lambda i: (i, 0, 0))]
    aliases = {}
    if rotate:
        in_specs += [_const_spec((TB, GROUP_W)), _const_spec((TB, GROUP_W)),
                     pl.BlockSpec((None, None, 2, HEADS, HEAD_D, HEAD_D), lambda i: (i, layer, 0, 0, 0, 0))]
        args += [rope_tabs[0], rope_tabs[1], s0]
    else:
        out_shape.append(jax.ShapeDtypeStruct((nb * n_seq, DEPTH, 2, HEADS, HEAD_D, HEAD_D), F32))
        if layer == 0:
            out_specs.append(pl.BlockSpec((n_seq, DEPTH, 2, HEADS, HEAD_D, HEAD_D),
                                          lambda i: (i, 0, 0, 0, 0, 0)))
        else:
            in_specs.append(pl.BlockSpec(memory_space=pl.ANY))
            args.append(states)
            aliases = {len(args) - 1: 1}
            out_specs.append(pl.BlockSpec((n_seq, None, 2, HEADS, HEAD_D, HEAD_D),
                                          lambda i: (i, layer, 0, 0, 0, 0)))
    scratch = [
        pltpu.VMEM((TB, D_MODEL), BF16),
        pltpu.VMEM((TB, IN_COLS), F32),
        pltpu.VMEM((n_seq, seq_len + 2 * PAD, GROUP_W), F32),
        pltpu.VMEM((n_seq, seq_len + 2 * PAD, GROUP_W), F32),
        pltpu.VMEM((TB, GROUP_W), F32),
        pltpu.VMEM((TB, GROUP_W), F32),
        pltpu.VMEM((2, CHUNK, HEADS * CHUNK), F32),
        pltpu.VMEM((2, CHUNK, GROUP_W), F32),
        pltpu.VMEM((2, CHUNK, GROUP_W), F32),
        pltpu.VMEM((2, 1, GROUP_W), F32),
        pltpu.VMEM((GROUP_W, GROUP_W), BF16),
        pltpu.VMEM((2, TB, GROUP_W), BF16),
        pltpu.VMEM((2, N_CHUNKS, HEAD_D, GROUP_W), F32),
        pltpu.VMEM((2, N_CHUNKS, GROUP_W, GROUP_W), BF16),
        pltpu.VMEM((GROUP_W, GROUP_W), BF16),
        pltpu.VMEM((CHUNK, HEADS * CHUNK), BF16),
        pltpu.VMEM((GROUP_W, GROUP_W), BF16),
        pltpu.VMEM((2, HEAD_D, GROUP_W), F32),
    ]
    outs = pl.pallas_call(
        functools.partial(_mixer_kernel, layer=layer, seq_len=seq_len, rotate=rotate),
        grid=(nb,),
        in_specs=in_specs,
        out_specs=out_specs,
        out_shape=out_shape,
        scratch_shapes=scratch,
        input_output_aliases=aliases,
        compiler_params=pltpu.CompilerParams(
            dimension_semantics=("arbitrary",), vmem_limit_bytes=VMEM_LIMIT_BYTES),
        name="mixer_lat" if rotate else "mixer_ctx",
    )(*args)
    return outs


def _ffn_kernel(x_ref, mod_ref, g2_ref, w_in_hbm, w_out_hbm, gf_ref, y_ref,
                hbf_ref, act_ref, wgu_ref, wo_ref, stg_g_ref, stg_u_ref, stg_o_ref, sem,
                *, layer, final_norm):
    pid = pl.program_id(0)
    sh2 = mod_ref[:, 3 * D_MODEL:4 * D_MODEL]
    sc2 = mod_ref[:, 4 * D_MODEL:5 * D_MODEL]
    ga2 = mod_ref[:, 5 * D_MODEL:6 * D_MODEL]

    def tile_copies(j, slot):
        return (
            pltpu.make_async_copy(w_in_hbm.at[layer, :, pl.ds(j * FF_TILE, FF_TILE)],
                                  stg_g_ref.at[slot], sem.at[0, slot]),
            pltpu.make_async_copy(w_in_hbm.at[layer, :, pl.ds(D_FF + j * FF_TILE, FF_TILE)],
                                  stg_u_ref.at[slot], sem.at[1, slot]),
            pltpu.make_async_copy(w_out_hbm.at[layer, pl.ds(j * FF_TILE, FF_TILE), :],
                                  stg_o_ref.at[slot], sem.at[2, slot]),
        )

    def start_tile(j):
        for cp in tile_copies(j, j % FF_RING):
            cp.start()

    @pl.when(pid == 0)
    def _():
        for j in range(FF_LOOKAHEAD):
            start_tile(j)

    def norm_body(i, _):
        r0 = pl.multiple_of(i * ROW_TILE, ROW_TILE)
        h = _norm_mod(x_ref[pl.ds(r0, ROW_TILE), :], g2_ref[...], sc2, sh2)
        hbf_ref[pl.ds(r0, ROW_TILE), :] = h.astype(BF16)
        return 0
    lax.fori_loop(0, TB // ROW_TILE, norm_body, 0)

    def hidden_tiles(fetch):
        for j in range(N_FF_TILES):
            gcols = slice(j * FF_TILE, (j + 1) * FF_TILE)
            ucols = slice(D_FF + j * FF_TILE, D_FF + (j + 1) * FF_TILE)
            if fetch:
                if j + FF_LOOKAHEAD < N_FF_TILES:
                    start_tile(j + FF_LOOKAHEAD)
                slot = j % FF_RING
                for cp in tile_copies(j, slot):
                    cp.wait()
                wgu_ref[:, gcols] = stg_g_ref[slot].astype(BF16)
                wgu_ref[:, ucols] = stg_u_ref[slot].astype(BF16)
                wo_ref[gcols, :] = stg_o_ref[slot].astype(BF16)
            gate = jnp.dot(hbf_ref[...], wgu_ref[:, gcols], preferred_element_type=F32)
            up = jnp.dot(hbf_ref[...], wgu_ref[:, ucols], preferred_element_type=F32)
            act_ref[:, gcols] = (_silu(gate) * up).astype(BF16)

    @pl.when(pid == 0)
    def _():
        hidden_tiles(True)

    @pl.when(pid != 0)
    def _():
        hidden_tiles(False)

    for r0 in range(0, TB, FF_OUT_ROWS):
        rows = slice(r0, r0 + FF_OUT_ROWS)
        y = x_ref[rows, :] + ga2 * jnp.dot(act_ref[rows, :], wo_ref[...], preferred_element_type=F32)
        if final_norm:
            ms = jnp.mean(y * y, axis=-1, keepdims=True)
            y = y * lax.rsqrt(ms + EPS) * gf_ref[...]
        y_ref[rows, :] = y


def _ffn(x, mods, g2, w_ffn_in, w_ffn_out, g_final, *, layer, mod_base, mod_stride, final_norm):
    nb = x.shape[0]
    return pl.pallas_call(
        functools.partial(_ffn_kernel, layer=layer, final_norm=final_norm),
        grid=(nb,),
        in_specs=[
            pl.BlockSpec((None, TB, D_MODEL), lambda i: (i, 0, 0)),
            _mod_spec(layer, mod_base, mod_stride),
            _layer_spec((1, D_MODEL), layer),
            pl.BlockSpec(memory_space=pl.ANY),
            pl.BlockSpec(memory_space=pl.ANY),
            _const_spec((1, D_MODEL)),
        ],
        out_specs=pl.BlockSpec((None, TB, D_MODEL), lambda i: (i, 0, 0)),
        out_shape=jax.ShapeDtypeStruct((nb, TB, D_MODEL), F32),
        scratch_shapes=[
            pltpu.VMEM((TB, D_MODEL), BF16),
            pltpu.VMEM((TB, D_FF), BF16),
            pltpu.VMEM((D_MODEL, 2 * D_FF), BF16),
            pltpu.VMEM((D_FF, D_MODEL), BF16),
            pltpu.VMEM((FF_RING, D_MODEL, FF_TILE), F32),
            pltpu.VMEM((FF_RING, D_MODEL, FF_TILE), F32),
            pltpu.VMEM((FF_RING, FF_TILE, D_MODEL), F32),
            pltpu.SemaphoreType.DMA((3, FF_RING)),
        ],
        compiler_params=pltpu.CompilerParams(
            dimension_semantics=("arbitrary",), vmem_limit_bytes=VMEM_LIMIT_BYTES),
        name="ffn",
    )(x, mods, g2, w_ffn_in, w_ffn_out, g_final)


def _rope_tables(seq_len):
    t = np.arange(seq_len)
    r = (t // GRID_W).astype(np.float32)
    c = (t % GRID_W).astype(np.float32)
    nf = HEAD_D // 4
    inv = np.float32(ROPE_BASE) ** (-np.arange(nf, dtype=np.float32) / np.float32(nf))
    ang_r = r[:, None] * inv
    ang_c = c[:, None] * inv
    cos = np.concatenate([np.cos(ang_r), np.cos(ang_r), np.cos(ang_c), np.cos(ang_c)], axis=-1)
    sin = np.concatenate([-np.sin(ang_r), np.sin(ang_r), -np.sin(ang_c), np.sin(ang_c)], axis=-1)
    return (jnp.asarray(np.tile(cos, (1, HEADS)), dtype=F32), jnp.asarray(np.tile(sin, (1, HEADS)), dtype=F32))


def kernel(x_prompt, x_sample, state_ret, c, c_ctx, w_ada, b_ada, g_norm1, g_norm2, w_in, w_out, conv_dw,
           conv_b, conv_ln_g, conv_ln_b, conv_pw, gmlp_ws, gmlp_b, pool_w, pool_scale, ret_decay, w_ffn_in,
           w_ffn_out, g_final):
    batch, seq, _ = x_prompt.shape
    dec_batch, dec_seq, _ = x_sample.shape
    assert dec_seq == TB and TB % seq == 0 and (batch * seq) % TB == 0
    assert 1 + dec_batch <= MOD_ROWS

    cs = jnp.concatenate([c_ctx[None, :], c, jnp.zeros((MOD_ROWS - 1 - dec_batch, D_MODEL), F32)], axis=0)
    mods = _ada_rows(cs, w_ada, b_ada).reshape(DEPTH * MOD_ROWS, 1, 6 * D_MODEL)
    rope_tabs = _rope_tables(dec_seq)
    g_final2 = g_final.reshape(1, D_MODEL)
    pp = {
        "g1": g_norm1.reshape(DEPTH, 1, D_MODEL),
        "g2": g_norm2.reshape(DEPTH, 1, D_MODEL),
        "w_in": w_in.astype(BF16),
        "w_out": w_out.astype(BF16),
        "dw": conv_dw,
        "cb": conv_b.reshape(DEPTH, 1, GROUP_W),
        "lng": conv_ln_g.reshape(DEPTH, 1, GROUP_W),
        "lnb": conv_ln_b.reshape(DEPTH, 1, GROUP_W),
        "pw": conv_pw,
        "ws": gmlp_ws,
        "gbias": jnp.repeat(jnp.swapaxes(gmlp_b, 1, 2), HEAD_D, axis=2),
        "pool_w": pool_w,
        "pscale": pool_scale.reshape(DEPTH, 1, GROUP_W),
        "lg": jax.nn.log_sigmoid(ret_decay.astype(F32)).reshape(DEPTH * 2 * HEADS),
    }

    xc = x_prompt.reshape(batch * seq // TB, TB, D_MODEL)
    xl = x_sample
    states = None
    for l in range(DEPTH):
        last = l == DEPTH - 1
        xc, states = _mixer(xc, mods, pp, layer=l, seq_len=seq, rotate=False, mod_base=0, mod_stride=0,
                            states=states)
        xc = _ffn(xc, mods, pp["g2"], w_ffn_in, w_ffn_out, g_final2, layer=l, mod_base=0, mod_stride=0,
                  final_norm=last)
        (xl,) = _mixer(xl, mods, pp, layer=l, seq_len=dec_seq, rotate=True, mod_base=1, mod_stride=1,
                       rope_tabs=rope_tabs, s0=state_ret.astype(F32))
        xl = _ffn(xl, mods, pp["g2"], w_ffn_in, w_ffn_out, g_final2, layer=l, mod_base=1, mod_stride=1,
                  final_norm=last)

    y_prompt = xc.reshape(batch, seq, D_MODEL)
    return (y_prompt, xl, states.astype(x_prompt.dtype))
```

```python
import functools

import jax
import jax.numpy as jnp
import numpy as np
from jax import lax
from jax.experimental import pallas as pl
from jax.experimental.pallas import tpu as pltpu

F32 = jnp.float32
BF16 = jnp.bfloat16

D_MODEL = 1024
DEPTH = 2
GRID_W = 64
GROUP_W = D_MODEL // 4
CONV_W = 31
CHUNK = 128
HEADS = 4
HEAD_D = GROUP_W // HEADS
POOL_WINDOWS = (2, 4, 8, 16)
ROPE_BASE = 10000.0
D_FF = 2816
IN_COLS = 11 * GROUP_W
EPS = 1e-6

TB = 1024
N_CHUNKS = TB // CHUNK
ROW_TILE = 256
CONV_TILE = 64
N_CONV_TILES = TB // CONV_TILE
PAD = 16
FF_TILE = 256
N_FF_TILES = D_FF // FF_TILE
FF_OUT_ROWS = 256
FF_LOOKAHEAD = 2
FF_RING = FF_LOOKAHEAD + 1
ADA_TILE = 1536
MOD_ROWS = 8
VMEM_LIMIT_BYTES = 60 * 1024 * 1024

COL_A, COL_B, COL_C, COL_D = 0, 2 * GROUP_W, 4 * GROUP_W, 5 * GROUP_W
COL_V, COL_G = COL_D + 4 * GROUP_W, COL_D + 5 * GROUP_W


def _sigmoid(x):
    return 1.0 / (1.0 + jnp.exp(-x))


def _silu(x):
    return x * _sigmoid(x)


def _norm_mod(x, g, scale, shift):
    ms = jnp.mean(x * x, axis=-1, keepdims=True)
    return (x * lax.rsqrt(ms + EPS) * g) * (1.0 + scale) + shift


def _head_stack(x, lane_head):
    return jnp.concatenate([jnp.where(lane_head == h, x, 0.0) for h in range(HEADS)], axis=0)


def _const_spec(shape):
    zeros = (0,) * len(shape)
    return pl.BlockSpec(shape, lambda i: zeros, pipeline_mode=pl.Buffered(1))


def _ada_kernel(c_ref, w_ref, b_ref, o_ref):
    a = _silu(c_ref[...]).astype(BF16)
    o_ref[...] = jnp.dot(a, w_ref[...].astype(BF16), preferred_element_type=F32) + b_ref[...]


def _ada_rows(cs, w_ada, b_ada):
    n_tiles = 6 * D_MODEL // ADA_TILE
    return pl.pallas_call(
        _ada_kernel,
        grid=(DEPTH, n_tiles),
        in_specs=[
            pl.BlockSpec((MOD_ROWS, D_MODEL), lambda l, j: (0, 0)),
            pl.BlockSpec((None, D_MODEL, ADA_TILE), lambda l, j: (l, 0, j)),
            pl.BlockSpec((None, 1, ADA_TILE), lambda l, j: (l, 0, j)),
        ],
        out_specs=pl.BlockSpec((None, MOD_ROWS, ADA_TILE), lambda l, j: (l, 0, j)),
        out_shape=jax.ShapeDtypeStruct((DEPTH, MOD_ROWS, 6 * D_MODEL), F32),
        compiler_params=pltpu.CompilerParams(
            dimension_semantics=("arbitrary", "arbitrary"), vmem_limit_bytes=VMEM_LIMIT_BYTES),
        name="ada_rows",
    )(cs, w_ada, b_ada.reshape(DEPTH, 1, 6 * D_MODEL))


def _mixer_kernel(*refs, layer, seq_len, rotate):
    n_seq = TB // seq_len
    n_chunk = seq_len // CHUNK
    it = iter(refs)
    x_ref, mod_ref, g1_ref, w_in_ref, w_out_ref = (next(it) for _ in range(5))
    dw_ref, cb_ref, lng_ref, lnb_ref, pw_ref = (next(it) for _ in range(5))
    ws_ref, gbias_ref, poolw_ref, pscale_ref, lg_ref = (next(it) for _ in range(5))
    if rotate:
        cos_ref, sin_ref, s0_ref = (next(it) for _ in range(3))
    elif layer > 0:
        next(it)
    y_ref = next(it)
    if not rotate:
        st_ref = next(it)
    (hbf_ref, proj_ref, pad_ref, pad2_ref, tmp_ref, tmp2_ref,
     dcat_ref, qdec_ref, kdec_ref, sdec_ref, gmat_ref, qb_ref, upd_ref, sall_ref,
     pwb_ref, wcat_ref, pbd_ref, s0c_ref) = it
    cat_ref = hbf_ref
    of_ref = tmp_ref

    pwb_ref[...] = pw_ref[...].astype(BF16)
    pbd_ref[...] = jnp.zeros((GROUP_W, GROUP_W), BF16)
    for h in range(HEADS):
        wcat_ref[:, h * CHUNK:(h + 1) * CHUNK] = ws_ref[h].astype(BF16)
        pbd_ref[h * HEAD_D:(h + 1) * HEAD_D, h * HEAD_D:(h + 1) * HEAD_D] = poolw_ref[h].astype(BF16)
        if rotate:
            for d in range(2):
                s0c_ref[d, :, h * HEAD_D:(h + 1) * HEAD_D] = s0_ref[d, h]

    sh1 = mod_ref[:, 0:D_MODEL]
    sc1 = mod_ref[:, D_MODEL:2 * D_MODEL]
    ga1 = mod_ref[:, 2 * D_MODEL:3 * D_MODEL]

    lane = lax.broadcasted_iota(jnp.int32, (1, GROUP_W), 1)
    lane_head = lane // HEAD_D
    tiles_per_seq = seq_len // CONV_TILE
    chunks_per_seq = seq_len // CHUNK

    def seq_and_offset(i, per_seq, size):
        if n_seq == 1:
            return 0, pl.multiple_of(i * size, size)
        return i // per_seq, pl.multiple_of((i % per_seq) * size, size)

    def norm_body(i, _):
        r0 = pl.multiple_of(i * ROW_TILE, ROW_TILE)
        h = _norm_mod(x_ref[pl.ds(r0, ROW_TILE), :], g1_ref[...], sc1, sh1).astype(BF16)
        hbf_ref[pl.ds(r0, ROW_TILE), :] = h
        proj_ref[pl.ds(r0, ROW_TILE), COL_A:COL_B] = jnp.dot(
            h, w_in_ref[:, COL_A:COL_B], preferred_element_type=F32)
        return 0
    lax.fori_loop(0, TB // ROW_TILE, norm_body, 0, unroll=True)

    for s in range(n_seq):
        for ref in (pad_ref, pad2_ref):
            ref[s, 0:PAD, :] = jnp.zeros((PAD, GROUP_W), F32)
            ref[s, PAD + seq_len:PAD + seq_len + PAD, :] = jnp.zeros((PAD, GROUP_W), F32)

    def glu_body(i, _):
        s, t0 = seq_and_offset(i, chunks_per_seq, CHUNK)
        r0 = pl.multiple_of(i * CHUNK, CHUNK)
        a1 = proj_ref[pl.ds(r0, CHUNK), COL_A:COL_A + GROUP_W]
        a2 = proj_ref[pl.ds(r0, CHUNK), COL_A + GROUP_W:COL_B]
        pad_ref[s, pl.ds(pl.multiple_of(t0 + PAD, 8), CHUNK), :] = a1 * _sigmoid(a2)
        return 0
    lax.fori_loop(0, N_CHUNKS, glu_body, 0, unroll=True)

    win = CONV_TILE + 2 * PAD

    def conv_tile(i):
        s, t0 = divmod(i, tiles_per_seq)
        t0 *= CONV_TILE
        for c0 in range(0, GROUP_W, 128):
            w = pad_ref[s, t0:t0 + win, c0:c0 + 128]
            acc = jnp.zeros((CONV_TILE, 128), F32) + cb_ref[:, c0:c0 + 128]
            for b in range(8):
                wb = w if b == 0 else pltpu.roll(w, win - b, axis=0)
                for a in range(4):
                    k = 8 * a + b - 1
                    if 0 <= k < CONV_W:
                        acc = acc + dw_ref[k:k + 1, c0:c0 + 128] * wb[8 * a:8 * a + CONV_TILE, :]
            tmp_ref[i * CONV_TILE:(i + 1) * CONV_TILE, c0:c0 + 128] = acc

    pwin = CONV_TILE + 16
    lane128 = lax.broadcasted_iota(jnp.int32, (1, 128), 1)
    first_half = lane128 < HEAD_D

    def pool_tile(i):
        s, t0 = divmod(i, tiles_per_seq)
        t0 *= CONV_TILE
        interior = t0 >= 8 and t0 + CONV_TILE + 8 <= seq_len
        outs = []
        for col, (w_small, w_big) in enumerate(((2, 4), (8, 16))):
            w = pad2_ref[s, t0 + PAD - 8:t0 + PAD - 8 + pwin, col * 128:(col + 1) * 128]

            rolled = {0: w}

            def shifted(b, w=w, rolled=rolled):
                if b not in rolled:
                    rolled[b] = pltpu.roll(w, pwin - b, axis=0)
                return rolled[b]

            def lo(b):
                return shifted(b)[0:CONV_TILE, :]

            def hi(b):
                return shifted(b)[8:8 + CONV_TILE, :]

            tok = hi(0)
            if col == 0:
                s_small = lo(7) + tok
                s_big = s_small + lo(6) + hi(1)
            else:
                s_small = lo(7) + tok + lo(6) + hi(1) + lo(5) + lo(4) + hi(2) + hi(3)
                s_big = s_small + lo(3) + lo(2) + lo(1) + lo(0) + hi(4) + hi(5) + hi(6) + hi(7)
            ssum = jnp.where(first_half, s_small, s_big)
            if interior:
                mean = ssum * jnp.where(first_half, 1.0 / w_small, 1.0 / w_big)
            else:
                half = jnp.where(first_half, w_small // 2, w_big // 2)
                tpos = t0 + lax.broadcasted_iota(jnp.int32, (CONV_TILE, 128), 0)
                cnt = jnp.minimum(tpos + half, seq_len) - jnp.maximum(tpos - half, 0)
                mean = ssum / cnt.astype(F32)
            outs.append(mean - tok)
        tmp2_ref[i * CONV_TILE:(i + 1) * CONV_TILE, :] = jnp.concatenate(outs, axis=1)

    col_tiles = [COL_C] + list(range(COL_B, COL_C, GROUP_W)) + list(range(COL_D, IN_COLS, GROUP_W))
    per_step = -(-N_CONV_TILES // (len(col_tiles) - 1))
    conv_next = 0
    pool_next = 0
    for step, c0 in enumerate(col_tiles):
        proj_ref[:, c0:c0 + GROUP_W] = jnp.dot(
            hbf_ref[...], w_in_ref[:, c0:c0 + GROUP_W], preferred_element_type=F32)
        if step == 1:
            for s in range(n_seq):
                pad2_ref[s, PAD:PAD + seq_len, :] = proj_ref[s * seq_len:(s + 1) * seq_len, COL_C:COL_D]
        for _ in range(per_step):
            if conv_next < N_CONV_TILES:
                conv_tile(conv_next)
                conv_next += 1
            if step >= 1 and pool_next < N_CONV_TILES:
                pool_tile(pool_next)
                pool_next += 1
    assert conv_next == N_CONV_TILES and pool_next == N_CONV_TILES

    def ln_pw_body(i, _):
        r0 = pl.multiple_of(i * ROW_TILE, ROW_TILE)
        c = tmp_ref[pl.ds(r0, ROW_TILE), :]
        mu = jnp.mean(c, axis=-1, keepdims=True)
        cen = c - mu
        var = jnp.mean(cen * cen, axis=-1, keepdims=True)
        hn = cen * lax.rsqrt(var + EPS) * lng_ref[...] + lnb_ref[...]
        ya = jnp.dot(_silu(hn).astype(BF16), pwb_ref[...], preferred_element_type=F32)
        cat_ref[pl.ds(r0, ROW_TILE), 0:GROUP_W] = ya.astype(BF16)
        return 0
    lax.fori_loop(0, TB // ROW_TILE, ln_pw_body, 0, unroll=True)

    def gmlp_body(i, _):
        r0 = pl.multiple_of(i * CHUNK, CHUNK)
        u = proj_ref[pl.ds(r0, CHUNK), COL_B:COL_B + GROUP_W]
        v = proj_ref[pl.ds(r0, CHUNK), COL_B + GROUP_W:COL_C]
        vstack = _head_stack(v, lane_head).astype(BF16)
        sg = jnp.dot(wcat_ref[...], vstack, preferred_element_type=F32) + gbias_ref[...]
        cat_ref[pl.ds(r0, CHUNK), GROUP_W:2 * GROUP_W] = (u * sg).astype(BF16)
        return 0
    lax.fori_loop(0, N_CHUNKS, gmlp_body, 0, unroll=True)

    def pool_mix_body(i, _):
        r0 = pl.multiple_of(i * ROW_TILE, ROW_TILE)
        yc = jnp.dot(tmp2_ref[pl.ds(r0, ROW_TILE), :].astype(BF16), pbd_ref[...],
                     preferred_element_type=F32) * pscale_ref[...]
        cat_ref[pl.ds(r0, ROW_TILE), 2 * GROUP_W:3 * GROUP_W] = yc.astype(BF16)
        return 0
    lax.fori_loop(0, TB // ROW_TILE, pool_mix_body, 0, unroll=True)

    ri = lax.broadcasted_iota(jnp.int32, (CHUNK, HEADS * CHUNK), 0)
    ci = lax.broadcasted_iota(jnp.int32, (CHUNK, HEADS * CHUNK), 1)
    cj = ci % CHUNK
    chead = ci // CHUNK
    rq = lax.broadcasted_iota(jnp.int32, (CHUNK, GROUP_W), 0).astype(F32)
    for d in range(2):
        lgs = [lg_ref[(2 * layer + d) * HEADS + h] for h in range(HEADS)]
        lg_wide = jnp.where(chead == 0, lgs[0], jnp.where(chead == 1, lgs[1],
                                                          jnp.where(chead == 2, lgs[2], lgs[3])))
        lg_lane = jnp.where(lane_head == 0, lgs[0], jnp.where(lane_head == 1, lgs[1],
                                                              jnp.where(lane_head == 2, lgs[2], lgs[3])))
        dist = (ri - cj) if d == 0 else (cj - ri)
        keep = dist >= 0
        dcat_ref[d] = jnp.where(keep, jnp.exp(jnp.where(keep, dist, 0).astype(F32) * lg_wide), 0.0)
        if d == 0:
            qdec_ref[d] = jnp.exp((rq + 1.0) * lg_lane)
            kdec_ref[d] = jnp.exp((CHUNK - 1.0 - rq) * lg_lane)
        else:
            qdec_ref[d] = jnp.exp((CHUNK - rq) * lg_lane)
            kdec_ref[d] = jnp.exp(rq * lg_lane)
        sdec_ref[d] = jnp.exp(float(CHUNK) * lg_lane)

    rr = lax.broadcasted_iota(jnp.int32, (GROUP_W, GROUP_W), 0) // HEAD_D
    cc = lax.broadcasted_iota(jnp.int32, (GROUP_W, GROUP_W), 1) // HEAD_D
    gmat_ref[...] = jnp.where(rr == cc, 1.0 / HEAD_D, 0.0).astype(BF16)

    lane_bit = (lane & 16) == 0
    k_scale = HEAD_D ** -0.5

    def rope(z, r0):
        if not rotate:
            return z
        cos = cos_ref[pl.ds(r0, CHUNK), :]
        sin = sin_ref[pl.ds(r0, CHUNK), :]
        halves = []
        for c0 in (0, 128):
            zz = z[:, c0:c0 + 128]
            partner = jnp.where(lane_bit[:, c0:c0 + 128],
                                pltpu.roll(zz, 128 - 16, axis=1), pltpu.roll(zz, 16, axis=1))
            halves.append(partner)
        return z * cos + jnp.concatenate(halves, axis=1) * sin

    def pair_stack(zb, pair):
        zero = jnp.zeros_like(zb)
        return jnp.concatenate([jnp.where(lane_head == h, zb, zero) for h in (2 * pair, 2 * pair + 1)], axis=0)

    def intra_body(c, _):
        r0 = pl.multiple_of(c * CHUNK, CHUNK)
        v = proj_ref[pl.ds(r0, CHUNK), COL_V:COL_V + GROUP_W]
        vb = v.astype(BF16)
        vstacks = [pair_stack(vb, pair) for pair in range(2)]
        o = None
        for d in range(2):
            qc0 = COL_D + 2 * d * GROUP_W
            q = rope(proj_ref[pl.ds(r0, CHUNK), qc0:qc0 + GROUP_W], r0)
            k = rope(proj_ref[pl.ds(r0, CHUNK), qc0 + GROUP_W:qc0 + 2 * GROUP_W], r0) * k_scale
            qb = q.astype(BF16)
            kb = k.astype(BF16)
            qb_ref[d, pl.ds(r0, CHUNK), :] = qb
            for pair in range(2):
                att = lax.dot_general(qb, pair_stack(kb, pair), (((1,), (1,)), ((), ())),
                                      preferred_element_type=F32)
                att = (att * dcat_ref[d, :, pair * 2 * CHUNK:(pair + 1) * 2 * CHUNK]).astype(BF16)
                od = jnp.dot(att, vstacks[pair], preferred_element_type=F32)
                o = od if o is None else o + od
            kd = (k * kdec_ref[d]).astype(BF16)
            upd = lax.dot_general(kd, vb, (((0,), (0,)), ((), ())), preferred_element_type=F32)
            compact = None
            for h in range(HEADS):
                part = jnp.where(lane_head == h, upd[h * HEAD_D:(h + 1) * HEAD_D, :], 0.0)
                compact = part if compact is None else compact + part
            upd_ref[d, c] = compact
        of_ref[pl.ds(r0, CHUNK), :] = o
        return 0
    lax.fori_loop(0, N_CHUNKS, intra_body, 0, unroll=True)

    for s in range(n_seq):
        for d in range(2):
            st = s0c_ref[d] if rotate else jnp.zeros((HEAD_D, GROUP_W), F32)
            order = range(n_chunk) if d == 0 else range(n_chunk - 1, -1, -1)
            for c in order:
                cg = s * n_chunk + c
                for h in range(HEADS):
                    sall_ref[d, cg, h * HEAD_D:(h + 1) * HEAD_D, :] = (
                        jnp.where(lane_head == h, st, 0.0).astype(BF16))
                st = st * sdec_ref[d] + upd_ref[d, cg]
            if not rotate:
                for h in range(HEADS):
                    if layer == 0:
                        st_ref[s, 0, d, h] = st[:, h * HEAD_D:(h + 1) * HEAD_D]
                    else:
                        st_ref[s, d, h] = st[:, h * HEAD_D:(h + 1) * HEAD_D]
    if not rotate and layer == 0:
        for s in range(n_seq):
            for later in range(1, DEPTH):
                for d in range(2):
                    for h in range(HEADS):
                        st_ref[s, later, d, h] = jnp.zeros((HEAD_D, HEAD_D), F32)

    def cross_body(c, _):
        r0 = pl.multiple_of(c * CHUNK, CHUNK)
        o = of_ref[pl.ds(r0, CHUNK), :]
        for d in range(2):
            o = o + jnp.dot(qb_ref[d, pl.ds(r0, CHUNK), :], sall_ref[d, c],
                            preferred_element_type=F32) * qdec_ref[d]
        of_ref[pl.ds(r0, CHUNK), :] = o
        return 0
    lax.fori_loop(0, N_CHUNKS, cross_body, 0, unroll=True)

    def center_body(i, _):
        r0 = pl.multiple_of(i * ROW_TILE, ROW_TILE)
        o = of_ref[pl.ds(r0, ROW_TILE), :]
        gmat = gmat_ref[...]
        o_hi = o.astype(BF16)
        o_lo = (o - o_hi.astype(F32)).astype(BF16)
        mu = (jnp.dot(o_hi, gmat, preferred_element_type=F32)
              + jnp.dot(o_lo, gmat, preferred_element_type=F32))
        of_ref[pl.ds(r0, ROW_TILE), :] = o - mu
        return 0
    lax.fori_loop(0, TB // ROW_TILE, center_body, 0, unroll=True)

    def gate_body(i, _):
        r0 = pl.multiple_of(i * ROW_TILE, ROW_TILE)
        cen = of_ref[pl.ds(r0, ROW_TILE), :]
        var = jnp.dot((cen * cen).astype(BF16), gmat_ref[...], preferred_element_type=F32)
        on = cen * lax.rsqrt(var + EPS)
        g = proj_ref[pl.ds(r0, ROW_TILE), COL_G:COL_G + GROUP_W]
        cat_ref[pl.ds(r0, ROW_TILE), 3 * GROUP_W:4 * GROUP_W] = (_silu(g) * on).astype(BF16)
        return 0
    lax.fori_loop(0, TB // ROW_TILE, gate_body, 0, unroll=True)

    def out_body(i, _):
        r0 = pl.multiple_of(i * ROW_TILE, ROW_TILE)
        y = jnp.dot(cat_ref[pl.ds(r0, ROW_TILE), :], w_out_ref[...], preferred_element_type=F32)
        y_ref[pl.ds(r0, ROW_TILE), :] = x_ref[pl.ds(r0, ROW_TILE), :] + ga1 * y
        return 0
    lax.fori_loop(0, TB // ROW_TILE, out_body, 0, unroll=True)


def _layer_spec(shape, layer):
    zeros = (0,) * len(shape)
    return pl.BlockSpec((None,) + tuple(shape), lambda i: (layer,) + zeros, pipeline_mode=pl.Buffered(1))


def _mod_spec(layer, mod_base, mod_stride):
    return pl.BlockSpec((None, 1, 6 * D_MODEL),
                        lambda i: (layer * MOD_ROWS + jnp.maximum(mod_base + mod_stride * i, 0), 0, 0))


def _mixer(x, mods, pp, *, layer, seq_len, rotate, mod_base, mod_stride, rope_tabs=None, s0=None, states=None):
    nb = x.shape[0]
    n_seq = TB // seq_len
    in_specs = [
        pl.BlockSpec((None, TB, D_MODEL), lambda i: (i, 0, 0)),
        _mod_spec(layer, mod_base, mod_stride),
        _layer_spec((1, D_MODEL), layer),
        _layer_spec((D_MODEL, IN_COLS), layer),
        _layer_spec((D_MODEL, D_MODEL), layer),
        _layer_spec((CONV_W, GROUP_W), layer),
        _layer_spec((1, GROUP_W), layer), _layer_spec((1, GROUP_W), layer), _layer_spec((1, GROUP_W), layer),
        _layer_spec((GROUP_W, GROUP_W), layer),
        _layer_spec((HEADS, CHUNK, CHUNK), layer),
        _layer_spec((CHUNK, GROUP_W), layer),
        _layer_spec((HEADS, HEAD_D, HEAD_D), layer),
        _layer_spec((1, GROUP_W), layer),
        pl.BlockSpec(memory_space=pltpu.SMEM),
    ]
    args = [x, mods, pp["g1"], pp["w_in"], pp["w_out"], pp["dw"], pp["cb"], pp["lng"], pp["lnb"],
            pp["pw"], pp["ws"], pp["gbias"], pp["pool_w"], pp["pscale"], pp["lg"]]
    out_shape = [jax.ShapeDtypeStruct((nb, TB, D_MODEL), F32)]
    out_specs = [pl.BlockSpec((None, TB, D_MODEL), lambda i: (i, 0, 0))]
    aliases = {}
    if rotate:
        in_specs += [_const_spec((TB, GROUP_W)), _const_spec((TB, GROUP_W)),
                     pl.BlockSpec((None, None, 2, HEADS, HEAD_D, HEAD_D), lambda i: (i, layer, 0, 0, 0, 0))]
        args += [rope_tabs[0], rope_tabs[1], s0]
    else:
        out_shape.append(jax.ShapeDtypeStruct((nb * n_seq, DEPTH, 2, HEADS, HEAD_D, HEAD_D), F32))
        if layer == 0:
            out_specs.append(pl.BlockSpec((n_seq, DEPTH, 2, HEADS, HEAD_D, HEAD_D),
                                          lambda i: (i, 0, 0, 0, 0, 0)))
        else:
            in_specs.append(pl.BlockSpec(memory_space=pl.ANY))
            args.append(states)
            aliases = {len(args) - 1: 1}
            out_specs.append(pl.BlockSpec((n_seq, None, 2, HEADS, HEAD_D, HEAD_D),
                                          lambda i: (i, layer, 0, 0, 0, 0)))
    scratch = [
        pltpu.VMEM((TB, D_MODEL), BF16),
        pltpu.VMEM((TB, IN_COLS), F32),
        pltpu.VMEM((n_seq, seq_len + 2 * PAD, GROUP_W), F32),
        pltpu.VMEM((n_seq, seq_len + 2 * PAD, GROUP_W), F32),
        pltpu.VMEM((TB, GROUP_W), F32),
        pltpu.VMEM((TB, GROUP_W), F32),
        pltpu.VMEM((2, CHUNK, HEADS * CHUNK), F32),
        pltpu.VMEM((2, CHUNK, GROUP_W), F32),
        pltpu.VMEM((2, CHUNK, GROUP_W), F32),
        pltpu.VMEM((2, 1, GROUP_W), F32),
        pltpu.VMEM((GROUP_W, GROUP_W), BF16),
        pltpu.VMEM((2, TB, GROUP_W), BF16),
        pltpu.VMEM((2, N_CHUNKS, HEAD_D, GROUP_W), F32),
        pltpu.VMEM((2, N_CHUNKS, GROUP_W, GROUP_W), BF16),
        pltpu.VMEM((GROUP_W, GROUP_W), BF16),
        pltpu.VMEM((CHUNK, HEADS * CHUNK), BF16),
        pltpu.VMEM((GROUP_W, GROUP_W), BF16),
        pltpu.VMEM((2, HEAD_D, GROUP_W), F32),
    ]
    outs = pl.pallas_call(
        functools.partial(_mixer_kernel, layer=layer, seq_len=seq_len, rotate=rotate),
        grid=(nb,),
        in_specs=in_specs,
        out_specs=out_specs,
        out_shape=out_shape,
        scratch_shapes=scratch,
        input_output_aliases=aliases,
        compiler_params=pltpu.CompilerParams(
            dimension_semantics=("arbitrary",), vmem_limit_bytes=VMEM_LIMIT_BYTES),
        name="mixer_lat" if rotate else "mixer_ctx",
    )(*args)
    return outs


def _ffn_kernel(x_ref, mod_ref, g2_ref, w_in_hbm, w_out_hbm, gf_ref, y_ref,
                hbf_ref, act_ref, wgu_ref, wo_ref, stg_g_ref, stg_u_ref, stg_o_ref, sem,
                *, layer, final_norm):
    pid = pl.program_id(0)
    sh2 = mod_ref[:, 3 * D_MODEL:4 * D_MODEL]
    sc2 = mod_ref[:, 4 * D_MODEL:5 * D_MODEL]
    ga2 = mod_ref[:, 5 * D_MODEL:6 * D_MODEL]

    def tile_copies(j, slot):
        return (
            pltpu.make_async_copy(w_in_hbm.at[layer, :, pl.ds(j * FF_TILE, FF_TILE)],
                                  stg_g_ref.at[slot], sem.at[0, slot]),
            pltpu.make_async_copy(w_in_hbm.at[layer, :, pl.ds(D_FF + j * FF_TILE, FF_TILE)],
                                  stg_u_ref.at[slot], sem.at[1, slot]),
            pltpu.make_async_copy(w_out_hbm.at[layer, pl.ds(j * FF_TILE, FF_TILE), :],
                                  stg_o_ref.at[slot], sem.at[2, slot]),
        )

    def start_tile(j):
        for cp in tile_copies(j, j % FF_RING):
            cp.start()

    @pl.when(pid == 0)
    def _():
        for j in range(FF_LOOKAHEAD):
            start_tile(j)

    def norm_body(i, _):
        r0 = pl.multiple_of(i * ROW_TILE, ROW_TILE)
        h = _norm_mod(x_ref[pl.ds(r0, ROW_TILE), :], g2_ref[...], sc2, sh2)
        hbf_ref[pl.ds(r0, ROW_TILE), :] = h.astype(BF16)
        return 0
    lax.fori_loop(0, TB // ROW_TILE, norm_body, 0)

    def hidden_tiles(fetch):
        for j in range(N_FF_TILES):
            gcols = slice(j * FF_TILE, (j + 1) * FF_TILE)
            ucols = slice(D_FF + j * FF_TILE, D_FF + (j + 1) * FF_TILE)
            if fetch:
                if j + FF_LOOKAHEAD < N_FF_TILES:
                    start_tile(j + FF_LOOKAHEAD)
                slot = j % FF_RING
                for cp in tile_copies(j, slot):
                    cp.wait()
                wgu_ref[:, gcols] = stg_g_ref[slot].astype(BF16)
                wgu_ref[:, ucols] = stg_u_ref[slot].astype(BF16)
                wo_ref[gcols, :] = stg_o_ref[slot].astype(BF16)
            gate = jnp.dot(hbf_ref[...], wgu_ref[:, gcols], preferred_element_type=F32)
            up = jnp.dot(hbf_ref[...], wgu_ref[:, ucols], preferred_element_type=F32)
            act_ref[:, gcols] = (_silu(gate) * up).astype(BF16)

    @pl.when(pid == 0)
    def _():
        hidden_tiles(True)

    @pl.when(pid != 0)
    def _():
        hidden_tiles(False)

    for r0 in range(0, TB, FF_OUT_ROWS):
        rows = slice(r0, r0 + FF_OUT_ROWS)
        y = x_ref[rows, :] + ga2 * jnp.dot(act_ref[rows, :], wo_ref[...], preferred_element_type=F32)
        if final_norm:
            ms = jnp.mean(y * y, axis=-1, keepdims=True)
            y = y * lax.rsqrt(ms + EPS) * gf_ref[...]
        y_ref[rows, :] = y


def _ffn(x, mods, g2, w_ffn_in, w_ffn_out, g_final, *, layer, mod_base, mod_stride, final_norm):
    nb = x.shape[0]
    return pl.pallas_call(
        functools.partial(_ffn_kernel, layer=layer, final_norm=final_norm),
        grid=(nb,),
        in_specs=[
            pl.BlockSpec((None, TB, D_MODEL), lambda i: (i, 0, 0)),
            _mod_spec(layer, mod_base, mod_stride),
            _layer_spec((1, D_MODEL), layer),
            pl.BlockSpec(memory_space=pl.ANY),
            pl.BlockSpec(memory_space=pl.ANY),
            _const_spec((1, D_MODEL)),
        ],
        out_specs=pl.BlockSpec((None, TB, D_MODEL), lambda i: (i, 0, 0)),
        out_shape=jax.ShapeDtypeStruct((nb, TB, D_MODEL), F32),
        scratch_shapes=[
            pltpu.VMEM((TB, D_MODEL), BF16),
            pltpu.VMEM((TB, D_FF), BF16),
            pltpu.VMEM((D_MODEL, 2 * D_FF), BF16),
            pltpu.VMEM((D_FF, D_MODEL), BF16),
            pltpu.VMEM((FF_RING, D_MODEL, FF_TILE), F32),
            pltpu.VMEM((FF_RING, D_MODEL, FF_TILE), F32),
            pltpu.VMEM((FF_RING, FF_TILE, D_MODEL), F32),
            pltpu.SemaphoreType.DMA((3, FF_RING)),
        ],
        compiler_params=pltpu.CompilerParams(
            dimension_semantics=("arbitrary",), vmem_limit_bytes=VMEM_LIMIT_BYTES),
        name="ffn",
    )(x, mods, g2, w_ffn_in, w_ffn_out, g_final)


def _rope_tables(seq_len):
    t = np.arange(seq_len)
    r = (t // GRID_W).astype(np.float32)
    c = (t % GRID_W).astype(np.float32)
    nf = HEAD_D // 4
    inv = np.float32(ROPE_BASE) ** (-np.arange(nf, dtype=np.float32) / np.float32(nf))
    ang_r = r[:, None] * inv
    ang_c = c[:, None] * inv
    cos = np.concatenate([np.cos(ang_r), np.cos(ang_r), np.cos(ang_c), np.cos(ang_c)], axis=-1)
    sin = np.concatenate([-np.sin(ang_r), np.sin(ang_r), -np.sin(ang_c), np.sin(ang_c)], axis=-1)
    return (jnp.asarray(np.tile(cos, (1, HEADS)), dtype=F32), jnp.asarray(np.tile(sin, (1, HEADS)), dtype=F32))


def kernel(x_prompt, x_sample, state_ret, c, c_ctx, w_ada, b_ada, g_norm1, g_norm2, w_in, w_out, conv_dw,
           conv_b, conv_ln_g, conv_ln_b, conv_pw, gmlp_ws, gmlp_b, pool_w, pool_scale, ret_decay, w_ffn_in,
           w_ffn_out, g_final):
    batch, seq, _ = x_prompt.shape
    dec_batch, dec_seq, _ = x_sample.shape
    assert dec_seq == TB and TB % seq == 0 and (batch * seq) % TB == 0
    assert 1 + dec_batch <= MOD_ROWS

    cs = jnp.concatenate([c_ctx[None, :], c, jnp.zeros((MOD_ROWS - 1 - dec_batch, D_MODEL), F32)], axis=0)
    mods = _ada_rows(cs, w_ada, b_ada).reshape(DEPTH * MOD_ROWS, 1, 6 * D_MODEL)
    rope_tabs = _rope_tables(dec_seq)
    g_final2 = g_final.reshape(1, D_MODEL)
    pp = {
        "g1": g_norm1.reshape(DEPTH, 1, D_MODEL),
        "g2": g_norm2.reshape(DEPTH, 1, D_MODEL),
        "w_in": w_in.astype(BF16),
        "w_out": w_out.astype(BF16),
        "dw": conv_dw,
        "cb": conv_b.reshape(DEPTH, 1, GROUP_W),
        "lng": conv_ln_g.reshape(DEPTH, 1, GROUP_W),
        "lnb": conv_ln_b.reshape(DEPTH, 1, GROUP_W),
        "pw": conv_pw,
        "ws": gmlp_ws,
        "gbias": jnp.repeat(jnp.swapaxes(gmlp_b, 1, 2), HEAD_D, axis=2),
        "pool_w": pool_w,
        "pscale": pool_scale.reshape(DEPTH, 1, GROUP_W),
        "lg": jax.nn.log_sigmoid(ret_decay.astype(F32)).reshape(DEPTH * 2 * HEADS),
    }

    xc = x_prompt.reshape(batch * seq // TB, TB, D_MODEL)
    xl = x_sample
    states = None
    for l in range(DEPTH):
        last = l == DEPTH - 1
        xc, states = _mixer(xc, mods, pp, layer=l, seq_len=seq, rotate=False, mod_base=0, mod_stride=0,
                            states=states)
        xc = _ffn(xc, mods, pp["g2"], w_ffn_in, w_ffn_out, g_final2, layer=l, mod_base=0, mod_stride=0,
                  final_norm=last)
        (xl,) = _mixer(xl, mods, pp, layer=l, seq_len=dec_seq, rotate=True, mod_base=1, mod_stride=1,
                       rope_tabs=rope_tabs, s0=state_ret.astype(F32))
        xl = _ffn(xl, mods, pp["g2"], w_ffn_in, w_ffn_out, g_final2, layer=l, mod_base=1, mod_stride=1,
                  final_norm=last)

    y_prompt = xc.reshape(batch, seq, D_MODEL)
    return (y_prompt, xl, states.astype(x_prompt.dtype))
```

```python
import functools

import jax
import jax.numpy as jnp
import numpy as np
from jax import lax
from jax.experimental import pallas as pl
from jax.experimental.pallas import tpu as pltpu

F32 = jnp.float32
BF16 = jnp.bfloat16

D_MODEL = 1024
DEPTH = 2
GRID_W = 64
GROUP_W = D_MODEL // 4
CONV_W = 31
CHUNK = 128
HEADS = 4
HEAD_D = GROUP_W // HEADS
POOL_WINDOWS = (2, 4, 8, 16)
ROPE_BASE = 10000.0
D_FF = 2816
IN_COLS = 11 * GROUP_W
EPS = 1e-6

TB = 1024
N_CHUNKS = TB // CHUNK
ROW_TILE = 256
CONV_TILE = 64
N_CONV_TILES = TB // CONV_TILE
PAD = 16
FF_TILE = 256
N_FF_TILES = D_FF // FF_TILE
FF_OUT_ROWS = 256
FF_LOOKAHEAD = 2
FF_RING = FF_LOOKAHEAD + 1
ADA_TILE = 1536
MOD_ROWS = 8
VMEM_LIMIT_BYTES = 60 * 1024 * 1024

COL_A, COL_B, COL_C, COL_D = 0, 2 * GROUP_W, 4 * GROUP_W, 5 * GROUP_W
COL_V, COL_G = COL_D + 4 * GROUP_W, COL_D + 5 * GROUP_W


def _sigmoid(x):
    return 1.0 / (1.0 + jnp.exp(-x))


def _silu(x):
    return x * _sigmoid(x)


def _norm_mod(x, g, scale, shift):
    ms = jnp.mean(x * x, axis=-1, keepdims=True)
    return (x * lax.rsqrt(ms + EPS) * g) * (1.0 + scale) + shift


def _head_stack(x, lane_head):
    return jnp.concatenate([jnp.where(lane_head == h, x, 0.0) for h in range(HEADS)], axis=0)


def _const_spec(shape):
    zeros = (0,) * len(shape)
    return pl.BlockSpec(shape, lambda i: zeros, pipeline_mode=pl.Buffered(1))


def _ada_kernel(c_ref, w_ref, b_ref, o_ref):
    a = _silu(c_ref[...]).astype(BF16)
    o_ref[...] = jnp.dot(a, w_ref[...].astype(BF16), preferred_element_type=F32) + b_ref[...]


def _ada_rows(cs, w_ada, b_ada):
    n_tiles = 6 * D_MODEL // ADA_TILE
    return pl.pallas_call(
        _ada_kernel,
        grid=(DEPTH, n_tiles),
        in_specs=[
            pl.BlockSpec((MOD_ROWS, D_MODEL), lambda l, j: (0, 0)),
            pl.BlockSpec((None, D_MODEL, ADA_TILE), lambda l, j: (l, 0, j)),
            pl.BlockSpec((None, 1, ADA_TILE), lambda l, j: (l, 0, j)),
        ],
        out_specs=pl.BlockSpec((None, MOD_ROWS, ADA_TILE), lambda l, j: (l, 0, j)),
        out_shape=jax.ShapeDtypeStruct((DEPTH, MOD_ROWS, 6 * D_MODEL), F32),
        compiler_params=pltpu.CompilerParams(
            dimension_semantics=("arbitrary", "arbitrary"), vmem_limit_bytes=VMEM_LIMIT_BYTES),
        name="ada_rows",
    )(cs, w_ada, b_ada.reshape(DEPTH, 1, 6 * D_MODEL))


def _mixer_kernel(*refs, layer, seq_len, rotate):
    n_seq = TB // seq_len
    n_chunk = seq_len // CHUNK
    it = iter(refs)
    x_ref, mod_ref, g1_ref, w_in_ref, w_out_ref = (next(it) for _ in range(5))
    dw_ref, cb_ref, lng_ref, lnb_ref, pw_ref = (next(it) for _ in range(5))
    ws_ref, gbias_ref, poolw_ref, pscale_ref, lg_ref = (next(it) for _ in range(5))
    if rotate:
        cos_ref, sin_ref, s0_ref = (next(it) for _ in range(3))
    elif layer > 0:
        next(it)
    y_ref = next(it)
    if not rotate:
        st_ref = next(it)
    (hbf_ref, proj_ref, pad_ref, pad2_ref, tmp_ref, tmp2_ref,
     dcat_ref, qdec_ref, kdec_ref, sdec_ref, gmat_ref, qb_ref, upd_ref, sall_ref,
     pwb_ref, wcat_ref, pbd_ref, s0c_ref) = it
    cat_ref = hbf_ref
    of_ref = tmp_ref

    pwb_ref[...] = pw_ref[...].astype(BF16)
    pbd_ref[...] = jnp.zeros((GROUP_W, GROUP_W), BF16)
    for h in range(HEADS):
        wcat_ref[:, h * CHUNK:(h + 1) * CHUNK] = ws_ref[h].astype(BF16)
        pbd_ref[h * HEAD_D:(h + 1) * HEAD_D, h * HEAD_D:(h + 1) * HEAD_D] = poolw_ref[h].astype(BF16)
        if rotate:
            for d in range(2):
                s0c_ref[d, :, h * HEAD_D:(h + 1) * HEAD_D] = s0_ref[d, h]

    sh1 = mod_ref[:, 0:D_MODEL]
    sc1 = mod_ref[:, D_MODEL:2 * D_MODEL]
    ga1 = mod_ref[:, 2 * D_MODEL:3 * D_MODEL]

    lane = lax.broadcasted_iota(jnp.int32, (1, GROUP_W), 1)
    lane_head = lane // HEAD_D
    tiles_per_seq = seq_len // CONV_TILE
    chunks_per_seq = seq_len // CHUNK

    def seq_and_offset(i, per_seq, size):
        if n_seq == 1:
            return 0, pl.multiple_of(i * size, size)
        return i // per_seq, pl.multiple_of((i % per_seq) * size, size)

    def norm_body(i, _):
        r0 = pl.multiple_of(i * ROW_TILE, ROW_TILE)
        h = _norm_mod(x_ref[pl.ds(r0, ROW_TILE), :], g1_ref[...], sc1, sh1).astype(BF16)
        hbf_ref[pl.ds(r0, ROW_TILE), :] = h
        proj_ref[pl.ds(r0, ROW_TILE), COL_A:COL_B] = jnp.dot(
            h, w_in_ref[:, COL_A:COL_B], preferred_element_type=F32)
        return 0
    lax.fori_loop(0, TB // ROW_TILE, norm_body, 0, unroll=True)

    for s in range(n_seq):
        for ref in (pad_ref, pad2_ref):
            ref[s, 0:PAD, :] = jnp.zeros((PAD, GROUP_W), F32)
            ref[s, PAD + seq_len:PAD + seq_len + PAD, :] = jnp.zeros((PAD, GROUP_W), F32)

    def glu_body(i, _):
        s, t0 = seq_and_offset(i, chunks_per_seq, CHUNK)
        r0 = pl.multiple_of(i * CHUNK, CHUNK)
        a1 = proj_ref[pl.ds(r0, CHUNK), COL_A:COL_A + GROUP_W]
        a2 = proj_ref[pl.ds(r0, CHUNK), COL_A + GROUP_W:COL_B]
        pad_ref[s, pl.ds(pl.multiple_of(t0 + PAD, 8), CHUNK), :] = a1 * _sigmoid(a2)
        return 0
    lax.fori_loop(0, N_CHUNKS, glu_body, 0, unroll=True)

    win = CONV_TILE + 2 * PAD

    def conv_tile(i):
        s, t0 = divmod(i, tiles_per_seq)
        t0 *= CONV_TILE
        for c0 in range(0, GROUP_W, 128):
            w = pad_ref[s, t0:t0 + win, c0:c0 + 128]
            acc = jnp.zeros((CONV_TILE, 128), F32) + cb_ref[:, c0:c0 + 128]
            for b in range(8):
                wb = w if b == 0 else pltpu.roll(w, win - b, axis=0)
                for a in range(4):
                    k = 8 * a + b - 1
                    if 0 <= k < CONV_W:
                        acc = acc + dw_ref[k:k + 1, c0:c0 + 128] * wb[8 * a:8 * a + CONV_TILE, :]
            tmp_ref[i * CONV_TILE:(i + 1) * CONV_TILE, c0:c0 + 128] = acc

    pwin = CONV_TILE + 16
    lane128 = lax.broadcasted_iota(jnp.int32, (1, 128), 1)
    first_half = lane128 < HEAD_D

    def pool_tile(i):
        s, t0 = divmod(i, tiles_per_seq)
        t0 *= CONV_TILE
        interior = t0 >= 8 and t0 + CONV_TILE + 8 <= seq_len
        outs = []
        for col, (w_small, w_big) in enumerate(((2, 4), (8, 16))):
            w = pad2_ref[s, t0 + PAD - 8:t0 + PAD - 8 + pwin, col * 128:(col + 1) * 128]

            rolled = {0: w}

            def shifted(b, w=w, rolled=rolled):
                if b not in rolled:
                    rolled[b] = pltpu.roll(w, pwin - b, axis=0)
                return rolled[b]

            def lo(b):
                return shifted(b)[0:CONV_TILE, :]

            def hi(b):
                return shifted(b)[8:8 + CONV_TILE, :]

            tok = hi(0)
            if col == 0:
                s_small = lo(7) + tok
                s_big = s_small + lo(6) + hi(1)
            else:
                s_small = lo(7) + tok + lo(6) + hi(1) + lo(5) + lo(4) + hi(2) + hi(3)
                s_big = s_small + lo(3) + lo(2) + lo(1) + lo(0) + hi(4) + hi(5) + hi(6) + hi(7)
            ssum = jnp.where(first_half, s_small, s_big)
            if interior:
                mean = ssum * jnp.where(first_half, 1.0 / w_small, 1.0 / w_big)
            else:
                half = jnp.where(first_half, w_small // 2, w_big // 2)
                tpos = t0 + lax.broadcasted_iota(jnp.int32, (CONV_TILE, 128), 0)
                cnt = jnp.minimum(tpos + half, seq_len) - jnp.maximum(tpos - half, 0)
                mean = ssum / cnt.astype(F32)
            outs.append(mean - tok)
        tmp2_ref[i * CONV_TILE:(i + 1) * CONV_TILE, :] = jnp.concatenate(outs, axis=1)

    col_tiles = [COL_C] + list(range(COL_B, COL_C, GROUP_W)) + list(range(COL_D, IN_COLS, GROUP_W))
    per_step = -(-N_CONV_TILES // (len(col_tiles) - 1))
    conv_next = 0
    pool_next = 0
    for step, c0 in enumerate(col_tiles):
        proj_ref[:, c0:c0 + GROUP_W] = jnp.dot(
            hbf_ref[...], w_in_ref[:, c0:c0 + GROUP_W], preferred_element_type=F32)
        if step == 1:
            for s in range(n_seq):
                pad2_ref[s, PAD:PAD + seq_len, :] = proj_ref[s * seq_len:(s + 1) * seq_len, COL_C:COL_D]
        for _ in range(per_step):
            if conv_next < N_CONV_TILES:
                conv_tile(conv_next)
                conv_next += 1
            if step >= 1 and pool_next < N_CONV_TILES:
                pool_tile(pool_next)
                pool_next += 1
    assert conv_next == N_CONV_TILES and pool_next == N_CONV_TILES

    def ln_pw_body(i, _):
        r0 = pl.multiple_of(i * ROW_TILE, ROW_TILE)
        c = tmp_ref[pl.ds(r0, ROW_TILE), :]
        mu = jnp.mean(c, axis=-1, keepdims=True)
        cen = c - mu
        var = jnp.mean(cen * cen, axis=-1, keepdims=True)
        hn = cen * lax.rsqrt(var + EPS) * lng_ref[...] + lnb_ref[...]
        ya = jnp.dot(_silu(hn).astype(BF16), pwb_ref[...], preferred_element_type=F32)
        cat_ref[pl.ds(r0, ROW_TILE), 0:GROUP_W] = ya.astype(BF16)
        return 0
    lax.fori_loop(0, TB // ROW_TILE, ln_pw_body, 0, unroll=True)

    def gmlp_body(i, _):
        r0 = pl.multiple_of(i * CHUNK, CHUNK)
        u = proj_ref[pl.ds(r0, CHUNK), COL_B:COL_B + GROUP_W]
        v = proj_ref[pl.ds(r0, CHUNK), COL_B + GROUP_W:COL_C]
        vstack = _head_stack(v, lane_head).astype(BF16)
        sg = jnp.dot(wcat_ref[...], vstack, preferred_element_type=F32) + gbias_ref[...]
        cat_ref[pl.ds(r0, CHUNK), GROUP_W:2 * GROUP_W] = (u * sg).astype(BF16)
        return 0
    lax.fori_loop(0, N_CHUNKS, gmlp_body, 0, unroll=True)

    def pool_mix_body(i, _):
        r0 = pl.multiple_of(i * ROW_TILE, ROW_TILE)
        yc = jnp.dot(tmp2_ref[pl.ds(r0, ROW_TILE), :].astype(BF16), pbd_ref[...],
                     preferred_element_type=F32) * pscale_ref[...]
        cat_ref[pl.ds(r0, ROW_TILE), 2 * GROUP_W:3 * GROUP_W] = yc.astype(BF16)
        return 0
    lax.fori_loop(0, TB // ROW_TILE, pool_mix_body, 0, unroll=True)

    ri = lax.broadcasted_iota(jnp.int32, (CHUNK, HEADS * CHUNK), 0)
    ci = lax.broadcasted_iota(jnp.int32, (CHUNK, HEADS * CHUNK), 1)
    cj = ci % CHUNK
    chead = ci // CHUNK
    rq = lax.broadcasted_iota(jnp.int32, (CHUNK, GROUP_W), 0).astype(F32)
    for d in range(2):
        lgs = [lg_ref[(2 * layer + d) * HEADS + h] for h in range(HEADS)]
        lg_wide = jnp.where(chead == 0, lgs[0], jnp.where(chead == 1, lgs[1],
                                                          jnp.where(chead == 2, lgs[2], lgs[3])))
        lg_lane = jnp.where(lane_head == 0, lgs[0], jnp.where(lane_head == 1, lgs[1],
                                                              jnp.where(lane_head == 2, lgs[2], lgs[3])))
        dist = (ri - cj) if d == 0 else (cj - ri)
        keep = dist >= 0
        dcat_ref[d] = jnp.where(keep, jnp.exp(jnp.where(keep, dist, 0).astype(F32) * lg_wide), 0.0)
        if d == 0:
            qdec_ref[d] = jnp.exp((rq + 1.0) * lg_lane)
            kdec_ref[d] = jnp.exp((CHUNK - 1.0 - rq) * lg_lane)
        else:
            qdec_ref[d] = jnp.exp((CHUNK - rq) * lg_lane)
            kdec_ref[d] = jnp.exp(rq * lg_lane)
        sdec_ref[d] = jnp.exp(float(CHUNK) * lg_lane)

    rr = lax.broadcasted_iota(jnp.int32, (GROUP_W, GROUP_W), 0) // HEAD_D
    cc = lax.broadcasted_iota(jnp.int32, (GROUP_W, GROUP_W), 1) // HEAD_D
    gmat_ref[...] = jnp.where(rr == cc, 1.0 / HEAD_D, 0.0).astype(BF16)

    lane_bit = (lane & 16) == 0
    k_scale = HEAD_D ** -0.5

    def rope(z, r0):
        if not rotate:
            return z
        cos = cos_ref[pl.ds(r0, CHUNK), :]
        sin = sin_ref[pl.ds(r0, CHUNK), :]
        halves = []
        for c0 in (0, 128):
            zz = z[:, c0:c0 + 128]
            partner = jnp.where(lane_bit[:, c0:c0 + 128],
                                pltpu.roll(zz, 128 - 16, axis=1), pltpu.roll(zz, 16, axis=1))
            halves.append(partner)
        return z * cos + jnp.concatenate(halves, axis=1) * sin

    def pair_stack(zb, pair):
        zero = jnp.zeros_like(zb)
        return jnp.concatenate([jnp.where(lane_head == h, zb, zero) for h in (2 * pair, 2 * pair + 1)], axis=0)

    def intra_body(c, _):
        r0 = pl.multiple_of(c * CHUNK, CHUNK)
        v = proj_ref[pl.ds(r0, CHUNK), COL_V:COL_V + GROUP_W]
        vb = v.astype(BF16)
        vstacks = [pair_stack(vb, pair) for pair in range(2)]
        o = None
        for d in range(2):
            qc0 = COL_D + 2 * d * GROUP_W
            q = rope(proj_ref[pl.ds(r0, CHUNK), qc0:qc0 + GROUP_W], r0)
            k = rope(proj_ref[pl.ds(r0, CHUNK), qc0 + GROUP_W:qc0 + 2 * GROUP_W], r0) * k_scale
            qb = q.astype(BF16)
            kb = k.astype(BF16)
            qb_ref[d, pl.ds(r0, CHUNK), :] = qb
            for pair in range(2):
                att = lax.dot_general(qb, pair_stack(kb, pair), (((1,), (1,)), ((), ())),
                                      preferred_element_type=F32)
                att = (att * dcat_ref[d, :, pair * 2 * CHUNK:(pair + 1) * 2 * CHUNK]).astype(BF16)
                od = jnp.dot(att, vstacks[pair], preferred_element_type=F32)
                o = od if o is None else o + od
            kd = (k * kdec_ref[d]).astype(BF16)
            upd = lax.dot_general(kd, vb, (((0,), (0,)), ((), ())), preferred_element_type=F32)
            compact = None
            for h in range(HEADS):
                part = jnp.where(lane_head == h, upd[h * HEAD_D:(h + 1) * HEAD_D, :], 0.0)
                compact = part if compact is None else compact + part
            upd_ref[d, c] = compact
        of_ref[pl.ds(r0, CHUNK), :] = o
        return 0
    lax.fori_loop(0, N_CHUNKS, intra_body, 0, unroll=True)

    for s in range(n_seq):
        for d in range(2):
            st = s0c_ref[d] if rotate else jnp.zeros((HEAD_D, GROUP_W), F32)
            order = range(n_chunk) if d == 0 else range(n_chunk - 1, -1, -1)
            for c in order:
                cg = s * n_chunk + c
                for h in range(HEADS):
                    sall_ref[d, cg, h * HEAD_D:(h + 1) * HEAD_D, :] = (
                        jnp.where(lane_head == h, st, 0.0).astype(BF16))
                st = st * sdec_ref[d] + upd_ref[d, cg]
            if not rotate:
                for h in range(HEADS):
                    if layer == 0:
                        st_ref[s, 0, d, h] = st[:, h * HEAD_D:(h + 1) * HEAD_D]
                    else:
                        st_ref[s, d, h] = st[:, h * HEAD_D:(h + 1) * HEAD_D]
    if not rotate and layer == 0:
        for s in range(n_seq):
            for later in range(1, DEPTH):
                for d in range(2):
                    for h in range(HEADS):
                        st_ref[s, later, d, h] = jnp.zeros((HEAD_D, HEAD_D), F32)

    def cross_body(c, _):
        r0 = pl.multiple_of(c * CHUNK, CHUNK)
        o = of_ref[pl.ds(r0, CHUNK), :]
        for d in range(2):
            o = o + jnp.dot(qb_ref[d, pl.ds(r0, CHUNK), :], sall_ref[d, c],
                            preferred_element_type=F32) * qdec_ref[d]
        of_ref[pl.ds(r0, CHUNK), :] = o
        return 0
    lax.fori_loop(0, N_CHUNKS, cross_body, 0, unroll=True)

    def center_body(i, _):
        r0 = pl.multiple_of(i * ROW_TILE, ROW_TILE)
        o = of_ref[pl.ds(r0, ROW_TILE), :]
        gmat = gmat_ref[...]
        o_hi = o.astype(BF16)
        o_lo = (o - o_hi.astype(F32)).astype(BF16)
        mu = (jnp.dot(o_hi, gmat, preferred_element_type=F32)
              + jnp.dot(o_lo, gmat, preferred_element_type=F32))
        of_ref[pl.ds(r0, ROW_TILE), :] = o - mu
        return 0
    lax.fori_loop(0, TB // ROW_TILE, center_body, 0, unroll=True)

    def gate_body(i, _):
        r0 = pl.multiple_of(i * ROW_TILE, ROW_TILE)
        cen = of_ref[pl.ds(r0, ROW_TILE), :]
        var = jnp.dot((cen * cen).astype(BF16), gmat_ref[...], preferred_element_type=F32)
        on = cen * lax.rsqrt(var + EPS)
        g = proj_ref[pl.ds(r0, ROW_TILE), COL_G:COL_G + GROUP_W]
        cat_ref[pl.ds(r0, ROW_TILE), 3 * GROUP_W:4 * GROUP_W] = (_silu(g) * on).astype(BF16)
        return 0
    lax.fori_loop(0, TB // ROW_TILE, gate_body, 0, unroll=True)

    def out_body(i, _):
        r0 = pl.multiple_of(i * ROW_TILE, ROW_TILE)
        y = jnp.dot(cat_ref[pl.ds(r0, ROW_TILE), :], w_out_ref[...], preferred_element_type=F32)
        y_ref[pl.ds(r0, ROW_TILE), :] = x_ref[pl.ds(r0, ROW_TILE), :] + ga1 * y
        return 0
    lax.fori_loop(0, TB // ROW_TILE, out_body, 0, unroll=True)


def _layer_spec(shape, layer):
    zeros = (0,) * len(shape)
    return pl.BlockSpec((None,) + tuple(shape), lambda i: (layer,) + zeros, pipeline_mode=pl.Buffered(1))


def _mod_spec(layer, mod_base, mod_stride):
    return pl.BlockSpec((None, 1, 6 * D_MODEL),
                        lambda i: (layer * MOD_ROWS + jnp.maximum(mod_base + mod_stride * i, 0), 0, 0))


def _mixer(x, mods, pp, *, layer, seq_len, rotate, mod_base, mod_stride, rope_tabs=None, s0=None, states=None):
    nb = x.shape[0]
    n_seq = TB // seq_len
    in_specs = [
        pl.BlockSpec((None, TB, D_MODEL), lambda i: (i, 0, 0)),
        _mod_spec(layer, mod_base, mod_stride),
        _layer_spec((1, D_MODEL), layer),
        _layer_spec((D_MODEL, IN_COLS), layer),
        _layer_spec((D_MODEL, D_MODEL), layer),
        _layer_spec((CONV_W, GROUP_W), layer),
        _layer_spec((1, GROUP_W), layer), _layer_spec((1, GROUP_W), layer), _layer_spec((1, GROUP_W), layer),
        _layer_spec((GROUP_W, GROUP_W), layer),
        _layer_spec((HEADS, CHUNK, CHUNK), layer),
        _layer_spec((CHUNK, GROUP_W), layer),
        _layer_spec((HEADS, HEAD_D, HEAD_D), layer),
        _layer_spec((1, GROUP_W), layer),
        pl.BlockSpec(memory_space=pltpu.SMEM),
    ]
    args = [x, mods, pp["g1"], pp["w_in"], pp["w_out"], pp["dw"], pp["cb"], pp["lng"], pp["lnb"],
            pp["pw"], pp["ws"], pp["gbias"], pp["pool_w"], pp["pscale"], pp["lg"]]
    out_shape = [jax.ShapeDtypeStruct((nb, TB, D_MODEL), F32)]
    out_specs = [pl.BlockSpec((None, TB, D_MODEL), lambda i: (i, 0, 0))]
    aliases = {}
    if rotate:
        in_specs += [_const_spec((TB, GROUP_W)), _const_spec((TB, GROUP_W)),
                     pl.BlockSpec((None, None, 2, HEADS, HEAD_D, HEAD_D), lambda i: (i, layer, 0, 0, 0, 0))]
        args += [rope_tabs[0], rope_tabs[1], s0]
    else:
        out_shape.append(jax.ShapeDtypeStruct((nb * n_seq, DEPTH, 2, HEADS, HEAD_D, HEAD_D), F32))
        if layer == 0:
            out_specs.append(pl.BlockSpec((n_seq, DEPTH, 2, HEADS, HEAD_D, HEAD_D),
                                          lambda i: (i, 0, 0, 0, 0, 0)))
        else:
            in_specs.append(pl.BlockSpec(memory_space=pl.ANY))
            args.append(states)
            aliases = {len(args) - 1: 1}
            out_specs.append(pl.BlockSpec((n_seq, None, 2, HEADS, HEAD_D, HEAD_D),
                                          lambda i: (i, layer, 0, 0, 0, 0)))
    scratch = [
        pltpu.VMEM((TB, D_MODEL), BF16),
        pltpu.VMEM((TB, IN_COLS), F32),
        pltpu.VMEM((n_seq, seq_len + 2 * PAD, GROUP_W), F32),
        pltpu.VMEM((n_seq, seq_len + 2 * PAD, GROUP_W), F32),
        pltpu.VMEM((TB, GROUP_W), F32),
        pltpu.VMEM((TB, GROUP_W), F32),
        pltpu.VMEM((2, CHUNK, HEADS * CHUNK), F32),
        pltpu.VMEM((2, CHUNK, GROUP_W), F32),
        pltpu.VMEM((2, CHUNK, GROUP_W), F32),
        pltpu.VMEM((2, 1, GROUP_W), F32),
        pltpu.VMEM((GROUP_W, GROUP_W), BF16),
        pltpu.VMEM((2, TB, GROUP_W), BF16),
        pltpu.VMEM((2, N_CHUNKS, HEAD_D, GROUP_W), F32),
        pltpu.VMEM((2, N_CHUNKS, GROUP_W, GROUP_W), BF16),
        pltpu.VMEM((GROUP_W, GROUP_W), BF16),
        pltpu.VMEM((CHUNK, HEADS * CHUNK), BF16),
        pltpu.VMEM((GROUP_W, GROUP_W), BF16),
        pltpu.VMEM((2, HEAD_D, GROUP_W), F32),
    ]
    outs = pl.pallas_call(
        functools.partial(_mixer_kernel, layer=layer, seq_len=seq_len, rotate=rotate),
        grid=(nb,),
        in_specs=in_specs,
        out_specs=out_specs,
        out_shape=out_shape,
        scratch_shapes=scratch,
        input_output_aliases=aliases,
        compiler_params=pltpu.CompilerParams(
            dimension_semantics=("arbitrary",), vmem_limit_bytes=VMEM_LIMIT_BYTES),
        name="mixer_lat" if rotate else "mixer_ctx",
    )(*args)
    return outs


def _ffn_kernel(x_ref, mod_ref, g2_ref, w_in_hbm, w_out_hbm, gf_ref, y_ref,
                hbf_ref, act_ref, wgu_ref, wo_ref, stg_g_ref, stg_u_ref, stg_o_ref, sem,
                *, layer, final_norm):
    pid = pl.program_id(0)
    sh2 = mod_ref[:, 3 * D_MODEL:4 * D_MODEL]
    sc2 = mod_ref[:, 4 * D_MODEL:5 * D_MODEL]
    ga2 = mod_ref[:, 5 * D_MODEL:6 * D_MODEL]

    def tile_copies(j, slot):
        return (
            pltpu.make_async_copy(w_in_hbm.at[layer, :, pl.ds(j * FF_TILE, FF_TILE)],
                                  stg_g_ref.at[slot], sem.at[0, slot]),
            pltpu.make_async_copy(w_in_hbm.at[layer, :, pl.ds(D_FF + j * FF_TILE, FF_TILE)],
                                  stg_u_ref.at[slot], sem.at[1, slot]),
            pltpu.make_async_copy(w_out_hbm.at[layer, pl.ds(j * FF_TILE, FF_TILE), :],
                                  stg_o_ref.at[slot], sem.at[2, slot]),
        )

    def start_tile(j):
        for cp in tile_copies(j, j % FF_RING):
            cp.start()

    @pl.when(pid == 0)
    def _():
        for j in range(FF_LOOKAHEAD):
            start_tile(j)

    def norm_body(i, _):
        r0 = pl.multiple_of(i * ROW_TILE, ROW_TILE)
        h = _norm_mod(x_ref[pl.ds(r0, ROW_TILE), :], g2_ref[...], sc2, sh2)
        hbf_ref[pl.ds(r0, ROW_TILE), :] = h.astype(BF16)
        return 0
    lax.fori_loop(0, TB // ROW_TILE, norm_body, 0)

    def hidden_tiles(fetch):
        for j in range(N_FF_TILES):
            gcols = slice(j * FF_TILE, (j + 1) * FF_TILE)
            ucols = slice(D_FF + j * FF_TILE, D_FF + (j + 1) * FF_TILE)
            if fetch:
                if j + FF_LOOKAHEAD < N_FF_TILES:
                    start_tile(j + FF_LOOKAHEAD)
                slot = j % FF_RING
                for cp in tile_copies(j, slot):
                    cp.wait()
                wgu_ref[:, gcols] = stg_g_ref[slot].astype(BF16)
                wgu_ref[:, ucols] = stg_u_ref[slot].astype(BF16)
                wo_ref[gcols, :] = stg_o_ref[slot].astype(BF16)
            gate = jnp.dot(hbf_ref[...], wgu_ref[:, gcols], preferred_element_type=F32)
            up = jnp.dot(hbf_ref[...], wgu_ref[:, ucols], preferred_element_type=F32)
            act_ref[:, gcols] = (_silu(gate) * up).astype(BF16)

    @pl.when(pid == 0)
    def _():
        hidden_tiles(True)

    @pl.when(pid != 0)
    def _():
        hidden_tiles(False)

    def down_body(i, _):
        rows = pl.ds(pl.multiple_of(i * FF_OUT_ROWS, FF_OUT_ROWS), FF_OUT_ROWS)
        y = x_ref[rows, :] + ga2 * jnp.dot(act_ref[rows, :], wo_ref[...], preferred_element_type=F32)
        if final_norm:
            ms = jnp.mean(y * y, axis=-1, keepdims=True)
            y = y * lax.rsqrt(ms + EPS) * gf_ref[...]
        y_ref[rows, :] = y
        return 0
    lax.fori_loop(0, TB // FF_OUT_ROWS, down_body, 0)


def _ffn(x, mods, g2, w_ffn_in, w_ffn_out, g_final, *, layer, mod_base, mod_stride, final_norm):
    nb = x.shape[0]
    return pl.pallas_call(
        functools.partial(_ffn_kernel, layer=layer, final_norm=final_norm),
        grid=(nb,),
        in_specs=[
            pl.BlockSpec((None, TB, D_MODEL), lambda i: (i, 0, 0)),
            _mod_spec(layer, mod_base, mod_stride),
            _layer_spec((1, D_MODEL), layer),
            pl.BlockSpec(memory_space=pl.ANY),
            pl.BlockSpec(memory_space=pl.ANY),
            _const_spec((1, D_MODEL)),
        ],
        out_specs=pl.BlockSpec((None, TB, D_MODEL), lambda i: (i, 0, 0)),
        out_shape=jax.ShapeDtypeStruct((nb, TB, D_MODEL), F32),
        scratch_shapes=[
            pltpu.VMEM((TB, D_MODEL), BF16),
            pltpu.VMEM((TB, D_FF), BF16),
            pltpu.VMEM((D_MODEL, 2 * D_FF), BF16),
            pltpu.VMEM((D_FF, D_MODEL), BF16),
            pltpu.VMEM((FF_RING, D_MODEL, FF_TILE), F32),
            pltpu.VMEM((FF_RING, D_MODEL, FF_TILE), F32),
            pltpu.VMEM((FF_RING, FF_TILE, D_MODEL), F32),
            pltpu.SemaphoreType.DMA((3, FF_RING)),
        ],
        compiler_params=pltpu.CompilerParams(
            dimension_semantics=("arbitrary",), vmem_limit_bytes=VMEM_LIMIT_BYTES),
        name="ffn",
    )(x, mods, g2, w_ffn_in, w_ffn_out, g_final)


def _rope_tables(seq_len):
    t = np.arange(seq_len)
    r = (t // GRID_W).astype(np.float32)
    c = (t % GRID_W).astype(np.float32)
    nf = HEAD_D // 4
    inv = np.float32(ROPE_BASE) ** (-np.arange(nf, dtype=np.float32) / np.float32(nf))
    ang_r = r[:, None] * inv
    ang_c = c[:, None] * inv
    cos = np.concatenate([np.cos(ang_r), np.cos(ang_r), np.cos(ang_c), np.cos(ang_c)], axis=-1)
    sin = np.concatenate([-np.sin(ang_r), np.sin(ang_r), -np.sin(ang_c), np.sin(ang_c)], axis=-1)
    return (jnp.asarray(np.tile(cos, (1, HEADS)), dtype=F32), jnp.asarray(np.tile(sin, (1, HEADS)), dtype=F32))


def kernel(x_prompt, x_sample, state_ret, c, c_ctx, w_ada, b_ada, g_norm1, g_norm2, w_in, w_out, conv_dw,
           conv_b, conv_ln_g, conv_ln_b, conv_pw, gmlp_ws, gmlp_b, pool_w, pool_scale, ret_decay, w_ffn_in,
           w_ffn_out, g_final):
    batch, seq, _ = x_prompt.shape
    dec_batch, dec_seq, _ = x_sample.shape
    assert dec_seq == TB and TB % seq == 0 and (batch * seq) % TB == 0
    assert 1 + dec_batch <= MOD_ROWS

    cs = jnp.concatenate([c_ctx[None, :], c, jnp.zeros((MOD_ROWS - 1 - dec_batch, D_MODEL), F32)], axis=0)
    mods = _ada_rows(cs, w_ada, b_ada).reshape(DEPTH * MOD_ROWS, 1, 6 * D_MODEL)
    rope_tabs = _rope_tables(dec_seq)
    g_final2 = g_final.reshape(1, D_MODEL)
    pp = {
        "g1": g_norm1.reshape(DEPTH, 1, D_MODEL),
        "g2": g_norm2.reshape(DEPTH, 1, D_MODEL),
        "w_in": w_in.astype(BF16),
        "w_out": w_out.astype(BF16),
        "dw": conv_dw,
        "cb": conv_b.reshape(DEPTH, 1, GROUP_W),
        "lng": conv_ln_g.reshape(DEPTH, 1, GROUP_W),
        "lnb": conv_ln_b.reshape(DEPTH, 1, GROUP_W),
        "pw": conv_pw,
        "ws": gmlp_ws,
        "gbias": jnp.repeat(jnp.swapaxes(gmlp_b, 1, 2), HEAD_D, axis=2),
        "pool_w": pool_w,
        "pscale": pool_scale.reshape(DEPTH, 1, GROUP_W),
        "lg": jax.nn.log_sigmoid(ret_decay.astype(F32)).reshape(DEPTH * 2 * HEADS),
    }

    xc = x_prompt.reshape(batch * seq // TB, TB, D_MODEL)
    xl = x_sample
    states = None
    for l in range(DEPTH):
        last = l == DEPTH - 1
        xc, states = _mixer(xc, mods, pp, layer=l, seq_len=seq, rotate=False, mod_base=0, mod_stride=0,
                            states=states)
        xc = _ffn(xc, mods, pp["g2"], w_ffn_in, w_ffn_out, g_final2, layer=l, mod_base=0, mod_stride=0,
                  final_norm=last)
        (xl,) = _mixer(xl, mods, pp, layer=l, seq_len=dec_seq, rotate=True, mod_base=1, mod_stride=1,
                       rope_tabs=rope_tabs, s0=state_ret.astype(F32))
        xl = _ffn(xl, mods, pp["g2"], w_ffn_in, w_ffn_out, g_final2, layer=l, mod_base=1, mod_stride=1,
                  final_norm=last)

    y_prompt = xc.reshape(batch, seq, D_MODEL)
    return (y_prompt, xl, states.astype(x_prompt.dtype))
```

```python
import functools

import jax
import jax.numpy as jnp
import numpy as np
from jax import lax
from jax.experimental import pallas as pl
from jax.experimental.pallas import tpu as pltpu

F32 = jnp.float32
BF16 = jnp.bfloat16

D_MODEL = 1024
DEPTH = 2
GRID_W = 64
GROUP_W = D_MODEL // 4
CONV_W = 31
CHUNK = 128
HEADS = 4
HEAD_D = GROUP_W // HEADS
POOL_WINDOWS = (2, 4, 8, 16)
ROPE_BASE = 10000.0
D_FF = 2816
IN_COLS = 11 * GROUP_W
EPS = 1e-6

TB = 1024
N_CHUNKS = TB // CHUNK
ROW_TILE = 256
CONV_TILE = 64
N_CONV_TILES = TB // CONV_TILE
PAD = 16
FF_TILE = 256
N_FF_TILES = D_FF // FF_TILE
FF_OUT_ROWS = 256
FF_LOOKAHEAD = 2
FF_RING = FF_LOOKAHEAD + 1
ADA_TILE = 1536
MOD_ROWS = 8
VMEM_LIMIT_BYTES = 60 * 1024 * 1024

COL_A, COL_B, COL_C, COL_D = 0, 2 * GROUP_W, 4 * GROUP_W, 5 * GROUP_W
COL_V, COL_G = COL_D + 4 * GROUP_W, COL_D + 5 * GROUP_W


def _sigmoid(x):
    return 1.0 / (1.0 + jnp.exp(-x))


def _silu(x):
    return x * _sigmoid(x)


def _norm_mod(x, g, scale, shift):
    ms = jnp.mean(x * x, axis=-1, keepdims=True)
    return (x * lax.rsqrt(ms + EPS) * g) * (1.0 + scale) + shift


def _head_stack(x, lane_head):
    return jnp.concatenate([jnp.where(lane_head == h, x, 0.0) for h in range(HEADS)], axis=0)


def _mod_row(mod_ref, mod_base, mod_stride):
    row = jnp.maximum(mod_base + mod_stride * pl.program_id(0), 0) if mod_stride else max(mod_base, 0)

    def part(k):
        return mod_ref[pl.ds(row, 1), k * D_MODEL:(k + 1) * D_MODEL]
    return part


def _const_spec(shape):
    zeros = (0,) * len(shape)
    return pl.BlockSpec(shape, lambda i: zeros, pipeline_mode=pl.Buffered(1))


def _ada_kernel(c_ref, w_ref, b_ref, o_ref):
    a = _silu(c_ref[...]).astype(BF16)
    o_ref[...] = jnp.dot(a, w_ref[...].astype(BF16), preferred_element_type=F32) + b_ref[...]


def _ada_rows(cs, w_ada, b_ada):
    n_tiles = 6 * D_MODEL // ADA_TILE
    return pl.pallas_call(
        _ada_kernel,
        grid=(DEPTH, n_tiles),
        in_specs=[
            pl.BlockSpec((MOD_ROWS, D_MODEL), lambda l, j: (0, 0)),
            pl.BlockSpec((None, D_MODEL, ADA_TILE), lambda l, j: (l, 0, j)),
            pl.BlockSpec((None, 1, ADA_TILE), lambda l, j: (l, 0, j)),
        ],
        out_specs=pl.BlockSpec((None, MOD_ROWS, ADA_TILE), lambda l, j: (l, 0, j)),
        out_shape=jax.ShapeDtypeStruct((DEPTH, MOD_ROWS, 6 * D_MODEL), F32),
        compiler_params=pltpu.CompilerParams(
            dimension_semantics=("arbitrary", "arbitrary"), vmem_limit_bytes=VMEM_LIMIT_BYTES),
        name="ada_rows",
    )(cs, w_ada, b_ada.reshape(DEPTH, 1, 6 * D_MODEL))


def _mixer_kernel(*refs, layer, seq_len, rotate, mod_base, mod_stride):
    n_seq = TB // seq_len
    n_chunk = seq_len // CHUNK
    it = iter(refs)
    x_ref, mod_ref, g1_ref, w_in_ref, w_out_ref = (next(it) for _ in range(5))
    dw_ref, cb_ref, lng_ref, lnb_ref, pw_ref = (next(it) for _ in range(5))
    ws_ref, gbias_ref, poolw_ref, pscale_ref, lg_ref = (next(it) for _ in range(5))
    if rotate:
        cos_ref, sin_ref, s0_ref = (next(it) for _ in range(3))
    elif layer > 0:
        next(it)
    y_ref = next(it)
    if not rotate:
        st_ref = next(it)
    (hbf_ref, proj_ref, pad_ref, pad2_ref, tmp_ref, tmp2_ref,
     dcat_ref, qdec_ref, kdec_ref, sdec_ref, gmat_ref, qb_ref, upd_ref, sall_ref,
     pwb_ref, wcat_ref, pbd_ref, s0c_ref) = it
    cat_ref = hbf_ref
    of_ref = tmp_ref

    pwb_ref[...] = pw_ref[...].astype(BF16)
    pbd_ref[...] = jnp.zeros((GROUP_W, GROUP_W), BF16)
    for h in range(HEADS):
        wcat_ref[:, h * CHUNK:(h + 1) * CHUNK] = ws_ref[h].astype(BF16)
        pbd_ref[h * HEAD_D:(h + 1) * HEAD_D, h * HEAD_D:(h + 1) * HEAD_D] = poolw_ref[h].astype(BF16)
        if rotate:
            for d in range(2):
                s0c_ref[d, :, h * HEAD_D:(h + 1) * HEAD_D] = s0_ref[d, h]

    mod = _mod_row(mod_ref, mod_base, mod_stride)
    sh1 = mod(0)
    sc1 = mod(1)
    ga1 = mod(2)
    g1, cb, lng, lnb, pscale = (r[layer:layer + 1, :] for r in (g1_ref, cb_ref, lng_ref, lnb_ref, pscale_ref))

    lane = lax.broadcasted_iota(jnp.int32, (1, GROUP_W), 1)
    lane_head = lane // HEAD_D
    tiles_per_seq = seq_len // CONV_TILE
    chunks_per_seq = seq_len // CHUNK

    def seq_and_offset(i, per_seq, size):
        if n_seq == 1:
            return 0, pl.multiple_of(i * size, size)
        return i // per_seq, pl.multiple_of((i % per_seq) * size, size)

    def norm_body(i, _):
        r0 = pl.multiple_of(i * ROW_TILE, ROW_TILE)
        h = _norm_mod(x_ref[pl.ds(r0, ROW_TILE), :], g1, sc1, sh1).astype(BF16)
        hbf_ref[pl.ds(r0, ROW_TILE), :] = h
        proj_ref[pl.ds(r0, ROW_TILE), COL_A:COL_B] = jnp.dot(
            h, w_in_ref[:, COL_A:COL_B], preferred_element_type=F32)
        return 0
    lax.fori_loop(0, TB // ROW_TILE, norm_body, 0, unroll=True)

    for s in range(n_seq):
        for ref in (pad_ref, pad2_ref):
            ref[s, 0:PAD, :] = jnp.zeros((PAD, GROUP_W), F32)
            ref[s, PAD + seq_len:PAD + seq_len + PAD, :] = jnp.zeros((PAD, GROUP_W), F32)

    def glu_body(i, _):
        s, t0 = seq_and_offset(i, chunks_per_seq, CHUNK)
        r0 = pl.multiple_of(i * CHUNK, CHUNK)
        a1 = proj_ref[pl.ds(r0, CHUNK), COL_A:COL_A + GROUP_W]
        a2 = proj_ref[pl.ds(r0, CHUNK), COL_A + GROUP_W:COL_B]
        pad_ref[s, pl.ds(pl.multiple_of(t0 + PAD, 8), CHUNK), :] = a1 * _sigmoid(a2)
        return 0
    lax.fori_loop(0, N_CHUNKS, glu_body, 0, unroll=True)

    win = CONV_TILE + 2 * PAD

    def conv_tile(i):
        s, t0 = divmod(i, tiles_per_seq)
        t0 *= CONV_TILE
        for c0 in range(0, GROUP_W, 128):
            w = pad_ref[s, t0:t0 + win, c0:c0 + 128]
            acc = jnp.zeros((CONV_TILE, 128), F32) + cb[:, c0:c0 + 128]
            for b in range(8):
                wb = w if b == 0 else pltpu.roll(w, win - b, axis=0)
                for a in range(4):
                    k = 8 * a + b - 1
                    if 0 <= k < CONV_W:
                        acc = acc + dw_ref[k:k + 1, c0:c0 + 128] * wb[8 * a:8 * a + CONV_TILE, :]
            tmp_ref[i * CONV_TILE:(i + 1) * CONV_TILE, c0:c0 + 128] = acc

    pwin = CONV_TILE + 16
    lane128 = lax.broadcasted_iota(jnp.int32, (1, 128), 1)
    first_half = lane128 < HEAD_D

    def pool_tile(i):
        s, t0 = divmod(i, tiles_per_seq)
        t0 *= CONV_TILE
        interior = t0 >= 8 and t0 + CONV_TILE + 8 <= seq_len
        outs = []
        for col, (w_small, w_big) in enumerate(((2, 4), (8, 16))):
            w = pad2_ref[s, t0 + PAD - 8:t0 + PAD - 8 + pwin, col * 128:(col + 1) * 128]

            rolled = {0: w}

            def shifted(b, w=w, rolled=rolled):
                if b not in rolled:
                    rolled[b] = pltpu.roll(w, pwin - b, axis=0)
                return rolled[b]

            def lo(b):
                return shifted(b)[0:CONV_TILE, :]

            def hi(b):
                return shifted(b)[8:8 + CONV_TILE, :]

            tok = hi(0)
            if col == 0:
                s_small = lo(7) + tok
                s_big = s_small + lo(6) + hi(1)
            else:
                s_small = lo(7) + tok + lo(6) + hi(1) + lo(5) + lo(4) + hi(2) + hi(3)
                s_big = s_small + lo(3) + lo(2) + lo(1) + lo(0) + hi(4) + hi(5) + hi(6) + hi(7)
            ssum = jnp.where(first_half, s_small, s_big)
            if interior:
                mean = ssum * jnp.where(first_half, 1.0 / w_small, 1.0 / w_big)
            else:
                half = jnp.where(first_half, w_small // 2, w_big // 2)
                tpos = t0 + lax.broadcasted_iota(jnp.int32, (CONV_TILE, 128), 0)
                cnt = jnp.minimum(tpos + half, seq_len) - jnp.maximum(tpos - half, 0)
                mean = ssum / cnt.astype(F32)
            outs.append(mean - tok)
        tmp2_ref[i * CONV_TILE:(i + 1) * CONV_TILE, :] = jnp.concatenate(outs, axis=1)

    col_tiles = [COL_C] + list(range(COL_B, COL_C, GROUP_W)) + list(range(COL_D, IN_COLS, GROUP_W))
    per_step = -(-N_CONV_TILES // (len(col_tiles) - 1))
    conv_next = 0
    pool_next = 0
    for step, c0 in enumerate(col_tiles):
        proj_ref[:, c0:c0 + GROUP_W] = jnp.dot(
            hbf_ref[...], w_in_ref[:, c0:c0 + GROUP_W], preferred_element_type=F32)
        if step == 1:
            for s in range(n_seq):
                pad2_ref[s, PAD:PAD + seq_len, :] = proj_ref[s * seq_len:(s + 1) * seq_len, COL_C:COL_D]
        for _ in range(per_step):
            if conv_next < N_CONV_TILES:
                conv_tile(conv_next)
                conv_next += 1
            if step >= 1 and pool_next < N_CONV_TILES:
                pool_tile(pool_next)
                pool_next += 1
    assert conv_next == N_CONV_TILES and pool_next == N_CONV_TILES

    def ln_pw_body(i, _):
        r0 = pl.multiple_of(i * ROW_TILE, ROW_TILE)
        c = tmp_ref[pl.ds(r0, ROW_TILE), :]
        mu = jnp.mean(c, axis=-1, keepdims=True)
        cen = c - mu
        var = jnp.mean(cen * cen, axis=-1, keepdims=True)
        hn = cen * lax.rsqrt(var + EPS) * lng + lnb
        ya = jnp.dot(_silu(hn).astype(BF16), pwb_ref[...], preferred_element_type=F32)
        cat_ref[pl.ds(r0, ROW_TILE), 0:GROUP_W] = ya.astype(BF16)
        return 0
    lax.fori_loop(0, TB // ROW_TILE, ln_pw_body, 0, unroll=True)

    def gmlp_body(i, _):
        r0 = pl.multiple_of(i * CHUNK, CHUNK)
        u = proj_ref[pl.ds(r0, CHUNK), COL_B:COL_B + GROUP_W]
        v = proj_ref[pl.ds(r0, CHUNK), COL_B + GROUP_W:COL_C]
        vstack = _head_stack(v, lane_head).astype(BF16)
        sg = jnp.dot(wcat_ref[...], vstack, preferred_element_type=F32) + gbias_ref[...]
        cat_ref[pl.ds(r0, CHUNK), GROUP_W:2 * GROUP_W] = (u * sg).astype(BF16)
        return 0
    lax.fori_loop(0, N_CHUNKS, gmlp_body, 0, unroll=True)

    def pool_mix_body(i, _):
        r0 = pl.multiple_of(i * ROW_TILE, ROW_TILE)
        yc = jnp.dot(tmp2_ref[pl.ds(r0, ROW_TILE), :].astype(BF16), pbd_ref[...],
                     preferred_element_type=F32) * pscale
        cat_ref[pl.ds(r0, ROW_TILE), 2 * GROUP_W:3 * GROUP_W] = yc.astype(BF16)
        return 0
    lax.fori_loop(0, TB // ROW_TILE, pool_mix_body, 0, unroll=True)

    ri = lax.broadcasted_iota(jnp.int32, (CHUNK, HEADS * CHUNK), 0)
    ci = lax.broadcasted_iota(jnp.int32, (CHUNK, HEADS * CHUNK), 1)
    cj = ci % CHUNK
    chead = ci // CHUNK
    rq = lax.broadcasted_iota(jnp.int32, (CHUNK, GROUP_W), 0).astype(F32)
    for d in range(2):
        lgs = [lg_ref[(2 * layer + d) * HEADS + h] for h in range(HEADS)]
        lg_wide = jnp.where(chead == 0, lgs[0], jnp.where(chead == 1, lgs[1],
                                                          jnp.where(chead == 2, lgs[2], lgs[3])))
        lg_lane = jnp.where(lane_head == 0, lgs[0], jnp.where(lane_head == 1, lgs[1],
                                                              jnp.where(lane_head == 2, lgs[2], lgs[3])))
        dist = (ri - cj) if d == 0 else (cj - ri)
        keep = dist >= 0
        dcat_ref[d] = jnp.where(keep, jnp.exp(jnp.where(keep, dist, 0).astype(F32) * lg_wide), 0.0)
        if d == 0:
            qdec_ref[d] = jnp.exp((rq + 1.0) * lg_lane)
            kdec_ref[d] = jnp.exp((CHUNK - 1.0 - rq) * lg_lane)
        else:
            qdec_ref[d] = jnp.exp((CHUNK - rq) * lg_lane)
            kdec_ref[d] = jnp.exp(rq * lg_lane)
        sdec_ref[d] = jnp.exp(float(CHUNK) * lg_lane)

    rr = lax.broadcasted_iota(jnp.int32, (GROUP_W, GROUP_W), 0) // HEAD_D
    cc = lax.broadcasted_iota(jnp.int32, (GROUP_W, GROUP_W), 1) // HEAD_D
    gmat_ref[...] = jnp.where(rr == cc, 1.0 / HEAD_D, 0.0).astype(BF16)

    lane_bit = (lane & 16) == 0
    k_scale = HEAD_D ** -0.5

    def rope(z, r0):
        if not rotate:
            return z
        cos = cos_ref[pl.ds(r0, CHUNK), :]
        sin = sin_ref[pl.ds(r0, CHUNK), :]
        halves = []
        for c0 in (0, 128):
            zz = z[:, c0:c0 + 128]
            partner = jnp.where(lane_bit[:, c0:c0 + 128],
                                pltpu.roll(zz, 128 - 16, axis=1), pltpu.roll(zz, 16, axis=1))
            halves.append(partner)
        return z * cos + jnp.concatenate(halves, axis=1) * sin

    def pair_stack(zb, pair):
        zero = jnp.zeros_like(zb)
        return jnp.concatenate([jnp.where(lane_head == h, zb, zero) for h in (2 * pair, 2 * pair + 1)], axis=0)

    def intra_body(c, _):
        r0 = pl.multiple_of(c * CHUNK, CHUNK)
        v = proj_ref[pl.ds(r0, CHUNK), COL_V:COL_V + GROUP_W]
        vb = v.astype(BF16)
        vstacks = [pair_stack(vb, pair) for pair in range(2)]
        o = None
        for d in range(2):
            qc0 = COL_D + 2 * d * GROUP_W
            q = rope(proj_ref[pl.ds(r0, CHUNK), qc0:qc0 + GROUP_W], r0)
            k = rope(proj_ref[pl.ds(r0, CHUNK), qc0 + GROUP_W:qc0 + 2 * GROUP_W], r0) * k_scale
            qb = q.astype(BF16)
            kb = k.astype(BF16)
            qb_ref[d, pl.ds(r0, CHUNK), :] = qb
            for pair in range(2):
                att = lax.dot_general(qb, pair_stack(kb, pair), (((1,), (1,)), ((), ())),
                                      preferred_element_type=F32)
                att = (att * dcat_ref[d, :, pair * 2 * CHUNK:(pair + 1) * 2 * CHUNK]).astype(BF16)
                od = jnp.dot(att, vstacks[pair], preferred_element_type=F32)
                o = od if o is None else o + od
            kd = (k * kdec_ref[d]).astype(BF16)
            upd = lax.dot_general(kd, vb, (((0,), (0,)), ((), ())), preferred_element_type=F32)
            compact = None
            for h in range(HEADS):
                part = jnp.where(lane_head == h, upd[h * HEAD_D:(h + 1) * HEAD_D, :], 0.0)
                compact = part if compact is None else compact + part
            upd_ref[d, c] = compact
        of_ref[pl.ds(r0, CHUNK), :] = o
        return 0
    lax.fori_loop(0, N_CHUNKS, intra_body, 0, unroll=True)

    for s in range(n_seq):
        for d in range(2):
            st = s0c_ref[d] if rotate else jnp.zeros((HEAD_D, GROUP_W), F32)
            order = range(n_chunk) if d == 0 else range(n_chunk - 1, -1, -1)
            for c in order:
                cg = s * n_chunk + c
                for h in range(HEADS):
                    sall_ref[d, cg, h * HEAD_D:(h + 1) * HEAD_D, :] = (
                        jnp.where(lane_head == h, st, 0.0).astype(BF16))
                st = st * sdec_ref[d] + upd_ref[d, cg]
            if not rotate:
                for h in range(HEADS):
                    if layer == 0:
                        st_ref[s, 0, d, h] = st[:, h * HEAD_D:(h + 1) * HEAD_D]
                    else:
                        st_ref[s, d, h] = st[:, h * HEAD_D:(h + 1) * HEAD_D]
    if not rotate and layer == 0:
        for s in range(n_seq):
            for later in range(1, DEPTH):
                for d in range(2):
                    for h in range(HEADS):
                        st_ref[s, later, d, h] = jnp.zeros((HEAD_D, HEAD_D), F32)

    def cross_body(c, _):
        r0 = pl.multiple_of(c * CHUNK, CHUNK)
        o = of_ref[pl.ds(r0, CHUNK), :]
        for d in range(2):
            o = o + jnp.dot(qb_ref[d, pl.ds(r0, CHUNK), :], sall_ref[d, c],
                            preferred_element_type=F32) * qdec_ref[d]
        of_ref[pl.ds(r0, CHUNK), :] = o
        return 0
    lax.fori_loop(0, N_CHUNKS, cross_body, 0, unroll=True)

    def center_body(i, _):
        r0 = pl.multiple_of(i * ROW_TILE, ROW_TILE)
        o = of_ref[pl.ds(r0, ROW_TILE), :]
        gmat = gmat_ref[...]
        o_hi = o.astype(BF16)
        o_lo = (o - o_hi.astype(F32)).astype(BF16)
        mu = (jnp.dot(o_hi, gmat, preferred_element_type=F32)
              + jnp.dot(o_lo, gmat, preferred_element_type=F32))
        of_ref[pl.ds(r0, ROW_TILE), :] = o - mu
        return 0
    lax.fori_loop(0, TB // ROW_TILE, center_body, 0, unroll=True)

    def gate_body(i, _):
        r0 = pl.multiple_of(i * ROW_TILE, ROW_TILE)
        cen = of_ref[pl.ds(r0, ROW_TILE), :]
        var = jnp.dot((cen * cen).astype(BF16), gmat_ref[...], preferred_element_type=F32)
        on = cen * lax.rsqrt(var + EPS)
        g = proj_ref[pl.ds(r0, ROW_TILE), COL_G:COL_G + GROUP_W]
        cat_ref[pl.ds(r0, ROW_TILE), 3 * GROUP_W:4 * GROUP_W] = (_silu(g) * on).astype(BF16)
        return 0
    lax.fori_loop(0, TB // ROW_TILE, gate_body, 0, unroll=True)

    def out_body(i, _):
        r0 = pl.multiple_of(i * ROW_TILE, ROW_TILE)
        y = jnp.dot(cat_ref[pl.ds(r0, ROW_TILE), :], w_out_ref[...], preferred_element_type=F32)
        y_ref[pl.ds(r0, ROW_TILE), :] = x_ref[pl.ds(r0, ROW_TILE), :] + ga1 * y
        return 0
    lax.fori_loop(0, TB // ROW_TILE, out_body, 0, unroll=True)


def _layer_spec(shape, layer):
    zeros = (0,) * len(shape)
    return pl.BlockSpec((None,) + tuple(shape), lambda i: (layer,) + zeros, pipeline_mode=pl.Buffered(1))


def _mod_spec(layer):
    return pl.BlockSpec((MOD_ROWS, 6 * D_MODEL), lambda i: (layer, 0), pipeline_mode=pl.Buffered(1))


def _mixer(x, mods, pp, *, layer, seq_len, rotate, mod_base, mod_stride, rope_tabs=None, s0=None, states=None):
    nb = x.shape[0]
    n_seq = TB // seq_len
    in_specs = [
        pl.BlockSpec((None, TB, D_MODEL), lambda i: (i, 0, 0)),
        _mod_spec(layer),
        _const_spec((DEPTH, D_MODEL)),
        _layer_spec((D_MODEL, IN_COLS), layer),
        _layer_spec((D_MODEL, D_MODEL), layer),
        _layer_spec((CONV_W, GROUP_W), layer),
        _const_spec((DEPTH, GROUP_W)), _const_spec((DEPTH, GROUP_W)), _const_spec((DEPTH, GROUP_W)),
        _layer_spec((GROUP_W, GROUP_W), layer),
        _layer_spec((HEADS, CHUNK, CHUNK), layer),
        _layer_spec((CHUNK, GROUP_W), layer),
        _layer_spec((HEADS, HEAD_D, HEAD_D), layer),
        _const_spec((DEPTH, GROUP_W)),
        pl.BlockSpec(memory_space=pltpu.SMEM),
    ]
    args = [x, mods, pp["g1"], pp["w_in"], pp["w_out"], pp["dw"], pp["cb"], pp["lng"], pp["lnb"],
            pp["pw"], pp["ws"], pp["gbias"], pp["pool_w"], pp["pscale"], pp["lg"]]
    out_shape = [jax.ShapeDtypeStruct((nb, TB, D_MODEL), F32)]
    out_specs = [pl.BlockSpec((None, TB, D_MODEL), lambda i: (i, 0, 0))]
    aliases = {}
    if rotate:
        in_specs += [_const_spec((TB, GROUP_W)), _const_spec((TB, GROUP_W)),
                     pl.BlockSpec((None, None, 2, HEADS, HEAD_D, HEAD_D), lambda i: (i, layer, 0, 0, 0, 0))]
        args += [rope_tabs[0], rope_tabs[1], s0]
    else:
        out_shape.append(jax.ShapeDtypeStruct((nb * n_seq, DEPTH, 2, HEADS, HEAD_D, HEAD_D), F32))
        if layer == 0:
            out_specs.append(pl.BlockSpec((n_seq, DEPTH, 2, HEADS, HEAD_D, HEAD_D),
                                          lambda i: (i, 0, 0, 0, 0, 0)))
        else:
            in_specs.append(pl.BlockSpec(memory_space=pl.ANY))
            args.append(states)
            aliases = {len(args) - 1: 1}
            out_specs.append(pl.BlockSpec((n_seq, None, 2, HEADS, HEAD_D, HEAD_D),
                                          lambda i: (i, layer, 0, 0, 0, 0)))
    scratch = [
        pltpu.VMEM((TB, D_MODEL), BF16),
        pltpu.VMEM((TB, IN_COLS), F32),
        pltpu.VMEM((n_seq, seq_len + 2 * PAD, GROUP_W), F32),
        pltpu.VMEM((n_seq, seq_len + 2 * PAD, GROUP_W), F32),
        pltpu.VMEM((TB, GROUP_W), F32),
        pltpu.VMEM((TB, GROUP_W), F32),
        pltpu.VMEM((2, CHUNK, HEADS * CHUNK), F32),
        pltpu.VMEM((2, CHUNK, GROUP_W), F32),
        pltpu.VMEM((2, CHUNK, GROUP_W), F32),
        pltpu.VMEM((2, 1, GROUP_W), F32),
        pltpu.VMEM((GROUP_W, GROUP_W), BF16),
        pltpu.VMEM((2, TB, GROUP_W), BF16),
        pltpu.VMEM((2, N_CHUNKS, HEAD_D, GROUP_W), F32),
        pltpu.VMEM((2, N_CHUNKS, GROUP_W, GROUP_W), BF16),
        pltpu.VMEM((GROUP_W, GROUP_W), BF16),
        pltpu.VMEM((CHUNK, HEADS * CHUNK), BF16),
        pltpu.VMEM((GROUP_W, GROUP_W), BF16),
        pltpu.VMEM((2, HEAD_D, GROUP_W), F32),
    ]
    outs = pl.pallas_call(
        functools.partial(_mixer_kernel, layer=layer, seq_len=seq_len, rotate=rotate,
                          mod_base=mod_base, mod_stride=mod_stride),
        grid=(nb,),
        in_specs=in_specs,
        out_specs=out_specs,
        out_shape=out_shape,
        scratch_shapes=scratch,
        input_output_aliases=aliases,
        compiler_params=pltpu.CompilerParams(
            dimension_semantics=("arbitrary",), vmem_limit_bytes=VMEM_LIMIT_BYTES),
        name="mixer_lat" if rotate else "mixer_ctx",
    )(*args)
    return outs


def _ffn_kernel(x_ref, mod_ref, g2_ref, w_in_hbm, w_out_hbm, gf_ref, y_ref,
                hbf_ref, act_ref, wgu_ref, wo_ref, stg_g_ref, stg_u_ref, stg_o_ref, sem,
                *, layer, final_norm, mod_base, mod_stride):
    pid = pl.program_id(0)
    mod = _mod_row(mod_ref, mod_base, mod_stride)
    sh2 = mod(3)
    sc2 = mod(4)
    ga2 = mod(5)
    g2 = g2_ref[layer:layer + 1, :]

    def tile_copies(j, slot):
        return (
            pltpu.make_async_copy(w_in_hbm.at[layer, :, pl.ds(j * FF_TILE, FF_TILE)],
                                  stg_g_ref.at[slot], sem.at[0, slot]),
            pltpu.make_async_copy(w_in_hbm.at[layer, :, pl.ds(D_FF + j * FF_TILE, FF_TILE)],
                                  stg_u_ref.at[slot], sem.at[1, slot]),
            pltpu.make_async_copy(w_out_hbm.at[layer, pl.ds(j * FF_TILE, FF_TILE), :],
                                  stg_o_ref.at[slot], sem.at[2, slot]),
        )

    def start_tile(j):
        for cp in tile_copies(j, j % FF_RING):
            cp.start()

    @pl.when(pid == 0)
    def _():
        for j in range(FF_LOOKAHEAD):
            start_tile(j)

    def norm_body(i, _):
        r0 = pl.multiple_of(i * ROW_TILE, ROW_TILE)
        h = _norm_mod(x_ref[pl.ds(r0, ROW_TILE), :], g2, sc2, sh2)
        hbf_ref[pl.ds(r0, ROW_TILE), :] = h.astype(BF16)
        return 0
    lax.fori_loop(0, TB // ROW_TILE, norm_body, 0)

    def hidden_tiles(fetch):
        for j in range(N_FF_TILES):
            gcols = slice(j * FF_TILE, (j + 1) * FF_TILE)
            ucols = slice(D_FF + j * FF_TILE, D_FF + (j + 1) * FF_TILE)
            if fetch:
                if j + FF_LOOKAHEAD < N_FF_TILES:
                    start_tile(j + FF_LOOKAHEAD)
                slot = j % FF_RING
                for cp in tile_copies(j, slot):
                    cp.wait()
                wgu_ref[:, gcols] = stg_g_ref[slot].astype(BF16)
                wgu_ref[:, ucols] = stg_u_ref[slot].astype(BF16)
                wo_ref[gcols, :] = stg_o_ref[slot].astype(BF16)
            gate = jnp.dot(hbf_ref[...], wgu_ref[:, gcols], preferred_element_type=F32)
            up = jnp.dot(hbf_ref[...], wgu_ref[:, ucols], preferred_element_type=F32)
            act_ref[:, gcols] = (_silu(gate) * up).astype(BF16)

    @pl.when(pid == 0)
    def _():
        hidden_tiles(True)

    @pl.when(pid != 0)
    def _():
        hidden_tiles(False)

    def down_body(i, _):
        rows = pl.ds(pl.multiple_of(i * FF_OUT_ROWS, FF_OUT_ROWS), FF_OUT_ROWS)
        y = x_ref[rows, :] + ga2 * jnp.dot(act_ref[rows, :], wo_ref[...], preferred_element_type=F32)
        if final_norm:
            ms = jnp.mean(y * y, axis=-1, keepdims=True)
            y = y * lax.rsqrt(ms + EPS) * gf_ref[...]
        y_ref[rows, :] = y
        return 0
    lax.fori_loop(0, TB // FF_OUT_ROWS, down_body, 0)


def _ffn(x, mods, g2, w_ffn_in, w_ffn_out, g_final, *, layer, mod_base, mod_stride, final_norm):
    nb = x.shape[0]
    return pl.pallas_call(
        functools.partial(_ffn_kernel, layer=layer, final_norm=final_norm, mod_base=mod_base,
                          mod_stride=mod_stride),
        grid=(nb,),
        in_specs=[
            pl.BlockSpec((None, TB, D_MODEL), lambda i: (i, 0, 0)),
            _mod_spec(layer),
            _const_spec((DEPTH, D_MODEL)),
            pl.BlockSpec(memory_space=pl.ANY),
            pl.BlockSpec(memory_space=pl.ANY),
            _const_spec((1, D_MODEL)),
        ],
        out_specs=pl.BlockSpec((None, TB, D_MODEL), lambda i: (i, 0, 0)),
        out_shape=jax.ShapeDtypeStruct((nb, TB, D_MODEL), F32),
        scratch_shapes=[
            pltpu.VMEM((TB, D_MODEL), BF16),
            pltpu.VMEM((TB, D_FF), BF16),
            pltpu.VMEM((D_MODEL, 2 * D_FF), BF16),
            pltpu.VMEM((D_FF, D_MODEL), BF16),
            pltpu.VMEM((FF_RING, D_MODEL, FF_TILE), F32),
            pltpu.VMEM((FF_RING, D_MODEL, FF_TILE), F32),
            pltpu.VMEM((FF_RING, FF_TILE, D_MODEL), F32),
            pltpu.SemaphoreType.DMA((3, FF_RING)),
        ],
        compiler_params=pltpu.CompilerParams(
            dimension_semantics=("arbitrary",), vmem_limit_bytes=VMEM_LIMIT_BYTES),
        name="ffn",
    )(x, mods, g2, w_ffn_in, w_ffn_out, g_final)


def _rope_tables(seq_len):
    t = np.arange(seq_len)
    r = (t // GRID_W).astype(np.float32)
    c = (t % GRID_W).astype(np.float32)
    nf = HEAD_D // 4
    inv = np.float32(ROPE_BASE) ** (-np.arange(nf, dtype=np.float32) / np.float32(nf))
    ang_r = r[:, None] * inv
    ang_c = c[:, None] * inv
    cos = np.concatenate([np.cos(ang_r), np.cos(ang_r), np.cos(ang_c), np.cos(ang_c)], axis=-1)
    sin = np.concatenate([-np.sin(ang_r), np.sin(ang_r), -np.sin(ang_c), np.sin(ang_c)], axis=-1)
    return (jnp.asarray(np.tile(cos, (1, HEADS)), dtype=F32), jnp.asarray(np.tile(sin, (1, HEADS)), dtype=F32))


def kernel(x_prompt, x_sample, state_ret, c, c_ctx, w_ada, b_ada, g_norm1, g_norm2, w_in, w_out, conv_dw,
           conv_b, conv_ln_g, conv_ln_b, conv_pw, gmlp_ws, gmlp_b, pool_w, pool_scale, ret_decay, w_ffn_in,
           w_ffn_out, g_final):
    batch, seq, _ = x_prompt.shape
    dec_batch, dec_seq, _ = x_sample.shape
    assert dec_seq == TB and TB % seq == 0 and (batch * seq) % TB == 0
    assert 1 + dec_batch <= MOD_ROWS

    cs = jnp.concatenate([c_ctx[None, :], c, jnp.zeros((MOD_ROWS - 1 - dec_batch, D_MODEL), F32)], axis=0)
    mods = _ada_rows(cs, w_ada, b_ada).reshape(DEPTH * MOD_ROWS, 6 * D_MODEL)
    rope_tabs = _rope_tables(dec_seq)
    g_final2 = g_final.reshape(1, D_MODEL)
    pp = {
        "g1": g_norm1,
        "g2": g_norm2,
        "w_in": w_in.astype(BF16),
        "w_out": w_out.astype(BF16),
        "dw": conv_dw,
        "cb": conv_b,
        "lng": conv_ln_g,
        "lnb": conv_ln_b,
        "pw": conv_pw,
        "ws": gmlp_ws,
        "gbias": jnp.repeat(jnp.swapaxes(gmlp_b, 1, 2), HEAD_D, axis=2),
        "pool_w": pool_w,
        "pscale": pool_scale,
        "lg": jax.nn.log_sigmoid(ret_decay.astype(F32)).reshape(DEPTH * 2 * HEADS),
    }

    xc = x_prompt.reshape(batch * seq // TB, TB, D_MODEL)
    xl = x_sample
    states = None
    for l in range(DEPTH):
        last = l == DEPTH - 1
        xc, states = _mixer(xc, mods, pp, layer=l, seq_len=seq, rotate=False, mod_base=0, mod_stride=0,
                            states=states)
        xc = _ffn(xc, mods, pp["g2"], w_ffn_in, w_ffn_out, g_final2, layer=l, mod_base=0, mod_stride=0,
                  final_norm=last)
        (xl,) = _mixer(xl, mods, pp, layer=l, seq_len=dec_seq, rotate=True, mod_base=1, mod_stride=1,
                       rope_tabs=rope_tabs, s0=state_ret.astype(F32))
        xl = _ffn(xl, mods, pp["g2"], w_ffn_in, w_ffn_out, g_final2, layer=l, mod_base=1, mod_stride=1,
                  final_norm=last)

    y_prompt = xc.reshape(batch, seq, D_MODEL)
    return (y_prompt, xl, states.astype(x_prompt.dtype))
```

```python
import functools

import jax
import jax.numpy as jnp
import numpy as np
from jax import lax
from jax.experimental import pallas as pl
from jax.experimental.pallas import tpu as pltpu

F32 = jnp.float32
BF16 = jnp.bfloat16

D_MODEL = 1024
DEPTH = 2
GRID_W = 64
GROUP_W = D_MODEL // 4
CONV_W = 31
CHUNK = 128
HEADS = 4
HEAD_D = GROUP_W // HEADS
POOL_WINDOWS = (2, 4, 8, 16)
ROPE_BASE = 10000.0
D_FF = 2816
IN_COLS = 11 * GROUP_W
EPS = 1e-6

TB = 1024
N_CHUNKS = TB // CHUNK
ROW_TILE = 256
CONV_TILE = 64
N_CONV_TILES = TB // CONV_TILE
PAD = 16
FF_TILE = 256
N_FF_TILES = D_FF // FF_TILE
FF_OUT_ROWS = 256
FF_LOOKAHEAD = 2
FF_RING = FF_LOOKAHEAD + 1
ADA_TILE = 1536
MOD_ROWS = 8
VMEM_LIMIT_BYTES = 60 * 1024 * 1024

COL_A, COL_B, COL_C, COL_D = 0, 2 * GROUP_W, 4 * GROUP_W, 5 * GROUP_W
COL_V, COL_G = COL_D + 4 * GROUP_W, COL_D + 5 * GROUP_W


def _sigmoid(x):
    return 1.0 / (1.0 + jnp.exp(-x))


def _silu(x):
    return x * _sigmoid(x)


def _norm_mod(x, g, scale, shift):
    ms = jnp.mean(x * x, axis=-1, keepdims=True)
    return (x * lax.rsqrt(ms + EPS) * g) * (1.0 + scale) + shift


def _head_stack(x, lane_head):
    return jnp.concatenate([jnp.where(lane_head == h, x, 0.0) for h in range(HEADS)], axis=0)


def _mod_row(mod_ref, mod_base, mod_stride):
    row = jnp.maximum(mod_base + mod_stride * pl.program_id(0), 0) if mod_stride else max(mod_base, 0)

    def part(k):
        return mod_ref[pl.ds(row, 1), k * D_MODEL:(k + 1) * D_MODEL]
    return part


def _const_spec(shape):
    zeros = (0,) * len(shape)
    return pl.BlockSpec(shape, lambda i: zeros, pipeline_mode=pl.Buffered(1))


def _ada_kernel(c_ref, w_ref, b_ref, o_ref):
    a = _silu(c_ref[...]).astype(BF16)
    o_ref[...] = jnp.dot(a, w_ref[...].astype(BF16), preferred_element_type=F32) + b_ref[...]


def _ada_rows(cs, w_ada, b_ada):
    n_tiles = 6 * D_MODEL // ADA_TILE
    return pl.pallas_call(
        _ada_kernel,
        grid=(DEPTH, n_tiles),
        in_specs=[
            pl.BlockSpec((MOD_ROWS, D_MODEL), lambda l, j: (0, 0)),
            pl.BlockSpec((None, D_MODEL, ADA_TILE), lambda l, j: (l, 0, j)),
            pl.BlockSpec((None, 1, ADA_TILE), lambda l, j: (l, 0, j)),
        ],
        out_specs=pl.BlockSpec((None, MOD_ROWS, ADA_TILE), lambda l, j: (l, 0, j)),
        out_shape=jax.ShapeDtypeStruct((DEPTH, MOD_ROWS, 6 * D_MODEL), F32),
        compiler_params=pltpu.CompilerParams(
            dimension_semantics=("arbitrary", "arbitrary"), vmem_limit_bytes=VMEM_LIMIT_BYTES),
        name="ada_rows",
    )(cs, w_ada, b_ada.reshape(DEPTH, 1, 6 * D_MODEL))


def _mixer_kernel(*refs, layer, seq_len, rotate, mod_base, mod_stride):
    n_seq = TB // seq_len
    n_chunk = seq_len // CHUNK
    it = iter(refs)
    x_ref, mod_ref, g1_ref, w_in_ref, w_out_ref = (next(it) for _ in range(5))
    dw_ref, cb_ref, lng_ref, lnb_ref, pw_ref = (next(it) for _ in range(5))
    ws_ref, gbias_ref, poolw_ref, pscale_ref, lg_ref, g2_ref = (next(it) for _ in range(6))
    if rotate:
        cos_ref, sin_ref, s0_ref = (next(it) for _ in range(3))
    elif layer > 0:
        next(it)
    y_ref, h_ref = next(it), next(it)
    if not rotate:
        st_ref = next(it)
    (hbf_ref, proj_ref, pad_ref, pad2_ref, tmp_ref, tmp2_ref,
     dcat_ref, qdec_ref, kdec_ref, sdec_ref, gmat_ref, qb_ref, upd_ref, sall_ref,
     pwb_ref, wcat_ref, pbd_ref, s0c_ref) = it
    cat_ref = hbf_ref
    of_ref = tmp_ref

    pwb_ref[...] = pw_ref[...].astype(BF16)
    pbd_ref[...] = jnp.zeros((GROUP_W, GROUP_W), BF16)
    for h in range(HEADS):
        wcat_ref[:, h * CHUNK:(h + 1) * CHUNK] = ws_ref[h].astype(BF16)
        pbd_ref[h * HEAD_D:(h + 1) * HEAD_D, h * HEAD_D:(h + 1) * HEAD_D] = poolw_ref[h].astype(BF16)
        if rotate:
            for d in range(2):
                s0c_ref[d, :, h * HEAD_D:(h + 1) * HEAD_D] = s0_ref[d, h]

    mod = _mod_row(mod_ref, mod_base, mod_stride)
    sh1 = mod(0)
    sc1 = mod(1)
    ga1 = mod(2)
    g1, cb, lng, lnb, pscale = (r[layer:layer + 1, :] for r in (g1_ref, cb_ref, lng_ref, lnb_ref, pscale_ref))

    lane = lax.broadcasted_iota(jnp.int32, (1, GROUP_W), 1)
    lane_head = lane // HEAD_D
    tiles_per_seq = seq_len // CONV_TILE
    chunks_per_seq = seq_len // CHUNK

    def seq_and_offset(i, per_seq, size):
        if n_seq == 1:
            return 0, pl.multiple_of(i * size, size)
        return i // per_seq, pl.multiple_of((i % per_seq) * size, size)

    def norm_body(i, _):
        r0 = pl.multiple_of(i * ROW_TILE, ROW_TILE)
        h = _norm_mod(x_ref[pl.ds(r0, ROW_TILE), :], g1, sc1, sh1).astype(BF16)
        hbf_ref[pl.ds(r0, ROW_TILE), :] = h
        proj_ref[pl.ds(r0, ROW_TILE), COL_A:COL_B] = jnp.dot(
            h, w_in_ref[:, COL_A:COL_B], preferred_element_type=F32)
        return 0
    lax.fori_loop(0, TB // ROW_TILE, norm_body, 0, unroll=True)

    for s in range(n_seq):
        for ref in (pad_ref, pad2_ref):
            ref[s, 0:PAD, :] = jnp.zeros((PAD, GROUP_W), F32)
            ref[s, PAD + seq_len:PAD + seq_len + PAD, :] = jnp.zeros((PAD, GROUP_W), F32)

    def glu_body(i, _):
        s, t0 = seq_and_offset(i, chunks_per_seq, CHUNK)
        r0 = pl.multiple_of(i * CHUNK, CHUNK)
        a1 = proj_ref[pl.ds(r0, CHUNK), COL_A:COL_A + GROUP_W]
        a2 = proj_ref[pl.ds(r0, CHUNK), COL_A + GROUP_W:COL_B]
        pad_ref[s, pl.ds(pl.multiple_of(t0 + PAD, 8), CHUNK), :] = a1 * _sigmoid(a2)
        return 0
    lax.fori_loop(0, N_CHUNKS, glu_body, 0, unroll=True)

    win = CONV_TILE + 2 * PAD

    def conv_tile(i):
        s, t0 = divmod(i, tiles_per_seq)
        t0 *= CONV_TILE
        for c0 in range(0, GROUP_W, 128):
            w = pad_ref[s, t0:t0 + win, c0:c0 + 128]
            acc = jnp.zeros((CONV_TILE, 128), F32) + cb[:, c0:c0 + 128]
            for b in range(8):
                wb = w if b == 0 else pltpu.roll(w, win - b, axis=0)
                for a in range(4):
                    k = 8 * a + b - 1
                    if 0 <= k < CONV_W:
                        acc = acc + dw_ref[k:k + 1, c0:c0 + 128] * wb[8 * a:8 * a + CONV_TILE, :]
            tmp_ref[i * CONV_TILE:(i + 1) * CONV_TILE, c0:c0 + 128] = acc

    pwin = CONV_TILE + 16
    lane128 = lax.broadcasted_iota(jnp.int32, (1, 128), 1)
    first_half = lane128 < HEAD_D

    def pool_tile(i):
        s, t0 = divmod(i, tiles_per_seq)
        t0 *= CONV_TILE
        interior = t0 >= 8 and t0 + CONV_TILE + 8 <= seq_len
        outs = []
        for col, (w_small, w_big) in enumerate(((2, 4), (8, 16))):
            w = pad2_ref[s, t0 + PAD - 8:t0 + PAD - 8 + pwin, col * 128:(col + 1) * 128]

            rolled = {0: w}

            def shifted(b, w=w, rolled=rolled):
                if b not in rolled:
                    rolled[b] = pltpu.roll(w, pwin - b, axis=0)
                return rolled[b]

            def lo(b):
                return shifted(b)[0:CONV_TILE, :]

            def hi(b):
                return shifted(b)[8:8 + CONV_TILE, :]

            tok = hi(0)
            if col == 0:
                s_small = lo(7) + tok
                s_big = s_small + lo(6) + hi(1)
            else:
                s_small = lo(7) + tok + lo(6) + hi(1) + lo(5) + lo(4) + hi(2) + hi(3)
                s_big = s_small + lo(3) + lo(2) + lo(1) + lo(0) + hi(4) + hi(5) + hi(6) + hi(7)
            ssum = jnp.where(first_half, s_small, s_big)
            if interior:
                mean = ssum * jnp.where(first_half, 1.0 / w_small, 1.0 / w_big)
            else:
                half = jnp.where(first_half, w_small // 2, w_big // 2)
                tpos = t0 + lax.broadcasted_iota(jnp.int32, (CONV_TILE, 128), 0)
                cnt = jnp.minimum(tpos + half, seq_len) - jnp.maximum(tpos - half, 0)
                mean = ssum / cnt.astype(F32)
            outs.append(mean - tok)
        tmp2_ref[i * CONV_TILE:(i + 1) * CONV_TILE, :] = jnp.concatenate(outs, axis=1)

    col_tiles = [COL_C] + list(range(COL_B, COL_C, GROUP_W)) + list(range(COL_D, IN_COLS, GROUP_W))
    per_step = -(-N_CONV_TILES // (len(col_tiles) - 1))
    conv_next = 0
    pool_next = 0
    for step, c0 in enumerate(col_tiles):
        proj_ref[:, c0:c0 + GROUP_W] = jnp.dot(
            hbf_ref[...], w_in_ref[:, c0:c0 + GROUP_W], preferred_element_type=F32)
        if step == 1:
            for s in range(n_seq):
                pad2_ref[s, PAD:PAD + seq_len, :] = proj_ref[s * seq_len:(s + 1) * seq_len, COL_C:COL_D]
        for _ in range(per_step):
            if conv_next < N_CONV_TILES:
                conv_tile(conv_next)
                conv_next += 1
            if step >= 1 and pool_next < N_CONV_TILES:
                pool_tile(pool_next)
                pool_next += 1
    assert conv_next == N_CONV_TILES and pool_next == N_CONV_TILES

    def ln_pw_body(i, _):
        r0 = pl.multiple_of(i * ROW_TILE, ROW_TILE)
        c = tmp_ref[pl.ds(r0, ROW_TILE), :]
        mu = jnp.mean(c, axis=-1, keepdims=True)
        cen = c - mu
        var = jnp.mean(cen * cen, axis=-1, keepdims=True)
        hn = cen * lax.rsqrt(var + EPS) * lng + lnb
        ya = jnp.dot(_silu(hn).astype(BF16), pwb_ref[...], preferred_element_type=F32)
        cat_ref[pl.ds(r0, ROW_TILE), 0:GROUP_W] = ya.astype(BF16)
        return 0
    lax.fori_loop(0, TB // ROW_TILE, ln_pw_body, 0, unroll=True)

    def gmlp_body(i, _):
        r0 = pl.multiple_of(i * CHUNK, CHUNK)
        u = proj_ref[pl.ds(r0, CHUNK), COL_B:COL_B + GROUP_W]
        v = proj_ref[pl.ds(r0, CHUNK), COL_B + GROUP_W:COL_C]
        vstack = _head_stack(v, lane_head).astype(BF16)
        sg = jnp.dot(wcat_ref[...], vstack, preferred_element_type=F32) + gbias_ref[...]
        cat_ref[pl.ds(r0, CHUNK), GROUP_W:2 * GROUP_W] = (u * sg).astype(BF16)
        return 0
    lax.fori_loop(0, N_CHUNKS, gmlp_body, 0, unroll=True)

    def pool_mix_body(i, _):
        r0 = pl.multiple_of(i * ROW_TILE, ROW_TILE)
        yc = jnp.dot(tmp2_ref[pl.ds(r0, ROW_TILE), :].astype(BF16), pbd_ref[...],
                     preferred_element_type=F32) * pscale
        cat_ref[pl.ds(r0, ROW_TILE), 2 * GROUP_W:3 * GROUP_W] = yc.astype(BF16)
        return 0
    lax.fori_loop(0, TB // ROW_TILE, pool_mix_body, 0, unroll=True)

    ri = lax.broadcasted_iota(jnp.int32, (CHUNK, HEADS * CHUNK), 0)
    ci = lax.broadcasted_iota(jnp.int32, (CHUNK, HEADS * CHUNK), 1)
    cj = ci % CHUNK
    chead = ci // CHUNK
    rq = lax.broadcasted_iota(jnp.int32, (CHUNK, GROUP_W), 0).astype(F32)
    for d in range(2):
        lgs = [lg_ref[(2 * layer + d) * HEADS + h] for h in range(HEADS)]
        lg_wide = jnp.where(chead == 0, lgs[0], jnp.where(chead == 1, lgs[1],
                                                          jnp.where(chead == 2, lgs[2], lgs[3])))
        lg_lane = jnp.where(lane_head == 0, lgs[0], jnp.where(lane_head == 1, lgs[1],
                                                              jnp.where(lane_head == 2, lgs[2], lgs[3])))
        dist = (ri - cj) if d == 0 else (cj - ri)
        keep = dist >= 0
        dcat_ref[d] = jnp.where(keep, jnp.exp(jnp.where(keep, dist, 0).astype(F32) * lg_wide), 0.0)
        if d == 0:
            qdec_ref[d] = jnp.exp((rq + 1.0) * lg_lane)
            kdec_ref[d] = jnp.exp((CHUNK - 1.0 - rq) * lg_lane)
        else:
            qdec_ref[d] = jnp.exp((CHUNK - rq) * lg_lane)
            kdec_ref[d] = jnp.exp(rq * lg_lane)
        sdec_ref[d] = jnp.exp(float(CHUNK) * lg_lane)

    rr = lax.broadcasted_iota(jnp.int32, (GROUP_W, GROUP_W), 0) // HEAD_D
    cc = lax.broadcasted_iota(jnp.int32, (GROUP_W, GROUP_W), 1) // HEAD_D
    gmat_ref[...] = jnp.where(rr == cc, 1.0 / HEAD_D, 0.0).astype(BF16)

    lane_bit = (lane & 16) == 0
    k_scale = HEAD_D ** -0.5

    def rope(z, r0):
        if not rotate:
            return z
        cos = cos_ref[pl.ds(r0, CHUNK), :]
        sin = sin_ref[pl.ds(r0, CHUNK), :]
        halves = []
        for c0 in (0, 128):
            zz = z[:, c0:c0 + 128]
            partner = jnp.where(lane_bit[:, c0:c0 + 128],
                                pltpu.roll(zz, 128 - 16, axis=1), pltpu.roll(zz, 16, axis=1))
            halves.append(partner)
        return z * cos + jnp.concatenate(halves, axis=1) * sin

    def pair_stack(zb, pair):
        zero = jnp.zeros_like(zb)
        return jnp.concatenate([jnp.where(lane_head == h, zb, zero) for h in (2 * pair, 2 * pair + 1)], axis=0)

    def intra_body(c, _):
        r0 = pl.multiple_of(c * CHUNK, CHUNK)
        v = proj_ref[pl.ds(r0, CHUNK), COL_V:COL_V + GROUP_W]
        vb = v.astype(BF16)
        vstacks = [pair_stack(vb, pair) for pair in range(2)]
        o = None
        for d in range(2):
            qc0 = COL_D + 2 * d * GROUP_W
            q = rope(proj_ref[pl.ds(r0, CHUNK), qc0:qc0 + GROUP_W], r0)
            k = rope(proj_ref[pl.ds(r0, CHUNK), qc0 + GROUP_W:qc0 + 2 * GROUP_W], r0) * k_scale
            qb = q.astype(BF16)
            kb = k.astype(BF16)
            qb_ref[d, pl.ds(r0, CHUNK), :] = qb
            for pair in range(2):
                att = lax.dot_general(qb, pair_stack(kb, pair), (((1,), (1,)), ((), ())),
                                      preferred_element_type=F32)
                att = (att * dcat_ref[d, :, pair * 2 * CHUNK:(pair + 1) * 2 * CHUNK]).astype(BF16)
                od = jnp.dot(att, vstacks[pair], preferred_element_type=F32)
                o = od if o is None else o + od
            kd = (k * kdec_ref[d]).astype(BF16)
            upd = lax.dot_general(kd, vb, (((0,), (0,)), ((), ())), preferred_element_type=F32)
            compact = None
            for h in range(HEADS):
                part = jnp.where(lane_head == h, upd[h * HEAD_D:(h + 1) * HEAD_D, :], 0.0)
                compact = part if compact is None else compact + part
            upd_ref[d, c] = compact
        of_ref[pl.ds(r0, CHUNK), :] = o
        return 0
    lax.fori_loop(0, N_CHUNKS, intra_body, 0, unroll=True)

    for s in range(n_seq):
        for d in range(2):
            st = s0c_ref[d] if rotate else jnp.zeros((HEAD_D, GROUP_W), F32)
            order = range(n_chunk) if d == 0 else range(n_chunk - 1, -1, -1)
            for c in order:
                cg = s * n_chunk + c
                for h in range(HEADS):
                    sall_ref[d, cg, h * HEAD_D:(h + 1) * HEAD_D, :] = (
                        jnp.where(lane_head == h, st, 0.0).astype(BF16))
                st = st * sdec_ref[d] + upd_ref[d, cg]
            if not rotate:
                for h in range(HEADS):
                    if layer == 0:
                        st_ref[s, 0, d, h] = st[:, h * HEAD_D:(h + 1) * HEAD_D]
                    else:
                        st_ref[s, d, h] = st[:, h * HEAD_D:(h + 1) * HEAD_D]
    if not rotate and layer == 0:
        for s in range(n_seq):
            for later in range(1, DEPTH):
                for d in range(2):
                    for h in range(HEADS):
                        st_ref[s, later, d, h] = jnp.zeros((HEAD_D, HEAD_D), F32)

    def cross_body(c, _):
        r0 = pl.multiple_of(c * CHUNK, CHUNK)
        o = of_ref[pl.ds(r0, CHUNK), :]
        for d in range(2):
            o = o + jnp.dot(qb_ref[d, pl.ds(r0, CHUNK), :], sall_ref[d, c],
                            preferred_element_type=F32) * qdec_ref[d]
        of_ref[pl.ds(r0, CHUNK), :] = o
        return 0
    lax.fori_loop(0, N_CHUNKS, cross_body, 0, unroll=True)

    def center_body(i, _):
        r0 = pl.multiple_of(i * ROW_TILE, ROW_TILE)
        o = of_ref[pl.ds(r0, ROW_TILE), :]
        gmat = gmat_ref[...]
        o_hi = o.astype(BF16)
        o_lo = (o - o_hi.astype(F32)).astype(BF16)
        mu = (jnp.dot(o_hi, gmat, preferred_element_type=F32)
              + jnp.dot(o_lo, gmat, preferred_element_type=F32))
        of_ref[pl.ds(r0, ROW_TILE), :] = o - mu
        return 0
    lax.fori_loop(0, TB // ROW_TILE, center_body, 0, unroll=True)

    def gate_body(i, _):
        r0 = pl.multiple_of(i * ROW_TILE, ROW_TILE)
        cen = of_ref[pl.ds(r0, ROW_TILE), :]
        var = jnp.dot((cen * cen).astype(BF16), gmat_ref[...], preferred_element_type=F32)
        on = cen * lax.rsqrt(var + EPS)
        g = proj_ref[pl.ds(r0, ROW_TILE), COL_G:COL_G + GROUP_W]
        cat_ref[pl.ds(r0, ROW_TILE), 3 * GROUP_W:4 * GROUP_W] = (_silu(g) * on).astype(BF16)
        return 0
    lax.fori_loop(0, TB // ROW_TILE, gate_body, 0, unroll=True)

    g2 = g2_ref[layer:layer + 1, :]
    sh2 = mod(3)
    sc2 = mod(4)

    def out_body(i, _):
        r0 = pl.multiple_of(i * ROW_TILE, ROW_TILE)
        y = jnp.dot(cat_ref[pl.ds(r0, ROW_TILE), :], w_out_ref[...], preferred_element_type=F32)
        x_new = x_ref[pl.ds(r0, ROW_TILE), :] + ga1 * y
        y_ref[pl.ds(r0, ROW_TILE), :] = x_new
        h_ref[pl.ds(r0, ROW_TILE), :] = _norm_mod(x_new, g2, sc2, sh2).astype(BF16)
        return 0
    lax.fori_loop(0, TB // ROW_TILE, out_body, 0, unroll=True)


def _layer_spec(shape, layer):
    zeros = (0,) * len(shape)
    return pl.BlockSpec((None,) + tuple(shape), lambda i: (layer,) + zeros, pipeline_mode=pl.Buffered(1))


def _mod_spec(layer):
    return pl.BlockSpec((MOD_ROWS, 6 * D_MODEL), lambda i: (layer, 0), pipeline_mode=pl.Buffered(1))


def _mixer(x, mods, pp, *, layer, seq_len, rotate, mod_base, mod_stride, rope_tabs=None, s0=None, states=None):
    nb = x.shape[0]
    n_seq = TB // seq_len
    in_specs = [
        pl.BlockSpec((None, TB, D_MODEL), lambda i: (i, 0, 0)),
        _mod_spec(layer),
        _const_spec((DEPTH, D_MODEL)),
        _layer_spec((D_MODEL, IN_COLS), layer),
        _layer_spec((D_MODEL, D_MODEL), layer),
        _layer_spec((CONV_W, GROUP_W), layer),
        _const_spec((DEPTH, GROUP_W)), _const_spec((DEPTH, GROUP_W)), _const_spec((DEPTH, GROUP_W)),
        _layer_spec((GROUP_W, GROUP_W), layer),
        _layer_spec((HEADS, CHUNK, CHUNK), layer),
        _layer_spec((CHUNK, GROUP_W), layer),
        _layer_spec((HEADS, HEAD_D, HEAD_D), layer),
        _const_spec((DEPTH, GROUP_W)),
        pl.BlockSpec(memory_space=pltpu.SMEM),
        _const_spec((DEPTH, D_MODEL)),
    ]
    args = [x, mods, pp["g1"], pp["w_in"], pp["w_out"], pp["dw"], pp["cb"], pp["lng"], pp["lnb"],
            pp["pw"], pp["ws"], pp["gbias"], pp["pool_w"], pp["pscale"], pp["lg"], pp["g2"]]
    out_shape = [jax.ShapeDtypeStruct((nb, TB, D_MODEL), F32), jax.ShapeDtypeStruct((nb, TB, D_MODEL), BF16)]
    out_specs = [pl.BlockSpec((None, TB, D_MODEL), lambda i: (i, 0, 0)),
                 pl.BlockSpec((None, TB, D_MODEL), lambda i: (i, 0, 0))]
    aliases = {}
    if rotate:
        in_specs += [_const_spec((TB, GROUP_W)), _const_spec((TB, GROUP_W)),
                     pl.BlockSpec((None, None, 2, HEADS, HEAD_D, HEAD_D), lambda i: (i, layer, 0, 0, 0, 0))]
        args += [rope_tabs[0], rope_tabs[1], s0]
    else:
        out_shape.append(jax.ShapeDtypeStruct((nb * n_seq, DEPTH, 2, HEADS, HEAD_D, HEAD_D), F32))
        if layer == 0:
            out_specs.append(pl.BlockSpec((n_seq, DEPTH, 2, HEADS, HEAD_D, HEAD_D),
                                          lambda i: (i, 0, 0, 0, 0, 0)))
        else:
            in_specs.append(pl.BlockSpec(memory_space=pl.ANY))
            args.append(states)
            aliases = {len(args) - 1: 2}
            out_specs.append(pl.BlockSpec((n_seq, None, 2, HEADS, HEAD_D, HEAD_D),
                                          lambda i: (i, layer, 0, 0, 0, 0)))
    scratch = [
        pltpu.VMEM((TB, D_MODEL), BF16),
        pltpu.VMEM((TB, IN_COLS), F32),
        pltpu.VMEM((n_seq, seq_len + 2 * PAD, GROUP_W), F32),
        pltpu.VMEM((n_seq, seq_len + 2 * PAD, GROUP_W), F32),
        pltpu.VMEM((TB, GROUP_W), F32),
        pltpu.VMEM((TB, GROUP_W), F32),
        pltpu.VMEM((2, CHUNK, HEADS * CHUNK), F32),
        pltpu.VMEM((2, CHUNK, GROUP_W), F32),
        pltpu.VMEM((2, CHUNK, GROUP_W), F32),
        pltpu.VMEM((2, 1, GROUP_W), F32),
        pltpu.VMEM((GROUP_W, GROUP_W), BF16),
        pltpu.VMEM((2, TB, GROUP_W), BF16),
        pltpu.VMEM((2, N_CHUNKS, HEAD_D, GROUP_W), F32),
        pltpu.VMEM((2, N_CHUNKS, GROUP_W, GROUP_W), BF16),
        pltpu.VMEM((GROUP_W, GROUP_W), BF16),
        pltpu.VMEM((CHUNK, HEADS * CHUNK), BF16),
        pltpu.VMEM((GROUP_W, GROUP_W), BF16),
        pltpu.VMEM((2, HEAD_D, GROUP_W), F32),
    ]
    outs = pl.pallas_call(
        functools.partial(_mixer_kernel, layer=layer, seq_len=seq_len, rotate=rotate,
                          mod_base=mod_base, mod_stride=mod_stride),
        grid=(nb,),
        in_specs=in_specs,
        out_specs=out_specs,
        out_shape=out_shape,
        scratch_shapes=scratch,
        input_output_aliases=aliases,
        compiler_params=pltpu.CompilerParams(
            dimension_semantics=("arbitrary",), vmem_limit_bytes=VMEM_LIMIT_BYTES),
        name="mixer_lat" if rotate else "mixer_ctx",
    )(*args)
    return outs


def _ffn_kernel(x_ref, hbf_ref, mod_ref, w_in_hbm, w_out_hbm, gf_ref, y_ref,
                act_ref, wgu_ref, wo_ref, stg_g_ref, stg_u_ref, stg_o_ref, sem,
                *, layer, final_norm, mod_base, mod_stride):
    pid = pl.program_id(0)
    ga2 = _mod_row(mod_ref, mod_base, mod_stride)(5)

    def tile_copies(j, slot):
        return (
            pltpu.make_async_copy(w_in_hbm.at[layer, :, pl.ds(j * FF_TILE, FF_TILE)],
                                  stg_g_ref.at[slot], sem.at[0, slot]),
            pltpu.make_async_copy(w_in_hbm.at[layer, :, pl.ds(D_FF + j * FF_TILE, FF_TILE)],
                                  stg_u_ref.at[slot], sem.at[1, slot]),
            pltpu.make_async_copy(w_out_hbm.at[layer, pl.ds(j * FF_TILE, FF_TILE), :],
                                  stg_o_ref.at[slot], sem.at[2, slot]),
        )

    def start_tile(j):
        for cp in tile_copies(j, j % FF_RING):
            cp.start()

    @pl.when(pid == 0)
    def _():
        for j in range(FF_LOOKAHEAD):
            start_tile(j)

    def hidden_tiles(fetch):
        for j in range(N_FF_TILES):
            gcols = slice(j * FF_TILE, (j + 1) * FF_TILE)
            ucols = slice(D_FF + j * FF_TILE, D_FF + (j + 1) * FF_TILE)
            if fetch:
                if j + FF_LOOKAHEAD < N_FF_TILES:
                    start_tile(j + FF_LOOKAHEAD)
                slot = j % FF_RING
                for cp in tile_copies(j, slot):
                    cp.wait()
                wgu_ref[:, gcols] = stg_g_ref[slot].astype(BF16)
                wgu_ref[:, ucols] = stg_u_ref[slot].astype(BF16)
                wo_ref[gcols, :] = stg_o_ref[slot].astype(BF16)
            gate = jnp.dot(hbf_ref[...], wgu_ref[:, gcols], preferred_element_type=F32)
            up = jnp.dot(hbf_ref[...], wgu_ref[:, ucols], preferred_element_type=F32)
            act_ref[:, gcols] = (_silu(gate) * up).astype(BF16)

    @pl.when(pid == 0)
    def _():
        hidden_tiles(True)

    @pl.when(pid != 0)
    def _():
        hidden_tiles(False)

    def down_body(i, _):
        rows = pl.ds(pl.multiple_of(i * FF_OUT_ROWS, FF_OUT_ROWS), FF_OUT_ROWS)
        y = x_ref[rows, :] + ga2 * jnp.dot(act_ref[rows, :], wo_ref[...], preferred_element_type=F32)
        if final_norm:
            ms = jnp.mean(y * y, axis=-1, keepdims=True)
            y = y * lax.rsqrt(ms + EPS) * gf_ref[...]
        y_ref[rows, :] = y
        return 0
    lax.fori_loop(0, TB // FF_OUT_ROWS, down_body, 0)


def _ffn(x, h, mods, w_ffn_in, w_ffn_out, g_final, *, layer, mod_base, mod_stride, final_norm):
    nb = x.shape[0]
    return pl.pallas_call(
        functools.partial(_ffn_kernel, layer=layer, final_norm=final_norm, mod_base=mod_base,
                          mod_stride=mod_stride),
        grid=(nb,),
        in_specs=[
            pl.BlockSpec((None, TB, D_MODEL), lambda i: (i, 0, 0)),
            pl.BlockSpec((None, TB, D_MODEL), lambda i: (i, 0, 0)),
            _mod_spec(layer),
            pl.BlockSpec(memory_space=pl.ANY),
            pl.BlockSpec(memory_space=pl.ANY),
            _const_spec((1, D_MODEL)),
        ],
        out_specs=pl.BlockSpec((None, TB, D_MODEL), lambda i: (i, 0, 0)),
        out_shape=jax.ShapeDtypeStruct((nb, TB, D_MODEL), F32),
        scratch_shapes=[
            pltpu.VMEM((TB, D_FF), BF16),
            pltpu.VMEM((D_MODEL, 2 * D_FF), BF16),
            pltpu.VMEM((D_FF, D_MODEL), BF16),
            pltpu.VMEM((FF_RING, D_MODEL, FF_TILE), F32),
            pltpu.VMEM((FF_RING, D_MODEL, FF_TILE), F32),
            pltpu.VMEM((FF_RING, FF_TILE, D_MODEL), F32),
            pltpu.SemaphoreType.DMA((3, FF_RING)),
        ],
        compiler_params=pltpu.CompilerParams(
            dimension_semantics=("arbitrary",), vmem_limit_bytes=VMEM_LIMIT_BYTES),
        name="ffn",
    )(x, h, mods, w_ffn_in, w_ffn_out, g_final)


def _rope_tables(seq_len):
    t = np.arange(seq_len)
    r = (t // GRID_W).astype(np.float32)
    c = (t % GRID_W).astype(np.float32)
    nf = HEAD_D // 4
    inv = np.float32(ROPE_BASE) ** (-np.arange(nf, dtype=np.float32) / np.float32(nf))
    ang_r = r[:, None] * inv
    ang_c = c[:, None] * inv
    cos = np.concatenate([np.cos(ang_r), np.cos(ang_r), np.cos(ang_c), np.cos(ang_c)], axis=-1)
    sin = np.concatenate([-np.sin(ang_r), np.sin(ang_r), -np.sin(ang_c), np.sin(ang_c)], axis=-1)
    return (jnp.asarray(np.tile(cos, (1, HEADS)), dtype=F32), jnp.asarray(np.tile(sin, (1, HEADS)), dtype=F32))


def kernel(x_prompt, x_sample, state_ret, c, c_ctx, w_ada, b_ada, g_norm1, g_norm2, w_in, w_out, conv_dw,
           conv_b, conv_ln_g, conv_ln_b, conv_pw, gmlp_ws, gmlp_b, pool_w, pool_scale, ret_decay, w_ffn_in,
           w_ffn_out, g_final):
    batch, seq, _ = x_prompt.shape
    dec_batch, dec_seq, _ = x_sample.shape
    assert dec_seq == TB and TB % seq == 0 and (batch * seq) % TB == 0
    assert 1 + dec_batch <= MOD_ROWS

    cs = jnp.concatenate([c_ctx[None, :], c, jnp.zeros((MOD_ROWS - 1 - dec_batch, D_MODEL), F32)], axis=0)
    mods = _ada_rows(cs, w_ada, b_ada).reshape(DEPTH * MOD_ROWS, 6 * D_MODEL)
    rope_tabs = _rope_tables(dec_seq)
    g_final2 = g_final.reshape(1, D_MODEL)
    pp = {
        "g1": g_norm1,
        "g2": g_norm2,
        "w_in": w_in.astype(BF16),
        "w_out": w_out.astype(BF16),
        "dw": conv_dw,
        "cb": conv_b,
        "lng": conv_ln_g,
        "lnb": conv_ln_b,
        "pw": conv_pw,
        "ws": gmlp_ws,
        "gbias": jnp.repeat(jnp.swapaxes(gmlp_b, 1, 2), HEAD_D, axis=2),
        "pool_w": pool_w,
        "pscale": pool_scale,
        "lg": jax.nn.log_sigmoid(ret_decay.astype(F32)).reshape(DEPTH * 2 * HEADS),
    }

    xc = x_prompt.reshape(batch * seq // TB, TB, D_MODEL)
    xl = x_sample
    states = None
    for l in range(DEPTH):
        last = l == DEPTH - 1
        xc, hc, states = _mixer(xc, mods, pp, layer=l, seq_len=seq, rotate=False, mod_base=0, mod_stride=0,
                                states=states)
        xc = _ffn(xc, hc, mods, w_ffn_in, w_ffn_out, g_final2, layer=l, mod_base=0, mod_stride=0,
                  final_norm=last)
        xl, hl = _mixer(xl, mods, pp, layer=l, seq_len=dec_seq, rotate=True, mod_base=1, mod_stride=1,
                        rope_tabs=rope_tabs, s0=state_ret.astype(F32))
        xl = _ffn(xl, hl, mods, w_ffn_in, w_ffn_out, g_final2, layer=l, mod_base=1, mod_stride=1,
                  final_norm=last)

    y_prompt = xc.reshape(batch, seq, D_MODEL)
    return (y_prompt, xl, states.astype(x_prompt.dtype))
```

```python
import functools

import jax
import jax.numpy as jnp
import numpy as np
from jax import lax
from jax.experimental import pallas as pl
from jax.experimental.pallas import tpu as pltpu

F32 = jnp.float32
BF16 = jnp.bfloat16

D_MODEL = 1024
DEPTH = 2
GRID_W = 64
GROUP_W = D_MODEL // 4
CONV_W = 31
CHUNK = 128
HEADS = 4
HEAD_D = GROUP_W // HEADS
POOL_WINDOWS = (2, 4, 8, 16)
ROPE_BASE = 10000.0
D_FF = 2816
IN_COLS = 11 * GROUP_W
EPS = 1e-6

TB = 1024
N_CHUNKS = TB // CHUNK
ROW_TILE = 256
CONV_TILE = 64
N_CONV_TILES = TB // CONV_TILE
PAD = 16
FF_TILE = 256
N_FF_TILES = D_FF // FF_TILE
FF_OUT_ROWS = 256
FF_LOOKAHEAD = 2
FF_RING = FF_LOOKAHEAD + 1
ADA_TILE = 1536
MOD_ROWS = 8
VMEM_LIMIT_BYTES = 60 * 1024 * 1024

COL_A, COL_B, COL_C, COL_D = 0, 2 * GROUP_W, 4 * GROUP_W, 5 * GROUP_W
COL_V, COL_G = COL_D + 4 * GROUP_W, COL_D + 5 * GROUP_W


def _sigmoid(x):
    return 1.0 / (1.0 + jnp.exp(-x))


def _silu(x):
    return x * _sigmoid(x)


def _norm_mod(x, g, scale, shift):
    ms = jnp.mean(x * x, axis=-1, keepdims=True)
    return (x * lax.rsqrt(ms + EPS) * g) * (1.0 + scale) + shift


def _head_stack(x, lane_head):
    return jnp.concatenate([jnp.where(lane_head == h, x, 0.0) for h in range(HEADS)], axis=0)


def _mod_row(mod_ref, mod_base, mod_stride):
    row = jnp.maximum(mod_base + mod_stride * pl.program_id(0), 0) if mod_stride else max(mod_base, 0)

    def part(k):
        return mod_ref[pl.ds(row, 1), k * D_MODEL:(k + 1) * D_MODEL]
    return part


def _const_spec(shape):
    zeros = (0,) * len(shape)
    return pl.BlockSpec(shape, lambda i: zeros, pipeline_mode=pl.Buffered(1))


def _ada_kernel(c_ref, w_ref, b_ref, o_ref):
    a = _silu(c_ref[...]).astype(BF16)
    o_ref[...] = jnp.dot(a, w_ref[...].astype(BF16), preferred_element_type=F32) + b_ref[...]


def _ada_rows(cs, w_ada, b_ada):
    n_tiles = 6 * D_MODEL // ADA_TILE
    return pl.pallas_call(
        _ada_kernel,
        grid=(DEPTH, n_tiles),
        in_specs=[
            pl.BlockSpec((MOD_ROWS, D_MODEL), lambda l, j: (0, 0)),
            pl.BlockSpec((None, D_MODEL, ADA_TILE), lambda l, j: (l, 0, j)),
            pl.BlockSpec((None, 1, ADA_TILE), lambda l, j: (l, 0, j)),
        ],
        out_specs=pl.BlockSpec((None, MOD_ROWS, ADA_TILE), lambda l, j: (l, 0, j)),
        out_shape=jax.ShapeDtypeStruct((DEPTH, MOD_ROWS, 6 * D_MODEL), F32),
        compiler_params=pltpu.CompilerParams(
            dimension_semantics=("arbitrary", "arbitrary"), vmem_limit_bytes=VMEM_LIMIT_BYTES),
        name="ada_rows",
    )(cs, w_ada, b_ada.reshape(DEPTH, 1, 6 * D_MODEL))


def _mixer_kernel(*refs, layer, seq_len, rotate, mod_base, mod_stride):
    n_seq = TB // seq_len
    n_chunk = seq_len // CHUNK
    it = iter(refs)
    x_ref, mod_ref, g1_ref, w_in_ref, w_out_ref = (next(it) for _ in range(5))
    dw_ref, cb_ref, lng_ref, lnb_ref, pw_ref = (next(it) for _ in range(5))
    ws_ref, gbias_ref, poolw_ref, pscale_ref, lg_ref = (next(it) for _ in range(5))
    if rotate:
        cos_ref, sin_ref, s0_ref = (next(it) for _ in range(3))
    elif layer > 0:
        next(it)
    y_ref = next(it)
    if not rotate:
        st_ref = next(it)
    (hbf_ref, proj_ref, pad_ref, pad2_ref, tmp_ref, tmp2_ref,
     dcat_ref, qdec_ref, kdec_ref, sdec_ref, gmat_ref, qb_ref, upd_ref, sall_ref,
     pwb_ref, wcat_ref, pbd_ref, s0c_ref) = it
    cat_ref = hbf_ref
    of_ref = tmp_ref

    pwb_ref[...] = pw_ref[...].astype(BF16)
    pbd_ref[...] = jnp.zeros((GROUP_W, GROUP_W), BF16)
    for h in range(HEADS):
        wcat_ref[:, h * CHUNK:(h + 1) * CHUNK] = ws_ref[h].astype(BF16)
        pbd_ref[h * HEAD_D:(h + 1) * HEAD_D, h * HEAD_D:(h + 1) * HEAD_D] = poolw_ref[h].astype(BF16)
        if rotate:
            for d in range(2):
                s0c_ref[d, :, h * HEAD_D:(h + 1) * HEAD_D] = s0_ref[d, h]

    mod = _mod_row(mod_ref, mod_base, mod_stride)
    sh1 = mod(0)
    sc1 = mod(1)
    ga1 = mod(2)
    g1, cb, lng, lnb, pscale = (r[layer:layer + 1, :] for r in (g1_ref, cb_ref, lng_ref, lnb_ref, pscale_ref))

    lane = lax.broadcasted_iota(jnp.int32, (1, GROUP_W), 1)
    lane_head = lane // HEAD_D
    tiles_per_seq = seq_len // CONV_TILE
    chunks_per_seq = seq_len // CHUNK

    def seq_and_offset(i, per_seq, size):
        if n_seq == 1:
            return 0, pl.multiple_of(i * size, size)
        return i // per_seq, pl.multiple_of((i % per_seq) * size, size)

    def norm_body(i, _):
        r0 = pl.multiple_of(i * ROW_TILE, ROW_TILE)
        h = _norm_mod(x_ref[pl.ds(r0, ROW_TILE), :], g1, sc1, sh1).astype(BF16)
        hbf_ref[pl.ds(r0, ROW_TILE), :] = h
        proj_ref[pl.ds(r0, ROW_TILE), COL_A:COL_B] = jnp.dot(
            h, w_in_ref[:, COL_A:COL_B], preferred_element_type=F32)
        return 0
    lax.fori_loop(0, TB // ROW_TILE, norm_body, 0, unroll=True)

    for s in range(n_seq):
        for ref in (pad_ref, pad2_ref):
            ref[s, 0:PAD, :] = jnp.zeros((PAD, GROUP_W), F32)
            ref[s, PAD + seq_len:PAD + seq_len + PAD, :] = jnp.zeros((PAD, GROUP_W), F32)

    def glu_body(i, _):
        s, t0 = seq_and_offset(i, chunks_per_seq, CHUNK)
        r0 = pl.multiple_of(i * CHUNK, CHUNK)
        a1 = proj_ref[pl.ds(r0, CHUNK), COL_A:COL_A + GROUP_W]
        a2 = proj_ref[pl.ds(r0, CHUNK), COL_A + GROUP_W:COL_B]
        pad_ref[s, pl.ds(pl.multiple_of(t0 + PAD, 8), CHUNK), :] = a1 * _sigmoid(a2)
        return 0
    lax.fori_loop(0, N_CHUNKS, glu_body, 0, unroll=True)

    win = CONV_TILE + 2 * PAD

    def conv_tile(i):
        s, t0 = divmod(i, tiles_per_seq)
        t0 *= CONV_TILE
        for c0 in range(0, GROUP_W, 128):
            w = pad_ref[s, t0:t0 + win, c0:c0 + 128]
            acc = jnp.zeros((CONV_TILE, 128), F32) + cb[:, c0:c0 + 128]
            for b in range(8):
                wb = w if b == 0 else pltpu.roll(w, win - b, axis=0)
                for a in range(4):
                    k = 8 * a + b - 1
                    if 0 <= k < CONV_W:
                        acc = acc + dw_ref[k:k + 1, c0:c0 + 128] * wb[8 * a:8 * a + CONV_TILE, :]
            tmp_ref[i * CONV_TILE:(i + 1) * CONV_TILE, c0:c0 + 128] = acc

    pwin = CONV_TILE + 16
    lane128 = lax.broadcasted_iota(jnp.int32, (1, 128), 1)
    first_half = lane128 < HEAD_D

    def pool_tile(i):
        s, t0 = divmod(i, tiles_per_seq)
        t0 *= CONV_TILE
        interior = t0 >= 8 and t0 + CONV_TILE + 8 <= seq_len
        outs = []
        for col, (w_small, w_big) in enumerate(((2, 4), (8, 16))):
            w = pad2_ref[s, t0 + PAD - 8:t0 + PAD - 8 + pwin, col * 128:(col + 1) * 128]

            rolled = {0: w}

            def shifted(b, w=w, rolled=rolled):
                if b not in rolled:
                    rolled[b] = pltpu.roll(w, pwin - b, axis=0)
                return rolled[b]

            def lo(b):
                return shifted(b)[0:CONV_TILE, :]

            def hi(b):
                return shifted(b)[8:8 + CONV_TILE, :]

            tok = hi(0)
            if col == 0:
                s_small = lo(7) + tok
                s_big = s_small + lo(6) + hi(1)
            else:
                s_small = lo(7) + tok + lo(6) + hi(1) + lo(5) + lo(4) + hi(2) + hi(3)
                s_big = s_small + lo(3) + lo(2) + lo(1) + lo(0) + hi(4) + hi(5) + hi(6) + hi(7)
            ssum = jnp.where(first_half, s_small, s_big)
            if interior:
                mean = ssum * jnp.where(first_half, 1.0 / w_small, 1.0 / w_big)
            else:
                half = jnp.where(first_half, w_small // 2, w_big // 2)
                tpos = t0 + lax.broadcasted_iota(jnp.int32, (CONV_TILE, 128), 0)
                cnt = jnp.minimum(tpos + half, seq_len) - jnp.maximum(tpos - half, 0)
                mean = ssum / cnt.astype(F32)
            outs.append(mean - tok)
        tmp2_ref[i * CONV_TILE:(i + 1) * CONV_TILE, :] = jnp.concatenate(outs, axis=1)

    col_tiles = [COL_C] + list(range(COL_B, COL_C, GROUP_W)) + list(range(COL_D, IN_COLS, GROUP_W))
    per_step = -(-N_CONV_TILES // (len(col_tiles) - 1))
    conv_next = 0
    pool_next = 0
    for step, c0 in enumerate(col_tiles):
        proj_ref[:, c0:c0 + GROUP_W] = jnp.dot(
            hbf_ref[...], w_in_ref[:, c0:c0 + GROUP_W], preferred_element_type=F32)
        if step == 1:
            for s in range(n_seq):
                pad2_ref[s, PAD:PAD + seq_len, :] = proj_ref[s * seq_len:(s + 1) * seq_len, COL_C:COL_D]
        for _ in range(per_step):
            if conv_next < N_CONV_TILES:
                conv_tile(conv_next)
                conv_next += 1
            if step >= 1 and pool_next < N_CONV_TILES:
                pool_tile(pool_next)
                pool_next += 1
    assert conv_next == N_CONV_TILES and pool_next == N_CONV_TILES

    def ln_pw_body(i, _):
        r0 = pl.multiple_of(i * ROW_TILE, ROW_TILE)
        c = tmp_ref[pl.ds(r0, ROW_TILE), :]
        mu = jnp.mean(c, axis=-1, keepdims=True)
        cen = c - mu
        var = jnp.mean(cen * cen, axis=-1, keepdims=True)
        hn = cen * lax.rsqrt(var + EPS) * lng + lnb
        ya = jnp.dot(_silu(hn).astype(BF16), pwb_ref[...], preferred_element_type=F32)
        cat_ref[pl.ds(r0, ROW_TILE), 0:GROUP_W] = ya.astype(BF16)
        return 0
    lax.fori_loop(0, TB // ROW_TILE, ln_pw_body, 0, unroll=True)

    def gmlp_body(i, _):
        r0 = pl.multiple_of(i * CHUNK, CHUNK)
        u = proj_ref[pl.ds(r0, CHUNK), COL_B:COL_B + GROUP_W]
        v = proj_ref[pl.ds(r0, CHUNK), COL_B + GROUP_W:COL_C]
        vstack = _head_stack(v, lane_head).astype(BF16)
        sg = jnp.dot(wcat_ref[...], vstack, preferred_element_type=F32) + gbias_ref[...]
        cat_ref[pl.ds(r0, CHUNK), GROUP_W:2 * GROUP_W] = (u * sg).astype(BF16)
        return 0
    lax.fori_loop(0, N_CHUNKS, gmlp_body, 0, unroll=True)

    def pool_mix_body(i, _):
        r0 = pl.multiple_of(i * ROW_TILE, ROW_TILE)
        yc = jnp.dot(tmp2_ref[pl.ds(r0, ROW_TILE), :].astype(BF16), pbd_ref[...],
                     preferred_element_type=F32) * pscale
        cat_ref[pl.ds(r0, ROW_TILE), 2 * GROUP_W:3 * GROUP_W] = yc.astype(BF16)
        return 0
    lax.fori_loop(0, TB // ROW_TILE, pool_mix_body, 0, unroll=True)

    ri = lax.broadcasted_iota(jnp.int32, (CHUNK, HEADS * CHUNK), 0)
    ci = lax.broadcasted_iota(jnp.int32, (CHUNK, HEADS * CHUNK), 1)
    cj = ci % CHUNK
    chead = ci // CHUNK
    rq = lax.broadcasted_iota(jnp.int32, (CHUNK, GROUP_W), 0).astype(F32)
    for d in range(2):
        lgs = [lg_ref[(2 * layer + d) * HEADS + h] for h in range(HEADS)]
        lg_wide = jnp.where(chead == 0, lgs[0], jnp.where(chead == 1, lgs[1],
                                                          jnp.where(chead == 2, lgs[2], lgs[3])))
        lg_lane = jnp.where(lane_head == 0, lgs[0], jnp.where(lane_head == 1, lgs[1],
                                                              jnp.where(lane_head == 2, lgs[2], lgs[3])))
        dist = (ri - cj) if d == 0 else (cj - ri)
        keep = dist >= 0
        dcat_ref[d] = jnp.where(keep, jnp.exp(jnp.where(keep, dist, 0).astype(F32) * lg_wide), 0.0)
        if d == 0:
            qdec_ref[d] = jnp.exp((rq + 1.0) * lg_lane)
            kdec_ref[d] = jnp.exp((CHUNK - 1.0 - rq) * lg_lane)
        else:
            qdec_ref[d] = jnp.exp((CHUNK - rq) * lg_lane)
            kdec_ref[d] = jnp.exp(rq * lg_lane)
        sdec_ref[d] = jnp.exp(float(CHUNK) * lg_lane)

    rr = lax.broadcasted_iota(jnp.int32, (GROUP_W, GROUP_W), 0) // HEAD_D
    cc = lax.broadcasted_iota(jnp.int32, (GROUP_W, GROUP_W), 1) // HEAD_D
    gmat_ref[...] = jnp.where(rr == cc, 1.0 / HEAD_D, 0.0).astype(BF16)

    lane_bit = (lane & 16) == 0
    k_scale = HEAD_D ** -0.5

    def rope(z, r0):
        if not rotate:
            return z
        cos = cos_ref[pl.ds(r0, CHUNK), :]
        sin = sin_ref[pl.ds(r0, CHUNK), :]
        halves = []
        for c0 in (0, 128):
            zz = z[:, c0:c0 + 128]
            partner = jnp.where(lane_bit[:, c0:c0 + 128],
                                pltpu.roll(zz, 128 - 16, axis=1), pltpu.roll(zz, 16, axis=1))
            halves.append(partner)
        return z * cos + jnp.concatenate(halves, axis=1) * sin

    def pair_stack(zb, pair):
        zero = jnp.zeros_like(zb)
        return jnp.concatenate([jnp.where(lane_head == h, zb, zero) for h in (2 * pair, 2 * pair + 1)], axis=0)

    def intra_body(c, _):
        r0 = pl.multiple_of(c * CHUNK, CHUNK)
        v = proj_ref[pl.ds(r0, CHUNK), COL_V:COL_V + GROUP_W]
        vb = v.astype(BF16)
        vstacks = [pair_stack(vb, pair) for pair in range(2)]
        o = None
        for d in range(2):
            qc0 = COL_D + 2 * d * GROUP_W
            q = rope(proj_ref[pl.ds(r0, CHUNK), qc0:qc0 + GROUP_W], r0)
            k = rope(proj_ref[pl.ds(r0, CHUNK), qc0 + GROUP_W:qc0 + 2 * GROUP_W], r0) * k_scale
            qb = q.astype(BF16)
            kb = k.astype(BF16)
            qb_ref[d, pl.ds(r0, CHUNK), :] = qb
            for pair in range(2):
                att = lax.dot_general(qb, pair_stack(kb, pair), (((1,), (1,)), ((), ())),
                                      preferred_element_type=F32)
                att = (att * dcat_ref[d, :, pair * 2 * CHUNK:(pair + 1) * 2 * CHUNK]).astype(BF16)
                od = jnp.dot(att, vstacks[pair], preferred_element_type=F32)
                o = od if o is None else o + od
            kd = (k * kdec_ref[d]).astype(BF16)
            upd = lax.dot_general(kd, vb, (((0,), (0,)), ((), ())), preferred_element_type=F32)
            compact = None
            for h in range(HEADS):
                part = jnp.where(lane_head == h, upd[h * HEAD_D:(h + 1) * HEAD_D, :], 0.0)
                compact = part if compact is None else compact + part
            upd_ref[d, c] = compact
        of_ref[pl.ds(r0, CHUNK), :] = o
        return 0
    lax.fori_loop(0, N_CHUNKS, intra_body, 0, unroll=True)

    for s in range(n_seq):
        for d in range(2):
            st = s0c_ref[d] if rotate else jnp.zeros((HEAD_D, GROUP_W), F32)
            order = range(n_chunk) if d == 0 else range(n_chunk - 1, -1, -1)
            for c in order:
                cg = s * n_chunk + c
                for h in range(HEADS):
                    sall_ref[d, cg, h * HEAD_D:(h + 1) * HEAD_D, :] = (
                        jnp.where(lane_head == h, st, 0.0).astype(BF16))
                st = st * sdec_ref[d] + upd_ref[d, cg]
            if not rotate:
                for h in range(HEADS):
                    if layer == 0:
                        st_ref[s, 0, d, h] = st[:, h * HEAD_D:(h + 1) * HEAD_D]
                    else:
                        st_ref[s, d, h] = st[:, h * HEAD_D:(h + 1) * HEAD_D]
    if not rotate and layer == 0:
        for s in range(n_seq):
            for later in range(1, DEPTH):
                for d in range(2):
                    for h in range(HEADS):
                        st_ref[s, later, d, h] = jnp.zeros((HEAD_D, HEAD_D), F32)

    def cross_body(c, _):
        r0 = pl.multiple_of(c * CHUNK, CHUNK)
        o = of_ref[pl.ds(r0, CHUNK), :]
        for d in range(2):
            o = o + jnp.dot(qb_ref[d, pl.ds(r0, CHUNK), :], sall_ref[d, c],
                            preferred_element_type=F32) * qdec_ref[d]
        of_ref[pl.ds(r0, CHUNK), :] = o
        return 0
    lax.fori_loop(0, N_CHUNKS, cross_body, 0, unroll=True)

    def center_body(i, _):
        r0 = pl.multiple_of(i * ROW_TILE, ROW_TILE)
        o = of_ref[pl.ds(r0, ROW_TILE), :]
        gmat = gmat_ref[...]
        o_hi = o.astype(BF16)
        o_lo = (o - o_hi.astype(F32)).astype(BF16)
        mu = (jnp.dot(o_hi, gmat, preferred_element_type=F32)
              + jnp.dot(o_lo, gmat, preferred_element_type=F32))
        of_ref[pl.ds(r0, ROW_TILE), :] = o - mu
        return 0
    lax.fori_loop(0, TB // ROW_TILE, center_body, 0, unroll=True)

    def gate_body(i, _):
        r0 = pl.multiple_of(i * ROW_TILE, ROW_TILE)
        cen = of_ref[pl.ds(r0, ROW_TILE), :]
        var = jnp.dot((cen * cen).astype(BF16), gmat_ref[...], preferred_element_type=F32)
        on = cen * lax.rsqrt(var + EPS)
        g = proj_ref[pl.ds(r0, ROW_TILE), COL_G:COL_G + GROUP_W]
        cat_ref[pl.ds(r0, ROW_TILE), 3 * GROUP_W:4 * GROUP_W] = (_silu(g) * on).astype(BF16)
        return 0
    lax.fori_loop(0, TB // ROW_TILE, gate_body, 0, unroll=True)

    def out_body(i, _):
        r0 = pl.multiple_of(i * ROW_TILE, ROW_TILE)
        y = jnp.dot(cat_ref[pl.ds(r0, ROW_TILE), :], w_out_ref[...], preferred_element_type=F32)
        y_ref[pl.ds(r0, ROW_TILE), :] = x_ref[pl.ds(r0, ROW_TILE), :] + ga1 * y
        return 0
    lax.fori_loop(0, TB // ROW_TILE, out_body, 0, unroll=True)


def _layer_spec(shape, layer):
    zeros = (0,) * len(shape)
    return pl.BlockSpec((None,) + tuple(shape), lambda i: (layer,) + zeros, pipeline_mode=pl.Buffered(1))


def _mod_spec(layer):
    return pl.BlockSpec((MOD_ROWS, 6 * D_MODEL), lambda i: (layer, 0), pipeline_mode=pl.Buffered(1))


def _mixer(x, mods, pp, *, layer, seq_len, rotate, mod_base, mod_stride, rope_tabs=None, s0=None, states=None):
    nb = x.shape[0]
    n_seq = TB // seq_len
    in_specs = [
        pl.BlockSpec((None, TB, D_MODEL), lambda i: (i, 0, 0)),
        _mod_spec(layer),
        _const_spec((DEPTH, D_MODEL)),
        _layer_spec((D_MODEL, IN_COLS), layer),
        _layer_spec((D_MODEL, D_MODEL), layer),
        _layer_spec((CONV_W, GROUP_W), layer),
        _const_spec((DEPTH, GROUP_W)), _const_spec((DEPTH, GROUP_W)), _const_spec((DEPTH, GROUP_W)),
        _layer_spec((GROUP_W, GROUP_W), layer),
        _layer_spec((HEADS, CHUNK, CHUNK), layer),
        _layer_spec((CHUNK, GROUP_W), layer),
        _layer_spec((HEADS, HEAD_D, HEAD_D), layer),
        _const_spec((DEPTH, GROUP_W)),
        pl.BlockSpec(memory_space=pltpu.SMEM),
    ]
    args = [x, mods, pp["g1"], pp["w_in"], pp["w_out"], pp["dw"], pp["cb"], pp["lng"], pp["lnb"],
            pp["pw"], pp["ws"], pp["gbias"], pp["pool_w"], pp["pscale"], pp["lg"]]
    out_shape = [jax.ShapeDtypeStruct((nb, TB, D_MODEL), F32)]
    out_specs = [pl.BlockSpec((None, TB, D_MODEL), lambda i: (i, 0, 0))]
    aliases = {}
    if rotate:
        in_specs += [_const_spec((TB, GROUP_W)), _const_spec((TB, GROUP_W)),
                     pl.BlockSpec((None, None, 2, HEADS, HEAD_D, HEAD_D), lambda i: (i, layer, 0, 0, 0, 0))]
        args += [rope_tabs[0], rope_tabs[1], s0]
    else:
        out_shape.append(jax.ShapeDtypeStruct((nb * n_seq, DEPTH, 2, HEADS, HEAD_D, HEAD_D), F32))
        if layer == 0:
            out_specs.append(pl.BlockSpec((n_seq, DEPTH, 2, HEADS, HEAD_D, HEAD_D),
                                          lambda i: (i, 0, 0, 0, 0, 0)))
        else:
            in_specs.append(pl.BlockSpec(memory_space=pl.ANY))
            args.append(states)
            aliases = {len(args) - 1: 1}
            out_specs.append(pl.BlockSpec((n_seq, None, 2, HEADS, HEAD_D, HEAD_D),
                                          lambda i: (i, layer, 0, 0, 0, 0)))
    scratch = [
        pltpu.VMEM((TB, D_MODEL), BF16),
        pltpu.VMEM((TB, IN_COLS), F32),
        pltpu.VMEM((n_seq, seq_len + 2 * PAD, GROUP_W), F32),
        pltpu.VMEM((n_seq, seq_len + 2 * PAD, GROUP_W), F32),
        pltpu.VMEM((TB, GROUP_W), F32),
        pltpu.VMEM((TB, GROUP_W), F32),
        pltpu.VMEM((2, CHUNK, HEADS * CHUNK), F32),
        pltpu.VMEM((2, CHUNK, GROUP_W), F32),
        pltpu.VMEM((2, CHUNK, GROUP_W), F32),
        pltpu.VMEM((2, 1, GROUP_W), F32),
        pltpu.VMEM((GROUP_W, GROUP_W), BF16),
        pltpu.VMEM((2, TB, GROUP_W), BF16),
        pltpu.VMEM((2, N_CHUNKS, HEAD_D, GROUP_W), F32),
        pltpu.VMEM((2, N_CHUNKS, GROUP_W, GROUP_W), BF16),
        pltpu.VMEM((GROUP_W, GROUP_W), BF16),
        pltpu.VMEM((CHUNK, HEADS * CHUNK), BF16),
        pltpu.VMEM((GROUP_W, GROUP_W), BF16),
        pltpu.VMEM((2, HEAD_D, GROUP_W), F32),
    ]
    outs = pl.pallas_call(
        functools.partial(_mixer_kernel, layer=layer, seq_len=seq_len, rotate=rotate,
                          mod_base=mod_base, mod_stride=mod_stride),
        grid=(nb,),
        in_specs=in_specs,
        out_specs=out_specs,
        out_shape=out_shape,
        scratch_shapes=scratch,
        input_output_aliases=aliases,
        compiler_params=pltpu.CompilerParams(
            dimension_semantics=("arbitrary",), vmem_limit_bytes=VMEM_LIMIT_BYTES),
        name="mixer_lat" if rotate else "mixer_ctx",
    )(*args)
    return outs


def _ffn_kernel(x_ref, mod_ref, g2_ref, w_in_hbm, w_out_hbm, gf_ref, y_ref,
                hbf_ref, act_ref, act3_ref, wgu_ref, wo_ref, stg_g_ref, stg_u_ref, stg_o_ref, sem,
                *, layer, final_norm, mod_base, mod_stride):
    pid = pl.program_id(0)
    mod = _mod_row(mod_ref, mod_base, mod_stride)
    sh2 = mod(3)
    sc2 = mod(4)
    ga2 = mod(5)
    g2 = g2_ref[layer:layer + 1, :]

    def tile_copies(j, slot):
        return (
            pltpu.make_async_copy(w_in_hbm.at[layer, :, pl.ds(j * FF_TILE, FF_TILE)],
                                  stg_g_ref.at[slot], sem.at[0, slot]),
            pltpu.make_async_copy(w_in_hbm.at[layer, :, pl.ds(D_FF + j * FF_TILE, FF_TILE)],
                                  stg_u_ref.at[slot], sem.at[1, slot]),
            pltpu.make_async_copy(w_out_hbm.at[layer, pl.ds(j * FF_TILE, FF_TILE), :],
                                  stg_o_ref.at[slot], sem.at[2, slot]),
        )

    def start_tile(j):
        for cp in tile_copies(j, j % FF_RING):
            cp.start()

    def hidden_tile(j, act_out):
        gate = jnp.dot(hbf_ref[...], wgu_ref[j], preferred_element_type=F32)
        up = jnp.dot(hbf_ref[...], wgu_ref[N_FF_TILES + j], preferred_element_type=F32)
        act_out((_silu(gate) * up).astype(BF16))

    @pl.when(pid == 0)
    def _():
        for j in range(FF_LOOKAHEAD):
            start_tile(j)

    def norm_body(i, _):
        r0 = pl.multiple_of(i * ROW_TILE, ROW_TILE)
        h = _norm_mod(x_ref[pl.ds(r0, ROW_TILE), :], g2, sc2, sh2)
        hbf_ref[pl.ds(r0, ROW_TILE), :] = h.astype(BF16)
        return 0
    lax.fori_loop(0, TB // ROW_TILE, norm_body, 0)

    @pl.when(pid == 0)
    def _():
        def fetch_body(j, _):
            @pl.when(j + FF_LOOKAHEAD < N_FF_TILES)
            def _():
                start_tile(j + FF_LOOKAHEAD)
            slot = j % FF_RING
            for cp in tile_copies(j, slot):
                cp.wait()
            wgu_ref[j] = stg_g_ref[slot].astype(BF16)
            wgu_ref[N_FF_TILES + j] = stg_u_ref[slot].astype(BF16)
            wo_ref[pl.ds(pl.multiple_of(j * FF_TILE, FF_TILE), FF_TILE), :] = stg_o_ref[slot].astype(BF16)

            def act_out(act):
                act3_ref[j] = act
            hidden_tile(j, act_out)
            return 0
        lax.fori_loop(0, N_FF_TILES, fetch_body, 0)
        for j in range(N_FF_TILES):
            act_ref[:, j * FF_TILE:(j + 1) * FF_TILE] = act3_ref[j]

    @pl.when(pid != 0)
    def _():
        for j in range(N_FF_TILES):
            def act_out(act, j=j):
                act_ref[:, j * FF_TILE:(j + 1) * FF_TILE] = act
            hidden_tile(j, act_out)

    for r0 in range(0, TB, FF_OUT_ROWS):
        rows = slice(r0, r0 + FF_OUT_ROWS)
        y = x_ref[rows, :] + ga2 * jnp.dot(act_ref[rows, :], wo_ref[...], preferred_element_type=F32)
        if final_norm:
            ms = jnp.mean(y * y, axis=-1, keepdims=True)
            y = y * lax.rsqrt(ms + EPS) * gf_ref[...]
        y_ref[rows, :] = y


def _ffn(x, mods, g2, w_ffn_in, w_ffn_out, g_final, *, layer, mod_base, mod_stride, final_norm):
    nb = x.shape[0]
    return pl.pallas_call(
        functools.partial(_ffn_kernel, layer=layer, final_norm=final_norm, mod_base=mod_base,
                          mod_stride=mod_stride),
        grid=(nb,),
        in_specs=[
            pl.BlockSpec((None, TB, D_MODEL), lambda i: (i, 0, 0)),
            _mod_spec(layer),
            _const_spec((DEPTH, D_MODEL)),
            pl.BlockSpec(memory_space=pl.ANY),
            pl.BlockSpec(memory_space=pl.ANY),
            _const_spec((1, D_MODEL)),
        ],
        out_specs=pl.BlockSpec((None, TB, D_MODEL), lambda i: (i, 0, 0)),
        out_shape=jax.ShapeDtypeStruct((nb, TB, D_MODEL), F32),
        scratch_shapes=[
            pltpu.VMEM((TB, D_MODEL), BF16),
            pltpu.VMEM((TB, D_FF), BF16),
            pltpu.VMEM((N_FF_TILES, TB, FF_TILE), BF16),
            pltpu.VMEM((2 * N_FF_TILES, D_MODEL, FF_TILE), BF16),
            pltpu.VMEM((D_FF, D_MODEL), BF16),
            pltpu.VMEM((FF_RING, D_MODEL, FF_TILE), F32),
            pltpu.VMEM((FF_RING, D_MODEL, FF_TILE), F32),
            pltpu.VMEM((FF_RING, FF_TILE, D_MODEL), F32),
            pltpu.SemaphoreType.DMA((3, FF_RING)),
        ],
        compiler_params=pltpu.CompilerParams(
            dimension_semantics=("arbitrary",), vmem_limit_bytes=VMEM_LIMIT_BYTES),
        name="ffn",
    )(x, mods, g2, w_ffn_in, w_ffn_out, g_final)


def _rope_tables(seq_len):
    t = np.arange(seq_len)
    r = (t // GRID_W).astype(np.float32)
    c = (t % GRID_W).astype(np.float32)
    nf = HEAD_D // 4
    inv = np.float32(ROPE_BASE) ** (-np.arange(nf, dtype=np.float32) / np.float32(nf))
    ang_r = r[:, None] * inv
    ang_c = c[:, None] * inv
    cos = np.concatenate([np.cos(ang_r), np.cos(ang_r), np.cos(ang_c), np.cos(ang_c)], axis=-1)
    sin = np.concatenate([-np.sin(ang_r), np.sin(ang_r), -np.sin(ang_c), np.sin(ang_c)], axis=-1)
    return (jnp.asarray(np.tile(cos, (1, HEADS)), dtype=F32), jnp.asarray(np.tile(sin, (1, HEADS)), dtype=F32))


def kernel(x_prompt, x_sample, state_ret, c, c_ctx, w_ada, b_ada, g_norm1, g_norm2, w_in, w_out, conv_dw,
           conv_b, conv_ln_g, conv_ln_b, conv_pw, gmlp_ws, gmlp_b, pool_w, pool_scale, ret_decay, w_ffn_in,
           w_ffn_out, g_final):
    batch, seq, _ = x_prompt.shape
    dec_batch, dec_seq, _ = x_sample.shape
    assert dec_seq == TB and TB % seq == 0 and (batch * seq) % TB == 0
    assert 1 + dec_batch <= MOD_ROWS

    cs = jnp.concatenate([c_ctx[None, :], c, jnp.zeros((MOD_ROWS - 1 - dec_batch, D_MODEL), F32)], axis=0)
    mods = _ada_rows(cs, w_ada, b_ada).reshape(DEPTH * MOD_ROWS, 6 * D_MODEL)
    rope_tabs = _rope_tables(dec_seq)
    g_final2 = g_final.reshape(1, D_MODEL)
    pp = {
        "g1": g_norm1,
        "g2": g_norm2,
        "w_in": w_in.astype(BF16),
        "w_out": w_out.astype(BF16),
        "dw": conv_dw,
        "cb": conv_b,
        "lng": conv_ln_g,
        "lnb": conv_ln_b,
        "pw": conv_pw,
        "ws": gmlp_ws,
        "gbias": jnp.repeat(jnp.swapaxes(gmlp_b, 1, 2), HEAD_D, axis=2),
        "pool_w": pool_w,
        "pscale": pool_scale,
        "lg": jax.nn.log_sigmoid(ret_decay.astype(F32)).reshape(DEPTH * 2 * HEADS),
    }

    xc = x_prompt.reshape(batch * seq // TB, TB, D_MODEL)
    xl = x_sample
    states = None
    for l in range(DEPTH):
        last = l == DEPTH - 1
        xc, states = _mixer(xc, mods, pp, layer=l, seq_len=seq, rotate=False, mod_base=0, mod_stride=0,
                            states=states)
        xc = _ffn(xc, mods, pp["g2"], w_ffn_in, w_ffn_out, g_final2, layer=l, mod_base=0, mod_stride=0,
                  final_norm=last)
        (xl,) = _mixer(xl, mods, pp, layer=l, seq_len=dec_seq, rotate=True, mod_base=1, mod_stride=1,
                       rope_tabs=rope_tabs, s0=state_ret.astype(F32))
        xl = _ffn(xl, mods, pp["g2"], w_ffn_in, w_ffn_out, g_final2, layer=l, mod_base=1, mod_stride=1,
                  final_norm=last)

    y_prompt = xc.reshape(batch, seq, D_MODEL)
    return (y_prompt, xl, states.astype(x_prompt.dtype))
```

```python
import functools

import jax
import jax.numpy as jnp
import numpy as np
from jax import lax
from jax.experimental import pallas as pl
from jax.experimental.pallas import tpu as pltpu

F32 = jnp.float32
BF16 = jnp.bfloat16

D_MODEL = 1024
DEPTH = 2
GRID_W = 64
GROUP_W = D_MODEL // 4
CONV_W = 31
CHUNK = 128
HEADS = 4
HEAD_D = GROUP_W // HEADS
POOL_WINDOWS = (2, 4, 8, 16)
POOL_GW = GROUP_W // len(POOL_WINDOWS)
ROPE_BASE = 10000.0
D_FF = 2816
IN_COLS = 11 * GROUP_W
EPS = 1e-6

LANES = 128
SUBLANES = 8

TB = 1024
N_CHUNKS = TB // CHUNK
ROW_TILE = 256
CONV_TILE = 64
N_CONV_TILES = TB // CONV_TILE
PAD = 16
POOL_HALO = max(POOL_WINDOWS) // 2
ROPE_PAIR = HEAD_D // 4
FF_TILE = 256
N_FF_TILES = D_FF // FF_TILE
FF_OUT_ROWS = 256
FF_LOOKAHEAD = 2
FF_RING = FF_LOOKAHEAD + 1
ADA_TILE = 3072
MOD_ROWS = 8
VMEM_LIMIT_BYTES = 60 * 1024 * 1024

COL_A, COL_B, COL_C, COL_D = 0, 2 * GROUP_W, 4 * GROUP_W, 5 * GROUP_W
COL_V, COL_G = COL_D + 4 * GROUP_W, COL_D + 5 * GROUP_W


def _sigmoid(x):
    return 1.0 / (1.0 + jnp.exp(-x))


def _silu(x):
    return x * _sigmoid(x)


def _norm_mod(x, g, scale, shift):
    ms = jnp.mean(x * x, axis=-1, keepdims=True)
    return (x * lax.rsqrt(ms + EPS) * g) * (1.0 + scale) + shift


def _head_stack(x, lane_head):
    return jnp.concatenate([jnp.where(lane_head == h, x, 0.0) for h in range(HEADS)], axis=0)


def _mod_row(mod_ref, mod_base, mod_stride):
    row = jnp.maximum(mod_base + mod_stride * pl.program_id(0), 0) if mod_stride else max(mod_base, 0)

    def part(k):
        return mod_ref[pl.ds(row, 1), k * D_MODEL:(k + 1) * D_MODEL]
    return part


def _const_spec(shape):
    zeros = (0,) * len(shape)
    return pl.BlockSpec(shape, lambda i: zeros, pipeline_mode=pl.Buffered(1))


def _ada_kernel(c_ref, w_ref, b_ref, o_ref):
    a = _silu(c_ref[...]).astype(BF16)
    bias = b_ref[pl.ds(pl.program_id(0), 1), :]
    o_ref[...] = jnp.dot(a, w_ref[...].astype(BF16), preferred_element_type=F32) + bias


def _ada_rows(cs, w_ada, b_ada):
    n_tiles = 6 * D_MODEL // ADA_TILE
    return pl.pallas_call(
        _ada_kernel,
        grid=(DEPTH, n_tiles),
        in_specs=[
            pl.BlockSpec((MOD_ROWS, D_MODEL), lambda l, j: (0, 0)),
            pl.BlockSpec((None, D_MODEL, ADA_TILE), lambda l, j: (l, 0, j)),
            pl.BlockSpec((DEPTH, ADA_TILE), lambda l, j: (0, j)),
        ],
        out_specs=pl.BlockSpec((None, MOD_ROWS, ADA_TILE), lambda l, j: (l, 0, j)),
        out_shape=jax.ShapeDtypeStruct((DEPTH, MOD_ROWS, 6 * D_MODEL), F32),
        compiler_params=pltpu.CompilerParams(
            dimension_semantics=("arbitrary", "arbitrary"), vmem_limit_bytes=VMEM_LIMIT_BYTES),
        name="ada_rows",
    )(cs, w_ada, b_ada)


def _mixer_kernel(*refs, layer, seq_len, rotate, mod_base, mod_stride):
    n_seq = TB // seq_len
    n_chunk = seq_len // CHUNK
    it = iter(refs)
    x_ref, mod_ref, g1_ref, w_in_ref, w_out_ref = (next(it) for _ in range(5))
    dw_ref, cb_ref, lng_ref, lnb_ref, pw_ref = (next(it) for _ in range(5))
    ws_ref, gbias_ref, poolw_ref, pscale_ref, lg_ref = (next(it) for _ in range(5))
    if rotate:
        cos_ref, sin_ref, s0_ref = (next(it) for _ in range(3))
    elif layer > 0:
        next(it)
    y_ref = next(it)
    if not rotate:
        st_ref = next(it)
    (hbf_ref, proj_ref, pad_ref, pad2_ref, tmp_ref, tmp2_ref,
     dcat_ref, qdec_ref, kdec_ref, sdec_ref, gmat_ref, qb_ref, upd_ref, sall_ref,
     pwb_ref, wcat_ref, pbd_ref, s0c_ref) = it
    cat_ref = hbf_ref
    of_ref = tmp_ref

    pwb_ref[...] = pw_ref[...].astype(BF16)
    pbd_ref[...] = jnp.zeros((GROUP_W, GROUP_W), BF16)
    for h in range(HEADS):
        wcat_ref[:, h * CHUNK:(h + 1) * CHUNK] = ws_ref[h].astype(BF16)
        pbd_ref[h * HEAD_D:(h + 1) * HEAD_D, h * HEAD_D:(h + 1) * HEAD_D] = poolw_ref[h].astype(BF16)
        if rotate:
            for d in range(2):
                s0c_ref[d, :, h * HEAD_D:(h + 1) * HEAD_D] = s0_ref[d, h]

    mod = _mod_row(mod_ref, mod_base, mod_stride)
    sh1 = mod(0)
    sc1 = mod(1)
    ga1 = mod(2)
    g1, cb, lng, lnb, pscale = (r[layer:layer + 1, :] for r in (g1_ref, cb_ref, lng_ref, lnb_ref, pscale_ref))

    lane = lax.broadcasted_iota(jnp.int32, (1, GROUP_W), 1)
    lane_head = lane // HEAD_D
    tiles_per_seq = seq_len // CONV_TILE
    chunks_per_seq = seq_len // CHUNK

    def seq_and_offset(i, per_seq, size):
        if n_seq == 1:
            return 0, pl.multiple_of(i * size, size)
        return i // per_seq, pl.multiple_of((i % per_seq) * size, size)

    def norm_body(i, _):
        r0 = pl.multiple_of(i * ROW_TILE, ROW_TILE)
        h = _norm_mod(x_ref[pl.ds(r0, ROW_TILE), :], g1, sc1, sh1).astype(BF16)
        hbf_ref[pl.ds(r0, ROW_TILE), :] = h
        proj_ref[pl.ds(r0, ROW_TILE), COL_A:COL_B] = jnp.dot(
            h, w_in_ref[:, COL_A:COL_B], preferred_element_type=F32)
        return 0
    lax.fori_loop(0, TB // ROW_TILE, norm_body, 0, unroll=True)

    for s in range(n_seq):
        for ref in (pad_ref, pad2_ref):
            ref[s, 0:PAD, :] = jnp.zeros((PAD, GROUP_W), F32)
            ref[s, PAD + seq_len:PAD + seq_len + PAD, :] = jnp.zeros((PAD, GROUP_W), F32)

    def glu_body(i, _):
        s, t0 = seq_and_offset(i, chunks_per_seq, CHUNK)
        r0 = pl.multiple_of(i * CHUNK, CHUNK)
        a1 = proj_ref[pl.ds(r0, CHUNK), COL_A:COL_A + GROUP_W]
        a2 = proj_ref[pl.ds(r0, CHUNK), COL_A + GROUP_W:COL_B]
        pad_ref[s, pl.ds(pl.multiple_of(t0 + PAD, SUBLANES), CHUNK), :] = a1 * _sigmoid(a2)
        return 0
    lax.fori_loop(0, N_CHUNKS, glu_body, 0, unroll=True)

    win = CONV_TILE + 2 * PAD

    def conv_tile(i):
        s, t0 = divmod(i, tiles_per_seq)
        t0 *= CONV_TILE
        for c0 in range(0, GROUP_W, LANES):
            w = pad_ref[s, t0:t0 + win, c0:c0 + LANES]
            acc = jnp.zeros((CONV_TILE, LANES), F32) + cb[:, c0:c0 + LANES]
            for b in range(SUBLANES):
                wb = w if b == 0 else pltpu.roll(w, win - b, axis=0)
                for a in range(2 * PAD // SUBLANES):
                    k = SUBLANES * a + b - 1
                    if 0 <= k < CONV_W:
                        acc = acc + dw_ref[k:k + 1, c0:c0 + LANES] * wb[SUBLANES * a:SUBLANES * a + CONV_TILE, :]
            tmp_ref[i * CONV_TILE:(i + 1) * CONV_TILE, c0:c0 + LANES] = acc

    assert POOL_WINDOWS == (2, 4, 8, 16) and 2 * POOL_GW == LANES and POOL_HALO == SUBLANES
    pwin = CONV_TILE + 2 * POOL_HALO
    first_half = lax.broadcasted_iota(jnp.int32, (1, LANES), 1) < POOL_GW

    def pool_tile(i):
        s, t0 = divmod(i, tiles_per_seq)
        t0 *= CONV_TILE
        interior = t0 >= POOL_HALO and t0 + CONV_TILE + POOL_HALO <= seq_len
        outs = []
        for col, (w_small, w_big) in enumerate((POOL_WINDOWS[0:2], POOL_WINDOWS[2:4])):
            w = pad2_ref[s, t0 + PAD - POOL_HALO:t0 + PAD - POOL_HALO + pwin,
                         col * LANES:(col + 1) * LANES]

            rolled = {0: w}

            def shifted(b, w=w, rolled=rolled):
                if b not in rolled:
                    rolled[b] = pltpu.roll(w, pwin - b, axis=0)
                return rolled[b]

            def lo(b):
                return shifted(b)[0:CONV_TILE, :]

            def hi(b):
                return shifted(b)[POOL_HALO:POOL_HALO + CONV_TILE, :]

            tok = hi(0)
            if col == 0:
                s_small = lo(7) + tok
                s_big = s_small + lo(6) + hi(1)
            else:
                s_small = lo(7) + tok + lo(6) + hi(1) + lo(5) + lo(4) + hi(2) + hi(3)
                s_big = s_small + lo(3) + lo(2) + lo(1) + lo(0) + hi(4) + hi(5) + hi(6) + hi(7)
            ssum = jnp.where(first_half, s_small, s_big)
            if interior:
                mean = ssum * jnp.where(first_half, 1.0 / w_small, 1.0 / w_big)
            else:
                half = jnp.where(first_half, w_small // 2, w_big // 2)
                tpos = t0 + lax.broadcasted_iota(jnp.int32, (CONV_TILE, LANES), 0)
                cnt = jnp.minimum(tpos + half, seq_len) - jnp.maximum(tpos - half, 0)
                mean = ssum / cnt.astype(F32)
            outs.append(mean - tok)
        tmp2_ref[i * CONV_TILE:(i + 1) * CONV_TILE, :] = jnp.concatenate(outs, axis=1)

    col_tiles = [COL_C] + list(range(COL_B, COL_C, GROUP_W)) + list(range(COL_D, IN_COLS, GROUP_W))
    per_step = -(-N_CONV_TILES // (len(col_tiles) - 1))
    conv_next = 0
    pool_next = 0
    for step, c0 in enumerate(col_tiles):
        proj_ref[:, c0:c0 + GROUP_W] = jnp.dot(
            hbf_ref[...], w_in_ref[:, c0:c0 + GROUP_W], preferred_element_type=F32)
        if step == 1:
            for s in range(n_seq):
                pad2_ref[s, PAD:PAD + seq_len, :] = proj_ref[s * seq_len:(s + 1) * seq_len, COL_C:COL_D]
        for _ in range(per_step):
            if conv_next < N_CONV_TILES:
                conv_tile(conv_next)
                conv_next += 1
            if step >= 1 and pool_next < N_CONV_TILES:
                pool_tile(pool_next)
                pool_next += 1
    assert conv_next == N_CONV_TILES and pool_next == N_CONV_TILES

    def ln_pw_body(i, _):
        r0 = pl.multiple_of(i * ROW_TILE, ROW_TILE)
        c = tmp_ref[pl.ds(r0, ROW_TILE), :]
        mu = jnp.mean(c, axis=-1, keepdims=True)
        cen = c - mu
        var = jnp.mean(cen * cen, axis=-1, keepdims=True)
        hn = cen * lax.rsqrt(var + EPS) * lng + lnb
        ya = jnp.dot(_silu(hn).astype(BF16), pwb_ref[...], preferred_element_type=F32)
        cat_ref[pl.ds(r0, ROW_TILE), 0:GROUP_W] = ya.astype(BF16)
        return 0
    lax.fori_loop(0, TB // ROW_TILE, ln_pw_body, 0, unroll=True)

    def gmlp_body(i, _):
        r0 = pl.multiple_of(i * CHUNK, CHUNK)
        u = proj_ref[pl.ds(r0, CHUNK), COL_B:COL_B + GROUP_W]
        v = proj_ref[pl.ds(r0, CHUNK), COL_B + GROUP_W:COL_C]
        vstack = _head_stack(v, lane_head).astype(BF16)
        sg = jnp.dot(wcat_ref[...], vstack, preferred_element_type=F32) + gbias_ref[...]
        cat_ref[pl.ds(r0, CHUNK), GROUP_W:2 * GROUP_W] = (u * sg).astype(BF16)
        return 0
    lax.fori_loop(0, N_CHUNKS, gmlp_body, 0, unroll=True)

    def pool_mix_body(i, _):
        r0 = pl.multiple_of(i * ROW_TILE, ROW_TILE)
        yc = jnp.dot(tmp2_ref[pl.ds(r0, ROW_TILE), :].astype(BF16), pbd_ref[...],
                     preferred_element_type=F32) * pscale
        cat_ref[pl.ds(r0, ROW_TILE), 2 * GROUP_W:3 * GROUP_W] = yc.astype(BF16)
        return 0
    lax.fori_loop(0, TB // ROW_TILE, pool_mix_body, 0, unroll=True)

    ri = lax.broadcasted_iota(jnp.int32, (CHUNK, HEADS * CHUNK), 0)
    ci = lax.broadcasted_iota(jnp.int32, (CHUNK, HEADS * CHUNK), 1)
    cj = ci % CHUNK
    chead = ci // CHUNK
    rq = lax.broadcasted_iota(jnp.int32, (CHUNK, GROUP_W), 0).astype(F32)
    for d in range(2):
        lgs = [lg_ref[(2 * layer + d) * HEADS + h] for h in range(HEADS)]
        lg_wide = jnp.where(chead == 0, lgs[0], jnp.where(chead == 1, lgs[1],
                                                          jnp.where(chead == 2, lgs[2], lgs[3])))
        lg_lane = jnp.where(lane_head == 0, lgs[0], jnp.where(lane_head == 1, lgs[1],
                                                              jnp.where(lane_head == 2, lgs[2], lgs[3])))
        dist = (ri - cj) if d == 0 else (cj - ri)
        keep = dist >= 0
        dcat_ref[d] = jnp.where(keep, jnp.exp(jnp.where(keep, dist, 0).astype(F32) * lg_wide), 0.0)
        if d == 0:
            qdec_ref[d] = jnp.exp((rq + 1.0) * lg_lane)
            kdec_ref[d] = jnp.exp((CHUNK - 1.0 - rq) * lg_lane)
        else:
            qdec_ref[d] = jnp.exp((CHUNK - rq) * lg_lane)
            kdec_ref[d] = jnp.exp(rq * lg_lane)
        sdec_ref[d] = jnp.exp(float(CHUNK) * lg_lane)

    rr = lax.broadcasted_iota(jnp.int32, (GROUP_W, GROUP_W), 0) // HEAD_D
    cc = lax.broadcasted_iota(jnp.int32, (GROUP_W, GROUP_W), 1) // HEAD_D
    gmat_ref[...] = jnp.where(rr == cc, 1.0 / HEAD_D, 0.0).astype(BF16)

    lane_bit = (lane & ROPE_PAIR) == 0
    k_scale = HEAD_D ** -0.5

    def rope(z, r0):
        if not rotate:
            return z
        cos = cos_ref[pl.ds(r0, CHUNK), :]
        sin = sin_ref[pl.ds(r0, CHUNK), :]
        halves = []
        for c0 in range(0, GROUP_W, LANES):
            zz = z[:, c0:c0 + LANES]
            partner = jnp.where(lane_bit[:, c0:c0 + LANES],
                                pltpu.roll(zz, LANES - ROPE_PAIR, axis=1), pltpu.roll(zz, ROPE_PAIR, axis=1))
            halves.append(partner)
        return z * cos + jnp.concatenate(halves, axis=1) * sin

    def pair_stack(zb, pair):
        zero = jnp.zeros_like(zb)
        return jnp.concatenate([jnp.where(lane_head == h, zb, zero) for h in (2 * pair, 2 * pair + 1)], axis=0)

    def intra_body(c, _):
        r0 = pl.multiple_of(c * CHUNK, CHUNK)
        v = proj_ref[pl.ds(r0, CHUNK), COL_V:COL_V + GROUP_W]
        vb = v.astype(BF16)
        vstacks = [pair_stack(vb, pair) for pair in range(2)]
        o = None
        for d in range(2):
            qc0 = COL_D + 2 * d * GROUP_W
            q = rope(proj_ref[pl.ds(r0, CHUNK), qc0:qc0 + GROUP_W], r0)
            k = rope(proj_ref[pl.ds(r0, CHUNK), qc0 + GROUP_W:qc0 + 2 * GROUP_W], r0) * k_scale
            qb = q.astype(BF16)
            kb = k.astype(BF16)
            qb_ref[d, pl.ds(r0, CHUNK), :] = qb
            for pair in range(2):
                att = lax.dot_general(qb, pair_stack(kb, pair), (((1,), (1,)), ((), ())),
                                      preferred_element_type=F32)
                att = (att * dcat_ref[d, :, pair * 2 * CHUNK:(pair + 1) * 2 * CHUNK]).astype(BF16)
                od = jnp.dot(att, vstacks[pair], preferred_element_type=F32)
                o = od if o is None else o + od
            kd = (k * kdec_ref[d]).astype(BF16)
            upd = lax.dot_general(kd, vb, (((0,), (0,)), ((), ())), preferred_element_type=F32)
            compact = None
            for h in range(HEADS):
                part = jnp.where(lane_head == h, upd[h * HEAD_D:(h + 1) * HEAD_D, :], 0.0)
                compact = part if compact is None else compact + part
            upd_ref[d, c] = compact
        of_ref[pl.ds(r0, CHUNK), :] = o
        return 0
    lax.fori_loop(0, N_CHUNKS, intra_body, 0, unroll=True)

    for s in range(n_seq):
        for d in range(2):
            st = s0c_ref[d] if rotate else jnp.zeros((HEAD_D, GROUP_W), F32)
            order = range(n_chunk) if d == 0 else range(n_chunk - 1, -1, -1)
            for c in order:
                cg = s * n_chunk + c
                for h in range(HEADS):
                    sall_ref[d, cg, h * HEAD_D:(h + 1) * HEAD_D, :] = (
                        jnp.where(lane_head == h, st, 0.0).astype(BF16))
                st = st * sdec_ref[d] + upd_ref[d, cg]
            if not rotate:
                for h in range(HEADS):
                    if layer == 0:
                        st_ref[s, 0, d, h] = st[:, h * HEAD_D:(h + 1) * HEAD_D]
                    else:
                        st_ref[s, d, h] = st[:, h * HEAD_D:(h + 1) * HEAD_D]
    if not rotate and layer == 0:
        for s in range(n_seq):
            for later in range(1, DEPTH):
                for d in range(2):
                    for h in range(HEADS):
                        st_ref[s, later, d, h] = jnp.zeros((HEAD_D, HEAD_D), F32)

    def cross_body(c, _):
        r0 = pl.multiple_of(c * CHUNK, CHUNK)
        o = of_ref[pl.ds(r0, CHUNK), :]
        for d in range(2):
            o = o + jnp.dot(qb_ref[d, pl.ds(r0, CHUNK), :], sall_ref[d, c],
                            preferred_element_type=F32) * qdec_ref[d]
        of_ref[pl.ds(r0, CHUNK), :] = o
        return 0
    lax.fori_loop(0, N_CHUNKS, cross_body, 0, unroll=True)

    def center_body(i, _):
        r0 = pl.multiple_of(i * ROW_TILE, ROW_TILE)
        o = of_ref[pl.ds(r0, ROW_TILE), :]
        gmat = gmat_ref[...]
        o_hi = o.astype(BF16)
        o_lo = (o - o_hi.astype(F32)).astype(BF16)
        mu = (jnp.dot(o_hi, gmat, preferred_element_type=F32)
              + jnp.dot(o_lo, gmat, preferred_element_type=F32))
        of_ref[pl.ds(r0, ROW_TILE), :] = o - mu
        return 0
    lax.fori_loop(0, TB // ROW_TILE, center_body, 0, unroll=True)

    def gate_body(i, _):
        r0 = pl.multiple_of(i * ROW_TILE, ROW_TILE)
        cen = of_ref[pl.ds(r0, ROW_TILE), :]
        var = jnp.dot((cen * cen).astype(BF16), gmat_ref[...], preferred_element_type=F32)
        on = cen * lax.rsqrt(var + EPS)
        g = proj_ref[pl.ds(r0, ROW_TILE), COL_G:COL_G + GROUP_W]
        cat_ref[pl.ds(r0, ROW_TILE), 3 * GROUP_W:4 * GROUP_W] = (_silu(g) * on).astype(BF16)
        return 0
    lax.fori_loop(0, TB // ROW_TILE, gate_body, 0, unroll=True)

    def out_body(i, _):
        r0 = pl.multiple_of(i * ROW_TILE, ROW_TILE)
        y = jnp.dot(cat_ref[pl.ds(r0, ROW_TILE), :], w_out_ref[...], preferred_element_type=F32)
        y_ref[pl.ds(r0, ROW_TILE), :] = x_ref[pl.ds(r0, ROW_TILE), :] + ga1 * y
        return 0
    lax.fori_loop(0, TB // ROW_TILE, out_body, 0, unroll=True)


def _layer_spec(shape, layer):
    zeros = (0,) * len(shape)
    return pl.BlockSpec((None,) + tuple(shape), lambda i: (layer,) + zeros, pipeline_mode=pl.Buffered(1))


def _mod_spec(layer):
    return pl.BlockSpec((MOD_ROWS, 6 * D_MODEL), lambda i: (layer, 0), pipeline_mode=pl.Buffered(1))


def _mixer(x, mods, pp, *, layer, seq_len, rotate, mod_base, mod_stride, rope_tabs=None, s0=None, states=None):
    nb = x.shape[0]
    n_seq = TB // seq_len
    in_specs = [
        pl.BlockSpec((None, TB, D_MODEL), lambda i: (i, 0, 0)),
        _mod_spec(layer),
        _const_spec((DEPTH, D_MODEL)),
        _layer_spec((D_MODEL, IN_COLS), layer),
        _layer_spec((D_MODEL, D_MODEL), layer),
        _layer_spec((CONV_W, GROUP_W), layer),
        _const_spec((DEPTH, GROUP_W)), _const_spec((DEPTH, GROUP_W)), _const_spec((DEPTH, GROUP_W)),
        _layer_spec((GROUP_W, GROUP_W), layer),
        _layer_spec((HEADS, CHUNK, CHUNK), layer),
        _layer_spec((CHUNK, GROUP_W), layer),
        _layer_spec((HEADS, HEAD_D, HEAD_D), layer),
        _const_spec((DEPTH, GROUP_W)),
        pl.BlockSpec(memory_space=pltpu.SMEM),
    ]
    args = [x, mods, pp["g1"], pp["w_in"], pp["w_out"], pp["dw"], pp["cb"], pp["lng"], pp["lnb"],
            pp["pw"], pp["ws"], pp["gbias"], pp["pool_w"], pp["pscale"], pp["lg"]]
    out_shape = [jax.ShapeDtypeStruct((nb, TB, D_MODEL), F32)]
    out_specs = [pl.BlockSpec((None, TB, D_MODEL), lambda i: (i, 0, 0))]
    aliases = {}
    if rotate:
        in_specs += [_const_spec((TB, GROUP_W)), _const_spec((TB, GROUP_W)),
                     pl.BlockSpec((None, None, 2, HEADS, HEAD_D, HEAD_D), lambda i: (i, layer, 0, 0, 0, 0))]
        args += [rope_tabs[0], rope_tabs[1], s0]
    else:
        out_shape.append(jax.ShapeDtypeStruct((nb * n_seq, DEPTH, 2, HEADS, HEAD_D, HEAD_D), F32))
        if layer == 0:
            out_specs.append(pl.BlockSpec((n_seq, DEPTH, 2, HEADS, HEAD_D, HEAD_D),
                                          lambda i: (i, 0, 0, 0, 0, 0)))
        else:
            in_specs.append(pl.BlockSpec(memory_space=pl.ANY))
            args.append(states)
            aliases = {len(args) - 1: 1}
            out_specs.append(pl.BlockSpec((n_seq, None, 2, HEADS, HEAD_D, HEAD_D),
                                          lambda i: (i, layer, 0, 0, 0, 0)))
    scratch = [
        pltpu.VMEM((TB, D_MODEL), BF16),
        pltpu.VMEM((TB, IN_COLS), F32),
        pltpu.VMEM((n_seq, seq_len + 2 * PAD, GROUP_W), F32),
        pltpu.VMEM((n_seq, seq_len + 2 * PAD, GROUP_W), F32),
        pltpu.VMEM((TB, GROUP_W), F32),
        pltpu.VMEM((TB, GROUP_W), F32),
        pltpu.VMEM((2, CHUNK, HEADS * CHUNK), F32),
        pltpu.VMEM((2, CHUNK, GROUP_W), F32),
        pltpu.VMEM((2, CHUNK, GROUP_W), F32),
        pltpu.VMEM((2, 1, GROUP_W), F32),
        pltpu.VMEM((GROUP_W, GROUP_W), BF16),
        pltpu.VMEM((2, TB, GROUP_W), BF16),
        pltpu.VMEM((2, N_CHUNKS, HEAD_D, GROUP_W), F32),
        pltpu.VMEM((2, N_CHUNKS, GROUP_W, GROUP_W), BF16),
        pltpu.VMEM((GROUP_W, GROUP_W), BF16),
        pltpu.VMEM((CHUNK, HEADS * CHUNK), BF16),
        pltpu.VMEM((GROUP_W, GROUP_W), BF16),
        pltpu.VMEM((2, HEAD_D, GROUP_W), F32),
    ]
    outs = pl.pallas_call(
        functools.partial(_mixer_kernel, layer=layer, seq_len=seq_len, rotate=rotate,
                          mod_base=mod_base, mod_stride=mod_stride),
        grid=(nb,),
        in_specs=in_specs,
        out_specs=out_specs,
        out_shape=out_shape,
        scratch_shapes=scratch,
        input_output_aliases=aliases,
        compiler_params=pltpu.CompilerParams(
            dimension_semantics=("arbitrary",), vmem_limit_bytes=VMEM_LIMIT_BYTES),
        name="mixer_lat" if rotate else "mixer_ctx",
    )(*args)
    return outs


def _ffn_kernel(x_ref, mod_ref, g2_ref, w_in_hbm, w_out_hbm, gf_ref, y_ref,
                hbf_ref, act_ref, act3_ref, wgu_ref, wo_ref, stg_g_ref, stg_u_ref, stg_o_ref, sem,
                *, layer, final_norm, mod_base, mod_stride):
    pid = pl.program_id(0)
    mod = _mod_row(mod_ref, mod_base, mod_stride)
    sh2 = mod(3)
    sc2 = mod(4)
    ga2 = mod(5)
    g2 = g2_ref[layer:layer + 1, :]

    def tile_copies(j, slot):
        return (
            pltpu.make_async_copy(w_in_hbm.at[layer, :, pl.ds(j * FF_TILE, FF_TILE)],
                                  stg_g_ref.at[slot], sem.at[0, slot]),
            pltpu.make_async_copy(w_in_hbm.at[layer, :, pl.ds(D_FF + j * FF_TILE, FF_TILE)],
                                  stg_u_ref.at[slot], sem.at[1, slot]),
            pltpu.make_async_copy(w_out_hbm.at[layer, pl.ds(j * FF_TILE, FF_TILE), :],
                                  stg_o_ref.at[slot], sem.at[2, slot]),
        )

    def start_tile(j):
        for cp in tile_copies(j, j % FF_RING):
            cp.start()

    def hidden_tile(j, act_out):
        gate = jnp.dot(hbf_ref[...], wgu_ref[j], preferred_element_type=F32)
        up = jnp.dot(hbf_ref[...], wgu_ref[N_FF_TILES + j], preferred_element_type=F32)
        act_out((_silu(gate) * up).astype(BF16))

    @pl.when(pid == 0)
    def _():
        for j in range(FF_LOOKAHEAD):
            start_tile(j)

    def norm_body(i, _):
        r0 = pl.multiple_of(i * ROW_TILE, ROW_TILE)
        h = _norm_mod(x_ref[pl.ds(r0, ROW_TILE), :], g2, sc2, sh2)
        hbf_ref[pl.ds(r0, ROW_TILE), :] = h.astype(BF16)
        return 0
    lax.fori_loop(0, TB // ROW_TILE, norm_body, 0)

    @pl.when(pid == 0)
    def _():
        def fetch_body(j, _):
            @pl.when(j + FF_LOOKAHEAD < N_FF_TILES)
            def _():
                start_tile(j + FF_LOOKAHEAD)
            slot = j % FF_RING
            for cp in tile_copies(j, slot):
                cp.wait()
            wgu_ref[j] = stg_g_ref[slot].astype(BF16)
            wgu_ref[N_FF_TILES + j] = stg_u_ref[slot].astype(BF16)
            wo_ref[pl.ds(pl.multiple_of(j * FF_TILE, FF_TILE), FF_TILE), :] = stg_o_ref[slot].astype(BF16)

            def act_out(act):
                act3_ref[j] = act
            hidden_tile(j, act_out)
            return 0
        lax.fori_loop(0, N_FF_TILES, fetch_body, 0)
        for j in range(N_FF_TILES):
            act_ref[:, j * FF_TILE:(j + 1) * FF_TILE] = act3_ref[j]

    @pl.when(pid != 0)
    def _():
        for j in range(N_FF_TILES):
            def act_out(act, j=j):
                act_ref[:, j * FF_TILE:(j + 1) * FF_TILE] = act
            hidden_tile(j, act_out)

    for r0 in range(0, TB, FF_OUT_ROWS):
        rows = slice(r0, r0 + FF_OUT_ROWS)
        y = x_ref[rows, :] + ga2 * jnp.dot(act_ref[rows, :], wo_ref[...], preferred_element_type=F32)
        if final_norm:
            ms = jnp.mean(y * y, axis=-1, keepdims=True)
            y = y * lax.rsqrt(ms + EPS) * gf_ref[...]
        y_ref[rows, :] = y


def _ffn(x, mods, g2, w_ffn_in, w_ffn_out, g_final, *, layer, mod_base, mod_stride, final_norm):
    nb = x.shape[0]
    return pl.pallas_call(
        functools.partial(_ffn_kernel, layer=layer, final_norm=final_norm, mod_base=mod_base,
                          mod_stride=mod_stride),
        grid=(nb,),
        in_specs=[
            pl.BlockSpec((None, TB, D_MODEL), lambda i: (i, 0, 0)),
            _mod_spec(layer),
            _const_spec((DEPTH, D_MODEL)),
            pl.BlockSpec(memory_space=pl.ANY),
            pl.BlockSpec(memory_space=pl.ANY),
            _const_spec((1, D_MODEL)),
        ],
        out_specs=pl.BlockSpec((None, TB, D_MODEL), lambda i: (i, 0, 0)),
        out_shape=jax.ShapeDtypeStruct((nb, TB, D_MODEL), F32),
        scratch_shapes=[
            pltpu.VMEM((TB, D_MODEL), BF16),
            pltpu.VMEM((TB, D_FF), BF16),
            pltpu.VMEM((N_FF_TILES, TB, FF_TILE), BF16),
            pltpu.VMEM((2 * N_FF_TILES, D_MODEL, FF_TILE), BF16),
            pltpu.VMEM((D_FF, D_MODEL), BF16),
            pltpu.VMEM((FF_RING, D_MODEL, FF_TILE), F32),
            pltpu.VMEM((FF_RING, D_MODEL, FF_TILE), F32),
            pltpu.VMEM((FF_RING, FF_TILE, D_MODEL), F32),
            pltpu.SemaphoreType.DMA((3, FF_RING)),
        ],
        compiler_params=pltpu.CompilerParams(
            dimension_semantics=("arbitrary",), vmem_limit_bytes=VMEM_LIMIT_BYTES),
        name="ffn",
    )(x, mods, g2, w_ffn_in, w_ffn_out, g_final)


def _rope_tables(seq_len):
    t = np.arange(seq_len)
    r = (t // GRID_W).astype(np.float32)
    c = (t % GRID_W).astype(np.float32)
    nf = HEAD_D // 4
    inv = np.float32(ROPE_BASE) ** (-np.arange(nf, dtype=np.float32) / np.float32(nf))
    ang_r = r[:, None] * inv
    ang_c = c[:, None] * inv
    cos = np.concatenate([np.cos(ang_r), np.cos(ang_r), np.cos(ang_c), np.cos(ang_c)], axis=-1)
    sin = np.concatenate([-np.sin(ang_r), np.sin(ang_r), -np.sin(ang_c), np.sin(ang_c)], axis=-1)
    return (jnp.asarray(np.tile(cos, (1, HEADS)), dtype=F32), jnp.asarray(np.tile(sin, (1, HEADS)), dtype=F32))


def kernel(x_prompt, x_sample, state_ret, c, c_ctx, w_ada, b_ada, g_norm1, g_norm2, w_in, w_out, conv_dw,
           conv_b, conv_ln_g, conv_ln_b, conv_pw, gmlp_ws, gmlp_b, pool_w, pool_scale, ret_decay, w_ffn_in,
           w_ffn_out, g_final):
    batch, seq, _ = x_prompt.shape
    dec_batch, dec_seq, _ = x_sample.shape
    assert dec_seq == TB and TB % seq == 0 and (batch * seq) % TB == 0
    assert 1 + dec_batch <= MOD_ROWS

    cs = jnp.concatenate([c_ctx[None, :], c, jnp.zeros((MOD_ROWS - 1 - dec_batch, D_MODEL), F32)], axis=0)
    mods = _ada_rows(cs, w_ada, b_ada).reshape(DEPTH * MOD_ROWS, 6 * D_MODEL)
    rope_tabs = _rope_tables(dec_seq)
    g_final2 = g_final.reshape(1, D_MODEL)
    pp = {
        "g1": g_norm1,
        "g2": g_norm2,
        "w_in": w_in.astype(BF16),
        "w_out": w_out.astype(BF16),
        "dw": conv_dw,
        "cb": conv_b,
        "lng": conv_ln_g,
        "lnb": conv_ln_b,
        "pw": conv_pw,
        "ws": gmlp_ws,
        "gbias": jnp.repeat(jnp.swapaxes(gmlp_b, 1, 2), HEAD_D, axis=2),
        "pool_w": pool_w,
        "pscale": pool_scale,
        "lg": jax.nn.log_sigmoid(ret_decay.astype(F32)).reshape(DEPTH * 2 * HEADS),
    }

    xc = x_prompt.reshape(batch * seq // TB, TB, D_MODEL)
    xl = x_sample
    states = None
    for l in range(DEPTH):
        last = l == DEPTH - 1
        xc, states = _mixer(xc, mods, pp, layer=l, seq_len=seq, rotate=False, mod_base=0, mod_stride=0,
                            states=states)
        xc = _ffn(xc, mods, pp["g2"], w_ffn_in, w_ffn_out, g_final2, layer=l, mod_base=0, mod_stride=0,
                  final_norm=last)
        (xl,) = _mixer(xl, mods, pp, layer=l, seq_len=dec_seq, rotate=True, mod_base=1, mod_stride=1,
                       rope_tabs=rope_tabs, s0=state_ret.astype(F32))
        xl = _ffn(xl, mods, pp["g2"], w_ffn_in, w_ffn_out, g_final2, layer=l, mod_base=1, mod_stride=1,
                  final_norm=last)

    y_prompt = xc.reshape(batch, seq, D_MODEL)
    return (y_prompt, xl, states.astype(x_prompt.dtype))
```

```python
import functools

import jax
import jax.numpy as jnp
import numpy as np
from jax import lax
from jax.experimental import pallas as pl
from jax.experimental.pallas import tpu as pltpu

F32 = jnp.float32
BF16 = jnp.bfloat16

D_MODEL = 1024
DEPTH = 2
GRID_W = 64
GROUP_W = D_MODEL // 4
CONV_W = 31
CHUNK = 128
HEADS = 4
HEAD_D = GROUP_W // HEADS
POOL_WINDOWS = (2, 4, 8, 16)
POOL_GW = GROUP_W // len(POOL_WINDOWS)
ROPE_BASE = 10000.0
D_FF = 2816
IN_COLS = 11 * GROUP_W
EPS = 1e-6

LANES = 128
SUBLANES = 8

TB = 1024
N_CHUNKS = TB // CHUNK
ROW_TILE = 256
CONV_TILE = 64
N_CONV_TILES = TB // CONV_TILE
PAD = 16
POOL_HALO = max(POOL_WINDOWS) // 2
ROPE_PAIR = HEAD_D // 4
FF_TILE = 256
N_FF_TILES = D_FF // FF_TILE
FF_OUT_ROWS = 256
FF_LOOKAHEAD = 2
FF_RING = FF_LOOKAHEAD + 1
ADA_TILE = 3072
MOD_ROWS = 8
VMEM_LIMIT_BYTES = 60 * 1024 * 1024

COL_A, COL_B, COL_C, COL_D = 0, 2 * GROUP_W, 4 * GROUP_W, 5 * GROUP_W
COL_V, COL_G = COL_D + 4 * GROUP_W, COL_D + 5 * GROUP_W


def _sigmoid(x):
    return 1.0 / (1.0 + jnp.exp(-x))


def _silu(x):
    return x * _sigmoid(x)


def _norm_mod(x, g, scale, shift):
    ms = jnp.mean(x * x, axis=-1, keepdims=True)
    return (x * lax.rsqrt(ms + EPS) * g) * (1.0 + scale) + shift


def _head_stack(x, lane_head):
    return jnp.concatenate([jnp.where(lane_head == h, x, 0.0) for h in range(HEADS)], axis=0)


def _mod_row(mod_ref, mod_base, mod_stride):
    row = jnp.maximum(mod_base + mod_stride * pl.program_id(0), 0) if mod_stride else max(mod_base, 0)

    def part(k):
        return mod_ref[pl.ds(row, 1), k * D_MODEL:(k + 1) * D_MODEL]
    return part


def _const_spec(shape):
    zeros = (0,) * len(shape)
    return pl.BlockSpec(shape, lambda i: zeros, pipeline_mode=pl.Buffered(1))


def _ada_kernel(c_ref, w_ref, b_ref, o_ref):
    a = _silu(c_ref[...]).astype(BF16)
    bias = b_ref[pl.ds(pl.program_id(0), 1), :]
    o_ref[...] = jnp.dot(a, w_ref[...].astype(BF16), preferred_element_type=F32) + bias


def _ada_rows(cs, w_ada, b_ada):
    n_tiles = 6 * D_MODEL // ADA_TILE
    return pl.pallas_call(
        _ada_kernel,
        grid=(DEPTH, n_tiles),
        in_specs=[
            pl.BlockSpec((MOD_ROWS, D_MODEL), lambda l, j: (0, 0)),
            pl.BlockSpec((None, D_MODEL, ADA_TILE), lambda l, j: (l, 0, j)),
            pl.BlockSpec((DEPTH, ADA_TILE), lambda l, j: (0, j)),
        ],
        out_specs=pl.BlockSpec((None, MOD_ROWS, ADA_TILE), lambda l, j: (l, 0, j)),
        out_shape=jax.ShapeDtypeStruct((DEPTH, MOD_ROWS, 6 * D_MODEL), F32),
        compiler_params=pltpu.CompilerParams(
            dimension_semantics=("arbitrary", "arbitrary"), vmem_limit_bytes=VMEM_LIMIT_BYTES),
        name="ada_rows",
    )(cs, w_ada, b_ada)


def _mixer_kernel(*refs, layer, seq_len, rotate, mod_base, mod_stride):
    n_seq = TB // seq_len
    n_chunk = seq_len // CHUNK
    it = iter(refs)
    x_ref, mod_ref, g1_ref, w_in_ref, w_out_ref = (next(it) for _ in range(5))
    dw_ref, cb_ref, lng_ref, lnb_ref, pw_ref = (next(it) for _ in range(5))
    ws_ref, gbias_ref, poolw_ref, pscale_ref, lg_ref = (next(it) for _ in range(5))
    if rotate:
        cos_ref, sin_ref, s0_ref = (next(it) for _ in range(3))
    elif layer > 0:
        next(it)
    y_ref = next(it)
    if not rotate:
        st_ref = next(it)
    (hbf_ref, proj_ref, pad_ref, pad2_ref, tmp_ref, tmp2_ref,
     dcat_ref, qdec_ref, kdec_ref, sdec_ref, gmat_ref, qb_ref, upd_ref, sall_ref,
     pwb_ref, wcat_ref, pbd_ref, s0c_ref) = it
    cat_ref = hbf_ref
    of_ref = tmp_ref

    pwb_ref[...] = pw_ref[...].astype(BF16)
    pbd_ref[...] = jnp.zeros((GROUP_W, GROUP_W), BF16)
    for h in range(HEADS):
        wcat_ref[:, h * CHUNK:(h + 1) * CHUNK] = ws_ref[h].astype(BF16)
        pbd_ref[h * HEAD_D:(h + 1) * HEAD_D, h * HEAD_D:(h + 1) * HEAD_D] = poolw_ref[h].astype(BF16)
        if rotate:
            for d in range(2):
                s0c_ref[d, :, h * HEAD_D:(h + 1) * HEAD_D] = s0_ref[d, h]

    mod = _mod_row(mod_ref, mod_base, mod_stride)
    sh1 = mod(0)
    sc1 = mod(1)
    ga1 = mod(2)
    g1, cb, lng, lnb, pscale = (r[layer:layer + 1, :] for r in (g1_ref, cb_ref, lng_ref, lnb_ref, pscale_ref))

    lane = lax.broadcasted_iota(jnp.int32, (1, GROUP_W), 1)
    lane_head = lane // HEAD_D
    tiles_per_seq = seq_len // CONV_TILE
    chunks_per_seq = seq_len // CHUNK

    def seq_and_offset(i, per_seq, size):
        if n_seq == 1:
            return 0, pl.multiple_of(i * size, size)
        return i // per_seq, pl.multiple_of((i % per_seq) * size, size)

    def norm_body(i, _):
        r0 = pl.multiple_of(i * ROW_TILE, ROW_TILE)
        h = _norm_mod(x_ref[pl.ds(r0, ROW_TILE), :], g1, sc1, sh1).astype(BF16)
        hbf_ref[pl.ds(r0, ROW_TILE), :] = h
        proj_ref[pl.ds(r0, ROW_TILE), COL_A:COL_B] = jnp.dot(
            h, w_in_ref[:, COL_A:COL_B], preferred_element_type=F32)
        return 0
    lax.fori_loop(0, TB // ROW_TILE, norm_body, 0, unroll=True)

    for s in range(n_seq):
        for ref in (pad_ref, pad2_ref):
            ref[s, 0:PAD, :] = jnp.zeros((PAD, GROUP_W), F32)
            ref[s, PAD + seq_len:PAD + seq_len + PAD, :] = jnp.zeros((PAD, GROUP_W), F32)

    def glu_body(i, _):
        s, t0 = seq_and_offset(i, chunks_per_seq, CHUNK)
        r0 = pl.multiple_of(i * CHUNK, CHUNK)
        a1 = proj_ref[pl.ds(r0, CHUNK), COL_A:COL_A + GROUP_W]
        a2 = proj_ref[pl.ds(r0, CHUNK), COL_A + GROUP_W:COL_B]
        pad_ref[s, pl.ds(pl.multiple_of(t0 + PAD, SUBLANES), CHUNK), :] = a1 * _sigmoid(a2)
        return 0
    lax.fori_loop(0, N_CHUNKS, glu_body, 0, unroll=True)

    win = CONV_TILE + 2 * PAD

    def conv_tile(i):
        s, t0 = divmod(i, tiles_per_seq)
        t0 *= CONV_TILE
        for c0 in range(0, GROUP_W, LANES):
            w = pad_ref[s, t0:t0 + win, c0:c0 + LANES]
            acc = jnp.zeros((CONV_TILE, LANES), F32) + cb[:, c0:c0 + LANES]
            for b in range(SUBLANES):
                wb = w if b == 0 else pltpu.roll(w, win - b, axis=0)
                for a in range(2 * PAD // SUBLANES):
                    k = SUBLANES * a + b - 1
                    if 0 <= k < CONV_W:
                        acc = acc + dw_ref[k:k + 1, c0:c0 + LANES] * wb[SUBLANES * a:SUBLANES * a + CONV_TILE, :]
            tmp_ref[i * CONV_TILE:(i + 1) * CONV_TILE, c0:c0 + LANES] = acc

    assert POOL_WINDOWS == (2, 4, 8, 16) and 2 * POOL_GW == LANES and POOL_HALO == SUBLANES
    pwin = CONV_TILE + 2 * POOL_HALO
    first_half = lax.broadcasted_iota(jnp.int32, (1, LANES), 1) < POOL_GW

    def pool_tile(i):
        s, t0 = divmod(i, tiles_per_seq)
        t0 *= CONV_TILE
        interior = t0 >= POOL_HALO and t0 + CONV_TILE + POOL_HALO <= seq_len
        outs = []
        for col, (w_small, w_big) in enumerate((POOL_WINDOWS[0:2], POOL_WINDOWS[2:4])):
            w = pad2_ref[s, t0 + PAD - POOL_HALO:t0 + PAD - POOL_HALO + pwin,
                         col * LANES:(col + 1) * LANES]

            rolled = {0: w}

            def shifted(b, w=w, rolled=rolled):
                if b not in rolled:
                    rolled[b] = pltpu.roll(w, pwin - b, axis=0)
                return rolled[b]

            def lo(b):
                return shifted(b)[0:CONV_TILE, :]

            def hi(b):
                return shifted(b)[POOL_HALO:POOL_HALO + CONV_TILE, :]

            tok = hi(0)
            if col == 0:
                s_small = lo(7) + tok
                s_big = s_small + lo(6) + hi(1)
            else:
                s_small = lo(7) + tok + lo(6) + hi(1) + lo(5) + lo(4) + hi(2) + hi(3)
                s_big = s_small + lo(3) + lo(2) + lo(1) + lo(0) + hi(4) + hi(5) + hi(6) + hi(7)
            ssum = jnp.where(first_half, s_small, s_big)
            if interior:
                mean = ssum * jnp.where(first_half, 1.0 / w_small, 1.0 / w_big)
            else:
                half = jnp.where(first_half, w_small // 2, w_big // 2)
                tpos = t0 + lax.broadcasted_iota(jnp.int32, (CONV_TILE, LANES), 0)
                cnt = jnp.minimum(tpos + half, seq_len) - jnp.maximum(tpos - half, 0)
                mean = ssum / cnt.astype(F32)
            outs.append(mean - tok)
        tmp2_ref[i * CONV_TILE:(i + 1) * CONV_TILE, :] = jnp.concatenate(outs, axis=1)

    col_tiles = [COL_C] + list(range(COL_B, COL_C, GROUP_W)) + list(range(COL_D, IN_COLS, GROUP_W))
    per_step = -(-N_CONV_TILES // (len(col_tiles) - 1))
    conv_next = 0
    pool_next = 0
    for step, c0 in enumerate(col_tiles):
        proj_ref[:, c0:c0 + GROUP_W] = jnp.dot(
            hbf_ref[...], w_in_ref[:, c0:c0 + GROUP_W], preferred_element_type=F32)
        if step == 1:
            for s in range(n_seq):
                pad2_ref[s, PAD:PAD + seq_len, :] = proj_ref[s * seq_len:(s + 1) * seq_len, COL_C:COL_D]
        for _ in range(per_step):
            if conv_next < N_CONV_TILES:
                conv_tile(conv_next)
                conv_next += 1
            if step >= 1 and pool_next < N_CONV_TILES:
                pool_tile(pool_next)
                pool_next += 1
    assert conv_next == N_CONV_TILES and pool_next == N_CONV_TILES

    def ln_pw_body(i, _):
        r0 = pl.multiple_of(i * ROW_TILE, ROW_TILE)
        c = tmp_ref[pl.ds(r0, ROW_TILE), :]
        mu = jnp.mean(c, axis=-1, keepdims=True)
        cen = c - mu
        var = jnp.mean(cen * cen, axis=-1, keepdims=True)
        hn = cen * lax.rsqrt(var + EPS) * lng + lnb
        ya = jnp.dot(_silu(hn).astype(BF16), pwb_ref[...], preferred_element_type=F32)
        cat_ref[pl.ds(r0, ROW_TILE), 0:GROUP_W] = ya.astype(BF16)
        return 0
    lax.fori_loop(0, TB // ROW_TILE, ln_pw_body, 0, unroll=True)

    def gmlp_body(i, _):
        r0 = pl.multiple_of(i * CHUNK, CHUNK)
        u = proj_ref[pl.ds(r0, CHUNK), COL_B:COL_B + GROUP_W]
        v = proj_ref[pl.ds(r0, CHUNK), COL_B + GROUP_W:COL_C]
        vstack = _head_stack(v, lane_head).astype(BF16)
        sg = jnp.dot(wcat_ref[...], vstack, preferred_element_type=F32) + gbias_ref[...]
        cat_ref[pl.ds(r0, CHUNK), GROUP_W:2 * GROUP_W] = (u * sg).astype(BF16)
        return 0
    lax.fori_loop(0, N_CHUNKS, gmlp_body, 0, unroll=True)

    def pool_mix_body(i, _):
        r0 = pl.multiple_of(i * ROW_TILE, ROW_TILE)
        yc = jnp.dot(tmp2_ref[pl.ds(r0, ROW_TILE), :].astype(BF16), pbd_ref[...],
                     preferred_element_type=F32) * pscale
        cat_ref[pl.ds(r0, ROW_TILE), 2 * GROUP_W:3 * GROUP_W] = yc.astype(BF16)
        return 0
    lax.fori_loop(0, TB // ROW_TILE, pool_mix_body, 0, unroll=True)

    ri = lax.broadcasted_iota(jnp.int32, (CHUNK, HEADS * CHUNK), 0)
    ci = lax.broadcasted_iota(jnp.int32, (CHUNK, HEADS * CHUNK), 1)
    cj = ci % CHUNK
    chead = ci // CHUNK
    rq = lax.broadcasted_iota(jnp.int32, (CHUNK, GROUP_W), 0).astype(F32)
    for d in range(2):
        lgs = [lg_ref[(2 * layer + d) * HEADS + h] for h in range(HEADS)]
        lg_wide = jnp.where(chead == 0, lgs[0], jnp.where(chead == 1, lgs[1],
                                                          jnp.where(chead == 2, lgs[2], lgs[3])))
        lg_lane = jnp.where(lane_head == 0, lgs[0], jnp.where(lane_head == 1, lgs[1],
                                                              jnp.where(lane_head == 2, lgs[2], lgs[3])))
        dist = (ri - cj) if d == 0 else (cj - ri)
        keep = dist >= 0
        dcat_ref[d] = jnp.where(keep, jnp.exp(jnp.where(keep, dist, 0).astype(F32) * lg_wide), 0.0)
        if d == 0:
            qdec_ref[d] = jnp.exp((rq + 1.0) * lg_lane)
            kdec_ref[d] = jnp.exp((CHUNK - 1.0 - rq) * lg_lane)
        else:
            qdec_ref[d] = jnp.exp((CHUNK - rq) * lg_lane)
            kdec_ref[d] = jnp.exp(rq * lg_lane)
        sdec_ref[d] = jnp.exp(float(CHUNK) * lg_lane)

    rr = lax.broadcasted_iota(jnp.int32, (GROUP_W, GROUP_W), 0) // HEAD_D
    cc = lax.broadcasted_iota(jnp.int32, (GROUP_W, GROUP_W), 1) // HEAD_D
    gmat_ref[...] = jnp.where(rr == cc, 1.0 / HEAD_D, 0.0).astype(BF16)

    lane_bit = (lane & ROPE_PAIR) == 0
    k_scale = HEAD_D ** -0.5

    def rope(z, r0):
        if not rotate:
            return z
        cos = cos_ref[pl.ds(r0, CHUNK), :]
        sin = sin_ref[pl.ds(r0, CHUNK), :]
        halves = []
        for c0 in range(0, GROUP_W, LANES):
            zz = z[:, c0:c0 + LANES]
            partner = jnp.where(lane_bit[:, c0:c0 + LANES],
                                pltpu.roll(zz, LANES - ROPE_PAIR, axis=1), pltpu.roll(zz, ROPE_PAIR, axis=1))
            halves.append(partner)
        return z * cos + jnp.concatenate(halves, axis=1) * sin

    def pair_stack(zb, pair):
        zero = jnp.zeros_like(zb)
        return jnp.concatenate([jnp.where(lane_head == h, zb, zero) for h in (2 * pair, 2 * pair + 1)], axis=0)

    def intra_body(c, _):
        r0 = pl.multiple_of(c * CHUNK, CHUNK)
        v = proj_ref[pl.ds(r0, CHUNK), COL_V:COL_V + GROUP_W]
        vb = v.astype(BF16)
        atts = [[None, None], [None, None]]
        for d in range(2):
            qc0 = COL_D + 2 * d * GROUP_W
            q = rope(proj_ref[pl.ds(r0, CHUNK), qc0:qc0 + GROUP_W], r0)
            k = rope(proj_ref[pl.ds(r0, CHUNK), qc0 + GROUP_W:qc0 + 2 * GROUP_W], r0) * k_scale
            qb = q.astype(BF16)
            kb = k.astype(BF16)
            qb_ref[d, pl.ds(r0, CHUNK), :] = qb
            for pair in range(2):
                att = lax.dot_general(qb, pair_stack(kb, pair), (((1,), (1,)), ((), ())),
                                      preferred_element_type=F32)
                atts[pair][d] = (att * dcat_ref[d, :, pair * 2 * CHUNK:(pair + 1) * 2 * CHUNK]).astype(BF16)
            kd = (k * kdec_ref[d]).astype(BF16)
            upd = lax.dot_general(kd, vb, (((0,), (0,)), ((), ())), preferred_element_type=F32)
            compact = None
            for h in range(HEADS):
                part = jnp.where(lane_head == h, upd[h * HEAD_D:(h + 1) * HEAD_D, :], 0.0)
                compact = part if compact is None else compact + part
            upd_ref[d, c] = compact
        o = None
        for pair in range(2):
            both = jnp.dot(jnp.concatenate(atts[pair], axis=0), pair_stack(vb, pair), preferred_element_type=F32)
            od = both[0:CHUNK, :] + both[CHUNK:2 * CHUNK, :]
            o = od if o is None else o + od
        of_ref[pl.ds(r0, CHUNK), :] = o
        return 0
    lax.fori_loop(0, N_CHUNKS, intra_body, 0, unroll=True)

    for s in range(n_seq):
        for d in range(2):
            st = s0c_ref[d] if rotate else jnp.zeros((HEAD_D, GROUP_W), F32)
            order = range(n_chunk) if d == 0 else range(n_chunk - 1, -1, -1)
            for c in order:
                cg = s * n_chunk + c
                for h in range(HEADS):
                    sall_ref[d, cg, h * HEAD_D:(h + 1) * HEAD_D, :] = (
                        jnp.where(lane_head == h, st, 0.0).astype(BF16))
                st = st * sdec_ref[d] + upd_ref[d, cg]
            if not rotate:
                for h in range(HEADS):
                    if layer == 0:
                        st_ref[s, 0, d, h] = st[:, h * HEAD_D:(h + 1) * HEAD_D]
                    else:
                        st_ref[s, d, h] = st[:, h * HEAD_D:(h + 1) * HEAD_D]
    if not rotate and layer == 0:
        for s in range(n_seq):
            for later in range(1, DEPTH):
                for d in range(2):
                    for h in range(HEADS):
                        st_ref[s, later, d, h] = jnp.zeros((HEAD_D, HEAD_D), F32)

    def cross_body(c, _):
        r0 = pl.multiple_of(c * CHUNK, CHUNK)
        o = of_ref[pl.ds(r0, CHUNK), :]
        for d in range(2):
            o = o + jnp.dot(qb_ref[d, pl.ds(r0, CHUNK), :], sall_ref[d, c],
                            preferred_element_type=F32) * qdec_ref[d]
        of_ref[pl.ds(r0, CHUNK), :] = o
        return 0
    lax.fori_loop(0, N_CHUNKS, cross_body, 0, unroll=True)

    def center_body(i, _):
        r0 = pl.multiple_of(i * ROW_TILE, ROW_TILE)
        o = of_ref[pl.ds(r0, ROW_TILE), :]
        gmat = gmat_ref[...]
        o_hi = o.astype(BF16)
        o_lo = (o - o_hi.astype(F32)).astype(BF16)
        mu = (jnp.dot(o_hi, gmat, preferred_element_type=F32)
              + jnp.dot(o_lo, gmat, preferred_element_type=F32))
        of_ref[pl.ds(r0, ROW_TILE), :] = o - mu
        return 0
    lax.fori_loop(0, TB // ROW_TILE, center_body, 0, unroll=True)

    def gate_body(i, _):
        r0 = pl.multiple_of(i * ROW_TILE, ROW_TILE)
        cen = of_ref[pl.ds(r0, ROW_TILE), :]
        var = jnp.dot((cen * cen).astype(BF16), gmat_ref[...], preferred_element_type=F32)
        on = cen * lax.rsqrt(var + EPS)
        g = proj_ref[pl.ds(r0, ROW_TILE), COL_G:COL_G + GROUP_W]
        cat_ref[pl.ds(r0, ROW_TILE), 3 * GROUP_W:4 * GROUP_W] = (_silu(g) * on).astype(BF16)
        return 0
    lax.fori_loop(0, TB // ROW_TILE, gate_body, 0, unroll=True)

    def out_body(i, _):
        r0 = pl.multiple_of(i * ROW_TILE, ROW_TILE)
        y = jnp.dot(cat_ref[pl.ds(r0, ROW_TILE), :], w_out_ref[...], preferred_element_type=F32)
        y_ref[pl.ds(r0, ROW_TILE), :] = x_ref[pl.ds(r0, ROW_TILE), :] + ga1 * y
        return 0
    lax.fori_loop(0, TB // ROW_TILE, out_body, 0, unroll=True)


def _layer_spec(shape, layer):
    zeros = (0,) * len(shape)
    return pl.BlockSpec((None,) + tuple(shape), lambda i: (layer,) + zeros, pipeline_mode=pl.Buffered(1))


def _mod_spec(layer):
    return pl.BlockSpec((MOD_ROWS, 6 * D_MODEL), lambda i: (layer, 0), pipeline_mode=pl.Buffered(1))


def _mixer(x, mods, pp, *, layer, seq_len, rotate, mod_base, mod_stride, rope_tabs=None, s0=None, states=None):
    nb = x.shape[0]
    n_seq = TB // seq_len
    in_specs = [
        pl.BlockSpec((None, TB, D_MODEL), lambda i: (i, 0, 0)),
        _mod_spec(layer),
        _const_spec((DEPTH, D_MODEL)),
        _layer_spec((D_MODEL, IN_COLS), layer),
        _layer_spec((D_MODEL, D_MODEL), layer),
        _layer_spec((CONV_W, GROUP_W), layer),
        _const_spec((DEPTH, GROUP_W)), _const_spec((DEPTH, GROUP_W)), _const_spec((DEPTH, GROUP_W)),
        _layer_spec((GROUP_W, GROUP_W), layer),
        _layer_spec((HEADS, CHUNK, CHUNK), layer),
        _layer_spec((CHUNK, GROUP_W), layer),
        _layer_spec((HEADS, HEAD_D, HEAD_D), layer),
        _const_spec((DEPTH, GROUP_W)),
        pl.BlockSpec(memory_space=pltpu.SMEM),
    ]
    args = [x, mods, pp["g1"], pp["w_in"], pp["w_out"], pp["dw"], pp["cb"], pp["lng"], pp["lnb"],
            pp["pw"], pp["ws"], pp["gbias"], pp["pool_w"], pp["pscale"], pp["lg"]]
    out_shape = [jax.ShapeDtypeStruct((nb, TB, D_MODEL), F32)]
    out_specs = [pl.BlockSpec((None, TB, D_MODEL), lambda i: (i, 0, 0))]
    aliases = {}
    if rotate:
        in_specs += [_const_spec((TB, GROUP_W)), _const_spec((TB, GROUP_W)),
                     pl.BlockSpec((None, None, 2, HEADS, HEAD_D, HEAD_D), lambda i: (i, layer, 0, 0, 0, 0))]
        args += [rope_tabs[0], rope_tabs[1], s0]
    else:
        out_shape.append(jax.ShapeDtypeStruct((nb * n_seq, DEPTH, 2, HEADS, HEAD_D, HEAD_D), F32))
        if layer == 0:
            out_specs.append(pl.BlockSpec((n_seq, DEPTH, 2, HEADS, HEAD_D, HEAD_D),
                                          lambda i: (i, 0, 0, 0, 0, 0)))
        else:
            in_specs.append(pl.BlockSpec(memory_space=pl.ANY))
            args.append(states)
            aliases = {len(args) - 1: 1}
            out_specs.append(pl.BlockSpec((n_seq, None, 2, HEADS, HEAD_D, HEAD_D),
                                          lambda i: (i, layer, 0, 0, 0, 0)))
    scratch = [
        pltpu.VMEM((TB, D_MODEL), BF16),
        pltpu.VMEM((TB, IN_COLS), F32),
        pltpu.VMEM((n_seq, seq_len + 2 * PAD, GROUP_W), F32),
        pltpu.VMEM((n_seq, seq_len + 2 * PAD, GROUP_W), F32),
        pltpu.VMEM((TB, GROUP_W), F32),
        pltpu.VMEM((TB, GROUP_W), F32),
        pltpu.VMEM((2, CHUNK, HEADS * CHUNK), F32),
        pltpu.VMEM((2, CHUNK, GROUP_W), F32),
        pltpu.VMEM((2, CHUNK, GROUP_W), F32),
        pltpu.VMEM((2, 1, GROUP_W), F32),
        pltpu.VMEM((GROUP_W, GROUP_W), BF16),
        pltpu.VMEM((2, TB, GROUP_W), BF16),
        pltpu.VMEM((2, N_CHUNKS, HEAD_D, GROUP_W), F32),
        pltpu.VMEM((2, N_CHUNKS, GROUP_W, GROUP_W), BF16),
        pltpu.VMEM((GROUP_W, GROUP_W), BF16),
        pltpu.VMEM((CHUNK, HEADS * CHUNK), BF16),
        pltpu.VMEM((GROUP_W, GROUP_W), BF16),
        pltpu.VMEM((2, HEAD_D, GROUP_W), F32),
    ]
    outs = pl.pallas_call(
        functools.partial(_mixer_kernel, layer=layer, seq_len=seq_len, rotate=rotate,
                          mod_base=mod_base, mod_stride=mod_stride),
        grid=(nb,),
        in_specs=in_specs,
        out_specs=out_specs,
        out_shape=out_shape,
        scratch_shapes=scratch,
        input_output_aliases=aliases,
        compiler_params=pltpu.CompilerParams(
            dimension_semantics=("arbitrary",), vmem_limit_bytes=VMEM_LIMIT_BYTES),
        name="mixer_lat" if rotate else "mixer_ctx",
    )(*args)
    return outs


def _ffn_kernel(x_ref, mod_ref, g2_ref, w_in_hbm, w_out_hbm, gf_ref, y_ref,
                hbf_ref, act_ref, act3_ref, wgu_ref, wo_ref, stg_g_ref, stg_u_ref, stg_o_ref, sem,
                *, layer, final_norm, mod_base, mod_stride):
    pid = pl.program_id(0)
    mod = _mod_row(mod_ref, mod_base, mod_stride)
    sh2 = mod(3)
    sc2 = mod(4)
    ga2 = mod(5)
    g2 = g2_ref[layer:layer + 1, :]

    def tile_copies(j, slot):
        return (
            pltpu.make_async_copy(w_in_hbm.at[layer, :, pl.ds(j * FF_TILE, FF_TILE)],
                                  stg_g_ref.at[slot], sem.at[0, slot]),
            pltpu.make_async_copy(w_in_hbm.at[layer, :, pl.ds(D_FF + j * FF_TILE, FF_TILE)],
                                  stg_u_ref.at[slot], sem.at[1, slot]),
            pltpu.make_async_copy(w_out_hbm.at[layer, pl.ds(j * FF_TILE, FF_TILE), :],
                                  stg_o_ref.at[slot], sem.at[2, slot]),
        )

    def start_tile(j):
        for cp in tile_copies(j, j % FF_RING):
            cp.start()

    def hidden_tile(j, act_out):
        gate = jnp.dot(hbf_ref[...], wgu_ref[j], preferred_element_type=F32)
        up = jnp.dot(hbf_ref[...], wgu_ref[N_FF_TILES + j], preferred_element_type=F32)
        act_out((_silu(gate) * up).astype(BF16))

    @pl.when(pid == 0)
    def _():
        for j in range(FF_LOOKAHEAD):
            start_tile(j)

    def norm_body(i, _):
        r0 = pl.multiple_of(i * ROW_TILE, ROW_TILE)
        h = _norm_mod(x_ref[pl.ds(r0, ROW_TILE), :], g2, sc2, sh2)
        hbf_ref[pl.ds(r0, ROW_TILE), :] = h.astype(BF16)
        return 0
    lax.fori_loop(0, TB // ROW_TILE, norm_body, 0)

    @pl.when(pid == 0)
    def _():
        def fetch_body(j, _):
            @pl.when(j + FF_LOOKAHEAD < N_FF_TILES)
            def _():
                start_tile(j + FF_LOOKAHEAD)
            slot = j % FF_RING
            for cp in tile_copies(j, slot):
                cp.wait()
            wgu_ref[j] = stg_g_ref[slot].astype(BF16)
            wgu_ref[N_FF_TILES + j] = stg_u_ref[slot].astype(BF16)
            wo_ref[pl.ds(pl.multiple_of(j * FF_TILE, FF_TILE), FF_TILE), :] = stg_o_ref[slot].astype(BF16)

            def act_out(act):
                act3_ref[j] = act
            hidden_tile(j, act_out)
            return 0
        lax.fori_loop(0, N_FF_TILES, fetch_body, 0)
        for j in range(N_FF_TILES):
            act_ref[:, j * FF_TILE:(j + 1) * FF_TILE] = act3_ref[j]

    @pl.when(pid != 0)
    def _():
        for j in range(N_FF_TILES):
            def act_out(act, j=j):
                act_ref[:, j * FF_TILE:(j + 1) * FF_TILE] = act
            hidden_tile(j, act_out)

    for r0 in range(0, TB, FF_OUT_ROWS):
        rows = slice(r0, r0 + FF_OUT_ROWS)
        y = x_ref[rows, :] + ga2 * jnp.dot(act_ref[rows, :], wo_ref[...], preferred_element_type=F32)
        if final_norm:
            ms = jnp.mean(y * y, axis=-1, keepdims=True)
            y = y * lax.rsqrt(ms + EPS) * gf_ref[...]
        y_ref[rows, :] = y


def _ffn(x, mods, g2, w_ffn_in, w_ffn_out, g_final, *, layer, mod_base, mod_stride, final_norm):
    nb = x.shape[0]
    return pl.pallas_call(
        functools.partial(_ffn_kernel, layer=layer, final_norm=final_norm, mod_base=mod_base,
                          mod_stride=mod_stride),
        grid=(nb,),
        in_specs=[
            pl.BlockSpec((None, TB, D_MODEL), lambda i: (i, 0, 0)),
            _mod_spec(layer),
            _const_spec((DEPTH, D_MODEL)),
            pl.BlockSpec(memory_space=pl.ANY),
            pl.BlockSpec(memory_space=pl.ANY),
            _const_spec((1, D_MODEL)),
        ],
        out_specs=pl.BlockSpec((None, TB, D_MODEL), lambda i: (i, 0, 0)),
        out_shape=jax.ShapeDtypeStruct((nb, TB, D_MODEL), F32),
        scratch_shapes=[
            pltpu.VMEM((TB, D_MODEL), BF16),
            pltpu.VMEM((TB, D_FF), BF16),
            pltpu.VMEM((N_FF_TILES, TB, FF_TILE), BF16),
            pltpu.VMEM((2 * N_FF_TILES, D_MODEL, FF_TILE), BF16),
            pltpu.VMEM((D_FF, D_MODEL), BF16),
            pltpu.VMEM((FF_RING, D_MODEL, FF_TILE), F32),
            pltpu.VMEM((FF_RING, D_MODEL, FF_TILE), F32),
            pltpu.VMEM((FF_RING, FF_TILE, D_MODEL), F32),
            pltpu.SemaphoreType.DMA((3, FF_RING)),
        ],
        compiler_params=pltpu.CompilerParams(
            dimension_semantics=("arbitrary",), vmem_limit_bytes=VMEM_LIMIT_BYTES),
        name="ffn",
    )(x, mods, g2, w_ffn_in, w_ffn_out, g_final)


def _rope_tables(seq_len):
    t = np.arange(seq_len)
    r = (t // GRID_W).astype(np.float32)
    c = (t % GRID_W).astype(np.float32)
    nf = HEAD_D // 4
    inv = np.float32(ROPE_BASE) ** (-np.arange(nf, dtype=np.float32) / np.float32(nf))
    ang_r = r[:, None] * inv
    ang_c = c[:, None] * inv
    cos = np.concatenate([np.cos(ang_r), np.cos(ang_r), np.cos(ang_c), np.cos(ang_c)], axis=-1)
    sin = np.concatenate([-np.sin(ang_r), np.sin(ang_r), -np.sin(ang_c), np.sin(ang_c)], axis=-1)
    return (jnp.asarray(np.tile(cos, (1, HEADS)), dtype=F32), jnp.asarray(np.tile(sin, (1, HEADS)), dtype=F32))


def kernel(x_prompt, x_sample, state_ret, c, c_ctx, w_ada, b_ada, g_norm1, g_norm2, w_in, w_out, conv_dw,
           conv_b, conv_ln_g, conv_ln_b, conv_pw, gmlp_ws, gmlp_b, pool_w, pool_scale, ret_decay, w_ffn_in,
           w_ffn_out, g_final):
    batch, seq, _ = x_prompt.shape
    dec_batch, dec_seq, _ = x_sample.shape
    assert dec_seq == TB and TB % seq == 0 and (batch * seq) % TB == 0
    assert 1 + dec_batch <= MOD_ROWS

    cs = jnp.concatenate([c_ctx[None, :], c, jnp.zeros((MOD_ROWS - 1 - dec_batch, D_MODEL), F32)], axis=0)
    mods = _ada_rows(cs, w_ada, b_ada).reshape(DEPTH * MOD_ROWS, 6 * D_MODEL)
    rope_tabs = _rope_tables(dec_seq)
    g_final2 = g_final.reshape(1, D_MODEL)
    pp = {
        "g1": g_norm1,
        "g2": g_norm2,
        "w_in": w_in.astype(BF16),
        "w_out": w_out.astype(BF16),
        "dw": conv_dw,
        "cb": conv_b,
        "lng": conv_ln_g,
        "lnb": conv_ln_b,
        "pw": conv_pw,
        "ws": gmlp_ws,
        "gbias": jnp.repeat(jnp.swapaxes(gmlp_b, 1, 2), HEAD_D, axis=2),
        "pool_w": pool_w,
        "pscale": pool_scale,
        "lg": jax.nn.log_sigmoid(ret_decay.astype(F32)).reshape(DEPTH * 2 * HEADS),
    }

    xc = x_prompt.reshape(batch * seq // TB, TB, D_MODEL)
    xl = x_sample
    states = None
    for l in range(DEPTH):
        last = l == DEPTH - 1
        xc, states = _mixer(xc, mods, pp, layer=l, seq_len=seq, rotate=False, mod_base=0, mod_stride=0,
                            states=states)
        xc = _ffn(xc, mods, pp["g2"], w_ffn_in, w_ffn_out, g_final2, layer=l, mod_base=0, mod_stride=0,
                  final_norm=last)
        (xl,) = _mixer(xl, mods, pp, layer=l, seq_len=dec_seq, rotate=True, mod_base=1, mod_stride=1,
                       rope_tabs=rope_tabs, s0=state_ret.astype(F32))
        xl = _ffn(xl, mods, pp["g2"], w_ffn_in, w_ffn_out, g_final2, layer=l, mod_base=1, mod_stride=1,
                  final_norm=last)

    y_prompt = xc.reshape(batch, seq, D_MODEL)
    return (y_prompt, xl, states.astype(x_prompt.dtype))
```

```python
import functools

import jax
import jax.numpy as jnp
import numpy as np
from jax import lax
from jax.experimental import pallas as pl
from jax.experimental.pallas import tpu as pltpu

F32 = jnp.float32
BF16 = jnp.bfloat16

D_MODEL = 1024
DEPTH = 2
GRID_W = 64
GROUP_W = D_MODEL // 4
CONV_W = 31
CHUNK = 128
HEADS = 4
HEAD_D = GROUP_W // HEADS
POOL_WINDOWS = (2, 4, 8, 16)
POOL_GW = GROUP_W // len(POOL_WINDOWS)
ROPE_BASE = 10000.0
D_FF = 2816
IN_COLS = 11 * GROUP_W
EPS = 1e-6

LANES = 128
SUBLANES = 8

TB = 1024
N_CHUNKS = TB // CHUNK
ROW_TILE = 256
CONV_TILE = 64
N_CONV_TILES = TB // CONV_TILE
PAD = 16
POOL_HALO = max(POOL_WINDOWS) // 2
ROPE_PAIR = HEAD_D // 4
FF_TILE = 256
N_FF_TILES = D_FF // FF_TILE
FF_OUT_ROWS = 256
FF_LOOKAHEAD = 2
FF_RING = FF_LOOKAHEAD + 1
ADA_TILE = 3072
MOD_ROWS = 8
VMEM_LIMIT_BYTES = 60 * 1024 * 1024

COL_A, COL_B, COL_C, COL_D = 0, 2 * GROUP_W, 4 * GROUP_W, 5 * GROUP_W
COL_V, COL_G = COL_D + 4 * GROUP_W, COL_D + 5 * GROUP_W


def _sigmoid(x):
    return 1.0 / (1.0 + jnp.exp(-x))


def _silu(x):
    return x * _sigmoid(x)


def _norm_mod(x, g, scale, shift):
    ms = jnp.mean(x * x, axis=-1, keepdims=True)
    return (x * lax.rsqrt(ms + EPS) * g) * (1.0 + scale) + shift


def _head_stack(x, lane_head):
    return jnp.concatenate([jnp.where(lane_head == h, x, 0.0) for h in range(HEADS)], axis=0)


def _mod_row(mod_ref, mod_base, mod_stride):
    row = jnp.maximum(mod_base + mod_stride * pl.program_id(0), 0) if mod_stride else max(mod_base, 0)

    def part(k):
        return mod_ref[pl.ds(row, 1), k * D_MODEL:(k + 1) * D_MODEL]
    return part


def _const_spec(shape):
    zeros = (0,) * len(shape)
    return pl.BlockSpec(shape, lambda i: zeros, pipeline_mode=pl.Buffered(1))


def _ada_kernel(c_ref, w_ref, b_ref, o_ref):
    a = _silu(c_ref[...]).astype(BF16)
    bias = b_ref[pl.ds(pl.program_id(0), 1), :]
    o_ref[...] = jnp.dot(a, w_ref[...].astype(BF16), preferred_element_type=F32) + bias


def _ada_rows(cs, w_ada, b_ada):
    n_tiles = 6 * D_MODEL // ADA_TILE
    return pl.pallas_call(
        _ada_kernel,
        grid=(DEPTH, n_tiles),
        in_specs=[
            pl.BlockSpec((MOD_ROWS, D_MODEL), lambda l, j: (0, 0)),
            pl.BlockSpec((None, D_MODEL, ADA_TILE), lambda l, j: (l, 0, j)),
            pl.BlockSpec((DEPTH, ADA_TILE), lambda l, j: (0, j)),
        ],
        out_specs=pl.BlockSpec((None, MOD_ROWS, ADA_TILE), lambda l, j: (l, 0, j)),
        out_shape=jax.ShapeDtypeStruct((DEPTH, MOD_ROWS, 6 * D_MODEL), F32),
        compiler_params=pltpu.CompilerParams(
            dimension_semantics=("arbitrary", "arbitrary"), vmem_limit_bytes=VMEM_LIMIT_BYTES),
        name="ada_rows",
    )(cs, w_ada, b_ada)


def _mixer_kernel(*refs, layer, seq_len, rotate, mod_base, mod_stride):
    n_seq = TB // seq_len
    n_chunk = seq_len // CHUNK
    it = iter(refs)
    x_ref, mod_ref, g1_ref, w_in_ref, w_out_ref = (next(it) for _ in range(5))
    dw_ref, cb_ref, lng_ref, lnb_ref, pw_ref = (next(it) for _ in range(5))
    ws_ref, gbias_ref, poolw_ref, pscale_ref, lg_ref = (next(it) for _ in range(5))
    if rotate:
        cos_ref, sin_ref, s0_ref = (next(it) for _ in range(3))
    elif layer > 0:
        next(it)
    y_ref = next(it)
    if not rotate:
        st_ref = next(it)
    (hbf_ref, proj_ref, pad_ref, pad2_ref, tmp_ref, tmp2_ref,
     dcat_ref, qdec_ref, kdec_ref, sdec_ref, gmat_ref, qb_ref, upd_ref, sall_ref,
     pwb_ref, wcat_ref, pbd_ref, s0c_ref) = it
    cat_ref = hbf_ref
    of_ref = tmp_ref

    pwb_ref[...] = pw_ref[...].astype(BF16)
    pbd_ref[...] = jnp.zeros((GROUP_W, GROUP_W), BF16)
    for h in range(HEADS):
        wcat_ref[:, h * CHUNK:(h + 1) * CHUNK] = ws_ref[h].astype(BF16)
        pbd_ref[h * HEAD_D:(h + 1) * HEAD_D, h * HEAD_D:(h + 1) * HEAD_D] = poolw_ref[h].astype(BF16)
        if rotate:
            for d in range(2):
                s0c_ref[d, :, h * HEAD_D:(h + 1) * HEAD_D] = s0_ref[d, h]

    mod = _mod_row(mod_ref, mod_base, mod_stride)
    sh1 = mod(0)
    sc1 = mod(1)
    ga1 = mod(2)
    g1, cb, lng, lnb, pscale = (r[layer:layer + 1, :] for r in (g1_ref, cb_ref, lng_ref, lnb_ref, pscale_ref))

    lane = lax.broadcasted_iota(jnp.int32, (1, GROUP_W), 1)
    lane_head = lane // HEAD_D
    tiles_per_seq = seq_len // CONV_TILE
    chunks_per_seq = seq_len // CHUNK

    def seq_and_offset(i, per_seq, size):
        if n_seq == 1:
            return 0, pl.multiple_of(i * size, size)
        return i // per_seq, pl.multiple_of((i % per_seq) * size, size)

    normed = []
    for r0 in range(0, TB, ROW_TILE):
        h = _norm_mod(x_ref[r0:r0 + ROW_TILE, :], g1, sc1, sh1).astype(BF16)
        hbf_ref[r0:r0 + ROW_TILE, :] = h
        normed.append(h)
    proj_ref[:, COL_A:COL_B] = jnp.dot(
        jnp.concatenate(normed, axis=0), w_in_ref[:, COL_A:COL_B], preferred_element_type=F32)

    for s in range(n_seq):
        for ref in (pad_ref, pad2_ref):
            ref[s, 0:PAD, :] = jnp.zeros((PAD, GROUP_W), F32)
            ref[s, PAD + seq_len:PAD + seq_len + PAD, :] = jnp.zeros((PAD, GROUP_W), F32)

    def glu_body(i, _):
        s, t0 = seq_and_offset(i, chunks_per_seq, CHUNK)
        r0 = pl.multiple_of(i * CHUNK, CHUNK)
        a1 = proj_ref[pl.ds(r0, CHUNK), COL_A:COL_A + GROUP_W]
        a2 = proj_ref[pl.ds(r0, CHUNK), COL_A + GROUP_W:COL_B]
        pad_ref[s, pl.ds(pl.multiple_of(t0 + PAD, SUBLANES), CHUNK), :] = a1 * _sigmoid(a2)
        return 0
    lax.fori_loop(0, N_CHUNKS, glu_body, 0, unroll=True)

    win = CONV_TILE + 2 * PAD

    def conv_tile(i):
        s, t0 = divmod(i, tiles_per_seq)
        t0 *= CONV_TILE
        for c0 in range(0, GROUP_W, LANES):
            w = pad_ref[s, t0:t0 + win, c0:c0 + LANES]
            acc = jnp.zeros((CONV_TILE, LANES), F32) + cb[:, c0:c0 + LANES]
            for b in range(SUBLANES):
                wb = w if b == 0 else pltpu.roll(w, win - b, axis=0)
                for a in range(2 * PAD // SUBLANES):
                    k = SUBLANES * a + b - 1
                    if 0 <= k < CONV_W:
                        acc = acc + dw_ref[k:k + 1, c0:c0 + LANES] * wb[SUBLANES * a:SUBLANES * a + CONV_TILE, :]
            tmp_ref[i * CONV_TILE:(i + 1) * CONV_TILE, c0:c0 + LANES] = acc

    assert POOL_WINDOWS == (2, 4, 8, 16) and 2 * POOL_GW == LANES and POOL_HALO == SUBLANES
    pwin = CONV_TILE + 2 * POOL_HALO
    first_half = lax.broadcasted_iota(jnp.int32, (1, LANES), 1) < POOL_GW

    def pool_tile(i):
        s, t0 = divmod(i, tiles_per_seq)
        t0 *= CONV_TILE
        interior = t0 >= POOL_HALO and t0 + CONV_TILE + POOL_HALO <= seq_len
        outs = []
        for col, (w_small, w_big) in enumerate((POOL_WINDOWS[0:2], POOL_WINDOWS[2:4])):
            w = pad2_ref[s, t0 + PAD - POOL_HALO:t0 + PAD - POOL_HALO + pwin,
                         col * LANES:(col + 1) * LANES]

            rolled = {0: w}

            def shifted(b, w=w, rolled=rolled):
                if b not in rolled:
                    rolled[b] = pltpu.roll(w, pwin - b, axis=0)
                return rolled[b]

            def lo(b):
                return shifted(b)[0:CONV_TILE, :]

            def hi(b):
                return shifted(b)[POOL_HALO:POOL_HALO + CONV_TILE, :]

            tok = hi(0)
            if col == 0:
                s_small = lo(7) + tok
                s_big = s_small + lo(6) + hi(1)
            else:
                s_small = lo(7) + tok + lo(6) + hi(1) + lo(5) + lo(4) + hi(2) + hi(3)
                s_big = s_small + lo(3) + lo(2) + lo(1) + lo(0) + hi(4) + hi(5) + hi(6) + hi(7)
            ssum = jnp.where(first_half, s_small, s_big)
            if interior:
                mean = ssum * jnp.where(first_half, 1.0 / w_small, 1.0 / w_big)
            else:
                half = jnp.where(first_half, w_small // 2, w_big // 2)
                tpos = t0 + lax.broadcasted_iota(jnp.int32, (CONV_TILE, LANES), 0)
                cnt = jnp.minimum(tpos + half, seq_len) - jnp.maximum(tpos - half, 0)
                mean = ssum / cnt.astype(F32)
            outs.append(mean - tok)
        tmp2_ref[i * CONV_TILE:(i + 1) * CONV_TILE, :] = jnp.concatenate(outs, axis=1)

    col_tiles = [COL_C] + list(range(COL_B, COL_C, GROUP_W)) + list(range(COL_D, IN_COLS, GROUP_W))
    per_step = -(-N_CONV_TILES // (len(col_tiles) - 1))
    conv_next = 0
    pool_next = 0
    for step, c0 in enumerate(col_tiles):
        proj_ref[:, c0:c0 + GROUP_W] = jnp.dot(
            hbf_ref[...], w_in_ref[:, c0:c0 + GROUP_W], preferred_element_type=F32)
        if step == 1:
            for s in range(n_seq):
                pad2_ref[s, PAD:PAD + seq_len, :] = proj_ref[s * seq_len:(s + 1) * seq_len, COL_C:COL_D]
        for _ in range(per_step):
            if conv_next < N_CONV_TILES:
                conv_tile(conv_next)
                conv_next += 1
            if step >= 1 and pool_next < N_CONV_TILES:
                pool_tile(pool_next)
                pool_next += 1
    assert conv_next == N_CONV_TILES and pool_next == N_CONV_TILES

    c = tmp_ref[...]
    mu = jnp.mean(c, axis=-1, keepdims=True)
    cen = c - mu
    var = jnp.mean(cen * cen, axis=-1, keepdims=True)
    hn = cen * lax.rsqrt(var + EPS) * lng + lnb
    ya = jnp.dot(_silu(hn).astype(BF16), pwb_ref[...], preferred_element_type=F32)
    cat_ref[:, 0:GROUP_W] = ya.astype(BF16)

    def gmlp_body(i, _):
        r0 = pl.multiple_of(i * CHUNK, CHUNK)
        u = proj_ref[pl.ds(r0, CHUNK), COL_B:COL_B + GROUP_W]
        v = proj_ref[pl.ds(r0, CHUNK), COL_B + GROUP_W:COL_C]
        vstack = _head_stack(v, lane_head).astype(BF16)
        sg = jnp.dot(wcat_ref[...], vstack, preferred_element_type=F32) + gbias_ref[...]
        cat_ref[pl.ds(r0, CHUNK), GROUP_W:2 * GROUP_W] = (u * sg).astype(BF16)
        return 0
    lax.fori_loop(0, N_CHUNKS, gmlp_body, 0, unroll=True)

    yc = jnp.dot(tmp2_ref[...].astype(BF16), pbd_ref[...], preferred_element_type=F32) * pscale
    cat_ref[:, 2 * GROUP_W:3 * GROUP_W] = yc.astype(BF16)

    ri = lax.broadcasted_iota(jnp.int32, (CHUNK, HEADS * CHUNK), 0)
    ci = lax.broadcasted_iota(jnp.int32, (CHUNK, HEADS * CHUNK), 1)
    cj = ci % CHUNK
    chead = ci // CHUNK
    rq = lax.broadcasted_iota(jnp.int32, (CHUNK, GROUP_W), 0).astype(F32)
    for d in range(2):
        lgs = [lg_ref[(2 * layer + d) * HEADS + h] for h in range(HEADS)]
        lg_wide = jnp.where(chead == 0, lgs[0], jnp.where(chead == 1, lgs[1],
                                                          jnp.where(chead == 2, lgs[2], lgs[3])))
        lg_lane = jnp.where(lane_head == 0, lgs[0], jnp.where(lane_head == 1, lgs[1],
                                                              jnp.where(lane_head == 2, lgs[2], lgs[3])))
        dist = (ri - cj) if d == 0 else (cj - ri)
        keep = dist >= 0
        dcat_ref[d] = jnp.where(keep, jnp.exp(jnp.where(keep, dist, 0).astype(F32) * lg_wide), 0.0)
        if d == 0:
            qdec_ref[d] = jnp.exp((rq + 1.0) * lg_lane)
            kdec_ref[d] = jnp.exp((CHUNK - 1.0 - rq) * lg_lane)
        else:
            qdec_ref[d] = jnp.exp((CHUNK - rq) * lg_lane)
            kdec_ref[d] = jnp.exp(rq * lg_lane)
        sdec_ref[d] = jnp.exp(float(CHUNK) * lg_lane)

    rr = lax.broadcasted_iota(jnp.int32, (GROUP_W, GROUP_W), 0) // HEAD_D
    cc = lax.broadcasted_iota(jnp.int32, (GROUP_W, GROUP_W), 1) // HEAD_D
    gmat_ref[...] = jnp.where(rr == cc, 1.0 / HEAD_D, 0.0).astype(BF16)

    lane_bit = (lane & ROPE_PAIR) == 0
    k_scale = HEAD_D ** -0.5

    def rope(z, r0):
        if not rotate:
            return z
        cos = cos_ref[pl.ds(r0, CHUNK), :]
        sin = sin_ref[pl.ds(r0, CHUNK), :]
        halves = []
        for c0 in range(0, GROUP_W, LANES):
            zz = z[:, c0:c0 + LANES]
            partner = jnp.where(lane_bit[:, c0:c0 + LANES],
                                pltpu.roll(zz, LANES - ROPE_PAIR, axis=1), pltpu.roll(zz, ROPE_PAIR, axis=1))
            halves.append(partner)
        return z * cos + jnp.concatenate(halves, axis=1) * sin

    def pair_stack(zb, pair):
        zero = jnp.zeros_like(zb)
        return jnp.concatenate([jnp.where(lane_head == h, zb, zero) for h in (2 * pair, 2 * pair + 1)], axis=0)

    def intra_body(c, _):
        r0 = pl.multiple_of(c * CHUNK, CHUNK)
        v = proj_ref[pl.ds(r0, CHUNK), COL_V:COL_V + GROUP_W]
        vb = v.astype(BF16)
        atts = [[None, None], [None, None]]
        kds = []
        for d in range(2):
            qc0 = COL_D + 2 * d * GROUP_W
            q = rope(proj_ref[pl.ds(r0, CHUNK), qc0:qc0 + GROUP_W], r0)
            k = rope(proj_ref[pl.ds(r0, CHUNK), qc0 + GROUP_W:qc0 + 2 * GROUP_W], r0) * k_scale
            qb = q.astype(BF16)
            kb = k.astype(BF16)
            qb_ref[d, pl.ds(r0, CHUNK), :] = qb
            for pair in range(2):
                att = lax.dot_general(qb, pair_stack(kb, pair), (((1,), (1,)), ((), ())),
                                      preferred_element_type=F32)
                atts[pair][d] = (att * dcat_ref[d, :, pair * 2 * CHUNK:(pair + 1) * 2 * CHUNK]).astype(BF16)
            kds.append((k * kdec_ref[d]).astype(BF16))
        upd = lax.dot_general(jnp.concatenate(kds, axis=1), vb, (((0,), (0,)), ((), ())),
                              preferred_element_type=F32)
        for d in range(2):
            compact = None
            for h in range(HEADS):
                r_h = d * GROUP_W + h * HEAD_D
                part = jnp.where(lane_head == h, upd[r_h:r_h + HEAD_D, :], 0.0)
                compact = part if compact is None else compact + part
            upd_ref[d, c] = compact
        o = None
        for pair in range(2):
            both = jnp.dot(jnp.concatenate(atts[pair], axis=0), pair_stack(vb, pair), preferred_element_type=F32)
            od = both[0:CHUNK, :] + both[CHUNK:2 * CHUNK, :]
            o = od if o is None else o + od
        of_ref[pl.ds(r0, CHUNK), :] = o
        return 0
    lax.fori_loop(0, N_CHUNKS, intra_body, 0, unroll=True)

    for s in range(n_seq):
        for d in range(2):
            st = s0c_ref[d] if rotate else jnp.zeros((HEAD_D, GROUP_W), F32)
            order = range(n_chunk) if d == 0 else range(n_chunk - 1, -1, -1)
            for c in order:
                cg = s * n_chunk + c
                for h in range(HEADS):
                    sall_ref[d, cg, h * HEAD_D:(h + 1) * HEAD_D, :] = (
                        jnp.where(lane_head == h, st, 0.0).astype(BF16))
                st = st * sdec_ref[d] + upd_ref[d, cg]
            if not rotate:
                for h in range(HEADS):
                    if layer == 0:
                        st_ref[s, 0, d, h] = st[:, h * HEAD_D:(h + 1) * HEAD_D]
                    else:
                        st_ref[s, d, h] = st[:, h * HEAD_D:(h + 1) * HEAD_D]
    if not rotate and layer == 0:
        for s in range(n_seq):
            for later in range(1, DEPTH):
                for d in range(2):
                    for h in range(HEADS):
                        st_ref[s, later, d, h] = jnp.zeros((HEAD_D, HEAD_D), F32)

    def cross_body(c, _):
        r0 = pl.multiple_of(c * CHUNK, CHUNK)
        o = of_ref[pl.ds(r0, CHUNK), :]
        for d in range(2):
            o = o + jnp.dot(qb_ref[d, pl.ds(r0, CHUNK), :], sall_ref[d, c],
                            preferred_element_type=F32) * qdec_ref[d]
        of_ref[pl.ds(r0, CHUNK), :] = o
        return 0
    lax.fori_loop(0, N_CHUNKS, cross_body, 0, unroll=True)

    o = of_ref[...]
    gmat = gmat_ref[...]
    o_hi = o.astype(BF16)
    o_lo = (o - o_hi.astype(F32)).astype(BF16)
    mu_h = jnp.dot(o_hi, gmat, preferred_element_type=F32) + jnp.dot(o_lo, gmat, preferred_element_type=F32)
    cen_h = o - mu_h
    var_h = jnp.dot((cen_h * cen_h).astype(BF16), gmat, preferred_element_type=F32)
    gate = proj_ref[:, COL_G:COL_G + GROUP_W]
    cat_ref[:, 3 * GROUP_W:4 * GROUP_W] = (_silu(gate) * (cen_h * lax.rsqrt(var_h + EPS))).astype(BF16)

    y_ref[...] = x_ref[...] + ga1 * jnp.dot(cat_ref[...], w_out_ref[...], preferred_element_type=F32)


def _layer_spec(shape, layer):
    zeros = (0,) * len(shape)
    return pl.BlockSpec((None,) + tuple(shape), lambda i: (layer,) + zeros, pipeline_mode=pl.Buffered(1))


def _mod_spec(layer):
    return pl.BlockSpec((MOD_ROWS, 6 * D_MODEL), lambda i: (layer, 0), pipeline_mode=pl.Buffered(1))


def _mixer(x, mods, pp, *, layer, seq_len, rotate, mod_base, mod_stride, rope_tabs=None, s0=None, states=None):
    nb = x.shape[0]
    n_seq = TB // seq_len
    in_specs = [
        pl.BlockSpec((None, TB, D_MODEL), lambda i: (i, 0, 0)),
        _mod_spec(layer),
        _const_spec((DEPTH, D_MODEL)),
        _layer_spec((D_MODEL, IN_COLS), layer),
        _layer_spec((D_MODEL, D_MODEL), layer),
        _layer_spec((CONV_W, GROUP_W), layer),
        _const_spec((DEPTH, GROUP_W)), _const_spec((DEPTH, GROUP_W)), _const_spec((DEPTH, GROUP_W)),
        _layer_spec((GROUP_W, GROUP_W), layer),
        _layer_spec((HEADS, CHUNK, CHUNK), layer),
        _layer_spec((CHUNK, GROUP_W), layer),
        _layer_spec((HEADS, HEAD_D, HEAD_D), layer),
        _const_spec((DEPTH, GROUP_W)),
        pl.BlockSpec(memory_space=pltpu.SMEM),
    ]
    args = [x, mods, pp["g1"], pp["w_in"], pp["w_out"], pp["dw"], pp["cb"], pp["lng"], pp["lnb"],
            pp["pw"], pp["ws"], pp["gbias"], pp["pool_w"], pp["pscale"], pp["lg"]]
    out_shape = [jax.ShapeDtypeStruct((nb, TB, D_MODEL), F32)]
    out_specs = [pl.BlockSpec((None, TB, D_MODEL), lambda i: (i, 0, 0))]
    aliases = {}
    if rotate:
        in_specs += [_const_spec((TB, GROUP_W)), _const_spec((TB, GROUP_W)),
                     pl.BlockSpec((None, None, 2, HEADS, HEAD_D, HEAD_D), lambda i: (i, layer, 0, 0, 0, 0))]
        args += [rope_tabs[0], rope_tabs[1], s0]
    else:
        out_shape.append(jax.ShapeDtypeStruct((nb * n_seq, DEPTH, 2, HEADS, HEAD_D, HEAD_D), F32))
        if layer == 0:
            out_specs.append(pl.BlockSpec((n_seq, DEPTH, 2, HEADS, HEAD_D, HEAD_D),
                                          lambda i: (i, 0, 0, 0, 0, 0)))
        else:
            in_specs.append(pl.BlockSpec(memory_space=pl.ANY))
            args.append(states)
            aliases = {len(args) - 1: 1}
            out_specs.append(pl.BlockSpec((n_seq, None, 2, HEADS, HEAD_D, HEAD_D),
                                          lambda i: (i, layer, 0, 0, 0, 0)))
    scratch = [
        pltpu.VMEM((TB, D_MODEL), BF16),
        pltpu.VMEM((TB, IN_COLS), F32),
        pltpu.VMEM((n_seq, seq_len + 2 * PAD, GROUP_W), F32),
        pltpu.VMEM((n_seq, seq_len + 2 * PAD, GROUP_W), F32),
        pltpu.VMEM((TB, GROUP_W), F32),
        pltpu.VMEM((TB, GROUP_W), F32),
        pltpu.VMEM((2, CHUNK, HEADS * CHUNK), F32),
        pltpu.VMEM((2, CHUNK, GROUP_W), F32),
        pltpu.VMEM((2, CHUNK, GROUP_W), F32),
        pltpu.VMEM((2, 1, GROUP_W), F32),
        pltpu.VMEM((GROUP_W, GROUP_W), BF16),
        pltpu.VMEM((2, TB, GROUP_W), BF16),
        pltpu.VMEM((2, N_CHUNKS, HEAD_D, GROUP_W), F32),
        pltpu.VMEM((2, N_CHUNKS, GROUP_W, GROUP_W), BF16),
        pltpu.VMEM((GROUP_W, GROUP_W), BF16),
        pltpu.VMEM((CHUNK, HEADS * CHUNK), BF16),
        pltpu.VMEM((GROUP_W, GROUP_W), BF16),
        pltpu.VMEM((2, HEAD_D, GROUP_W), F32),
    ]
    outs = pl.pallas_call(
        functools.partial(_mixer_kernel, layer=layer, seq_len=seq_len, rotate=rotate,
                          mod_base=mod_base, mod_stride=mod_stride),
        grid=(nb,),
        in_specs=in_specs,
        out_specs=out_specs,
        out_shape=out_shape,
        scratch_shapes=scratch,
        input_output_aliases=aliases,
        compiler_params=pltpu.CompilerParams(
            dimension_semantics=("arbitrary",), vmem_limit_bytes=VMEM_LIMIT_BYTES),
        name="mixer_lat" if rotate else "mixer_ctx",
    )(*args)
    return outs


def _ffn_kernel(x_ref, mod_ref, g2_ref, w_in_hbm, w_out_hbm, gf_ref, y_ref,
                hbf_ref, act_ref, act3_ref, wgu_ref, wo_ref, stg_g_ref, stg_u_ref, stg_o_ref, sem,
                *, layer, final_norm, mod_base, mod_stride):
    pid = pl.program_id(0)
    mod = _mod_row(mod_ref, mod_base, mod_stride)
    sh2 = mod(3)
    sc2 = mod(4)
    ga2 = mod(5)
    g2 = g2_ref[layer:layer + 1, :]

    def tile_copies(j, slot):
        return (
            pltpu.make_async_copy(w_in_hbm.at[layer, :, pl.ds(j * FF_TILE, FF_TILE)],
                                  stg_g_ref.at[slot], sem.at[0, slot]),
            pltpu.make_async_copy(w_in_hbm.at[layer, :, pl.ds(D_FF + j * FF_TILE, FF_TILE)],
                                  stg_u_ref.at[slot], sem.at[1, slot]),
            pltpu.make_async_copy(w_out_hbm.at[layer, pl.ds(j * FF_TILE, FF_TILE), :],
                                  stg_o_ref.at[slot], sem.at[2, slot]),
        )

    def start_tile(j):
        for cp in tile_copies(j, j % FF_RING):
            cp.start()

    def hidden_tile(j, act_out):
        gate = jnp.dot(hbf_ref[...], wgu_ref[j], preferred_element_type=F32)
        up = jnp.dot(hbf_ref[...], wgu_ref[N_FF_TILES + j], preferred_element_type=F32)
        act_out((_silu(gate) * up).astype(BF16))

    @pl.when(pid == 0)
    def _():
        for j in range(FF_LOOKAHEAD):
            start_tile(j)

    def norm_body(i, _):
        r0 = pl.multiple_of(i * ROW_TILE, ROW_TILE)
        h = _norm_mod(x_ref[pl.ds(r0, ROW_TILE), :], g2, sc2, sh2)
        hbf_ref[pl.ds(r0, ROW_TILE), :] = h.astype(BF16)
        return 0
    lax.fori_loop(0, TB // ROW_TILE, norm_body, 0)

    @pl.when(pid == 0)
    def _():
        def fetch_body(j, _):
            @pl.when(j + FF_LOOKAHEAD < N_FF_TILES)
            def _():
                start_tile(j + FF_LOOKAHEAD)
            slot = j % FF_RING
            for cp in tile_copies(j, slot):
                cp.wait()
            wgu_ref[j] = stg_g_ref[slot].astype(BF16)
            wgu_ref[N_FF_TILES + j] = stg_u_ref[slot].astype(BF16)
            wo_ref[pl.ds(pl.multiple_of(j * FF_TILE, FF_TILE), FF_TILE), :] = stg_o_ref[slot].astype(BF16)

            def act_out(act):
                act3_ref[j] = act
            hidden_tile(j, act_out)
            return 0
        lax.fori_loop(0, N_FF_TILES, fetch_body, 0)
        for j in range(N_FF_TILES):
            act_ref[:, j * FF_TILE:(j + 1) * FF_TILE] = act3_ref[j]

    @pl.when(pid != 0)
    def _():
        for j in range(N_FF_TILES):
            def act_out(act, j=j):
                act_ref[:, j * FF_TILE:(j + 1) * FF_TILE] = act
            hidden_tile(j, act_out)

    for r0 in range(0, TB, FF_OUT_ROWS):
        rows = slice(r0, r0 + FF_OUT_ROWS)
        y = x_ref[rows, :] + ga2 * jnp.dot(act_ref[rows, :], wo_ref[...], preferred_element_type=F32)
        if final_norm:
            ms = jnp.mean(y * y, axis=-1, keepdims=True)
            y = y * lax.rsqrt(ms + EPS) * gf_ref[...]
        y_ref[rows, :] = y


def _ffn(x, mods, g2, w_ffn_in, w_ffn_out, g_final, *, layer, mod_base, mod_stride, final_norm):
    nb = x.shape[0]
    return pl.pallas_call(
        functools.partial(_ffn_kernel, layer=layer, final_norm=final_norm, mod_base=mod_base,
                          mod_stride=mod_stride),
        grid=(nb,),
        in_specs=[
            pl.BlockSpec((None, TB, D_MODEL), lambda i: (i, 0, 0)),
            _mod_spec(layer),
            _const_spec((DEPTH, D_MODEL)),
            pl.BlockSpec(memory_space=pl.ANY),
            pl.BlockSpec(memory_space=pl.ANY),
            _const_spec((1, D_MODEL)),
        ],
        out_specs=pl.BlockSpec((None, TB, D_MODEL), lambda i: (i, 0, 0)),
        out_shape=jax.ShapeDtypeStruct((nb, TB, D_MODEL), F32),
        scratch_shapes=[
            pltpu.VMEM((TB, D_MODEL), BF16),
            pltpu.VMEM((TB, D_FF), BF16),
            pltpu.VMEM((N_FF_TILES, TB, FF_TILE), BF16),
            pltpu.VMEM((2 * N_FF_TILES, D_MODEL, FF_TILE), BF16),
            pltpu.VMEM((D_FF, D_MODEL), BF16),
            pltpu.VMEM((FF_RING, D_MODEL, FF_TILE), F32),
            pltpu.VMEM((FF_RING, D_MODEL, FF_TILE), F32),
            pltpu.VMEM((FF_RING, FF_TILE, D_MODEL), F32),
            pltpu.SemaphoreType.DMA((3, FF_RING)),
        ],
        compiler_params=pltpu.CompilerParams(
            dimension_semantics=("arbitrary",), vmem_limit_bytes=VMEM_LIMIT_BYTES),
        name="ffn",
    )(x, mods, g2, w_ffn_in, w_ffn_out, g_final)


def _rope_tables(seq_len):
    t = np.arange(seq_len)
    r = (t // GRID_W).astype(np.float32)
    c = (t % GRID_W).astype(np.float32)
    nf = HEAD_D // 4
    inv = np.float32(ROPE_BASE) ** (-np.arange(nf, dtype=np.float32) / np.float32(nf))
    ang_r = r[:, None] * inv
    ang_c = c[:, None] * inv
    cos = np.concatenate([np.cos(ang_r), np.cos(ang_r), np.cos(ang_c), np.cos(ang_c)], axis=-1)
    sin = np.concatenate([-np.sin(ang_r), np.sin(ang_r), -np.sin(ang_c), np.sin(ang_c)], axis=-1)
    return (jnp.asarray(np.tile(cos, (1, HEADS)), dtype=F32), jnp.asarray(np.tile(sin, (1, HEADS)), dtype=F32))


def kernel(x_prompt, x_sample, state_ret, c, c_ctx, w_ada, b_ada, g_norm1, g_norm2, w_in, w_out, conv_dw,
           conv_b, conv_ln_g, conv_ln_b, conv_pw, gmlp_ws, gmlp_b, pool_w, pool_scale, ret_decay, w_ffn_in,
           w_ffn_out, g_final):
    batch, seq, _ = x_prompt.shape
    dec_batch, dec_seq, _ = x_sample.shape
    assert dec_seq == TB and TB % seq == 0 and (batch * seq) % TB == 0
    assert 1 + dec_batch <= MOD_ROWS

    cs = jnp.concatenate([c_ctx[None, :], c, jnp.zeros((MOD_ROWS - 1 - dec_batch, D_MODEL), F32)], axis=0)
    mods = _ada_rows(cs, w_ada, b_ada).reshape(DEPTH * MOD_ROWS, 6 * D_MODEL)
    rope_tabs = _rope_tables(dec_seq)
    g_final2 = g_final.reshape(1, D_MODEL)
    pp = {
        "g1": g_norm1,
        "g2": g_norm2,
        "w_in": w_in.astype(BF16),
        "w_out": w_out.astype(BF16),
        "dw": conv_dw,
        "cb": conv_b,
        "lng": conv_ln_g,
        "lnb": conv_ln_b,
        "pw": conv_pw,
        "ws": gmlp_ws,
        "gbias": jnp.repeat(jnp.swapaxes(gmlp_b, 1, 2), HEAD_D, axis=2),
        "pool_w": pool_w,
        "pscale": pool_scale,
        "lg": jax.nn.log_sigmoid(ret_decay.astype(F32)).reshape(DEPTH * 2 * HEADS),
    }

    xc = x_prompt.reshape(batch * seq // TB, TB, D_MODEL)
    xl = x_sample
    states = None
    for l in range(DEPTH):
        last = l == DEPTH - 1
        xc, states = _mixer(xc, mods, pp, layer=l, seq_len=seq, rotate=False, mod_base=0, mod_stride=0,
                            states=states)
        xc = _ffn(xc, mods, pp["g2"], w_ffn_in, w_ffn_out, g_final2, layer=l, mod_base=0, mod_stride=0,
                  final_norm=last)
        (xl,) = _mixer(xl, mods, pp, layer=l, seq_len=dec_seq, rotate=True, mod_base=1, mod_stride=1,
                       rope_tabs=rope_tabs, s0=state_ret.astype(F32))
        xl = _ffn(xl, mods, pp["g2"], w_ffn_in, w_ffn_out, g_final2, layer=l, mod_base=1, mod_stride=1,
                  final_norm=last)

    y_prompt = xc.reshape(batch, seq, D_MODEL)
    return (y_prompt, xl, states.astype(x_prompt.dtype))
```

```python
import functools

import jax
import jax.numpy as jnp
import numpy as np
from jax import lax
from jax.experimental import pallas as pl
from jax.experimental.pallas import tpu as pltpu

F32 = jnp.float32
BF16 = jnp.bfloat16

D_MODEL = 1024
DEPTH = 2
GRID_W = 64
GROUP_W = D_MODEL // 4
CONV_W = 31
CHUNK = 128
HEADS = 4
HEAD_D = GROUP_W // HEADS
POOL_WINDOWS = (2, 4, 8, 16)
POOL_GW = GROUP_W // len(POOL_WINDOWS)
ROPE_BASE = 10000.0
D_FF = 2816
IN_COLS = 11 * GROUP_W
EPS = 1e-6

LANES = 128
SUBLANES = 8

TB = 1024
N_CHUNKS = TB // CHUNK
ROW_TILE = 256
CONV_TILE = 64
N_CONV_TILES = TB // CONV_TILE
PAD = 16
POOL_HALO = max(POOL_WINDOWS) // 2
ROPE_PAIR = HEAD_D // 4
FF_TILE = 256
N_FF_TILES = D_FF // FF_TILE
FF_OUT_ROWS = 256
FF_LOOKAHEAD = 2
FF_RING = FF_LOOKAHEAD + 1
ADA_TILE = 1536
MOD_ROWS = 8
VMEM_LIMIT_BYTES = 60 * 1024 * 1024

COL_A, COL_B, COL_C, COL_D = 0, 2 * GROUP_W, 4 * GROUP_W, 5 * GROUP_W
COL_V, COL_G = COL_D + 4 * GROUP_W, COL_D + 5 * GROUP_W


def _sigmoid(x):
    return 1.0 / (1.0 + jnp.exp(-x))


def _silu(x):
    return x * _sigmoid(x)


def _norm_mod(x, gain_scale, shift):
    ms = jnp.mean(x * x, axis=-1, keepdims=True)
    return x * lax.rsqrt(ms + EPS) * gain_scale + shift


def _head_stack(x, lane_head):
    return jnp.concatenate([jnp.where(lane_head == h, x, 0.0) for h in range(HEADS)], axis=0)


def _mod_row(mod_ref, mod_base, mod_stride):
    row = jnp.maximum(mod_base + mod_stride * pl.program_id(0), 0) if mod_stride else max(mod_base, 0)

    def part(k):
        return mod_ref[pl.ds(row, 1), k * D_MODEL:(k + 1) * D_MODEL]
    return part


def _const_spec(shape):
    zeros = (0,) * len(shape)
    return pl.BlockSpec(shape, lambda i: zeros, pipeline_mode=pl.Buffered(1))


def _ada_kernel(c_ref, w_ref, b_ref, o_ref):
    a = _silu(c_ref[...]).astype(BF16)
    bias = b_ref[pl.ds(pl.program_id(0), 1), :]
    o_ref[...] = jnp.dot(a, w_ref[...].astype(BF16), preferred_element_type=F32) + bias


def _ada_rows(cs, w_ada, b_ada):
    n_tiles = 6 * D_MODEL // ADA_TILE
    return pl.pallas_call(
        _ada_kernel,
        grid=(DEPTH, n_tiles),
        in_specs=[
            pl.BlockSpec((MOD_ROWS, D_MODEL), lambda l, j: (0, 0)),
            pl.BlockSpec((None, D_MODEL, ADA_TILE), lambda l, j: (l, 0, j)),
            pl.BlockSpec((DEPTH, ADA_TILE), lambda l, j: (0, j)),
        ],
        out_specs=pl.BlockSpec((None, MOD_ROWS, ADA_TILE), lambda l, j: (l, 0, j)),
        out_shape=jax.ShapeDtypeStruct((DEPTH, MOD_ROWS, 6 * D_MODEL), F32),
        compiler_params=pltpu.CompilerParams(
            dimension_semantics=("arbitrary", "arbitrary"), vmem_limit_bytes=VMEM_LIMIT_BYTES),
        name="ada_rows",
    )(cs, w_ada, b_ada)


def _mixer_kernel(*refs, layer, seq_len, rotate, mod_base, mod_stride):
    n_seq = TB // seq_len
    n_chunk = seq_len // CHUNK
    it = iter(refs)
    x_ref, mod_ref, g1_ref, w_in_ref, w_out_ref = (next(it) for _ in range(5))
    dw_ref, cb_ref, lng_ref, lnb_ref, pw_ref = (next(it) for _ in range(5))
    ws_ref, gbias_ref, poolw_ref, pscale_ref, lg_ref = (next(it) for _ in range(5))
    if rotate:
        cos_ref, sin_ref, s0_ref = (next(it) for _ in range(3))
    elif layer > 0:
        next(it)
    y_ref = next(it)
    if not rotate:
        st_ref = next(it)
    (hbf_ref, proj_ref, pad_ref, pad2_ref, tmp_ref, tmp2_ref,
     dcat_ref, qdec_ref, kdec_ref, sdec_ref, gmat_ref, qb_ref, upd_ref, sall_ref,
     pwb_ref, wcat_ref, pbd_ref, s0c_ref) = it
    cat_ref = hbf_ref
    of_ref = tmp_ref

    pwb_ref[...] = pw_ref[...].astype(BF16)
    pbd_ref[...] = jnp.zeros((GROUP_W, GROUP_W), BF16)
    for h in range(HEADS):
        wcat_ref[:, h * CHUNK:(h + 1) * CHUNK] = ws_ref[h].astype(BF16)
        pbd_ref[h * HEAD_D:(h + 1) * HEAD_D, h * HEAD_D:(h + 1) * HEAD_D] = poolw_ref[h].astype(BF16)
        if rotate:
            for d in range(2):
                s0c_ref[d, :, h * HEAD_D:(h + 1) * HEAD_D] = s0_ref[d, h]

    mod = _mod_row(mod_ref, mod_base, mod_stride)
    sh1 = mod(0)
    sc1 = mod(1)
    ga1 = mod(2)
    g1, cb, lng, lnb, pscale = (r[layer:layer + 1, :] for r in (g1_ref, cb_ref, lng_ref, lnb_ref, pscale_ref))

    lane = lax.broadcasted_iota(jnp.int32, (1, GROUP_W), 1)
    lane_head = lane // HEAD_D
    tiles_per_seq = seq_len // CONV_TILE
    chunks_per_seq = seq_len // CHUNK

    def seq_and_offset(i, per_seq, size):
        if n_seq == 1:
            return 0, pl.multiple_of(i * size, size)
        return i // per_seq, pl.multiple_of((i % per_seq) * size, size)

    gs1 = g1 * (1.0 + sc1)
    normed = []
    for r0 in range(0, TB, ROW_TILE):
        h = _norm_mod(x_ref[r0:r0 + ROW_TILE, :], gs1, sh1).astype(BF16)
        hbf_ref[r0:r0 + ROW_TILE, :] = h
        normed.append(h)
    proj_ref[:, COL_A:COL_B] = jnp.dot(
        jnp.concatenate(normed, axis=0), w_in_ref[:, COL_A:COL_B], preferred_element_type=F32)

    for s in range(n_seq):
        for ref in (pad_ref, pad2_ref):
            ref[s, 0:PAD, :] = jnp.zeros((PAD, GROUP_W), F32)
            ref[s, PAD + seq_len:PAD + seq_len + PAD, :] = jnp.zeros((PAD, GROUP_W), F32)

    def glu_body(i, _):
        s, t0 = seq_and_offset(i, chunks_per_seq, CHUNK)
        r0 = pl.multiple_of(i * CHUNK, CHUNK)
        a1 = proj_ref[pl.ds(r0, CHUNK), COL_A:COL_A + GROUP_W]
        a2 = proj_ref[pl.ds(r0, CHUNK), COL_A + GROUP_W:COL_B]
        pad_ref[s, pl.ds(pl.multiple_of(t0 + PAD, SUBLANES), CHUNK), :] = a1 * _sigmoid(a2)
        return 0
    lax.fori_loop(0, N_CHUNKS, glu_body, 0, unroll=True)

    win = CONV_TILE + 2 * PAD

    def conv_tile(i):
        s, t0 = divmod(i, tiles_per_seq)
        t0 *= CONV_TILE
        for c0 in range(0, GROUP_W, LANES):
            w = pad_ref[s, t0:t0 + win, c0:c0 + LANES]
            acc = jnp.zeros((CONV_TILE, LANES), F32) + cb[:, c0:c0 + LANES]
            for b in range(SUBLANES):
                wb = w if b == 0 else pltpu.roll(w, win - b, axis=0)
                for a in range(2 * PAD // SUBLANES):
                    k = SUBLANES * a + b - 1
                    if 0 <= k < CONV_W:
                        acc = acc + dw_ref[k:k + 1, c0:c0 + LANES] * wb[SUBLANES * a:SUBLANES * a + CONV_TILE, :]
            tmp_ref[i * CONV_TILE:(i + 1) * CONV_TILE, c0:c0 + LANES] = acc

    assert POOL_WINDOWS == (2, 4, 8, 16) and 2 * POOL_GW == LANES and POOL_HALO == SUBLANES
    pwin = CONV_TILE + 2 * POOL_HALO
    first_half = lax.broadcasted_iota(jnp.int32, (1, LANES), 1) < POOL_GW

    def pool_tile(i):
        s, t0 = divmod(i, tiles_per_seq)
        t0 *= CONV_TILE
        interior = t0 >= POOL_HALO and t0 + CONV_TILE + POOL_HALO <= seq_len
        outs = []
        for col, (w_small, w_big) in enumerate((POOL_WINDOWS[0:2], POOL_WINDOWS[2:4])):
            w = pad2_ref[s, t0 + PAD - POOL_HALO:t0 + PAD - POOL_HALO + pwin,
                         col * LANES:(col + 1) * LANES]

            rolled = {0: w}

            def shifted(b, w=w, rolled=rolled):
                if b not in rolled:
                    rolled[b] = pltpu.roll(w, pwin - b, axis=0)
                return rolled[b]

            def lo(b):
                return shifted(b)[0:CONV_TILE, :]

            def hi(b):
                return shifted(b)[POOL_HALO:POOL_HALO + CONV_TILE, :]

            tok = hi(0)
            if col == 0:
                s_small = lo(7) + tok
                s_big = s_small + lo(6) + hi(1)
            else:
                s_small = lo(7) + tok + lo(6) + hi(1) + lo(5) + lo(4) + hi(2) + hi(3)
                s_big = s_small + lo(3) + lo(2) + lo(1) + lo(0) + hi(4) + hi(5) + hi(6) + hi(7)
            ssum = jnp.where(first_half, s_small, s_big)
            if interior:
                mean = ssum * jnp.where(first_half, 1.0 / w_small, 1.0 / w_big)
            else:
                half = jnp.where(first_half, w_small // 2, w_big // 2)
                tpos = t0 + lax.broadcasted_iota(jnp.int32, (CONV_TILE, LANES), 0)
                cnt = jnp.minimum(tpos + half, seq_len) - jnp.maximum(tpos - half, 0)
                mean = ssum / cnt.astype(F32)
            outs.append(mean - tok)
        tmp2_ref[i * CONV_TILE:(i + 1) * CONV_TILE, :] = jnp.concatenate(outs, axis=1)

    col_tiles = [COL_C] + list(range(COL_B, COL_C, GROUP_W)) + list(range(COL_D, IN_COLS, GROUP_W))
    per_step = -(-N_CONV_TILES // (len(col_tiles) - 1))
    conv_next = 0
    pool_next = 0
    for step, c0 in enumerate(col_tiles):
        proj_ref[:, c0:c0 + GROUP_W] = jnp.dot(
            hbf_ref[...], w_in_ref[:, c0:c0 + GROUP_W], preferred_element_type=F32)
        if step == 1:
            for s in range(n_seq):
                pad2_ref[s, PAD:PAD + seq_len, :] = proj_ref[s * seq_len:(s + 1) * seq_len, COL_C:COL_D]
        for _ in range(per_step):
            if conv_next < N_CONV_TILES:
                conv_tile(conv_next)
                conv_next += 1
            if step >= 1 and pool_next < N_CONV_TILES:
                pool_tile(pool_next)
                pool_next += 1
    assert conv_next == N_CONV_TILES and pool_next == N_CONV_TILES

    c = tmp_ref[...]
    mu = jnp.mean(c, axis=-1, keepdims=True)
    cen = c - mu
    var = jnp.mean(cen * cen, axis=-1, keepdims=True)
    hn = cen * lax.rsqrt(var + EPS) * lng + lnb
    ya = jnp.dot(_silu(hn).astype(BF16), pwb_ref[...], preferred_element_type=F32)
    cat_ref[:, 0:GROUP_W] = ya.astype(BF16)

    def gmlp_body(i, _):
        r0 = pl.multiple_of(i * CHUNK, CHUNK)
        u = proj_ref[pl.ds(r0, CHUNK), COL_B:COL_B + GROUP_W]
        v = proj_ref[pl.ds(r0, CHUNK), COL_B + GROUP_W:COL_C]
        vstack = _head_stack(v, lane_head).astype(BF16)
        sg = jnp.dot(wcat_ref[...], vstack, preferred_element_type=F32) + gbias_ref[...]
        cat_ref[pl.ds(r0, CHUNK), GROUP_W:2 * GROUP_W] = (u * sg).astype(BF16)
        return 0
    lax.fori_loop(0, N_CHUNKS, gmlp_body, 0, unroll=True)

    yc = jnp.dot(tmp2_ref[...].astype(BF16), pbd_ref[...], preferred_element_type=F32) * pscale
    cat_ref[:, 2 * GROUP_W:3 * GROUP_W] = yc.astype(BF16)

    ri = lax.broadcasted_iota(jnp.int32, (CHUNK, HEADS * CHUNK), 0)
    ci = lax.broadcasted_iota(jnp.int32, (CHUNK, HEADS * CHUNK), 1)
    cj = ci % CHUNK
    chead = ci // CHUNK
    rq = lax.broadcasted_iota(jnp.int32, (CHUNK, GROUP_W), 0).astype(F32)
    for d in range(2):
        lgs = [lg_ref[(2 * layer + d) * HEADS + h] for h in range(HEADS)]
        lg_wide = jnp.where(chead == 0, lgs[0], jnp.where(chead == 1, lgs[1],
                                                          jnp.where(chead == 2, lgs[2], lgs[3])))
        lg_lane = jnp.where(lane_head == 0, lgs[0], jnp.where(lane_head == 1, lgs[1],
                                                              jnp.where(lane_head == 2, lgs[2], lgs[3])))
        dist = (ri - cj) if d == 0 else (cj - ri)
        keep = dist >= 0
        dcat_ref[d] = jnp.where(keep, jnp.exp(jnp.where(keep, dist, 0).astype(F32) * lg_wide), 0.0)
        if d == 0:
            qdec_ref[d] = jnp.exp((rq + 1.0) * lg_lane)
            kdec_ref[d] = jnp.exp((CHUNK - 1.0 - rq) * lg_lane)
        else:
            qdec_ref[d] = jnp.exp((CHUNK - rq) * lg_lane)
            kdec_ref[d] = jnp.exp(rq * lg_lane)
        sdec_ref[d] = jnp.exp(float(CHUNK) * lg_lane)

    rr = lax.broadcasted_iota(jnp.int32, (GROUP_W, GROUP_W), 0) // HEAD_D
    cc = lax.broadcasted_iota(jnp.int32, (GROUP_W, GROUP_W), 1) // HEAD_D
    gmat_ref[...] = jnp.where(rr == cc, 1.0 / HEAD_D, 0.0).astype(BF16)

    lane_bit = (lane & ROPE_PAIR) == 0
    k_scale = HEAD_D ** -0.5

    def rope(z, r0):
        if not rotate:
            return z
        cos = cos_ref[pl.ds(r0, CHUNK), :]
        sin = sin_ref[pl.ds(r0, CHUNK), :]
        halves = []
        for c0 in range(0, GROUP_W, LANES):
            zz = z[:, c0:c0 + LANES]
            partner = jnp.where(lane_bit[:, c0:c0 + LANES],
                                pltpu.roll(zz, LANES - ROPE_PAIR, axis=1), pltpu.roll(zz, ROPE_PAIR, axis=1))
            halves.append(partner)
        return z * cos + jnp.concatenate(halves, axis=1) * sin

    def pair_stack(zb, pair):
        zero = jnp.zeros_like(zb)
        return jnp.concatenate([jnp.where(lane_head == h, zb, zero) for h in (2 * pair, 2 * pair + 1)], axis=0)

    def intra_body(c, _):
        r0 = pl.multiple_of(c * CHUNK, CHUNK)
        v = proj_ref[pl.ds(r0, CHUNK), COL_V:COL_V + GROUP_W]
        vb = v.astype(BF16)
        atts = [[None, None], [None, None]]
        kds = []
        for d in range(2):
            qc0 = COL_D + 2 * d * GROUP_W
            q = rope(proj_ref[pl.ds(r0, CHUNK), qc0:qc0 + GROUP_W], r0)
            k = rope(proj_ref[pl.ds(r0, CHUNK), qc0 + GROUP_W:qc0 + 2 * GROUP_W], r0) * k_scale
            qb = q.astype(BF16)
            kb = k.astype(BF16)
            qb_ref[d, pl.ds(r0, CHUNK), :] = qb
            for pair in range(2):
                att = lax.dot_general(qb, pair_stack(kb, pair), (((1,), (1,)), ((), ())),
                                      preferred_element_type=F32)
                atts[pair][d] = (att * dcat_ref[d, :, pair * 2 * CHUNK:(pair + 1) * 2 * CHUNK]).astype(BF16)
            kds.append((k * kdec_ref[d]).astype(BF16))
        upd = lax.dot_general(jnp.concatenate(kds, axis=1), vb, (((0,), (0,)), ((), ())),
                              preferred_element_type=F32)
        for d in range(2):
            compact = None
            for h in range(HEADS):
                r_h = d * GROUP_W + h * HEAD_D
                part = jnp.where(lane_head == h, upd[r_h:r_h + HEAD_D, :], 0.0)
                compact = part if compact is None else compact + part
            upd_ref[d, c] = compact
        o = None
        for pair in range(2):
            both = jnp.dot(jnp.concatenate(atts[pair], axis=0), pair_stack(vb, pair), preferred_element_type=F32)
            od = both[0:CHUNK, :] + both[CHUNK:2 * CHUNK, :]
            o = od if o is None else o + od
        of_ref[pl.ds(r0, CHUNK), :] = o
        return 0
    lax.fori_loop(0, N_CHUNKS, intra_body, 0, unroll=True)

    for s in range(n_seq):
        for d in range(2):
            st = s0c_ref[d] if rotate else jnp.zeros((HEAD_D, GROUP_W), F32)
            order = range(n_chunk) if d == 0 else range(n_chunk - 1, -1, -1)
            for c in order:
                cg = s * n_chunk + c
                for h in range(HEADS):
                    sall_ref[d, cg, h * HEAD_D:(h + 1) * HEAD_D, :] = (
                        jnp.where(lane_head == h, st, 0.0).astype(BF16))
                st = st * sdec_ref[d] + upd_ref[d, cg]
            if not rotate:
                for h in range(HEADS):
                    if layer == 0:
                        st_ref[s, 0, d, h] = st[:, h * HEAD_D:(h + 1) * HEAD_D]
                    else:
                        st_ref[s, d, h] = st[:, h * HEAD_D:(h + 1) * HEAD_D]
    if not rotate and layer == 0:
        for s in range(n_seq):
            for later in range(1, DEPTH):
                for d in range(2):
                    for h in range(HEADS):
                        st_ref[s, later, d, h] = jnp.zeros((HEAD_D, HEAD_D), F32)

    def cross_body(c, _):
        r0 = pl.multiple_of(c * CHUNK, CHUNK)
        o = of_ref[pl.ds(r0, CHUNK), :]
        for d in range(2):
            o = o + jnp.dot(qb_ref[d, pl.ds(r0, CHUNK), :], sall_ref[d, c],
                            preferred_element_type=F32) * qdec_ref[d]
        of_ref[pl.ds(r0, CHUNK), :] = o
        return 0
    lax.fori_loop(0, N_CHUNKS, cross_body, 0, unroll=True)

    o = of_ref[...]
    gmat = gmat_ref[...]
    o_hi = o.astype(BF16)
    o_lo = (o - o_hi.astype(F32)).astype(BF16)
    mu_h = jnp.dot(o_hi, gmat, preferred_element_type=F32) + jnp.dot(o_lo, gmat, preferred_element_type=F32)
    cen_h = o - mu_h
    var_h = jnp.dot((cen_h * cen_h).astype(BF16), gmat, preferred_element_type=F32)
    gate = proj_ref[:, COL_G:COL_G + GROUP_W]
    cat_ref[:, 3 * GROUP_W:4 * GROUP_W] = (_silu(gate) * (cen_h * lax.rsqrt(var_h + EPS))).astype(BF16)

    y_ref[...] = x_ref[...] + ga1 * jnp.dot(cat_ref[...], w_out_ref[...], preferred_element_type=F32)


def _layer_spec(shape, layer):
    zeros = (0,) * len(shape)
    return pl.BlockSpec((None,) + tuple(shape), lambda i: (layer,) + zeros, pipeline_mode=pl.Buffered(1))


def _mod_spec(layer):
    return pl.BlockSpec((MOD_ROWS, 6 * D_MODEL), lambda i: (layer, 0), pipeline_mode=pl.Buffered(1))


def _mixer(x, mods, pp, *, layer, seq_len, rotate, mod_base, mod_stride, rope_tabs=None, s0=None, states=None):
    nb = x.shape[0]
    n_seq = TB // seq_len
    in_specs = [
        pl.BlockSpec((None, TB, D_MODEL), lambda i: (i, 0, 0)),
        _mod_spec(layer),
        _const_spec((DEPTH, D_MODEL)),
        _layer_spec((D_MODEL, IN_COLS), layer),
        _layer_spec((D_MODEL, D_MODEL), layer),
        _layer_spec((CONV_W, GROUP_W), layer),
        _const_spec((DEPTH, GROUP_W)), _const_spec((DEPTH, GROUP_W)), _const_spec((DEPTH, GROUP_W)),
        _layer_spec((GROUP_W, GROUP_W), layer),
        _layer_spec((HEADS, CHUNK, CHUNK), layer),
        _layer_spec((CHUNK, GROUP_W), layer),
        _layer_spec((HEADS, HEAD_D, HEAD_D), layer),
        _const_spec((DEPTH, GROUP_W)),
        pl.BlockSpec(memory_space=pltpu.SMEM),
    ]
    args = [x, mods, pp["g1"], pp["w_in"], pp["w_out"], pp["dw"], pp["cb"], pp["lng"], pp["lnb"],
            pp["pw"], pp["ws"], pp["gbias"], pp["pool_w"], pp["pscale"], pp["lg"]]
    out_shape = [jax.ShapeDtypeStruct((nb, TB, D_MODEL), F32)]
    out_specs = [pl.BlockSpec((None, TB, D_MODEL), lambda i: (i, 0, 0))]
    aliases = {}
    if rotate:
        in_specs += [_const_spec((TB, GROUP_W)), _const_spec((TB, GROUP_W)),
                     pl.BlockSpec((None, None, 2, HEADS, HEAD_D, HEAD_D), lambda i: (i, layer, 0, 0, 0, 0))]
        args += [rope_tabs[0], rope_tabs[1], s0]
    else:
        out_shape.append(jax.ShapeDtypeStruct((nb * n_seq, DEPTH, 2, HEADS, HEAD_D, HEAD_D), F32))
        if layer == 0:
            out_specs.append(pl.BlockSpec((n_seq, DEPTH, 2, HEADS, HEAD_D, HEAD_D),
                                          lambda i: (i, 0, 0, 0, 0, 0)))
        else:
            in_specs.append(pl.BlockSpec(memory_space=pl.ANY))
            args.append(states)
            aliases = {len(args) - 1: 1}
            out_specs.append(pl.BlockSpec((n_seq, None, 2, HEADS, HEAD_D, HEAD_D),
                                          lambda i: (i, layer, 0, 0, 0, 0)))
    scratch = [
        pltpu.VMEM((TB, D_MODEL), BF16),
        pltpu.VMEM((TB, IN_COLS), F32),
        pltpu.VMEM((n_seq, seq_len + 2 * PAD, GROUP_W), F32),
        pltpu.VMEM((n_seq, seq_len + 2 * PAD, GROUP_W), F32),
        pltpu.VMEM((TB, GROUP_W), F32),
        pltpu.VMEM((TB, GROUP_W), F32),
        pltpu.VMEM((2, CHUNK, HEADS * CHUNK), F32),
        pltpu.VMEM((2, CHUNK, GROUP_W), F32),
        pltpu.VMEM((2, CHUNK, GROUP_W), F32),
        pltpu.VMEM((2, 1, GROUP_W), F32),
        pltpu.VMEM((GROUP_W, GROUP_W), BF16),
        pltpu.VMEM((2, TB, GROUP_W), BF16),
        pltpu.VMEM((2, N_CHUNKS, HEAD_D, GROUP_W), F32),
        pltpu.VMEM((2, N_CHUNKS, GROUP_W, GROUP_W), BF16),
        pltpu.VMEM((GROUP_W, GROUP_W), BF16),
        pltpu.VMEM((CHUNK, HEADS * CHUNK), BF16),
        pltpu.VMEM((GROUP_W, GROUP_W), BF16),
        pltpu.VMEM((2, HEAD_D, GROUP_W), F32),
    ]
    outs = pl.pallas_call(
        functools.partial(_mixer_kernel, layer=layer, seq_len=seq_len, rotate=rotate,
                          mod_base=mod_base, mod_stride=mod_stride),
        grid=(nb,),
        in_specs=in_specs,
        out_specs=out_specs,
        out_shape=out_shape,
        scratch_shapes=scratch,
        input_output_aliases=aliases,
        compiler_params=pltpu.CompilerParams(
            dimension_semantics=("arbitrary",), vmem_limit_bytes=VMEM_LIMIT_BYTES),
        name="mixer_lat" if rotate else "mixer_ctx",
    )(*args)
    return outs


def _ffn_kernel(x_ref, mod_ref, g2_ref, w_in_hbm, w_out_hbm, gf_ref, y_ref,
                hbf_ref, act_ref, act3_ref, wgu_ref, wo_ref, stg_g_ref, stg_u_ref, stg_o_ref, sem,
                *, layer, final_norm, mod_base, mod_stride):
    pid = pl.program_id(0)
    mod = _mod_row(mod_ref, mod_base, mod_stride)
    sh2 = mod(3)
    sc2 = mod(4)
    ga2 = mod(5)
    gs2 = g2_ref[layer:layer + 1, :] * (1.0 + sc2)

    def tile_copies(j, slot):
        return (
            pltpu.make_async_copy(w_in_hbm.at[layer, :, pl.ds(j * FF_TILE, FF_TILE)],
                                  stg_g_ref.at[slot], sem.at[0, slot]),
            pltpu.make_async_copy(w_in_hbm.at[layer, :, pl.ds(D_FF + j * FF_TILE, FF_TILE)],
                                  stg_u_ref.at[slot], sem.at[1, slot]),
            pltpu.make_async_copy(w_out_hbm.at[layer, pl.ds(j * FF_TILE, FF_TILE), :],
                                  stg_o_ref.at[slot], sem.at[2, slot]),
        )

    def start_tile(j):
        for cp in tile_copies(j, j % FF_RING):
            cp.start()

    def hidden_tile(j, act_out):
        gate = jnp.dot(hbf_ref[...], wgu_ref[j], preferred_element_type=F32)
        up = jnp.dot(hbf_ref[...], wgu_ref[N_FF_TILES + j], preferred_element_type=F32)
        act_out((_silu(gate) * up).astype(BF16))

    @pl.when(pid == 0)
    def _():
        for j in range(FF_LOOKAHEAD):
            start_tile(j)

    def norm_body(i, _):
        r0 = pl.multiple_of(i * ROW_TILE, ROW_TILE)
        h = _norm_mod(x_ref[pl.ds(r0, ROW_TILE), :], gs2, sh2)
        hbf_ref[pl.ds(r0, ROW_TILE), :] = h.astype(BF16)
        return 0
    lax.fori_loop(0, TB // ROW_TILE, norm_body, 0)

    @pl.when(pid == 0)
    def _():
        def fetch_body(j, _):
            @pl.when(j + FF_LOOKAHEAD < N_FF_TILES)
            def _():
                start_tile(j + FF_LOOKAHEAD)
            slot = j % FF_RING
            for cp in tile_copies(j, slot):
                cp.wait()
            wgu_ref[j] = stg_g_ref[slot].astype(BF16)
            wgu_ref[N_FF_TILES + j] = stg_u_ref[slot].astype(BF16)
            wo_ref[pl.ds(pl.multiple_of(j * FF_TILE, FF_TILE), FF_TILE), :] = stg_o_ref[slot].astype(BF16)

            def act_out(act):
                act3_ref[j] = act
            hidden_tile(j, act_out)
            return 0
        lax.fori_loop(0, N_FF_TILES, fetch_body, 0)
        for j in range(N_FF_TILES):
            act_ref[:, j * FF_TILE:(j + 1) * FF_TILE] = act3_ref[j]

    @pl.when(pid != 0)
    def _():
        for j in range(N_FF_TILES):
            def act_out(act, j=j):
                act_ref[:, j * FF_TILE:(j + 1) * FF_TILE] = act
            hidden_tile(j, act_out)

    for r0 in range(0, TB, FF_OUT_ROWS):
        rows = slice(r0, r0 + FF_OUT_ROWS)
        y = x_ref[rows, :] + ga2 * jnp.dot(act_ref[rows, :], wo_ref[...], preferred_element_type=F32)
        if final_norm:
            ms = jnp.mean(y * y, axis=-1, keepdims=True)
            y = y * lax.rsqrt(ms + EPS) * gf_ref[...]
        y_ref[rows, :] = y


def _ffn(x, mods, g2, w_ffn_in, w_ffn_out, g_final, *, layer, mod_base, mod_stride, final_norm):
    nb = x.shape[0]
    return pl.pallas_call(
        functools.partial(_ffn_kernel, layer=layer, final_norm=final_norm, mod_base=mod_base,
                          mod_stride=mod_stride),
        grid=(nb,),
        in_specs=[
            pl.BlockSpec((None, TB, D_MODEL), lambda i: (i, 0, 0)),
            _mod_spec(layer),
            _const_spec((DEPTH, D_MODEL)),
            pl.BlockSpec(memory_space=pl.ANY),
            pl.BlockSpec(memory_space=pl.ANY),
            _const_spec((1, D_MODEL)),
        ],
        out_specs=pl.BlockSpec((None, TB, D_MODEL), lambda i: (i, 0, 0)),
        out_shape=jax.ShapeDtypeStruct((nb, TB, D_MODEL), F32),
        scratch_shapes=[
            pltpu.VMEM((TB, D_MODEL), BF16),
            pltpu.VMEM((TB, D_FF), BF16),
            pltpu.VMEM((N_FF_TILES, TB, FF_TILE), BF16),
            pltpu.VMEM((2 * N_FF_TILES, D_MODEL, FF_TILE), BF16),
            pltpu.VMEM((D_FF, D_MODEL), BF16),
            pltpu.VMEM((FF_RING, D_MODEL, FF_TILE), F32),
            pltpu.VMEM((FF_RING, D_MODEL, FF_TILE), F32),
            pltpu.VMEM((FF_RING, FF_TILE, D_MODEL), F32),
            pltpu.SemaphoreType.DMA((3, FF_RING)),
        ],
        compiler_params=pltpu.CompilerParams(
            dimension_semantics=("arbitrary",), vmem_limit_bytes=VMEM_LIMIT_BYTES),
        name="ffn",
    )(x, mods, g2, w_ffn_in, w_ffn_out, g_final)


def _rope_tables(seq_len):
    t = np.arange(seq_len)
    r = (t // GRID_W).astype(np.float32)
    c = (t % GRID_W).astype(np.float32)
    nf = HEAD_D // 4
    inv = np.float32(ROPE_BASE) ** (-np.arange(nf, dtype=np.float32) / np.float32(nf))
    ang_r = r[:, None] * inv
    ang_c = c[:, None] * inv
    cos = np.concatenate([np.cos(ang_r), np.cos(ang_r), np.cos(ang_c), np.cos(ang_c)], axis=-1)
    sin = np.concatenate([-np.sin(ang_r), np.sin(ang_r), -np.sin(ang_c), np.sin(ang_c)], axis=-1)
    return (jnp.asarray(np.tile(cos, (1, HEADS)), dtype=F32), jnp.asarray(np.tile(sin, (1, HEADS)), dtype=F32))


def kernel(x_prompt, x_sample, state_ret, c, c_ctx, w_ada, b_ada, g_norm1, g_norm2, w_in, w_out, conv_dw,
           conv_b, conv_ln_g, conv_ln_b, conv_pw, gmlp_ws, gmlp_b, pool_w, pool_scale, ret_decay, w_ffn_in,
           w_ffn_out, g_final):
    batch, seq, _ = x_prompt.shape
    dec_batch, dec_seq, _ = x_sample.shape
    assert dec_seq == TB and TB % seq == 0 and (batch * seq) % TB == 0
    assert 1 + dec_batch <= MOD_ROWS

    cs = jnp.concatenate([c_ctx[None, :], c, jnp.zeros((MOD_ROWS - 1 - dec_batch, D_MODEL), F32)], axis=0)
    mods = _ada_rows(cs, w_ada, b_ada).reshape(DEPTH * MOD_ROWS, 6 * D_MODEL)
    rope_tabs = _rope_tables(dec_seq)
    g_final2 = g_final.reshape(1, D_MODEL)
    pp = {
        "g1": g_norm1,
        "g2": g_norm2,
        "w_in": w_in.astype(BF16),
        "w_out": w_out.astype(BF16),
        "dw": conv_dw,
        "cb": conv_b,
        "lng": conv_ln_g,
        "lnb": conv_ln_b,
        "pw": conv_pw,
        "ws": gmlp_ws,
        "gbias": jnp.repeat(jnp.swapaxes(gmlp_b, 1, 2), HEAD_D, axis=2),
        "pool_w": pool_w,
        "pscale": pool_scale,
        "lg": jax.nn.log_sigmoid(ret_decay.astype(F32)).reshape(DEPTH * 2 * HEADS),
    }

    xc = x_prompt.reshape(batch * seq // TB, TB, D_MODEL)
    xl = x_sample
    states = None
    for l in range(DEPTH):
        last = l == DEPTH - 1
        xc, states = _mixer(xc, mods, pp, layer=l, seq_len=seq, rotate=False, mod_base=0, mod_stride=0,
                            states=states)
        xc = _ffn(xc, mods, pp["g2"], w_ffn_in, w_ffn_out, g_final2, layer=l, mod_base=0, mod_stride=0,
                  final_norm=last)
        (xl,) = _mixer(xl, mods, pp, layer=l, seq_len=dec_seq, rotate=True, mod_base=1, mod_stride=1,
                       rope_tabs=rope_tabs, s0=state_ret.astype(F32))
        xl = _ffn(xl, mods, pp["g2"], w_ffn_in, w_ffn_out, g_final2, layer=l, mod_base=1, mod_stride=1,
                  final_norm=last)

    y_prompt = xc.reshape(batch, seq, D_MODEL)
    return (y_prompt, xl, states.astype(x_prompt.dtype))
```

```python
import functools

import jax
import jax.numpy as jnp
import numpy as np
from jax import lax
from jax.experimental import pallas as pl
from jax.experimental.pallas import tpu as pltpu

F32 = jnp.float32
BF16 = jnp.bfloat16

D_MODEL = 1024
DEPTH = 2
GRID_W = 64
GROUP_W = D_MODEL // 4
CONV_W = 31
CHUNK = 128
HEADS = 4
HEAD_D = GROUP_W // HEADS
POOL_WINDOWS = (2, 4, 8, 16)
POOL_GW = GROUP_W // len(POOL_WINDOWS)
ROPE_BASE = 10000.0
D_FF = 2816
IN_COLS = 11 * GROUP_W
EPS = 1e-6

LANES = 128
SUBLANES = 8

TB = 1024
N_CHUNKS = TB // CHUNK
ROW_TILE = 256
CONV_TILE = 64
N_CONV_TILES = TB // CONV_TILE
PAD = 16
POOL_HALO = max(POOL_WINDOWS) // 2
ROPE_PAIR = HEAD_D // 4
FF_TILE = 256
N_FF_TILES = D_FF // FF_TILE
FF_OUT_ROWS = 256
FF_LOOKAHEAD = 2
FF_RING = FF_LOOKAHEAD + 1
ADA_TILE = 1536
MOD_ROWS = 8
VMEM_LIMIT_BYTES = 60 * 1024 * 1024

COL_A, COL_B, COL_C, COL_D = 0, 2 * GROUP_W, 4 * GROUP_W, 5 * GROUP_W
COL_V, COL_G = COL_D + 4 * GROUP_W, COL_D + 5 * GROUP_W


def _sigmoid(x):
    return 1.0 / (1.0 + jnp.exp(-x))


def _silu(x):
    return x * _sigmoid(x)


def _norm_mod(x, gain_scale, shift):
    ms = jnp.mean(x * x, axis=-1, keepdims=True)
    return x * lax.rsqrt(ms + EPS) * gain_scale + shift


def _head_stack(x, lane_head):
    return jnp.concatenate([jnp.where(lane_head == h, x, 0.0) for h in range(HEADS)], axis=0)


def _mod_row(mod_ref, mod_base, mod_stride):
    row = jnp.maximum(mod_base + mod_stride * pl.program_id(0), 0) if mod_stride else max(mod_base, 0)

    def part(k):
        return mod_ref[pl.ds(row, 1), k * D_MODEL:(k + 1) * D_MODEL]
    return part


def _const_spec(shape):
    zeros = (0,) * len(shape)
    return pl.BlockSpec(shape, lambda i: zeros, pipeline_mode=pl.Buffered(1))


def _ada_kernel(c_ref, w_ref, b_ref, o_ref):
    a = _silu(c_ref[...]).astype(BF16)
    bias = b_ref[pl.ds(pl.program_id(0), 1), :]
    o_ref[...] = jnp.dot(a, w_ref[...].astype(BF16), preferred_element_type=F32) + bias


def _ada_rows(cs, w_ada, b_ada):
    n_tiles = 6 * D_MODEL // ADA_TILE
    return pl.pallas_call(
        _ada_kernel,
        grid=(DEPTH, n_tiles),
        in_specs=[
            pl.BlockSpec((MOD_ROWS, D_MODEL), lambda l, j: (0, 0)),
            pl.BlockSpec((None, D_MODEL, ADA_TILE), lambda l, j: (l, 0, j)),
            pl.BlockSpec((DEPTH, ADA_TILE), lambda l, j: (0, j)),
        ],
        out_specs=pl.BlockSpec((None, MOD_ROWS, ADA_TILE), lambda l, j: (l, 0, j)),
        out_shape=jax.ShapeDtypeStruct((DEPTH, MOD_ROWS, 6 * D_MODEL), F32),
        compiler_params=pltpu.CompilerParams(
            dimension_semantics=("arbitrary", "arbitrary"), vmem_limit_bytes=VMEM_LIMIT_BYTES),
        name="ada_rows",
    )(cs, w_ada, b_ada)


def _mixer_kernel(*refs, layer, seq_len, rotate, mod_base, mod_stride):
    n_seq = TB // seq_len
    n_chunk = seq_len // CHUNK
    it = iter(refs)
    x_ref, mod_ref, g1_ref, w_in_ref, w_out_ref = (next(it) for _ in range(5))
    dw_ref, cb_ref, lng_ref, lnb_ref, pw_ref = (next(it) for _ in range(5))
    ws_ref, gbias_ref, poolw_ref, pscale_ref, lg_ref = (next(it) for _ in range(5))
    if rotate:
        cos_ref, sin_ref, s0_ref = (next(it) for _ in range(3))
    elif layer > 0:
        next(it)
    y_ref = next(it)
    if not rotate:
        st_ref = next(it)
    (hbf_ref, proj_ref, pad_ref, pad2_ref, tmp_ref, tmp2_ref,
     dcat_ref, qdec_ref, kdec_ref, sdec_ref, gmat_ref, qb_ref, upd_ref, sall_ref,
     pwb_ref, wcat_ref, pbd_ref, s0c_ref) = it
    cat_ref = hbf_ref
    of_ref = tmp_ref

    pwb_ref[...] = pw_ref[...].astype(BF16)
    pbd_ref[...] = jnp.zeros((GROUP_W, GROUP_W), BF16)
    for h in range(HEADS):
        wcat_ref[:, h * CHUNK:(h + 1) * CHUNK] = ws_ref[h].astype(BF16)
        pbd_ref[h * HEAD_D:(h + 1) * HEAD_D, h * HEAD_D:(h + 1) * HEAD_D] = poolw_ref[h].astype(BF16)
        if rotate:
            for d in range(2):
                s0c_ref[d, :, h * HEAD_D:(h + 1) * HEAD_D] = s0_ref[d, h]

    mod = _mod_row(mod_ref, mod_base, mod_stride)
    sh1 = mod(0)
    sc1 = mod(1)
    ga1 = mod(2)
    g1, cb, lng, lnb, pscale = (r[layer:layer + 1, :] for r in (g1_ref, cb_ref, lng_ref, lnb_ref, pscale_ref))

    lane = lax.broadcasted_iota(jnp.int32, (1, GROUP_W), 1)
    lane_head = lane // HEAD_D
    tiles_per_seq = seq_len // CONV_TILE
    chunks_per_seq = seq_len // CHUNK

    def seq_and_offset(i, per_seq, size):
        if n_seq == 1:
            return 0, pl.multiple_of(i * size, size)
        return i // per_seq, pl.multiple_of((i % per_seq) * size, size)

    gs1 = g1 * (1.0 + sc1)
    normed = []
    for r0 in range(0, TB, ROW_TILE):
        h = _norm_mod(x_ref[r0:r0 + ROW_TILE, :], gs1, sh1).astype(BF16)
        hbf_ref[r0:r0 + ROW_TILE, :] = h
        normed.append(h)
    proj_ref[:, COL_A:COL_B] = jnp.dot(
        jnp.concatenate(normed, axis=0), w_in_ref[:, COL_A:COL_B], preferred_element_type=F32)

    for s in range(n_seq):
        for ref in (pad_ref, pad2_ref):
            ref[s, 0:PAD, :] = jnp.zeros((PAD, GROUP_W), F32)
            ref[s, PAD + seq_len:PAD + seq_len + PAD, :] = jnp.zeros((PAD, GROUP_W), F32)

    def glu_body(i, _):
        s, t0 = seq_and_offset(i, chunks_per_seq, CHUNK)
        r0 = pl.multiple_of(i * CHUNK, CHUNK)
        a1 = proj_ref[pl.ds(r0, CHUNK), COL_A:COL_A + GROUP_W]
        a2 = proj_ref[pl.ds(r0, CHUNK), COL_A + GROUP_W:COL_B]
        pad_ref[s, pl.ds(pl.multiple_of(t0 + PAD, SUBLANES), CHUNK), :] = a1 * _sigmoid(a2)
        return 0
    lax.fori_loop(0, N_CHUNKS, glu_body, 0, unroll=True)

    win = CONV_TILE + 2 * PAD

    def conv_tile(i):
        s, t0 = divmod(i, tiles_per_seq)
        t0 *= CONV_TILE
        for c0 in range(0, GROUP_W, LANES):
            w = pad_ref[s, t0:t0 + win, c0:c0 + LANES]
            acc = jnp.zeros((CONV_TILE, LANES), F32) + cb[:, c0:c0 + LANES]
            for b in range(SUBLANES):
                wb = w if b == 0 else pltpu.roll(w, win - b, axis=0)
                for a in range(2 * PAD // SUBLANES):
                    k = SUBLANES * a + b - 1
                    if 0 <= k < CONV_W:
                        acc = acc + dw_ref[k:k + 1, c0:c0 + LANES] * wb[SUBLANES * a:SUBLANES * a + CONV_TILE, :]
            tmp_ref[i * CONV_TILE:(i + 1) * CONV_TILE, c0:c0 + LANES] = acc

    assert POOL_WINDOWS == (2, 4, 8, 16) and 2 * POOL_GW == LANES and POOL_HALO == SUBLANES
    pwin = CONV_TILE + 2 * POOL_HALO
    first_half = lax.broadcasted_iota(jnp.int32, (1, LANES), 1) < POOL_GW

    def pool_tile(i):
        s, t0 = divmod(i, tiles_per_seq)
        t0 *= CONV_TILE
        interior = t0 >= POOL_HALO and t0 + CONV_TILE + POOL_HALO <= seq_len
        outs = []
        for col, (w_small, w_big) in enumerate((POOL_WINDOWS[0:2], POOL_WINDOWS[2:4])):
            w = pad2_ref[s, t0 + PAD - POOL_HALO:t0 + PAD - POOL_HALO + pwin,
                         col * LANES:(col + 1) * LANES]

            rolled = {0: w}

            def shifted(b, w=w, rolled=rolled):
                if b not in rolled:
                    rolled[b] = pltpu.roll(w, pwin - b, axis=0)
                return rolled[b]

            def lo(b):
                return shifted(b)[0:CONV_TILE, :]

            def hi(b):
                return shifted(b)[POOL_HALO:POOL_HALO + CONV_TILE, :]

            tok = hi(0)
            if col == 0:
                s_small = lo(7) + tok
                s_big = s_small + lo(6) + hi(1)
            else:
                s_small = lo(7) + tok + lo(6) + hi(1) + lo(5) + lo(4) + hi(2) + hi(3)
                s_big = s_small + lo(3) + lo(2) + lo(1) + lo(0) + hi(4) + hi(5) + hi(6) + hi(7)
            ssum = jnp.where(first_half, s_small, s_big)
            if interior:
                mean = ssum * jnp.where(first_half, 1.0 / w_small, 1.0 / w_big)
            else:
                half = jnp.where(first_half, w_small // 2, w_big // 2)
                tpos = t0 + lax.broadcasted_iota(jnp.int32, (CONV_TILE, LANES), 0)
                cnt = jnp.minimum(tpos + half, seq_len) - jnp.maximum(tpos - half, 0)
                mean = ssum / cnt.astype(F32)
            outs.append(mean - tok)
        tmp2_ref[i * CONV_TILE:(i + 1) * CONV_TILE, :] = jnp.concatenate(outs, axis=1)

    col_tiles = [COL_C] + list(range(COL_B, COL_C, GROUP_W)) + list(range(COL_D, IN_COLS, GROUP_W))
    per_step = -(-N_CONV_TILES // (len(col_tiles) - 1))
    conv_next = 0
    pool_next = 0
    for step, c0 in enumerate(col_tiles):
        proj_ref[:, c0:c0 + GROUP_W] = jnp.dot(
            hbf_ref[...], w_in_ref[:, c0:c0 + GROUP_W], preferred_element_type=F32)
        if step == 1:
            for s in range(n_seq):
                pad2_ref[s, PAD:PAD + seq_len, :] = proj_ref[s * seq_len:(s + 1) * seq_len, COL_C:COL_D]
        for _ in range(per_step):
            if conv_next < N_CONV_TILES:
                conv_tile(conv_next)
                conv_next += 1
            if step >= 1 and pool_next < N_CONV_TILES:
                pool_tile(pool_next)
                pool_next += 1
    assert conv_next == N_CONV_TILES and pool_next == N_CONV_TILES

    c = tmp_ref[...]
    mu = jnp.mean(c, axis=-1, keepdims=True)
    cen = c - mu
    var = jnp.mean(cen * cen, axis=-1, keepdims=True)
    hn = cen * lax.rsqrt(var + EPS) * lng + lnb
    ya = jnp.dot(_silu(hn).astype(BF16), pwb_ref[...], preferred_element_type=F32)
    cat_ref[:, 0:GROUP_W] = ya.astype(BF16)

    def gmlp_body(i, _):
        r0 = pl.multiple_of(i * CHUNK, CHUNK)
        u = proj_ref[pl.ds(r0, CHUNK), COL_B:COL_B + GROUP_W]
        v = proj_ref[pl.ds(r0, CHUNK), COL_B + GROUP_W:COL_C]
        vstack = _head_stack(v, lane_head).astype(BF16)
        sg = jnp.dot(wcat_ref[...], vstack, preferred_element_type=F32) + gbias_ref[...]
        cat_ref[pl.ds(r0, CHUNK), GROUP_W:2 * GROUP_W] = (u * sg).astype(BF16)
        return 0
    lax.fori_loop(0, N_CHUNKS, gmlp_body, 0, unroll=True)

    yc = jnp.dot(tmp2_ref[...].astype(BF16), pbd_ref[...], preferred_element_type=F32) * pscale
    cat_ref[:, 2 * GROUP_W:3 * GROUP_W] = yc.astype(BF16)

    ri = lax.broadcasted_iota(jnp.int32, (CHUNK, HEADS * CHUNK), 0)
    ci = lax.broadcasted_iota(jnp.int32, (CHUNK, HEADS * CHUNK), 1)
    cj = ci % CHUNK
    chead = ci // CHUNK
    rq = lax.broadcasted_iota(jnp.int32, (CHUNK, GROUP_W), 0).astype(F32)
    for d in range(2):
        lgs = [lg_ref[(2 * layer + d) * HEADS + h] for h in range(HEADS)]
        lg_wide = jnp.where(chead == 0, lgs[0], jnp.where(chead == 1, lgs[1],
                                                          jnp.where(chead == 2, lgs[2], lgs[3])))
        lg_lane = jnp.where(lane_head == 0, lgs[0], jnp.where(lane_head == 1, lgs[1],
                                                              jnp.where(lane_head == 2, lgs[2], lgs[3])))
        dist = (ri - cj) if d == 0 else (cj - ri)
        keep = dist >= 0
        dcat_ref[d] = jnp.where(keep, jnp.exp(jnp.where(keep, dist, 0).astype(F32) * lg_wide), 0.0)
        if d == 0:
            qdec_ref[d] = jnp.exp((rq + 1.0) * lg_lane)
            kdec_ref[d] = jnp.exp((CHUNK - 1.0 - rq) * lg_lane)
        else:
            qdec_ref[d] = jnp.exp((CHUNK - rq) * lg_lane)
            kdec_ref[d] = jnp.exp(rq * lg_lane)
        sdec_ref[d] = jnp.exp(float(CHUNK) * lg_lane)

    rr = lax.broadcasted_iota(jnp.int32, (GROUP_W, GROUP_W), 0) // HEAD_D
    cc = lax.broadcasted_iota(jnp.int32, (GROUP_W, GROUP_W), 1) // HEAD_D
    gmat_ref[...] = jnp.where(rr == cc, 1.0 / HEAD_D, 0.0).astype(BF16)

    lane_bit = (lane & ROPE_PAIR) == 0
    k_scale = HEAD_D ** -0.5

    def rope(z, r0):
        if not rotate:
            return z
        cos = cos_ref[pl.ds(r0, CHUNK), :]
        sin = sin_ref[pl.ds(r0, CHUNK), :]
        halves = []
        for c0 in range(0, GROUP_W, LANES):
            zz = z[:, c0:c0 + LANES]
            partner = jnp.where(lane_bit[:, c0:c0 + LANES],
                                pltpu.roll(zz, LANES - ROPE_PAIR, axis=1), pltpu.roll(zz, ROPE_PAIR, axis=1))
            halves.append(partner)
        return z * cos + jnp.concatenate(halves, axis=1) * sin

    def pair_stack(zb, pair):
        zero = jnp.zeros_like(zb)
        return jnp.concatenate([jnp.where(lane_head == h, zb, zero) for h in (2 * pair, 2 * pair + 1)], axis=0)

    def intra_body(c, _):
        r0 = pl.multiple_of(c * CHUNK, CHUNK)
        v = proj_ref[pl.ds(r0, CHUNK), COL_V:COL_V + GROUP_W]
        vb = v.astype(BF16)
        atts = [[None, None], [None, None]]
        kds = []
        for d in range(2):
            qc0 = COL_D + 2 * d * GROUP_W
            q = rope(proj_ref[pl.ds(r0, CHUNK), qc0:qc0 + GROUP_W], r0)
            k = rope(proj_ref[pl.ds(r0, CHUNK), qc0 + GROUP_W:qc0 + 2 * GROUP_W], r0) * k_scale
            qb = q.astype(BF16)
            kb = k.astype(BF16)
            qb_ref[d, pl.ds(r0, CHUNK), :] = qb
            for pair in range(2):
                att = lax.dot_general(qb, pair_stack(kb, pair), (((1,), (1,)), ((), ())),
                                      preferred_element_type=F32)
                atts[pair][d] = (att * dcat_ref[d, :, pair * 2 * CHUNK:(pair + 1) * 2 * CHUNK]).astype(BF16)
            kds.append((k * kdec_ref[d]).astype(BF16))
        upd = lax.dot_general(jnp.concatenate(kds, axis=1), vb, (((0,), (0,)), ((), ())),
                              preferred_element_type=F32)
        for d in range(2):
            compact = None
            for h in range(HEADS):
                r_h = d * GROUP_W + h * HEAD_D
                part = jnp.where(lane_head == h, upd[r_h:r_h + HEAD_D, :], 0.0)
                compact = part if compact is None else compact + part
            upd_ref[d, c] = compact
        o = None
        for pair in range(2):
            both = jnp.dot(jnp.concatenate(atts[pair], axis=0), pair_stack(vb, pair), preferred_element_type=F32)
            od = both[0:CHUNK, :] + both[CHUNK:2 * CHUNK, :]
            o = od if o is None else o + od
        of_ref[pl.ds(r0, CHUNK), :] = o
        return 0
    lax.fori_loop(0, N_CHUNKS, intra_body, 0, unroll=True)

    for s in range(n_seq):
        for d in range(2):
            st = s0c_ref[d] if rotate else jnp.zeros((HEAD_D, GROUP_W), F32)
            order = range(n_chunk) if d == 0 else range(n_chunk - 1, -1, -1)
            for c in order:
                cg = s * n_chunk + c
                for h in range(HEADS):
                    sall_ref[d, cg, h * HEAD_D:(h + 1) * HEAD_D, :] = (
                        jnp.where(lane_head == h, st, 0.0).astype(BF16))
                st = st * sdec_ref[d] + upd_ref[d, cg]
            if not rotate:
                for h in range(HEADS):
                    if layer == 0:
                        st_ref[s, 0, d, h] = st[:, h * HEAD_D:(h + 1) * HEAD_D]
                    else:
                        st_ref[s, d, h] = st[:, h * HEAD_D:(h + 1) * HEAD_D]
    if not rotate and layer == 0:
        for s in range(n_seq):
            for later in range(1, DEPTH):
                for d in range(2):
                    for h in range(HEADS):
                        st_ref[s, later, d, h] = jnp.zeros((HEAD_D, HEAD_D), F32)

    def cross_body(c, _):
        r0 = pl.multiple_of(c * CHUNK, CHUNK)
        o = of_ref[pl.ds(r0, CHUNK), :]
        for d in range(2):
            o = o + jnp.dot(qb_ref[d, pl.ds(r0, CHUNK), :], sall_ref[d, c],
                            preferred_element_type=F32) * qdec_ref[d]
        of_ref[pl.ds(r0, CHUNK), :] = o
        return 0
    lax.fori_loop(0, N_CHUNKS, cross_body, 0, unroll=True)

    o = of_ref[...]
    gmat = gmat_ref[...]
    o_hi = o.astype(BF16)
    o_lo = (o - o_hi.astype(F32)).astype(BF16)
    mu_h = jnp.dot(o_hi, gmat, preferred_element_type=F32) + jnp.dot(o_lo, gmat, preferred_element_type=F32)
    cen_h = o - mu_h
    var_h = jnp.dot((cen_h * cen_h).astype(BF16), gmat, preferred_element_type=F32)
    gate = proj_ref[:, COL_G:COL_G + GROUP_W]
    cat_ref[:, 3 * GROUP_W:4 * GROUP_W] = (_silu(gate) * (cen_h * lax.rsqrt(var_h + EPS))).astype(BF16)

    y_ref[...] = x_ref[...] + ga1 * jnp.dot(cat_ref[...], w_out_ref[...], preferred_element_type=F32)


def _layer_spec(shape, layer):
    zeros = (0,) * len(shape)
    return pl.BlockSpec((None,) + tuple(shape), lambda i: (layer,) + zeros, pipeline_mode=pl.Buffered(1))


def _mod_spec(layer):
    return pl.BlockSpec((MOD_ROWS, 6 * D_MODEL), lambda i: (layer, 0), pipeline_mode=pl.Buffered(1))


def _mixer(x, mods, pp, *, layer, seq_len, rotate, mod_base, mod_stride, rope_tabs=None, s0=None, states=None):
    nb = x.shape[0]
    n_seq = TB // seq_len
    in_specs = [
        pl.BlockSpec((None, TB, D_MODEL), lambda i: (i, 0, 0)),
        _mod_spec(layer),
        _const_spec((DEPTH, D_MODEL)),
        _layer_spec((D_MODEL, IN_COLS), layer),
        _layer_spec((D_MODEL, D_MODEL), layer),
        _layer_spec((CONV_W, GROUP_W), layer),
        _const_spec((DEPTH, GROUP_W)), _const_spec((DEPTH, GROUP_W)), _const_spec((DEPTH, GROUP_W)),
        _layer_spec((GROUP_W, GROUP_W), layer),
        _layer_spec((HEADS, CHUNK, CHUNK), layer),
        _layer_spec((CHUNK, GROUP_W), layer),
        _layer_spec((HEADS, HEAD_D, HEAD_D), layer),
        _const_spec((DEPTH, GROUP_W)),
        pl.BlockSpec(memory_space=pltpu.SMEM),
    ]
    args = [x, mods, pp["g1"], pp["w_in"], pp["w_out"], pp["dw"], pp["cb"], pp["lng"], pp["lnb"],
            pp["pw"], pp["ws"], pp["gbias"], pp["pool_w"], pp["pscale"], pp["lg"]]
    out_shape = [jax.ShapeDtypeStruct((nb, TB, D_MODEL), F32)]
    out_specs = [pl.BlockSpec((None, TB, D_MODEL), lambda i: (i, 0, 0))]
    aliases = {}
    if rotate:
        in_specs += [_const_spec((TB, GROUP_W)), _const_spec((TB, GROUP_W)),
                     pl.BlockSpec((None, None, 2, HEADS, HEAD_D, HEAD_D), lambda i: (i, layer, 0, 0, 0, 0))]
        args += [rope_tabs[0], rope_tabs[1], s0]
    else:
        out_shape.append(jax.ShapeDtypeStruct((nb * n_seq, DEPTH, 2, HEADS, HEAD_D, HEAD_D), F32))
        if layer == 0:
            out_specs.append(pl.BlockSpec((n_seq, DEPTH, 2, HEADS, HEAD_D, HEAD_D),
                                          lambda i: (i, 0, 0, 0, 0, 0)))
        else:
            in_specs.append(pl.BlockSpec(memory_space=pl.ANY))
            args.append(states)
            aliases = {len(args) - 1: 1}
            out_specs.append(pl.BlockSpec((n_seq, None, 2, HEADS, HEAD_D, HEAD_D),
                                          lambda i: (i, layer, 0, 0, 0, 0)))
    scratch = [
        pltpu.VMEM((TB, D_MODEL), BF16),
        pltpu.VMEM((TB, IN_COLS), F32),
        pltpu.VMEM((n_seq, seq_len + 2 * PAD, GROUP_W), F32),
        pltpu.VMEM((n_seq, seq_len + 2 * PAD, GROUP_W), F32),
        pltpu.VMEM((TB, GROUP_W), F32),
        pltpu.VMEM((TB, GROUP_W), F32),
        pltpu.VMEM((2, CHUNK, HEADS * CHUNK), F32),
        pltpu.VMEM((2, CHUNK, GROUP_W), F32),
        pltpu.VMEM((2, CHUNK, GROUP_W), F32),
        pltpu.VMEM((2, 1, GROUP_W), F32),
        pltpu.VMEM((GROUP_W, GROUP_W), BF16),
        pltpu.VMEM((2, TB, GROUP_W), BF16),
        pltpu.VMEM((2, N_CHUNKS, HEAD_D, GROUP_W), F32),
        pltpu.VMEM((2, N_CHUNKS, GROUP_W, GROUP_W), BF16),
        pltpu.VMEM((GROUP_W, GROUP_W), BF16),
        pltpu.VMEM((CHUNK, HEADS * CHUNK), BF16),
        pltpu.VMEM((GROUP_W, GROUP_W), BF16),
        pltpu.VMEM((2, HEAD_D, GROUP_W), F32),
    ]
    outs = pl.pallas_call(
        functools.partial(_mixer_kernel, layer=layer, seq_len=seq_len, rotate=rotate,
                          mod_base=mod_base, mod_stride=mod_stride),
        grid=(nb,),
        in_specs=in_specs,
        out_specs=out_specs,
        out_shape=out_shape,
        scratch_shapes=scratch,
        input_output_aliases=aliases,
        compiler_params=pltpu.CompilerParams(
            dimension_semantics=("arbitrary",), vmem_limit_bytes=VMEM_LIMIT_BYTES),
        name="mixer_lat" if rotate else "mixer_ctx",
    )(*args)
    return outs


def _ffn_kernel(x_ref, mod_ref, g2_ref, w_in_hbm, w_out_hbm, gf_ref, y_ref,
                hbf_ref, act_ref, act3_ref, wgu_ref, wo_ref, stg_g_ref, stg_u_ref, stg_o_ref, sem,
                *, layer, final_norm, mod_base, mod_stride):
    pid = pl.program_id(0)
    mod = _mod_row(mod_ref, mod_base, mod_stride)
    sh2 = mod(3)
    sc2 = mod(4)
    ga2 = mod(5)
    gs2 = g2_ref[layer:layer + 1, :] * (1.0 + sc2)

    def tile_copies(j, slot):
        return (
            pltpu.make_async_copy(w_in_hbm.at[layer, :, pl.ds(j * FF_TILE, FF_TILE)],
                                  stg_g_ref.at[slot], sem.at[0, slot]),
            pltpu.make_async_copy(w_in_hbm.at[layer, :, pl.ds(D_FF + j * FF_TILE, FF_TILE)],
                                  stg_u_ref.at[slot], sem.at[1, slot]),
            pltpu.make_async_copy(w_out_hbm.at[layer, pl.ds(j * FF_TILE, FF_TILE), :],
                                  stg_o_ref.at[slot], sem.at[2, slot]),
        )

    def start_tile(j):
        for cp, priority in zip(tile_copies(j, j % FF_RING), (0, 1, 0)):
            cp.start(priority=priority)

    def hidden_tile(j, act_out):
        gate = jnp.dot(hbf_ref[...], wgu_ref[j], preferred_element_type=F32)
        up = jnp.dot(hbf_ref[...], wgu_ref[N_FF_TILES + j], preferred_element_type=F32)
        act_out((_silu(gate) * up).astype(BF16))

    @pl.when(pid == 0)
    def _():
        for j in range(FF_LOOKAHEAD):
            start_tile(j)

    def norm_body(i, _):
        r0 = pl.multiple_of(i * ROW_TILE, ROW_TILE)
        h = _norm_mod(x_ref[pl.ds(r0, ROW_TILE), :], gs2, sh2)
        hbf_ref[pl.ds(r0, ROW_TILE), :] = h.astype(BF16)
        return 0
    lax.fori_loop(0, TB // ROW_TILE, norm_body, 0)

    @pl.when(pid == 0)
    def _():
        def fetch_body(j, _):
            @pl.when(j + FF_LOOKAHEAD < N_FF_TILES)
            def _():
                start_tile(j + FF_LOOKAHEAD)
            slot = j % FF_RING
            for cp in tile_copies(j, slot):
                cp.wait()
            wgu_ref[j] = stg_g_ref[slot].astype(BF16)
            wgu_ref[N_FF_TILES + j] = stg_u_ref[slot].astype(BF16)
            wo_ref[pl.ds(pl.multiple_of(j * FF_TILE, FF_TILE), FF_TILE), :] = stg_o_ref[slot].astype(BF16)

            def act_out(act):
                act3_ref[j] = act
            hidden_tile(j, act_out)
            return 0
        lax.fori_loop(0, N_FF_TILES, fetch_body, 0)
        for j in range(N_FF_TILES):
            act_ref[:, j * FF_TILE:(j + 1) * FF_TILE] = act3_ref[j]

    @pl.when(pid != 0)
    def _():
        for j in range(N_FF_TILES):
            def act_out(act, j=j):
                act_ref[:, j * FF_TILE:(j + 1) * FF_TILE] = act
            hidden_tile(j, act_out)

    for r0 in range(0, TB, FF_OUT_ROWS):
        rows = slice(r0, r0 + FF_OUT_ROWS)
        y = x_ref[rows, :] + ga2 * jnp.dot(act_ref[rows, :], wo_ref[...], preferred_element_type=F32)
        if final_norm:
            ms = jnp.mean(y * y, axis=-1, keepdims=True)
            y = y * lax.rsqrt(ms + EPS) * gf_ref[...]
        y_ref[rows, :] = y


def _ffn(x, mods, g2, w_ffn_in, w_ffn_out, g_final, *, layer, mod_base, mod_stride, final_norm):
    nb = x.shape[0]
    return pl.pallas_call(
        functools.partial(_ffn_kernel, layer=layer, final_norm=final_norm, mod_base=mod_base,
                          mod_stride=mod_stride),
        grid=(nb,),
        in_specs=[
            pl.BlockSpec((None, TB, D_MODEL), lambda i: (i, 0, 0)),
            _mod_spec(layer),
            _const_spec((DEPTH, D_MODEL)),
            pl.BlockSpec(memory_space=pl.ANY),
            pl.BlockSpec(memory_space=pl.ANY),
            _const_spec((1, D_MODEL)),
        ],
        out_specs=pl.BlockSpec((None, TB, D_MODEL), lambda i: (i, 0, 0)),
        out_shape=jax.ShapeDtypeStruct((nb, TB, D_MODEL), F32),
        scratch_shapes=[
            pltpu.VMEM((TB, D_MODEL), BF16),
            pltpu.VMEM((TB, D_FF), BF16),
            pltpu.VMEM((N_FF_TILES, TB, FF_TILE), BF16),
            pltpu.VMEM((2 * N_FF_TILES, D_MODEL, FF_TILE), BF16),
            pltpu.VMEM((D_FF, D_MODEL), BF16),
            pltpu.VMEM((FF_RING, D_MODEL, FF_TILE), F32),
            pltpu.VMEM((FF_RING, D_MODEL, FF_TILE), F32),
            pltpu.VMEM((FF_RING, FF_TILE, D_MODEL), F32),
            pltpu.SemaphoreType.DMA((3, FF_RING)),
        ],
        compiler_params=pltpu.CompilerParams(
            dimension_semantics=("arbitrary",), vmem_limit_bytes=VMEM_LIMIT_BYTES),
        name="ffn",
    )(x, mods, g2, w_ffn_in, w_ffn_out, g_final)


def _rope_tables(seq_len):
    t = np.arange(seq_len)
    r = (t // GRID_W).astype(np.float32)
    c = (t % GRID_W).astype(np.float32)
    nf = HEAD_D // 4
    inv = np.float32(ROPE_BASE) ** (-np.arange(nf, dtype=np.float32) / np.float32(nf))
    ang_r = r[:, None] * inv
    ang_c = c[:, None] * inv
    cos = np.concatenate([np.cos(ang_r), np.cos(ang_r), np.cos(ang_c), np.cos(ang_c)], axis=-1)
    sin = np.concatenate([-np.sin(ang_r), np.sin(ang_r), -np.sin(ang_c), np.sin(ang_c)], axis=-1)
    return (jnp.asarray(np.tile(cos, (1, HEADS)), dtype=F32), jnp.asarray(np.tile(sin, (1, HEADS)), dtype=F32))


def kernel(x_prompt, x_sample, state_ret, c, c_ctx, w_ada, b_ada, g_norm1, g_norm2, w_in, w_out, conv_dw,
           conv_b, conv_ln_g, conv_ln_b, conv_pw, gmlp_ws, gmlp_b, pool_w, pool_scale, ret_decay, w_ffn_in,
           w_ffn_out, g_final):
    batch, seq, _ = x_prompt.shape
    dec_batch, dec_seq, _ = x_sample.shape
    assert dec_seq == TB and TB % seq == 0 and (batch * seq) % TB == 0
    assert 1 + dec_batch <= MOD_ROWS

    cs = jnp.concatenate([c_ctx[None, :], c, jnp.zeros((MOD_ROWS - 1 - dec_batch, D_MODEL), F32)], axis=0)
    mods = _ada_rows(cs, w_ada, b_ada).reshape(DEPTH * MOD_ROWS, 6 * D_MODEL)
    rope_tabs = _rope_tables(dec_seq)
    g_final2 = g_final.reshape(1, D_MODEL)
    pp = {
        "g1": g_norm1,
        "g2": g_norm2,
        "w_in": w_in.astype(BF16),
        "w_out": w_out.astype(BF16),
        "dw": conv_dw,
        "cb": conv_b,
        "lng": conv_ln_g,
        "lnb": conv_ln_b,
        "pw": conv_pw,
        "ws": gmlp_ws,
        "gbias": jnp.repeat(jnp.swapaxes(gmlp_b, 1, 2), HEAD_D, axis=2),
        "pool_w": pool_w,
        "pscale": pool_scale,
        "lg": jax.nn.log_sigmoid(ret_decay.astype(F32)).reshape(DEPTH * 2 * HEADS),
    }

    xc = x_prompt.reshape(batch * seq // TB, TB, D_MODEL)
    xl = x_sample
    states = None
    for l in range(DEPTH):
        last = l == DEPTH - 1
        xc, states = _mixer(xc, mods, pp, layer=l, seq_len=seq, rotate=False, mod_base=0, mod_stride=0,
                            states=states)
        xc = _ffn(xc, mods, pp["g2"], w_ffn_in, w_ffn_out, g_final2, layer=l, mod_base=0, mod_stride=0,
                  final_norm=last)
        (xl,) = _mixer(xl, mods, pp, layer=l, seq_len=dec_seq, rotate=True, mod_base=1, mod_stride=1,
                       rope_tabs=rope_tabs, s0=state_ret.astype(F32))
        xl = _ffn(xl, mods, pp["g2"], w_ffn_in, w_ffn_out, g_final2, layer=l, mod_base=1, mod_stride=1,
                  final_norm=last)

    y_prompt = xc.reshape(batch, seq, D_MODEL)
    return (y_prompt, xl, states.astype(x_prompt.dtype))
```

```python
import functools

import jax
import jax.numpy as jnp
import numpy as np
from jax import lax
from jax.experimental import pallas as pl
from jax.experimental.pallas import tpu as pltpu

F32 = jnp.float32
BF16 = jnp.bfloat16

D_MODEL = 1024
DEPTH = 2
GRID_W = 64
GROUP_W = D_MODEL // 4
CONV_W = 31
CHUNK = 128
HEADS = 4
HEAD_D = GROUP_W // HEADS
POOL_WINDOWS = (2, 4, 8, 16)
POOL_GW = GROUP_W // len(POOL_WINDOWS)
ROPE_BASE = 10000.0
D_FF = 2816
IN_COLS = 11 * GROUP_W
EPS = 1e-6

LANES = 128
SUBLANES = 8

TB = 1024
N_CHUNKS = TB // CHUNK
ROW_TILE = 256
CONV_TILE = 64
N_CONV_TILES = TB // CONV_TILE
PAD = 16
POOL_HALO = max(POOL_WINDOWS) // 2
ROPE_PAIR = HEAD_D // 4
FF_TILE = 256
N_FF_TILES = D_FF // FF_TILE
FF_OUT_ROWS = 256
FF_LOOKAHEAD = 2
FF_RING = FF_LOOKAHEAD + 1
ADA_TILE = 1536
ADA_RING = 3
MOD_ROWS = 8
VMEM_LIMIT_BYTES = 60 * 1024 * 1024

COL_A, COL_B, COL_C, COL_D = 0, 2 * GROUP_W, 4 * GROUP_W, 5 * GROUP_W
COL_V, COL_G = COL_D + 4 * GROUP_W, COL_D + 5 * GROUP_W


def _sigmoid(x):
    return 1.0 / (1.0 + jnp.exp(-x))


def _silu(x):
    return x * _sigmoid(x)


def _norm_mod(x, gain_scale, shift):
    ms = jnp.mean(x * x, axis=-1, keepdims=True)
    return x * lax.rsqrt(ms + EPS) * gain_scale + shift


def _head_stack(x, lane_head):
    return jnp.concatenate([jnp.where(lane_head == h, x, 0.0) for h in range(HEADS)], axis=0)


def _mod_row(mod_ref, mod_base, mod_stride):
    row = jnp.maximum(mod_base + mod_stride * pl.program_id(0), 0) if mod_stride else max(mod_base, 0)

    def part(k):
        return mod_ref[pl.ds(row, 1), k * D_MODEL:(k + 1) * D_MODEL]
    return part


def _const_spec(shape):
    zeros = (0,) * len(shape)
    return pl.BlockSpec(shape, lambda i: zeros, pipeline_mode=pl.Buffered(1))


def _ada_kernel(c_ref, w_hbm, b_ref, o_ref, stg_ref, sem):
    tiles = [(l, j * ADA_TILE) for l in range(DEPTH) for j in range(6 * D_MODEL // ADA_TILE)]

    def tile_copy(t):
        l, c0 = tiles[t]
        slot = t % ADA_RING
        return pltpu.make_async_copy(w_hbm.at[l, :, pl.ds(c0, ADA_TILE)], stg_ref.at[slot], sem.at[slot])

    for t in range(ADA_RING - 1):
        tile_copy(t).start()
    a = _silu(c_ref[...]).astype(BF16)
    for t, (l, c0) in enumerate(tiles):
        if t + ADA_RING - 1 < len(tiles):
            tile_copy(t + ADA_RING - 1).start()
        tile_copy(t).wait()
        w = stg_ref[t % ADA_RING].astype(BF16)
        o_ref[l, :, c0:c0 + ADA_TILE] = (jnp.dot(a, w, preferred_element_type=F32)
                                         + b_ref[l:l + 1, c0:c0 + ADA_TILE])


def _ada_rows(cs, w_ada, b_ada):
    assert DEPTH * (6 * D_MODEL // ADA_TILE) >= ADA_RING
    return pl.pallas_call(
        _ada_kernel,
        in_specs=[
            pl.BlockSpec(memory_space=pltpu.VMEM),
            pl.BlockSpec(memory_space=pl.ANY),
            pl.BlockSpec(memory_space=pltpu.VMEM),
        ],
        out_specs=pl.BlockSpec(memory_space=pltpu.VMEM),
        out_shape=jax.ShapeDtypeStruct((DEPTH, MOD_ROWS, 6 * D_MODEL), F32),
        scratch_shapes=[pltpu.VMEM((ADA_RING, D_MODEL, ADA_TILE), F32), pltpu.SemaphoreType.DMA((ADA_RING,))],
        compiler_params=pltpu.CompilerParams(vmem_limit_bytes=VMEM_LIMIT_BYTES),
        name="ada_rows",
    )(cs, w_ada, b_ada)


def _mixer_kernel(*refs, layer, seq_len, rotate, mod_base, mod_stride):
    n_seq = TB // seq_len
    n_chunk = seq_len // CHUNK
    it = iter(refs)
    x_ref, mod_ref, g1_ref, w_in_ref, w_out_ref = (next(it) for _ in range(5))
    dw_ref, cb_ref, lng_ref, lnb_ref, pw_ref = (next(it) for _ in range(5))
    ws_ref, gbias_ref, poolw_ref, pscale_ref, lg_ref = (next(it) for _ in range(5))
    if rotate:
        cos_ref, sin_ref, s0_ref = (next(it) for _ in range(3))
    elif layer > 0:
        next(it)
    y_ref = next(it)
    if not rotate:
        st_ref = next(it)
    (hbf_ref, proj_ref, pad_ref, pad2_ref, tmp_ref, tmp2_ref,
     dcat_ref, qdec_ref, kdec_ref, sdec_ref, gmat_ref, qb_ref, upd_ref, sall_ref,
     pwb_ref, wcat_ref, pbd_ref, s0c_ref) = it
    cat_ref = hbf_ref
    of_ref = tmp_ref

    pwb_ref[...] = pw_ref[...].astype(BF16)
    pbd_ref[...] = jnp.zeros((GROUP_W, GROUP_W), BF16)
    for h in range(HEADS):
        wcat_ref[:, h * CHUNK:(h + 1) * CHUNK] = ws_ref[h].astype(BF16)
        pbd_ref[h * HEAD_D:(h + 1) * HEAD_D, h * HEAD_D:(h + 1) * HEAD_D] = poolw_ref[h].astype(BF16)
        if rotate:
            for d in range(2):
                s0c_ref[d, :, h * HEAD_D:(h + 1) * HEAD_D] = s0_ref[d, h]

    mod = _mod_row(mod_ref, mod_base, mod_stride)
    sh1 = mod(0)
    sc1 = mod(1)
    ga1 = mod(2)
    g1, cb, lng, lnb, pscale = (r[layer:layer + 1, :] for r in (g1_ref, cb_ref, lng_ref, lnb_ref, pscale_ref))

    lane = lax.broadcasted_iota(jnp.int32, (1, GROUP_W), 1)
    lane_head = lane // HEAD_D
    tiles_per_seq = seq_len // CONV_TILE
    chunks_per_seq = seq_len // CHUNK

    def seq_and_offset(i, per_seq, size):
        if n_seq == 1:
            return 0, pl.multiple_of(i * size, size)
        return i // per_seq, pl.multiple_of((i % per_seq) * size, size)

    gs1 = g1 * (1.0 + sc1)
    normed = []
    for r0 in range(0, TB, ROW_TILE):
        h = _norm_mod(x_ref[r0:r0 + ROW_TILE, :], gs1, sh1).astype(BF16)
        hbf_ref[r0:r0 + ROW_TILE, :] = h
        normed.append(h)
    proj_ref[:, COL_A:COL_B] = jnp.dot(
        jnp.concatenate(normed, axis=0), w_in_ref[:, COL_A:COL_B], preferred_element_type=F32)

    for s in range(n_seq):
        for ref in (pad_ref, pad2_ref):
            ref[s, 0:PAD, :] = jnp.zeros((PAD, GROUP_W), F32)
            ref[s, PAD + seq_len:PAD + seq_len + PAD, :] = jnp.zeros((PAD, GROUP_W), F32)

    def glu_body(i, _):
        s, t0 = seq_and_offset(i, chunks_per_seq, CHUNK)
        r0 = pl.multiple_of(i * CHUNK, CHUNK)
        a1 = proj_ref[pl.ds(r0, CHUNK), COL_A:COL_A + GROUP_W]
        a2 = proj_ref[pl.ds(r0, CHUNK), COL_A + GROUP_W:COL_B]
        pad_ref[s, pl.ds(pl.multiple_of(t0 + PAD, SUBLANES), CHUNK), :] = a1 * _sigmoid(a2)
        return 0
    lax.fori_loop(0, N_CHUNKS, glu_body, 0, unroll=True)

    win = CONV_TILE + 2 * PAD

    def conv_tile(i):
        s, t0 = divmod(i, tiles_per_seq)
        t0 *= CONV_TILE
        for c0 in range(0, GROUP_W, LANES):
            w = pad_ref[s, t0:t0 + win, c0:c0 + LANES]
            acc = jnp.zeros((CONV_TILE, LANES), F32) + cb[:, c0:c0 + LANES]
            for b in range(SUBLANES):
                wb = w if b == 0 else pltpu.roll(w, win - b, axis=0)
                for a in range(2 * PAD // SUBLANES):
                    k = SUBLANES * a + b - 1
                    if 0 <= k < CONV_W:
                        acc = acc + dw_ref[k:k + 1, c0:c0 + LANES] * wb[SUBLANES * a:SUBLANES * a + CONV_TILE, :]
            tmp_ref[i * CONV_TILE:(i + 1) * CONV_TILE, c0:c0 + LANES] = acc

    assert POOL_WINDOWS == (2, 4, 8, 16) and 2 * POOL_GW == LANES and POOL_HALO == SUBLANES
    pwin = CONV_TILE + 2 * POOL_HALO
    first_half = lax.broadcasted_iota(jnp.int32, (1, LANES), 1) < POOL_GW

    def pool_tile(i):
        s, t0 = divmod(i, tiles_per_seq)
        t0 *= CONV_TILE
        interior = t0 >= POOL_HALO and t0 + CONV_TILE + POOL_HALO <= seq_len
        outs = []
        for col, (w_small, w_big) in enumerate((POOL_WINDOWS[0:2], POOL_WINDOWS[2:4])):
            w = pad2_ref[s, t0 + PAD - POOL_HALO:t0 + PAD - POOL_HALO + pwin,
                         col * LANES:(col + 1) * LANES]

            rolled = {0: w}

            def shifted(b, w=w, rolled=rolled):
                if b not in rolled:
                    rolled[b] = pltpu.roll(w, pwin - b, axis=0)
                return rolled[b]

            def lo(b):
                return shifted(b)[0:CONV_TILE, :]

            def hi(b):
                return shifted(b)[POOL_HALO:POOL_HALO + CONV_TILE, :]

            tok = hi(0)
            if col == 0:
                s_small = lo(7) + tok
                s_big = s_small + lo(6) + hi(1)
            else:
                s_small = lo(7) + tok + lo(6) + hi(1) + lo(5) + lo(4) + hi(2) + hi(3)
                s_big = s_small + lo(3) + lo(2) + lo(1) + lo(0) + hi(4) + hi(5) + hi(6) + hi(7)
            ssum = jnp.where(first_half, s_small, s_big)
            if interior:
                mean = ssum * jnp.where(first_half, 1.0 / w_small, 1.0 / w_big)
            else:
                half = jnp.where(first_half, w_small // 2, w_big // 2)
                tpos = t0 + lax.broadcasted_iota(jnp.int32, (CONV_TILE, LANES), 0)
                cnt = jnp.minimum(tpos + half, seq_len) - jnp.maximum(tpos - half, 0)
                mean = ssum / cnt.astype(F32)
            outs.append(mean - tok)
        tmp2_ref[i * CONV_TILE:(i + 1) * CONV_TILE, :] = jnp.concatenate(outs, axis=1)

    col_tiles = [COL_C] + list(range(COL_B, COL_C, GROUP_W)) + list(range(COL_D, IN_COLS, GROUP_W))
    per_step = -(-N_CONV_TILES // (len(col_tiles) - 1))
    conv_next = 0
    pool_next = 0
    for step, c0 in enumerate(col_tiles):
        proj_ref[:, c0:c0 + GROUP_W] = jnp.dot(
            hbf_ref[...], w_in_ref[:, c0:c0 + GROUP_W], preferred_element_type=F32)
        if step == 1:
            for s in range(n_seq):
                pad2_ref[s, PAD:PAD + seq_len, :] = proj_ref[s * seq_len:(s + 1) * seq_len, COL_C:COL_D]
        for _ in range(per_step):
            if conv_next < N_CONV_TILES:
                conv_tile(conv_next)
                conv_next += 1
            if step >= 1 and pool_next < N_CONV_TILES:
                pool_tile(pool_next)
                pool_next += 1
    assert conv_next == N_CONV_TILES and pool_next == N_CONV_TILES

    c = tmp_ref[...]
    mu = jnp.mean(c, axis=-1, keepdims=True)
    cen = c - mu
    var = jnp.mean(cen * cen, axis=-1, keepdims=True)
    hn = cen * lax.rsqrt(var + EPS) * lng + lnb
    ya = jnp.dot(_silu(hn).astype(BF16), pwb_ref[...], preferred_element_type=F32)
    cat_ref[:, 0:GROUP_W] = ya.astype(BF16)

    def gmlp_body(i, _):
        r0 = pl.multiple_of(i * CHUNK, CHUNK)
        u = proj_ref[pl.ds(r0, CHUNK), COL_B:COL_B + GROUP_W]
        v = proj_ref[pl.ds(r0, CHUNK), COL_B + GROUP_W:COL_C]
        vstack = _head_stack(v, lane_head).astype(BF16)
        sg = jnp.dot(wcat_ref[...], vstack, preferred_element_type=F32) + gbias_ref[...]
        cat_ref[pl.ds(r0, CHUNK), GROUP_W:2 * GROUP_W] = (u * sg).astype(BF16)
        return 0
    lax.fori_loop(0, N_CHUNKS, gmlp_body, 0, unroll=True)

    yc = jnp.dot(tmp2_ref[...].astype(BF16), pbd_ref[...], preferred_element_type=F32) * pscale
    cat_ref[:, 2 * GROUP_W:3 * GROUP_W] = yc.astype(BF16)

    ri = lax.broadcasted_iota(jnp.int32, (CHUNK, HEADS * CHUNK), 0)
    ci = lax.broadcasted_iota(jnp.int32, (CHUNK, HEADS * CHUNK), 1)
    cj = ci % CHUNK
    chead = ci // CHUNK
    rq = lax.broadcasted_iota(jnp.int32, (CHUNK, GROUP_W), 0).astype(F32)
    for d in range(2):
        lgs = [lg_ref[(2 * layer + d) * HEADS + h] for h in range(HEADS)]
        lg_wide = jnp.where(chead == 0, lgs[0], jnp.where(chead == 1, lgs[1],
                                                          jnp.where(chead == 2, lgs[2], lgs[3])))
        lg_lane = jnp.where(lane_head == 0, lgs[0], jnp.where(lane_head == 1, lgs[1],
                                                              jnp.where(lane_head == 2, lgs[2], lgs[3])))
        dist = (ri - cj) if d == 0 else (cj - ri)
        keep = dist >= 0
        dcat_ref[d] = jnp.where(keep, jnp.exp(jnp.where(keep, dist, 0).astype(F32) * lg_wide), 0.0)
        if d == 0:
            qdec_ref[d] = jnp.exp((rq + 1.0) * lg_lane)
            kdec_ref[d] = jnp.exp((CHUNK - 1.0 - rq) * lg_lane)
        else:
            qdec_ref[d] = jnp.exp((CHUNK - rq) * lg_lane)
            kdec_ref[d] = jnp.exp(rq * lg_lane)
        sdec_ref[d] = jnp.exp(float(CHUNK) * lg_lane)

    rr = lax.broadcasted_iota(jnp.int32, (GROUP_W, GROUP_W), 0) // HEAD_D
    cc = lax.broadcasted_iota(jnp.int32, (GROUP_W, GROUP_W), 1) // HEAD_D
    gmat_ref[...] = jnp.where(rr == cc, 1.0 / HEAD_D, 0.0).astype(BF16)

    lane_bit = (lane & ROPE_PAIR) == 0
    k_scale = HEAD_D ** -0.5

    def rope(z, r0):
        if not rotate:
            return z
        cos = cos_ref[pl.ds(r0, CHUNK), :]
        sin = sin_ref[pl.ds(r0, CHUNK), :]
        halves = []
        for c0 in range(0, GROUP_W, LANES):
            zz = z[:, c0:c0 + LANES]
            partner = jnp.where(lane_bit[:, c0:c0 + LANES],
                                pltpu.roll(zz, LANES - ROPE_PAIR, axis=1), pltpu.roll(zz, ROPE_PAIR, axis=1))
            halves.append(partner)
        return z * cos + jnp.concatenate(halves, axis=1) * sin

    def pair_stack(zb, pair):
        zero = jnp.zeros_like(zb)
        return jnp.concatenate([jnp.where(lane_head == h, zb, zero) for h in (2 * pair, 2 * pair + 1)], axis=0)

    def intra_body(c, _):
        r0 = pl.multiple_of(c * CHUNK, CHUNK)
        v = proj_ref[pl.ds(r0, CHUNK), COL_V:COL_V + GROUP_W]
        vb = v.astype(BF16)
        atts = [[None, None], [None, None]]
        kds = []
        for d in range(2):
            qc0 = COL_D + 2 * d * GROUP_W
            q = rope(proj_ref[pl.ds(r0, CHUNK), qc0:qc0 + GROUP_W], r0)
            k = rope(proj_ref[pl.ds(r0, CHUNK), qc0 + GROUP_W:qc0 + 2 * GROUP_W], r0) * k_scale
            qb = q.astype(BF16)
            kb = k.astype(BF16)
            qb_ref[d, pl.ds(r0, CHUNK), :] = qb
            for pair in range(2):
                att = lax.dot_general(qb, pair_stack(kb, pair), (((1,), (1,)), ((), ())),
                                      preferred_element_type=F32)
                atts[pair][d] = (att * dcat_ref[d, :, pair * 2 * CHUNK:(pair + 1) * 2 * CHUNK]).astype(BF16)
            kds.append((k * kdec_ref[d]).astype(BF16))
        upd = lax.dot_general(jnp.concatenate(kds, axis=1), vb, (((0,), (0,)), ((), ())),
                              preferred_element_type=F32)
        for d in range(2):
            compact = None
            for h in range(HEADS):
                r_h = d * GROUP_W + h * HEAD_D
                part = jnp.where(lane_head == h, upd[r_h:r_h + HEAD_D, :], 0.0)
                compact = part if compact is None else compact + part
            upd_ref[d, c] = compact
        o = None
        for pair in range(2):
            both = jnp.dot(jnp.concatenate(atts[pair], axis=0), pair_stack(vb, pair), preferred_element_type=F32)
            od = both[0:CHUNK, :] + both[CHUNK:2 * CHUNK, :]
            o = od if o is None else o + od
        of_ref[pl.ds(r0, CHUNK), :] = o
        return 0
    lax.fori_loop(0, N_CHUNKS, intra_body, 0, unroll=True)

    for s in range(n_seq):
        for d in range(2):
            st = s0c_ref[d] if rotate else jnp.zeros((HEAD_D, GROUP_W), F32)
            order = range(n_chunk) if d == 0 else range(n_chunk - 1, -1, -1)
            for c in order:
                cg = s * n_chunk + c
                for h in range(HEADS):
                    sall_ref[d, cg, h * HEAD_D:(h + 1) * HEAD_D, :] = (
                        jnp.where(lane_head == h, st, 0.0).astype(BF16))
                st = st * sdec_ref[d] + upd_ref[d, cg]
            if not rotate:
                for h in range(HEADS):
                    if layer == 0:
                        st_ref[s, 0, d, h] = st[:, h * HEAD_D:(h + 1) * HEAD_D]
                    else:
                        st_ref[s, d, h] = st[:, h * HEAD_D:(h + 1) * HEAD_D]
    if not rotate and layer == 0:
        for s in range(n_seq):
            for later in range(1, DEPTH):
                for d in range(2):
                    for h in range(HEADS):
                        st_ref[s, later, d, h] = jnp.zeros((HEAD_D, HEAD_D), F32)

    def cross_body(c, _):
        r0 = pl.multiple_of(c * CHUNK, CHUNK)
        o = of_ref[pl.ds(r0, CHUNK), :]
        for d in range(2):
            o = o + jnp.dot(qb_ref[d, pl.ds(r0, CHUNK), :], sall_ref[d, c],
                            preferred_element_type=F32) * qdec_ref[d]
        of_ref[pl.ds(r0, CHUNK), :] = o
        return 0
    lax.fori_loop(0, N_CHUNKS, cross_body, 0, unroll=True)

    o = of_ref[...]
    gmat = gmat_ref[...]
    o_hi = o.astype(BF16)
    o_lo = (o - o_hi.astype(F32)).astype(BF16)
    mu_h = jnp.dot(o_hi, gmat, preferred_element_type=F32) + jnp.dot(o_lo, gmat, preferred_element_type=F32)
    cen_h = o - mu_h
    var_h = jnp.dot((cen_h * cen_h).astype(BF16), gmat, preferred_element_type=F32)
    gate = proj_ref[:, COL_G:COL_G + GROUP_W]
    cat_ref[:, 3 * GROUP_W:4 * GROUP_W] = (_silu(gate) * (cen_h * lax.rsqrt(var_h + EPS))).astype(BF16)

    y_ref[...] = x_ref[...] + ga1 * jnp.dot(cat_ref[...], w_out_ref[...], preferred_element_type=F32)


def _layer_spec(shape, layer):
    zeros = (0,) * len(shape)
    return pl.BlockSpec((None,) + tuple(shape), lambda i: (layer,) + zeros, pipeline_mode=pl.Buffered(1))


def _mod_spec(layer):
    return pl.BlockSpec((MOD_ROWS, 6 * D_MODEL), lambda i: (layer, 0), pipeline_mode=pl.Buffered(1))


def _mixer(x, mods, pp, *, layer, seq_len, rotate, mod_base, mod_stride, rope_tabs=None, s0=None, states=None):
    nb = x.shape[0]
    n_seq = TB // seq_len
    in_specs = [
        pl.BlockSpec((None, TB, D_MODEL), lambda i: (i, 0, 0)),
        _mod_spec(layer),
        _const_spec((DEPTH, D_MODEL)),
        _layer_spec((D_MODEL, IN_COLS), layer),
        _layer_spec((D_MODEL, D_MODEL), layer),
        _layer_spec((CONV_W, GROUP_W), layer),
        _const_spec((DEPTH, GROUP_W)), _const_spec((DEPTH, GROUP_W)), _const_spec((DEPTH, GROUP_W)),
        _layer_spec((GROUP_W, GROUP_W), layer),
        _layer_spec((HEADS, CHUNK, CHUNK), layer),
        _layer_spec((CHUNK, GROUP_W), layer),
        _layer_spec((HEADS, HEAD_D, HEAD_D), layer),
        _const_spec((DEPTH, GROUP_W)),
        pl.BlockSpec(memory_space=pltpu.SMEM),
    ]
    args = [x, mods, pp["g1"], pp["w_in"], pp["w_out"], pp["dw"], pp["cb"], pp["lng"], pp["lnb"],
            pp["pw"], pp["ws"], pp["gbias"], pp["pool_w"], pp["pscale"], pp["lg"]]
    out_shape = [jax.ShapeDtypeStruct((nb, TB, D_MODEL), F32)]
    out_specs = [pl.BlockSpec((None, TB, D_MODEL), lambda i: (i, 0, 0))]
    aliases = {}
    if rotate:
        in_specs += [_const_spec((TB, GROUP_W)), _const_spec((TB, GROUP_W)),
                     pl.BlockSpec((None, None, 2, HEADS, HEAD_D, HEAD_D), lambda i: (i, layer, 0, 0, 0, 0))]
        args += [rope_tabs[0], rope_tabs[1], s0]
    else:
        out_shape.append(jax.ShapeDtypeStruct((nb * n_seq, DEPTH, 2, HEADS, HEAD_D, HEAD_D), F32))
        if layer == 0:
            out_specs.append(pl.BlockSpec((n_seq, DEPTH, 2, HEADS, HEAD_D, HEAD_D),
                                          lambda i: (i, 0, 0, 0, 0, 0)))
        else:
            in_specs.append(pl.BlockSpec(memory_space=pl.ANY))
            args.append(states)
            aliases = {len(args) - 1: 1}
            out_specs.append(pl.BlockSpec((n_seq, None, 2, HEADS, HEAD_D, HEAD_D),
                                          lambda i: (i, layer, 0, 0, 0, 0)))
    scratch = [
        pltpu.VMEM((TB, D_MODEL), BF16),
        pltpu.VMEM((TB, IN_COLS), F32),
        pltpu.VMEM((n_seq, seq_len + 2 * PAD, GROUP_W), F32),
        pltpu.VMEM((n_seq, seq_len + 2 * PAD, GROUP_W), F32),
        pltpu.VMEM((TB, GROUP_W), F32),
        pltpu.VMEM((TB, GROUP_W), F32),
        pltpu.VMEM((2, CHUNK, HEADS * CHUNK), F32),
        pltpu.VMEM((2, CHUNK, GROUP_W), F32),
        pltpu.VMEM((2, CHUNK, GROUP_W), F32),
        pltpu.VMEM((2, 1, GROUP_W), F32),
        pltpu.VMEM((GROUP_W, GROUP_W), BF16),
        pltpu.VMEM((2, TB, GROUP_W), BF16),
        pltpu.VMEM((2, N_CHUNKS, HEAD_D, GROUP_W), F32),
        pltpu.VMEM((2, N_CHUNKS, GROUP_W, GROUP_W), BF16),
        pltpu.VMEM((GROUP_W, GROUP_W), BF16),
        pltpu.VMEM((CHUNK, HEADS * CHUNK), BF16),
        pltpu.VMEM((GROUP_W, GROUP_W), BF16),
        pltpu.VMEM((2, HEAD_D, GROUP_W), F32),
    ]
    outs = pl.pallas_call(
        functools.partial(_mixer_kernel, layer=layer, seq_len=seq_len, rotate=rotate,
                          mod_base=mod_base, mod_stride=mod_stride),
        grid=(nb,),
        in_specs=in_specs,
        out_specs=out_specs,
        out_shape=out_shape,
        scratch_shapes=scratch,
        input_output_aliases=aliases,
        compiler_params=pltpu.CompilerParams(
            dimension_semantics=("arbitrary",), vmem_limit_bytes=VMEM_LIMIT_BYTES),
        name="mixer_lat" if rotate else "mixer_ctx",
    )(*args)
    return outs


def _ffn_kernel(x_ref, mod_ref, g2_ref, w_in_hbm, w_out_hbm, gf_ref, y_ref,
                hbf_ref, act_ref, act3_ref, wgu_ref, wo_ref, stg_g_ref, stg_u_ref, stg_o_ref, sem,
                *, layer, final_norm, mod_base, mod_stride):
    pid = pl.program_id(0)
    mod = _mod_row(mod_ref, mod_base, mod_stride)
    sh2 = mod(3)
    sc2 = mod(4)
    ga2 = mod(5)
    gs2 = g2_ref[layer:layer + 1, :] * (1.0 + sc2)

    def tile_copies(j, slot):
        return (
            pltpu.make_async_copy(w_in_hbm.at[layer, :, pl.ds(j * FF_TILE, FF_TILE)],
                                  stg_g_ref.at[slot], sem.at[0, slot]),
            pltpu.make_async_copy(w_in_hbm.at[layer, :, pl.ds(D_FF + j * FF_TILE, FF_TILE)],
                                  stg_u_ref.at[slot], sem.at[1, slot]),
            pltpu.make_async_copy(w_out_hbm.at[layer, pl.ds(j * FF_TILE, FF_TILE), :],
                                  stg_o_ref.at[slot], sem.at[2, slot]),
        )

    def start_tile(j):
        for cp in tile_copies(j, j % FF_RING):
            cp.start()

    def hidden_tile(j, act_out):
        gate = jnp.dot(hbf_ref[...], wgu_ref[j], preferred_element_type=F32)
        up = jnp.dot(hbf_ref[...], wgu_ref[N_FF_TILES + j], preferred_element_type=F32)
        act_out((_silu(gate) * up).astype(BF16))

    @pl.when(pid == 0)
    def _():
        for j in range(FF_LOOKAHEAD):
            start_tile(j)

    def norm_body(i, _):
        r0 = pl.multiple_of(i * ROW_TILE, ROW_TILE)
        h = _norm_mod(x_ref[pl.ds(r0, ROW_TILE), :], gs2, sh2)
        hbf_ref[pl.ds(r0, ROW_TILE), :] = h.astype(BF16)
        return 0
    lax.fori_loop(0, TB // ROW_TILE, norm_body, 0)

    @pl.when(pid == 0)
    def _():
        def fetch_body(j, _):
            @pl.when(j + FF_LOOKAHEAD < N_FF_TILES)
            def _():
                start_tile(j + FF_LOOKAHEAD)
            slot = j % FF_RING
            for cp in tile_copies(j, slot):
                cp.wait()
            wgu_ref[j] = stg_g_ref[slot].astype(BF16)
            wgu_ref[N_FF_TILES + j] = stg_u_ref[slot].astype(BF16)
            wo_ref[pl.ds(pl.multiple_of(j * FF_TILE, FF_TILE), FF_TILE), :] = stg_o_ref[slot].astype(BF16)

            def act_out(act):
                act3_ref[j] = act
            hidden_tile(j, act_out)
            return 0
        lax.fori_loop(0, N_FF_TILES, fetch_body, 0)
        for j in range(N_FF_TILES):
            act_ref[:, j * FF_TILE:(j + 1) * FF_TILE] = act3_ref[j]

    @pl.when(pid != 0)
    def _():
        for j in range(N_FF_TILES):
            def act_out(act, j=j):
                act_ref[:, j * FF_TILE:(j + 1) * FF_TILE] = act
            hidden_tile(j, act_out)

    for r0 in range(0, TB, FF_OUT_ROWS):
        rows = slice(r0, r0 + FF_OUT_ROWS)
        y = x_ref[rows, :] + ga2 * jnp.dot(act_ref[rows, :], wo_ref[...], preferred_element_type=F32)
        if final_norm:
            ms = jnp.mean(y * y, axis=-1, keepdims=True)
            y = y * lax.rsqrt(ms + EPS) * gf_ref[...]
        y_ref[rows, :] = y


def _ffn(x, mods, g2, w_ffn_in, w_ffn_out, g_final, *, layer, mod_base, mod_stride, final_norm):
    nb = x.shape[0]
    return pl.pallas_call(
        functools.partial(_ffn_kernel, layer=layer, final_norm=final_norm, mod_base=mod_base,
                          mod_stride=mod_stride),
        grid=(nb,),
        in_specs=[
            pl.BlockSpec((None, TB, D_MODEL), lambda i: (i, 0, 0)),
            _mod_spec(layer),
            _const_spec((DEPTH, D_MODEL)),
            pl.BlockSpec(memory_space=pl.ANY),
            pl.BlockSpec(memory_space=pl.ANY),
            _const_spec((1, D_MODEL)),
        ],
        out_specs=pl.BlockSpec((None, TB, D_MODEL), lambda i: (i, 0, 0)),
        out_shape=jax.ShapeDtypeStruct((nb, TB, D_MODEL), F32),
        scratch_shapes=[
            pltpu.VMEM((TB, D_MODEL), BF16),
            pltpu.VMEM((TB, D_FF), BF16),
            pltpu.VMEM((N_FF_TILES, TB, FF_TILE), BF16),
            pltpu.VMEM((2 * N_FF_TILES, D_MODEL, FF_TILE), BF16),
            pltpu.VMEM((D_FF, D_MODEL), BF16),
            pltpu.VMEM((FF_RING, D_MODEL, FF_TILE), F32),
            pltpu.VMEM((FF_RING, D_MODEL, FF_TILE), F32),
            pltpu.VMEM((FF_RING, FF_TILE, D_MODEL), F32),
            pltpu.SemaphoreType.DMA((3, FF_RING)),
        ],
        compiler_params=pltpu.CompilerParams(
            dimension_semantics=("arbitrary",), vmem_limit_bytes=VMEM_LIMIT_BYTES),
        name="ffn",
    )(x, mods, g2, w_ffn_in, w_ffn_out, g_final)


def _rope_tables(seq_len):
    t = np.arange(seq_len)
    r = (t // GRID_W).astype(np.float32)
    c = (t % GRID_W).astype(np.float32)
    nf = HEAD_D // 4
    inv = np.float32(ROPE_BASE) ** (-np.arange(nf, dtype=np.float32) / np.float32(nf))
    ang_r = r[:, None] * inv
    ang_c = c[:, None] * inv
    cos = np.concatenate([np.cos(ang_r), np.cos(ang_r), np.cos(ang_c), np.cos(ang_c)], axis=-1)
    sin = np.concatenate([-np.sin(ang_r), np.sin(ang_r), -np.sin(ang_c), np.sin(ang_c)], axis=-1)
    return (jnp.asarray(np.tile(cos, (1, HEADS)), dtype=F32), jnp.asarray(np.tile(sin, (1, HEADS)), dtype=F32))


def kernel(x_prompt, x_sample, state_ret, c, c_ctx, w_ada, b_ada, g_norm1, g_norm2, w_in, w_out, conv_dw,
           conv_b, conv_ln_g, conv_ln_b, conv_pw, gmlp_ws, gmlp_b, pool_w, pool_scale, ret_decay, w_ffn_in,
           w_ffn_out, g_final):
    batch, seq, _ = x_prompt.shape
    dec_batch, dec_seq, _ = x_sample.shape
    assert dec_seq == TB and TB % seq == 0 and (batch * seq) % TB == 0
    assert 1 + dec_batch <= MOD_ROWS

    cs = jnp.concatenate([c_ctx[None, :], c, jnp.zeros((MOD_ROWS - 1 - dec_batch, D_MODEL), F32)], axis=0)
    mods = _ada_rows(cs, w_ada, b_ada).reshape(DEPTH * MOD_ROWS, 6 * D_MODEL)
    rope_tabs = _rope_tables(dec_seq)
    g_final2 = g_final.reshape(1, D_MODEL)
    pp = {
        "g1": g_norm1,
        "g2": g_norm2,
        "w_in": w_in.astype(BF16),
        "w_out": w_out.astype(BF16),
        "dw": conv_dw,
        "cb": conv_b,
        "lng": conv_ln_g,
        "lnb": conv_ln_b,
        "pw": conv_pw,
        "ws": gmlp_ws,
        "gbias": jnp.repeat(jnp.swapaxes(gmlp_b, 1, 2), HEAD_D, axis=2),
        "pool_w": pool_w,
        "pscale": pool_scale,
        "lg": jax.nn.log_sigmoid(ret_decay.astype(F32)).reshape(DEPTH * 2 * HEADS),
    }

    xc = x_prompt.reshape(batch * seq // TB, TB, D_MODEL)
    xl = x_sample
    states = None
    for l in range(DEPTH):
        last = l == DEPTH - 1
        xc, states = _mixer(xc, mods, pp, layer=l, seq_len=seq, rotate=False, mod_base=0, mod_stride=0,
                            states=states)
        xc = _ffn(xc, mods, pp["g2"], w_ffn_in, w_ffn_out, g_final2, layer=l, mod_base=0, mod_stride=0,
                  final_norm=last)
        (xl,) = _mixer(xl, mods, pp, layer=l, seq_len=dec_seq, rotate=True, mod_base=1, mod_stride=1,
                       rope_tabs=rope_tabs, s0=state_ret.astype(F32))
        xl = _ffn(xl, mods, pp["g2"], w_ffn_in, w_ffn_out, g_final2, layer=l, mod_base=1, mod_stride=1,
                  final_norm=last)

    y_prompt = xc.reshape(batch, seq, D_MODEL)
    return (y_prompt, xl, states.astype(x_prompt.dtype))
```
